```python
import math
import jax, jax.numpy as jnp
from jax import lax
import numpy as np

D_MODEL = 1024
BATCH = 8
SEQ = 4096
DEPTH = 2

D_MIX = D_MODEL
A_HEADS = 4
A_HEAD_DIM = 64
A_WIDTH = A_HEADS * A_HEAD_DIM
CHUNK = 128
B_HEADS = 8
B_HEAD_DIM = 64
B_WIDTH = B_HEADS * B_HEAD_DIM
DILATED_PAIRS = ((128, 1), (512, 4), (2048, 16))
ROT_DIM = B_HEAD_DIM // 4
ROPE_THETA = 500000.0
C_WIDTH = D_MIX - A_WIDTH - B_WIDTH
SSM_GROUP = 16
SSM_GROUPS = C_WIDTH // SSM_GROUP
SSM_STATE = 64
DT_MIN = 0.001
DT_MAX = 0.1
PROJ_WIDTH = 2 * A_WIDTH + 3 * B_WIDTH + C_WIDTH
N_EXPERTS = 16
N_EXPERT_GROUPS = 4
EXPERTS_PER_GROUP = N_EXPERTS // N_EXPERT_GROUPS
TOP_K = 2
D_EXPERT = D_MODEL // 2
MOE_BLOCK = 128
ALPHA = (2.0 * DEPTH) ** 0.25
BETA = (8.0 * DEPTH) ** -0.25
LN_EPS = 1e-5
NEG_INF = -1e30

kernel_name = 'hybrid_gmlp_dilated_s5_moe_deepnorm'


def layer_norm(x):
    x32 = x.astype(jnp.float32)
    mu = x32.mean(-1, keepdims=True)
    var = jnp.square(x32 - mu).mean(-1, keepdims=True)
    return (x32 - mu) * lax.rsqrt(var + LN_EPS)


def partial_rope(x, positions):
    half = ROT_DIM // 2
    freqs = ROPE_THETA ** (-jnp.arange(half, dtype=jnp.float32) * 2.0 / ROT_DIM)
    ang = positions.astype(jnp.float32)[..., None] * freqs
    cos = jnp.cos(ang)[:, :, None, :]
    sin = jnp.sin(ang)[:, :, None, :]
    x1 = x[..., :half]
    x2 = x[..., half:ROT_DIM]
    return jnp.concatenate([x1 * cos - x2 * sin, x2 * cos + x1 * sin, x[..., ROT_DIM:]], axis=-1)


def chunked_spatial_gating(uv, ln_g, ln_b, w_s, b_s):
    u, v = jnp.split(jax.nn.gelu(uv), 2, axis=-1)
    v = layer_norm(v) * ln_g + ln_b
    bsz, t, _ = v.shape
    vh = v.reshape(bsz, t // CHUNK, CHUNK, A_HEADS, A_HEAD_DIM)
    causal = jnp.tril(jnp.ones((CHUNK, CHUNK), dtype=bool))
    w = jnp.where(causal[None], w_s, 0.0)
    sv = jnp.einsum('hts,bnshe->bnthe', w, vh) + b_s.T[:, :, None]
    return u * sv.reshape(bsz, t, A_WIDTH)


def dilated_branch(q, k, v, window, dilation):
    bsz, t, h, e = q.shape
    blk = window // dilation
    seg = blk * dilation
    tp = -(-t // seg) * seg
    nb = tp // seg
    pad = ((0, 0), (0, tp - t), (0, 0), (0, 0))

    def split(z):
        return jnp.pad(z, pad).reshape(bsz, nb, blk, dilation, h, e)

    def with_prev(z):
        prev = jnp.pad(z[:, :-1], ((0, 0), (1, 0), (0, 0), (0, 0), (0, 0), (0, 0)))
        return jnp.concatenate([prev, z], axis=2)

    qb = split(q)
    kb = with_prev(split(k))
    vb = with_prev(split(v))
    s = jnp.einsum('bnqrhe,bnkrhe->bnrhqk', qb, kb) * (e ** -0.5)
    qi = jnp.arange(blk)[:, None]
    kk = jnp.arange(2 * blk)[None, :]
    band = (kk >= qi) & (kk <= qi + blk)
    valid = band[None] & ((jnp.arange(nb)[:, None, None] > 0) | (kk >= blk)[None])
    s = jnp.where(valid[None, :, None, None], s, NEG_INF)
    m = s.max(-1)
    p = jnp.exp(s - m[..., None])
    den = p.sum(-1)
    o = jnp.einsum('bnrhqk,bnkrhe->bnqrhe', p, vb) / jnp.moveaxis(den, -1, 2)[..., None]
    o = o.reshape(bsz, tp, h, e)[:, :t]
    m = jnp.moveaxis(m, -1, 2).reshape(bsz, tp, h)[:, :t]
    den = jnp.moveaxis(den, -1, 2).reshape(bsz, tp, h)[:, :t]
    return o, m, den


def dilated_attention(q, k, v):
    outs = [dilated_branch(q, k, v, w, d) for (w, d) in DILATED_PAIRS]
    o_all = jnp.stack([o for o, _, _ in outs])
    m_all = jnp.stack([m for _, m, _ in outs])
    d_all = jnp.stack([dn for _, _, dn in outs])
    wts = d_all * jnp.exp(m_all - m_all.max(0))
    return jnp.einsum('gbth,gbthe->bthe', wts, o_all) / wts.sum(0)[..., None]


def s5_mixer(u, lam_re, lam_im, log_dt, b_re, b_im, c_re, c_im, d_skip, glu_w, glu_b):
    f32 = jnp.float32
    u = u.astype(f32)
    bsz, t, _ = u.shape
    ug = u.reshape(bsz, t, SSM_GROUPS, SSM_GROUP)
    lam_re = lam_re.astype(f32)
    lam_im = lam_im.astype(f32)
    dt = jnp.exp(log_dt.astype(f32))[:, None]
    mag = jnp.exp(lam_re * dt)
    ab_re = mag * jnp.cos(lam_im * dt)
    ab_im = mag * jnp.sin(lam_im * dt)
    nr = ab_re - 1.0
    ni = ab_im
    mod2 = lam_re * lam_re + lam_im * lam_im
    f_re = (nr * lam_re + ni * lam_im) / mod2
    f_im = (ni * lam_re - nr * lam_im) / mod2
    b_re = b_re.astype(f32)
    b_im = b_im.astype(f32)
    bb_re = f_re[..., None] * b_re - f_im[..., None] * b_im
    bb_im = f_re[..., None] * b_im + f_im[..., None] * b_re
    bu_re = jnp.einsum('btgc,gnc->btgn', ug, bb_re)
    bu_im = jnp.einsum('btgc,gnc->btgn', ug, bb_im)
    a_re = jnp.broadcast_to(ab_re, bu_re.shape)
    a_im = jnp.broadcast_to(ab_im, bu_im.shape)

    def combine(e1, e2):
        a1r, a1i, b1r, b1i = e1
        a2r, a2i, b2r, b2i = e2
        return (a2r * a1r - a2i * a1i,
                a2r * a1i + a2i * a1r,
                a2r * b1r - a2i * b1i + b2r,
                a2r * b1i + a2i * b1r + b2i)

    _, _, x_re, x_im = lax.associative_scan(combine, (a_re, a_im, bu_re, bu_im), axis=1)
    y = (jnp.einsum('btgn,gcn->btgc', x_re, c_re.astype(f32))
         - jnp.einsum('btgn,gcn->btgc', x_im, c_im.astype(f32))
         + d_skip.astype(f32) * ug)
    y = jax.nn.gelu(y.reshape(bsz, t, C_WIDTH))
    return y * jax.nn.sigmoid(y @ glu_w + glu_b)


def grouped_moe(h, router_w, router_bias, w_gate, w_up, w_down):
    bsz, t, d = h.shape
    n_tok = bsz * t
    ht = h.reshape(n_tok, d)
    scores = jax.nn.sigmoid(ht.astype(jnp.float32) @ router_w.astype(jnp.float32))
    sel = scores + router_bias.astype(jnp.float32)
    sel_g = sel.reshape(n_tok, N_EXPERT_GROUPS, EXPERTS_PER_GROUP)
    group_score = lax.top_k(sel_g, 2)[0].sum(-1)
    g_idx = jnp.argmax(group_score, axis=-1)
    sel_in = jnp.take_along_axis(sel_g, g_idx[:, None, None], axis=1)[:, 0]
    _, loc = lax.top_k(sel_in, TOP_K)
    e_idx = g_idx[:, None] * EXPERTS_PER_GROUP + loc
    gate = jnp.take_along_axis(scores, e_idx, axis=1)
    gate = gate / gate.sum(-1, keepdims=True)

    n_asg = n_tok * TOP_K
    flat_e = e_idx.reshape(n_asg)
    flat_tok = jnp.repeat(jnp.arange(n_tok, dtype=jnp.int32), TOP_K)
    flat_w = gate.reshape(n_asg)
    order = jnp.argsort(flat_e)
    se = flat_e[order]
    counts = jnp.bincount(flat_e, length=N_EXPERTS)
    starts = jnp.cumsum(counts) - counts
    pcounts = (counts + MOE_BLOCK - 1) // MOE_BLOCK * MOE_BLOCK
    pends = jnp.cumsum(pcounts)
    pstarts = pends - pcounts
    dest = pstarts[se] + (jnp.arange(n_asg) - starts[se])
    cap = n_asg + N_EXPERTS * MOE_BLOCK
    n_blk = cap // MOE_BLOCK
    slot_tok = jnp.zeros((cap,), jnp.int32).at[dest].set(flat_tok[order])
    slot_w = jnp.zeros((cap,), jnp.float32).at[dest].set(flat_w[order])
    blk_e = jnp.minimum(jnp.searchsorted(pends, jnp.arange(n_blk) * MOE_BLOCK, side='right'),
                        N_EXPERTS - 1)

    def run_block(args):
        tok, w, e = args
        rows = ht[tok]
        hid = jax.nn.silu(rows @ w_gate[e]) * (rows @ w_up[e])
        return ((hid @ w_down[e]) * w[:, None]).astype(jnp.float32)

    out = lax.map(run_block, (slot_tok.reshape(n_blk, MOE_BLOCK),
                              slot_w.reshape(n_blk, MOE_BLOCK), blk_e))
    y = jnp.zeros((n_tok, d), jnp.float32).at[slot_tok].add(out.reshape(cap, d))
    return y.reshape(bsz, t, d)


def setup_inputs(seed: int = 0) -> dict:
    key = jax.random.key(seed)
    ks = jax.random.split(key, 32)

    def nrm(k, shape, scale):
        return jax.random.normal(k, shape, jnp.float32) * scale

    n = jnp.arange(SSM_STATE, dtype=jnp.float32)
    return {
        'x': nrm(ks[0], (BATCH, SEQ, D_MODEL), 1.0),
        'c': nrm(ks[1], (BATCH, D_MODEL), 1.0),
        'positions': jnp.broadcast_to(jnp.arange(SEQ, dtype=jnp.int32)[None], (BATCH, SEQ)),
        'ada_w': nrm(ks[2], (DEPTH, D_MODEL, 6 * D_MODEL), 0.1 * D_MODEL ** -0.5),
        'ada_b': nrm(ks[3], (DEPTH, 6 * D_MODEL), 0.01),
        'w_in': nrm(ks[4], (DEPTH, D_MODEL, PROJ_WIDTH), D_MODEL ** -0.5),
        'gm_ln_g': 1.0 + nrm(ks[5], (DEPTH, A_WIDTH), 0.01),
        'gm_ln_b': nrm(ks[6], (DEPTH, A_WIDTH), 0.01),
        'gm_ws': nrm(ks[7], (DEPTH, A_HEADS, CHUNK, CHUNK), 0.5 * CHUNK ** -0.5),
        'gm_bs': 1.0 + nrm(ks[8], (DEPTH, A_HEADS, CHUNK), 0.01),
        'ssm_lam_re': -0.5 + nrm(ks[9], (DEPTH, SSM_GROUPS, SSM_STATE), 0.01),
        'ssm_lam_im': math.pi * n + nrm(ks[10], (DEPTH, SSM_GROUPS, SSM_STATE), 0.01),
        'ssm_log_dt': jax.random.uniform(ks[11], (DEPTH, SSM_GROUPS), jnp.float32,
                                         math.log(DT_MIN), math.log(DT_MAX)),
        'ssm_b_re': nrm(ks[12], (DEPTH, SSM_GROUPS, SSM_STATE, SSM_GROUP), (2 * SSM_GROUP) ** -0.5),
        'ssm_b_im': nrm(ks[13], (DEPTH, SSM_GROUPS, SSM_STATE, SSM_GROUP), (2 * SSM_GROUP) ** -0.5),
        'ssm_c_re': nrm(ks[14], (DEPTH, SSM_GROUPS, SSM_GROUP, SSM_STATE), SSM_STATE ** -0.5),
        'ssm_c_im': nrm(ks[15], (DEPTH, SSM_GROUPS, SSM_GROUP, SSM_STATE), SSM_STATE ** -0.5),
        'ssm_d': nrm(ks[16], (DEPTH, SSM_GROUPS, SSM_GROUP), 1.0),
        'glu_w': nrm(ks[17], (DEPTH, C_WIDTH, C_WIDTH), C_WIDTH ** -0.5),
        'glu_b': nrm(ks[18], (DEPTH, C_WIDTH), 0.01),
        'w_out': nrm(ks[19], (DEPTH, D_MIX, D_MODEL), BETA * D_MIX ** -0.5),
        'ln1_g': 1.0 + nrm(ks[20], (DEPTH, D_MODEL), 0.01),
        'ln1_b': nrm(ks[21], (DEPTH, D_MODEL), 0.01),
        'router_w': nrm(ks[22], (D_MODEL, N_EXPERTS), D_MODEL ** -0.5),
        'router_bias': nrm(ks[23], (N_EXPERTS,), 0.01),
        'exp_w_gate': nrm(ks[24], (DEPTH, N_EXPERTS, D_MODEL, D_EXPERT), D_MODEL ** -0.5),
        'exp_w_up': nrm(ks[25], (DEPTH, N_EXPERTS, D_MODEL, D_EXPERT), D_MODEL ** -0.5),
        'exp_w_down': nrm(ks[26], (DEPTH, N_EXPERTS, D_EXPERT, D_MODEL), BETA * D_EXPERT ** -0.5),
        'ln2_g': 1.0 + nrm(ks[27], (DEPTH, D_MODEL), 0.01),
        'ln2_b': nrm(ks[28], (DEPTH, D_MODEL), 0.01),
    }


def reference(x, c, positions, ada_w, ada_b, w_in, gm_ln_g, gm_ln_b, gm_ws, gm_bs,
              ssm_lam_re, ssm_lam_im, ssm_log_dt, ssm_b_re, ssm_b_im, ssm_c_re, ssm_c_im,
              ssm_d, glu_w, glu_b, w_out, ln1_g, ln1_b, router_w, router_bias,
              exp_w_gate, exp_w_up, exp_w_down, ln2_g, ln2_b):
    bsz, t, _ = x.shape
    cond = jax.nn.silu(c.astype(jnp.float32))
    for l in range(DEPTH):
        mod = cond @ ada_w[l] + ada_b[l]
        sh1, sc1, g1, sh2, sc2, g2 = [m[:, None, :] for m in jnp.split(mod, 6, axis=-1)]

        h = layer_norm(x) * (1.0 + sc1) + sh1
        proj = h @ w_in[l]
        a_uv, qkv, s_in = jnp.split(proj, [2 * A_WIDTH, 2 * A_WIDTH + 3 * B_WIDTH], axis=-1)
        a_out = chunked_spatial_gating(a_uv, gm_ln_g[l], gm_ln_b[l], gm_ws[l], gm_bs[l])
        qkv = qkv.reshape(bsz, t, 3, B_HEADS, B_HEAD_DIM)
        q = partial_rope(qkv[:, :, 0], positions)
        k = partial_rope(qkv[:, :, 1], positions)
        b_out = dilated_attention(q, k, qkv[:, :, 2]).reshape(bsz, t, B_WIDTH)
        c_out = s5_mixer(s_in, ssm_lam_re[l], ssm_lam_im[l], ssm_log_dt[l], ssm_b_re[l],
                         ssm_b_im[l], ssm_c_re[l], ssm_c_im[l], ssm_d[l], glu_w[l], glu_b[l])
        mix = jnp.concatenate([a_out, b_out, c_out], axis=-1) @ w_out[l]
        x = layer_norm(ALPHA * x + (1.0 + g1) * mix) * ln1_g[l] + ln1_b[l]

        h2 = layer_norm(x) * (1.0 + sc2) + sh2
        ffn = grouped_moe(h2, router_w, router_bias, exp_w_gate[l], exp_w_up[l], exp_w_down[l])
        x = layer_norm(ALPHA * x + (1.0 + g2) * ffn) * ln2_g[l] + ln2_b[l]
    return x
```

```python
import functools
import math

import jax
import jax.numpy as jnp
from jax import lax
from jax.experimental import pallas as pl
from jax.experimental.pallas import tpu as pltpu

F32 = jnp.float32
BF16 = jnp.bfloat16

A_HEADS = 4
A_HEAD_DIM = 64
A_WIDTH = A_HEADS * A_HEAD_DIM
CHUNK = 128
B_HEADS = 8
B_HEAD_DIM = 64
B_WIDTH = B_HEADS * B_HEAD_DIM
DILATIONS = (1, 4, 16)
ATT_BLK = 128
ROT_DIM = B_HEAD_DIM // 4
ROPE_THETA = 500000.0
SSM_GROUP = 16
SSM_STATE = 64
N_EXPERTS = 16
N_EXPERT_GROUPS = 4
EXPERTS_PER_GROUP = 4
TOP_K = 2
LN_EPS = 1e-5
NEG_INF = -1e30

LANES = 128
SUBLANES = 8
VMEM_LIMIT = 56 * 1024 * 1024
ROW_TILE = 512
SSM_STEPS = 64
MOE_BLK = 512


def _cparams(sem):
    return pltpu.CompilerParams(dimension_semantics=sem, vmem_limit_bytes=VMEM_LIMIT)


def _ln(x):
    mu = jnp.mean(x, axis=-1, keepdims=True)
    xc = x - mu
    var = jnp.mean(xc * xc, axis=-1, keepdims=True)
    return xc * lax.rsqrt(var + LN_EPS)


def _gelu_tanh(x):
    return 0.5 * x * (1.0 + jnp.tanh(math.sqrt(2.0 / math.pi) * (x + 0.044715 * (x * x * x))))


def _mod_kernel(c_ref, w_ref, b_ref, o_ref):
    c = c_ref[...]
    cond = c * jax.nn.sigmoid(c)
    o_ref[0] = jnp.dot(cond, w_ref[0], precision=lax.Precision.HIGHEST,
                       preferred_element_type=F32) + b_ref[0]


def _adaln_mod(c, ada_w, ada_b):
    depth, d, n = ada_w.shape
    bsz = c.shape[0]
    nb = d
    return pl.pallas_call(
        _mod_kernel,
        grid=(depth, n // nb),
        in_specs=[pl.BlockSpec((bsz, d), lambda l, j: (0, 0)),
                  pl.BlockSpec((1, d, nb), lambda l, j: (l, 0, j)),
                  pl.BlockSpec((1, 1, nb), lambda l, j: (l, 0, j))],
        out_specs=pl.BlockSpec((1, bsz, nb), lambda l, j: (l, 0, j)),
        out_shape=jax.ShapeDtypeStruct((depth, bsz, n), F32),
        compiler_params=_cparams(("arbitrary", "arbitrary")),
        name="adaln_mod",
    )(c, ada_w, ada_b.reshape(depth, 1, n))


def _rope_kernel(pos_ref, freq_ref, cos_ref, s1_ref, s2_ref):
    ang = pos_ref[...].astype(F32) * freq_ref[...]
    lane = lax.broadcasted_iota(jnp.int32, ang.shape, 1) % B_HEAD_DIM
    sn = jnp.sin(ang)
    cos_ref[...] = jnp.cos(ang)
    s1_ref[...] = jnp.where(lane < ROT_DIM // 2, -sn, 0.0)
    s2_ref[...] = jnp.where((lane >= ROT_DIM // 2) & (lane < ROT_DIM), sn, 0.0)


def _rope_tables(positions):
    n = positions.size
    half = ROT_DIM // 2
    lane = jnp.arange(LANES) % B_HEAD_DIM
    freqs = ROPE_THETA ** (-(lane % half).astype(F32) * 2.0 / ROT_DIM)
    freq_row = jnp.where(lane < ROT_DIM, freqs, 0.0).reshape(1, LANES).astype(F32)
    tm = min(2048, n)
    out = jax.ShapeDtypeStruct((n, LANES), F32)
    spec = pl.BlockSpec((tm, LANES), lambda i: (i, 0))
    return pl.pallas_call(
        _rope_kernel,
        grid=(n // tm,),
        in_specs=[pl.BlockSpec((tm, 1), lambda i: (i, 0)),
                  pl.BlockSpec((1, LANES), lambda i: (0, 0))],
        out_specs=[spec, spec, spec],
        out_shape=[out, out, out],
        compiler_params=_cparams(("arbitrary",)),
        name="rope_tables",
    )(positions.reshape(n, 1), freq_row)


def _inproj_kernel(x_ref, sh_ref, sc_ref, w_ref, cos_ref, s1_ref, s2_ref, lng_ref, lnb_ref,
                   ws_ref, bs_ref, a_ref, q_ref, k_ref, v_ref, s_ref):
    tm = x_ref.shape[0]
    h = _ln(x_ref[...]) * (1.0 + sc_ref[0]) + sh_ref[0]
    proj = jnp.dot(h.astype(BF16), w_ref[...], preferred_element_type=F32)

    uv = _gelu_tanh(proj[:, :2 * A_WIDTH])
    u = uv[:, :A_WIDTH]
    v = (_ln(uv[:, A_WIDTH:]) * lng_ref[...] + lnb_ref[...]).astype(BF16)
    row = lax.broadcasted_iota(jnp.int32, (CHUNK, CHUNK), 0)
    col = lax.broadcasted_iota(jnp.int32, (CHUNK, CHUNK), 1)
    head_of_lane = lax.broadcasted_iota(jnp.int32, (CHUNK, A_WIDTH), 1) // A_HEAD_DIM
    w_heads = [jnp.where(col <= row, ws_ref[hd], 0.0).astype(BF16) for hd in range(A_HEADS)]
    for cidx in range(tm // CHUNK):
        rows = slice(cidx * CHUNK, (cidx + 1) * CHUNK)
        vc = v[rows]
        sv = bs_ref[...]
        for hd in range(A_HEADS):
            full = jnp.dot(w_heads[hd], vc, preferred_element_type=F32)
            sv = sv + jnp.where(head_of_lane == hd, full, 0.0)
        a_ref[rows, :] = (u[rows] * sv).astype(a_ref.dtype)

    cos = cos_ref[...]
    s1 = s1_ref[...]
    s2 = s2_ref[...]
    q0 = 2 * A_WIDTH
    k0 = q0 + B_WIDTH
    v0 = k0 + B_WIDTH
    for j in range(B_WIDTH // LANES):
        for base, ref, scale in ((q0, q_ref, B_HEAD_DIM ** -0.5), (k0, k_ref, 1.0)):
            xs = proj[:, base + j * LANES: base + (j + 1) * LANES]
            rot = (xs * cos + pltpu.roll(xs, LANES - ROT_DIM // 2, 1) * s1
                   + pltpu.roll(xs, ROT_DIM // 2, 1) * s2)
            ref[:, j * LANES:(j + 1) * LANES] = (rot * scale).astype(ref.dtype)
    v_ref[...] = proj[:, v0:v0 + B_WIDTH].astype(v_ref.dtype)

    s_ref[...] = proj[:, v0 + B_WIDTH:].astype(s_ref.dtype)


def _inproj(x2d, mod3, w_in_bf, cos_t, s1_t, s2_t, ln_g, ln_b, ws, bs_full, bsz, t):
    n, d = x2d.shape
    tm = min(ROW_TILE, t)
    nt = t // tm
    pw = w_in_bf.shape[1]
    c_width = pw - 2 * A_WIDTH - 3 * B_WIDTH
    tok = lambda b, i: (b * nt + i, 0)
    const2 = lambda b, i: (0, 0)
    outs = [jax.ShapeDtypeStruct((n, A_WIDTH), BF16),
            jax.ShapeDtypeStruct((n, B_WIDTH), BF16),
            jax.ShapeDtypeStruct((n, B_WIDTH), BF16),
            jax.ShapeDtypeStruct((n, B_WIDTH), BF16),
            jax.ShapeDtypeStruct((t, bsz * c_width), BF16)]
    return pl.pallas_call(
        _inproj_kernel,
        grid=(bsz, nt),
        in_specs=[pl.BlockSpec((tm, d), tok),
                  pl.BlockSpec((1, 1, d), lambda b, i: (b, 0, 0)),
                  pl.BlockSpec((1, 1, d), lambda b, i: (b, 0, 1)),
                  pl.BlockSpec((d, pw), const2),
                  pl.BlockSpec((tm, LANES), tok),
                  pl.BlockSpec((tm, LANES), tok),
                  pl.BlockSpec((tm, LANES), tok),
                  pl.BlockSpec((1, A_WIDTH), const2),
                  pl.BlockSpec((1, A_WIDTH), const2),
                  pl.BlockSpec((A_HEADS, CHUNK, CHUNK), lambda b, i: (0, 0, 0)),
                  pl.BlockSpec((CHUNK, A_WIDTH), const2)],
        out_specs=[pl.BlockSpec((tm, A_WIDTH), tok),
                   pl.BlockSpec((tm, B_WIDTH), tok),
                   pl.BlockSpec((tm, B_WIDTH), tok),
                   pl.BlockSpec((tm, B_WIDTH), tok),
                   pl.BlockSpec((tm, c_width), lambda b, i: (i, b))],
        out_shape=outs,
        compiler_params=_cparams(("arbitrary", "arbitrary")),
        name="inproj_gmlp_rope",
    )(x2d, mod3, mod3, w_in_bf, cos_t, s1_t, s2_t, ln_g, ln_b, ws, bs_full)


def _att_kernel(q_ref, kp_ref, kc_ref, vp_ref, vc_ref, o_ref, m_ref, d_ref):
    i = pl.program_id(2)
    blk = q_ref.shape[1]
    qi = lax.broadcasted_iota(jnp.int32, (blk, blk), 0)
    kk = lax.broadcasted_iota(jnp.int32, (blk, blk), 1)
    mask_prev = (kk >= qi) & (i > 0)
    mask_cur = kk <= qi
    lane = lax.broadcasted_iota(jnp.int32, (blk, LANES), 1)
    m_all = jnp.zeros((blk, LANES), F32)
    d_all = jnp.zeros((blk, LANES), F32)
    nt = (((1,), (1,)), ((), ()))
    for hd in range(B_HEADS):
        cols = slice(hd * B_HEAD_DIM, (hd + 1) * B_HEAD_DIM)
        qh = q_ref[0, :, cols]
        sp = lax.dot_general(qh, kp_ref[0, :, cols], nt, preferred_element_type=F32)
        sc = lax.dot_general(qh, kc_ref[0, :, cols], nt, preferred_element_type=F32)
        sp = jnp.where(mask_prev, sp, NEG_INF)
        sc = jnp.where(mask_cur, sc, NEG_INF)
        m = jnp.maximum(jnp.max(sp, axis=1, keepdims=True), jnp.max(sc, axis=1, keepdims=True))
        pp = jnp.exp(sp - m)
        pc = jnp.exp(sc - m)
        den = jnp.sum(pp, axis=1, keepdims=True) + jnp.sum(pc, axis=1, keepdims=True)
        o = (jnp.dot(pp.astype(BF16), vp_ref[0, :, cols], preferred_element_type=F32)
             + jnp.dot(pc.astype(BF16), vc_ref[0, :, cols], preferred_element_type=F32))
        o_ref[0, :, cols] = (o / den).astype(o_ref.dtype)
        m_all = jnp.where(lane == hd, m, m_all)
        d_all = jnp.where(lane == hd, den, d_all)
    m_ref[0] = m_all
    d_ref[0] = d_all


def _att_branch(q, k, v, bsz, t, dil):
    rows = t // dil
    nq = rows // ATT_BLK
    qv = q.reshape(bsz, rows, dil * B_WIDTH)
    kv = k.reshape(bsz, rows, dil * B_WIDTH)
    vv = v.reshape(bsz, rows, dil * B_WIDTH)
    cur = lambda b, r, i: (b, i, r)
    prev = lambda b, r, i: (b, jnp.maximum(i - 1, 0), r)
    blk = (1, ATT_BLK, B_WIDTH)
    sblk = (1, ATT_BLK, LANES)
    o, m, den = pl.pallas_call(
        _att_kernel,
        grid=(bsz, dil, nq),
        in_specs=[pl.BlockSpec(blk, cur), pl.BlockSpec(blk, prev), pl.BlockSpec(blk, cur),
                  pl.BlockSpec(blk, prev), pl.BlockSpec(blk, cur)],
        out_specs=[pl.BlockSpec(blk, cur), pl.BlockSpec(sblk, cur), pl.BlockSpec(sblk, cur)],
        out_shape=[jax.ShapeDtypeStruct((bsz, rows, dil * B_WIDTH), BF16),
                   jax.ShapeDtypeStruct((bsz, rows, dil * LANES), F32),
                   jax.ShapeDtypeStruct((bsz, rows, dil * LANES), F32)],
        compiler_params=_cparams(("arbitrary", "arbitrary", "arbitrary")),
        name=f"dilated_attn_d{dil}",
    )(qv, kv, kv, vv, vv)
    n = bsz * t
    return o.reshape(n, B_WIDTH), m.reshape(n, LANES), den.reshape(n, LANES)


def _ssm_kernel(u_ref, bmat_ref, cmat_ref, are_ref, aim_ref, dskip_ref, gw_ref, gb_ref,
                o_ref, state_ref, xbuf_ref):
    bsz, two_n = state_ref.shape
    n_state = two_n // 2
    steps = u_ref.shape[0] // bsz

    @pl.when(pl.program_id(0) == 0)
    def _():
        state_ref[...] = jnp.zeros_like(state_ref)

    u = u_ref[...]
    xbuf_ref[...] = jnp.dot(u, bmat_ref[...], preferred_element_type=F32)
    a_re = jnp.broadcast_to(are_ref[...], (bsz, n_state))
    a_im = jnp.broadcast_to(aim_ref[...], (bsz, n_state))

    def step(tt, carry):
        x_re, x_im = carry
        r0 = pl.multiple_of(tt * bsz, bsz)
        b_re = xbuf_ref[pl.ds(r0, bsz), :n_state]
        b_im = xbuf_ref[pl.ds(r0, bsz), n_state:]
        n_re = a_re * x_re - a_im * x_im + b_re
        n_im = a_re * x_im + a_im * x_re + b_im
        xbuf_ref[pl.ds(r0, bsz), :n_state] = n_re
        xbuf_ref[pl.ds(r0, bsz), n_state:] = n_im
        return n_re, n_im

    x_re, x_im = lax.fori_loop(0, steps, step, (state_ref[:, :n_state], state_ref[:, n_state:]))
    state_ref[:, :n_state] = x_re
    state_ref[:, n_state:] = x_im

    y = jnp.dot(xbuf_ref[...].astype(BF16), cmat_ref[...], preferred_element_type=F32)
    y = _gelu_tanh(y + dskip_ref[...] * u.astype(F32))
    gate = jax.nn.sigmoid(jnp.dot(y.astype(BF16), gw_ref[...], preferred_element_type=F32) + gb_ref[...])
    o_ref[...] = (y * gate).astype(o_ref.dtype)


def _ssm_weights(lam_re, lam_im, log_dt, b_re, b_im, c_re, c_im):
    g, n_st = lam_re.shape
    cg = b_re.shape[-1]
    dt = jnp.exp(log_dt)[:, None]
    mag = jnp.exp(lam_re * dt)
    ab_re = mag * jnp.cos(lam_im * dt)
    ab_im = mag * jnp.sin(lam_im * dt)
    nr = ab_re - 1.0
    ni = ab_im
    mod2 = lam_re * lam_re + lam_im * lam_im
    f_re = (nr * lam_re + ni * lam_im) / mod2
    f_im = (ni * lam_re - nr * lam_im) / mod2
    bb_re = f_re[..., None] * b_re - f_im[..., None] * b_im
    bb_im = f_re[..., None] * b_im + f_im[..., None] * b_re
    eye = jnp.eye(g, dtype=F32)
    bm_re = jnp.einsum('gnc,gh->gchn', bb_re, eye).reshape(g * cg, g * n_st)
    bm_im = jnp.einsum('gnc,gh->gchn', bb_im, eye).reshape(g * cg, g * n_st)
    bmat = jnp.concatenate([bm_re, bm_im], axis=1)
    cm_re = jnp.einsum('gcn,gh->gnhc', c_re, eye).reshape(g * n_st, g * cg)
    cm_im = jnp.einsum('gcn,gh->gnhc', c_im, eye).reshape(g * n_st, g * cg)
    cmat = jnp.concatenate([cm_re, -cm_im], axis=0)
    return (bmat.astype(BF16), cmat.astype(BF16),
            ab_re.reshape(1, g * n_st), ab_im.reshape(1, g * n_st))


def _ssm(u_tm, bmat, cmat, a_re, a_im, d_skip, glu_w_bf, glu_b, bsz, t):
    cw = bmat.shape[0]
    two_n = bmat.shape[1]
    steps = min(SSM_STEPS, t)
    rows = steps * bsz
    u2 = u_tm.reshape(t * bsz, cw)
    const = lambda i: (0, 0)
    return pl.pallas_call(
        _ssm_kernel,
        grid=(t // steps,),
        in_specs=[pl.BlockSpec((rows, cw), lambda i: (i, 0)),
                  pl.BlockSpec((cw, two_n), const),
                  pl.BlockSpec((two_n, cw), const),
                  pl.BlockSpec((1, two_n // 2), const),
                  pl.BlockSpec((1, two_n // 2), const),
                  pl.BlockSpec((1, cw), const),
                  pl.BlockSpec((cw, cw), const),
                  pl.BlockSpec((1, cw), const)],
        out_specs=pl.BlockSpec((rows, cw), lambda i: (i, 0)),
        out_shape=jax.ShapeDtypeStruct((t * bsz, cw), BF16),
        scratch_shapes=[pltpu.VMEM((bsz, two_n), F32), pltpu.VMEM((rows, two_n), F32)],
        compiler_params=_cparams(("arbitrary",)),
        name="s5_scan_glu",
    )(u2, bmat, cmat, a_re, a_im, d_skip, glu_w_bf, glu_b).reshape(t, bsz * cw)


def _route(sel, scores):
    gs = []
    for g in range(N_EXPERT_GROUPS):
        a, b, c, d = sel[4 * g: 4 * g + 4]
        hi1, lo1 = jnp.maximum(a, b), jnp.minimum(a, b)
        hi2, lo2 = jnp.maximum(c, d), jnp.minimum(c, d)
        gs.append(jnp.maximum(hi1, hi2) + jnp.maximum(jnp.minimum(hi1, hi2), jnp.maximum(lo1, lo2)))
    g_idx = jnp.zeros(gs[0].shape, jnp.int32)
    best = gs[0]
    for g in range(1, N_EXPERT_GROUPS):
        better = gs[g] > best
        g_idx = jnp.where(better, g, g_idx)
        best = jnp.where(better, gs[g], best)

    def pick(rows, j):
        out = rows[j]
        for g in range(1, N_EXPERT_GROUPS):
            out = jnp.where(g_idx == g, rows[4 * g + j], out)
        return out

    v = [pick(sel, j) for j in range(EXPERTS_PER_GROUP)]
    s = [pick(scores, j) for j in range(EXPERTS_PER_GROUP)]
    i1 = jnp.zeros(g_idx.shape, jnp.int32)
    b1, g1 = v[0], s[0]
    for j in range(1, EXPERTS_PER_GROUP):
        better = v[j] > b1
        i1 = jnp.where(better, j, i1)
        b1 = jnp.where(better, v[j], b1)
        g1 = jnp.where(better, s[j], g1)
    i2 = jnp.full(g_idx.shape, -1, jnp.int32)
    b2 = jnp.zeros_like(b1)
    g2 = jnp.zeros_like(g1)
    for j in range(EXPERTS_PER_GROUP):
        better = (i1 != j) & ((i2 < 0) | (v[j] > b2))
        i2 = jnp.where(better, j, i2)
        b2 = jnp.where(better, v[j], b2)
        g2 = jnp.where(better, s[j], g2)
    tot = g1 + g2
    base = g_idx * EXPERTS_PER_GROUP
    return base + i1, base + i2, g1 / tot, g2 / tot


def _outproj_kernel(a_ref, o1_ref, o2_ref, o3_ref, m1_ref, m2_ref, m3_ref, d1_ref, d2_ref, d3_ref,
                    c_ref, x_ref, g1_ref, sh2_ref, sc2_ref, wout_ref, expand_ref, lng_ref, lnb_ref,
                    rwh_ref, rwl_ref, rb_ref, x1_ref, h2_ref, eidx_ref, gate_ref, *, alpha):
    m1, m2, m3 = m1_ref[...], m2_ref[...], m3_ref[...]
    mx = jnp.maximum(jnp.maximum(m1, m2), m3)
    w1 = d1_ref[...] * jnp.exp(m1 - mx)
    w2 = d2_ref[...] * jnp.exp(m2 - mx)
    w3 = d3_ref[...] * jnp.exp(m3 - mx)
    lane = lax.broadcasted_iota(jnp.int32, w1.shape, 1)
    tot = jnp.where(lane < B_HEADS, w1 + w2 + w3, 1.0)
    expand = expand_ref[...]

    def widen(w):
        wn = w / tot
        hi = wn.astype(BF16)
        lo = (wn - hi.astype(F32)).astype(BF16)
        return (jnp.dot(hi, expand, preferred_element_type=F32)
                + jnp.dot(lo, expand, preferred_element_type=F32))

    b_out = (widen(w1) * o1_ref[...].astype(F32) + widen(w2) * o2_ref[...].astype(F32)
             + widen(w3) * o3_ref[...].astype(F32)).astype(BF16)

    wout = wout_ref
    mix = (jnp.dot(a_ref[...], wout[:A_WIDTH, :], preferred_element_type=F32)
           + jnp.dot(b_out, wout[A_WIDTH:A_WIDTH + B_WIDTH, :], preferred_element_type=F32)
           + jnp.dot(c_ref[...], wout[A_WIDTH + B_WIDTH:, :], preferred_element_type=F32))
    x1 = _ln(alpha * x_ref[...] + (1.0 + g1_ref[0]) * mix) * lng_ref[...] + lnb_ref[...]
    x1_ref[...] = x1
    h2 = _ln(x1) * (1.0 + sc2_ref[0]) + sh2_ref[0]
    h2_ref[...] = h2

    hi = h2.astype(BF16)
    lo = (h2 - hi.astype(F32)).astype(BF16)
    nt = (((1,), (1,)), ((), ()))
    logits = (lax.dot_general(rwh_ref[...], hi, nt, preferred_element_type=F32)
              + lax.dot_general(rwh_ref[...], lo, nt, preferred_element_type=F32)
              + lax.dot_general(rwl_ref[...], hi, nt, preferred_element_type=F32))
    scores = jax.nn.sigmoid(logits)
    sel = scores + rb_ref[...]
    sel_rows = [sel[e:e + 1, :] for e in range(N_EXPERTS)]
    score_rows = [scores[e:e + 1, :] for e in range(N_EXPERTS)]
    e1, e2, gt1, gt2 = _route(sel_rows, score_rows)
    eidx_ref[...] = jnp.zeros_like(eidx_ref)
    gate_ref[...] = jnp.zeros_like(gate_ref)
    eidx_ref[0:1, :] = e1
    eidx_ref[1:2, :] = e2
    gate_ref[0:1, :] = gt1
    gate_ref[1:2, :] = gt2


def _outproj(a_out, att, c_tm, x2d, mod3, w_out_bf, ln_g, ln_b, rw_hi, rw_lo, rbias, bsz, t, alpha):
    n, d = x2d.shape
    tm = min(ROW_TILE, t)
    nt = t // tm
    cw = c_tm.shape[1] // bsz
    tok = lambda b, i: (b * nt + i, 0)
    const2 = lambda b, i: (0, 0)
    (o1, m1, d1), (o2, m2, d2), (o3, m3, d3) = att
    head = jnp.arange(B_WIDTH) // B_HEAD_DIM
    expand = (jnp.arange(LANES)[:, None] == head[None, :]).astype(BF16)
    wide = pl.BlockSpec((tm, B_WIDTH), tok)
    stat = pl.BlockSpec((tm, LANES), tok)
    modspec = lambda j: pl.BlockSpec((1, 1, d), lambda b, i: (b, 0, j))
    return pl.pallas_call(
        functools.partial(_outproj_kernel, alpha=alpha),
        grid=(bsz, nt),
        in_specs=[pl.BlockSpec((tm, A_WIDTH), tok), wide, wide, wide,
                  stat, stat, stat, stat, stat, stat,
                  pl.BlockSpec((tm, cw), lambda b, i: (i, b)),
                  pl.BlockSpec((tm, d), tok),
                  modspec(2), modspec(3), modspec(4),
                  pl.BlockSpec((d, d), const2),
                  pl.BlockSpec((LANES, B_WIDTH), const2),
                  pl.BlockSpec((1, d), const2), pl.BlockSpec((1, d), const2),
                  pl.BlockSpec((N_EXPERTS, d), const2), pl.BlockSpec((N_EXPERTS, d), const2),
                  pl.BlockSpec((N_EXPERTS, 1), const2)],
        out_specs=[pl.BlockSpec((tm, d), tok), pl.BlockSpec((tm, d), tok),
                   pl.BlockSpec((SUBLANES, tm), lambda b, i: (0, b * nt + i)),
                   pl.BlockSpec((SUBLANES, tm), lambda b, i: (0, b * nt + i))],
        out_shape=[jax.ShapeDtypeStruct((n, d), F32), jax.ShapeDtypeStruct((n, d), F32),
                   jax.ShapeDtypeStruct((SUBLANES, n), jnp.int32),
                   jax.ShapeDtypeStruct((SUBLANES, n), F32)],
        compiler_params=_cparams(("arbitrary", "arbitrary")),
        name="outproj_ln_router",
    )(a_out, o1, o2, o3, m1, m2, m3, d1, d2, d3, c_tm, x2d, mod3, mod3, mod3, w_out_bf, expand,
      ln_g, ln_b, rw_hi, rw_lo, rbias)


def _moe_kernel(blk_e_ref, nused_ref, tok_ref, dst_ref, gate_ref, h2_hbm, wg_ref, wu_ref, wd_ref,
                y_hbm, xbuf, obuf, gsem, ssem):
    j = pl.program_id(0)
    blk = xbuf.shape[0]

    @pl.when(j < nused_ref[0])
    def _():
        def gather(s, c):
            pltpu.make_async_copy(h2_hbm.at[pl.ds(tok_ref[0, 0, s], 1), :],
                                  xbuf.at[pl.ds(s, 1), :], gsem).start()
            return c
        lax.fori_loop(0, blk, gather, 0)

        def gwait(s, c):
            pltpu.make_async_copy(h2_hbm.at[pl.ds(0, 1), :], xbuf.at[pl.ds(s, 1), :], gsem).wait()
            return c
        lax.fori_loop(0, blk, gwait, 0)

        rows = xbuf[...].astype(BF16)
        gt = jnp.dot(rows, wg_ref[0], preferred_element_type=F32)
        up = jnp.dot(rows, wu_ref[0], preferred_element_type=F32)
        hid = (gt * jax.nn.sigmoid(gt) * up).astype(BF16)
        obuf[...] = jnp.dot(hid, wd_ref[0], preferred_element_type=F32) * gate_ref[...]

        def scatter(s, c):
            dst = dst_ref[0, 0, s]

            @pl.when(dst >= 0)
            def _():
                pltpu.make_async_copy(obuf.at[pl.ds(s, 1), :], y_hbm.at[pl.ds(dst, 1), :], ssem).start()
            return c
        lax.fori_loop(0, blk, scatter, 0)

        def swait(s, c):
            @pl.when(dst_ref[0, 0, s] >= 0)
            def _():
                pltpu.make_async_copy(obuf.at[pl.ds(s, 1), :], y_hbm.at[pl.ds(0, 1), :], ssem).wait()
            return c
        lax.fori_loop(0, blk, swait, 0)


def _dispatch_plan(eidx, gates, n_tok, blk):
    n_asg = n_tok * TOP_K
    flat_e = eidx[:TOP_K].T.reshape(n_asg)
    flat_w = gates[:TOP_K].T.reshape(n_asg)
    onehot = (flat_e[:, None] == jnp.arange(N_EXPERTS)[None, :]).astype(jnp.int32)
    csum = jnp.cumsum(onehot, axis=0)
    counts = csum[-1]
    rank = jnp.take_along_axis(csum, flat_e[:, None], axis=1)[:, 0] - 1
    pcounts = (counts + blk - 1) // blk * blk
    pends = jnp.cumsum(pcounts)
    pstarts = pends - pcounts
    dest = pstarts[flat_e] + rank
    cap = n_asg + N_EXPERTS * blk
    n_blk = cap // blk
    slot_asg = jnp.full((cap,), -1, jnp.int32).at[dest].set(jnp.arange(n_asg, dtype=jnp.int32))
    valid = slot_asg >= 0
    slot_tok = jnp.where(valid, slot_asg // TOP_K, 0)
    slot_dst = slot_asg
    slot_w = jnp.zeros((cap,), F32).at[dest].set(flat_w)
    blk_start = jnp.arange(n_blk, dtype=jnp.int32) * blk
    blk_e = jnp.minimum(jnp.sum((pends[None, :] <= blk_start[:, None]).astype(jnp.int32), axis=1),
                        N_EXPERTS - 1).astype(jnp.int32)
    n_used = (pends[-1] // blk).astype(jnp.int32).reshape(1)
    return (blk_e, n_used, slot_tok.reshape(n_blk, 1, blk), slot_dst.reshape(n_blk, 1, blk),
            slot_w.reshape(cap, 1))


def _moe(h2, eidx, gates, wg_bf, wu_bf, wd_bf):
    n_tok, d = h2.shape
    de = wg_bf.shape[-1]
    blk = min(MOE_BLK, n_tok)
    blk_e, n_used, slot_tok, slot_dst, slot_w = _dispatch_plan(eidx, gates, n_tok, blk)
    n_blk = slot_tok.shape[0]
    smem_idx = pl.BlockSpec((1, 1, blk), lambda j, be, nu: (j, 0, 0), memory_space=pltpu.SMEM)
    grid_spec = pltpu.PrefetchScalarGridSpec(
        num_scalar_prefetch=2,
        grid=(n_blk,),
        in_specs=[smem_idx, smem_idx,
                  pl.BlockSpec((blk, 1), lambda j, be, nu: (j, 0)),
                  pl.BlockSpec(memory_space=pl.ANY),
                  pl.BlockSpec((1, d, de), lambda j, be, nu: (be[j], 0, 0)),
                  pl.BlockSpec((1, d, de), lambda j, be, nu: (be[j], 0, 0)),
                  pl.BlockSpec((1, de, d), lambda j, be, nu: (be[j], 0, 0))],
        out_specs=pl.BlockSpec(memory_space=pl.ANY),
        scratch_shapes=[pltpu.VMEM((blk, d), F32), pltpu.VMEM((blk, d), F32),
                        pltpu.SemaphoreType.DMA, pltpu.SemaphoreType.DMA],
    )
    return pl.pallas_call(
        _moe_kernel,
        grid_spec=grid_spec,
        out_shape=jax.ShapeDtypeStruct((n_tok * TOP_K, d), F32),
        compiler_params=_cparams(("arbitrary",)),
        name="moe_experts",
    )(blk_e, n_used, slot_tok, slot_dst, slot_w, h2, wg_bf, wu_bf, wd_bf)


def _ffn_ln_kernel(x_ref, y_ref, g2_ref, lng_ref, lnb_ref, o_ref, *, alpha):
    d = x_ref.shape[1]
    y = y_ref[...]
    ffn = y[:, :d]
    for kk in range(1, TOP_K):
        ffn = ffn + y[:, kk * d:(kk + 1) * d]
    o_ref[...] = _ln(alpha * x_ref[...] + (1.0 + g2_ref[0]) * ffn) * lng_ref[...] + lnb_ref[...]


def _ffn_ln(x1, y2, mod3, ln_g, ln_b, bsz, t, alpha):
    n, d = x1.shape
    tm = min(ROW_TILE, t)
    nt = t // tm
    tok = lambda b, i: (b * nt + i, 0)
    const2 = lambda b, i: (0, 0)
    yv = y2.reshape(n, TOP_K * d)
    return pl.pallas_call(
        functools.partial(_ffn_ln_kernel, alpha=alpha),
        grid=(bsz, nt),
        in_specs=[pl.BlockSpec((tm, d), tok), pl.BlockSpec((tm, TOP_K * d), tok),
                  pl.BlockSpec((1, 1, d), lambda b, i: (b, 0, 5)),
                  pl.BlockSpec((1, d), const2), pl.BlockSpec((1, d), const2)],
        out_specs=pl.BlockSpec((tm, d), tok),
        out_shape=jax.ShapeDtypeStruct((n, d), F32),
        compiler_params=_cparams(("arbitrary", "arbitrary")),
        name="ffn_sum_ln",
    )(x1, yv, mod3, ln_g, ln_b)


def kernel(x, c, positions, ada_w, ada_b, w_in, gm_ln_g, gm_ln_b, gm_ws, gm_bs, ssm_lam_re, ssm_lam_im, ssm_log_dt, ssm_b_re, ssm_b_im, ssm_c_re, ssm_c_im, ssm_d, glu_w, glu_b, w_out, ln1_g, ln1_b, router_w, router_bias, exp_w_gate, exp_w_up, exp_w_down, ln2_g, ln2_b):
    bsz, t, d = x.shape
    depth = ada_w.shape[0]
    alpha = (2.0 * depth) ** 0.25
    n = bsz * t

    mod = _adaln_mod(c.astype(F32), ada_w, ada_b)
    cos_t, s1_t, s2_t = _rope_tables(positions)
    rw_t = router_w.astype(F32).T
    rw_hi = rw_t.astype(BF16)
    rw_lo = (rw_t - rw_hi.astype(F32)).astype(BF16)
    rbias = router_bias.astype(F32).reshape(N_EXPERTS, 1)

    xf = x.astype(F32).reshape(n, d)
    for l in range(depth):
        mod3 = mod[l].reshape(bsz, 1, 6 * d)
        bs_full = jnp.repeat(gm_bs[l].T, A_HEAD_DIM, axis=1)
        a_out, q, k, v, s_in = _inproj(
            xf, mod3, w_in[l].astype(BF16), cos_t, s1_t, s2_t,
            gm_ln_g[l].reshape(1, A_WIDTH), gm_ln_b[l].reshape(1, A_WIDTH), gm_ws[l], bs_full, bsz, t)
        att = [_att_branch(q, k, v, bsz, t, dil) for dil in DILATIONS]
        bmat, cmat, a_re, a_im = _ssm_weights(ssm_lam_re[l], ssm_lam_im[l], ssm_log_dt[l],
                                              ssm_b_re[l], ssm_b_im[l], ssm_c_re[l], ssm_c_im[l])
        cw = bmat.shape[0]
        c_out = _ssm(s_in, bmat, cmat, a_re, a_im, ssm_d[l].reshape(1, cw),
                     glu_w[l].astype(BF16), glu_b[l].reshape(1, cw), bsz, t)
        x1, h2, eidx, gates = _outproj(a_out, att, c_out, xf, mod3, w_out[l].astype(BF16),
                                       ln1_g[l].reshape(1, d), ln1_b[l].reshape(1, d),
                                       rw_hi, rw_lo, rbias, bsz, t, alpha)
        y2 = _moe(h2, eidx, gates, exp_w_gate[l].astype(BF16), exp_w_up[l].astype(BF16),
                  exp_w_down[l].astype(BF16))
        xf = _ffn_ln(x1, y2, mod3, ln2_g[l].reshape(1, d), ln2_b[l].reshape(1, d), bsz, t, alpha)
    return xf.reshape(bsz, t, d)
```

```python
import functools
import math

import jax
import jax.numpy as jnp
from jax import lax
from jax.experimental import pallas as pl
from jax.experimental.pallas import tpu as pltpu

F32 = jnp.float32
BF16 = jnp.bfloat16

A_HEADS = 4
A_HEAD_DIM = 64
A_WIDTH = A_HEADS * A_HEAD_DIM
CHUNK = 128
B_HEADS = 8
B_HEAD_DIM = 64
B_WIDTH = B_HEADS * B_HEAD_DIM
DILATIONS = (1, 4, 16)
ATT_BLK = 128
ROT_DIM = B_HEAD_DIM // 4
ROPE_THETA = 500000.0
SSM_GROUP = 16
SSM_STATE = 64
N_EXPERTS = 16
N_EXPERT_GROUPS = 4
EXPERTS_PER_GROUP = 4
TOP_K = 2
LN_EPS = 1e-5
NEG_INF = -1e30

LANES = 128
SUBLANES = 8
VMEM_LIMIT = 56 * 1024 * 1024
ROW_TILE = 512
SSM_STEPS = 64
MOE_BLK = 512
DISPATCH_ROWS = 2048


def _cparams(sem):
    return pltpu.CompilerParams(dimension_semantics=sem, vmem_limit_bytes=VMEM_LIMIT)


def _ln(x):
    mu = jnp.mean(x, axis=-1, keepdims=True)
    xc = x - mu
    var = jnp.mean(xc * xc, axis=-1, keepdims=True)
    return xc * lax.rsqrt(var + LN_EPS)


def _gelu_tanh(x):
    return 0.5 * x * (1.0 + jnp.tanh(math.sqrt(2.0 / math.pi) * (x + 0.044715 * (x * x * x))))


def _mod_kernel(c_ref, w_ref, b_ref, o_ref):
    c = c_ref[...]
    cond = c * jax.nn.sigmoid(c)
    o_ref[0] = jnp.dot(cond, w_ref[0], precision=lax.Precision.HIGHEST,
                       preferred_element_type=F32) + b_ref[0]


def _adaln_mod(c, ada_w, ada_b):
    depth, d, n = ada_w.shape
    bsz = c.shape[0]
    nb = d
    return pl.pallas_call(
        _mod_kernel,
        grid=(depth, n // nb),
        in_specs=[pl.BlockSpec((bsz, d), lambda l, j: (0, 0)),
                  pl.BlockSpec((1, d, nb), lambda l, j: (l, 0, j)),
                  pl.BlockSpec((1, 1, nb), lambda l, j: (l, 0, j))],
        out_specs=pl.BlockSpec((1, bsz, nb), lambda l, j: (l, 0, j)),
        out_shape=jax.ShapeDtypeStruct((depth, bsz, n), F32),
        compiler_params=_cparams(("arbitrary", "arbitrary")),
        name="adaln_mod",
    )(c, ada_w, ada_b.reshape(depth, 1, n))


def _rope_kernel(pos_ref, freq_ref, cos_ref, s1_ref, s2_ref):
    ang = pos_ref[...].astype(F32) * freq_ref[...]
    lane = lax.broadcasted_iota(jnp.int32, ang.shape, 1) % B_HEAD_DIM
    sn = jnp.sin(ang)
    cos_ref[...] = jnp.cos(ang)
    s1_ref[...] = jnp.where(lane < ROT_DIM // 2, -sn, 0.0)
    s2_ref[...] = jnp.where((lane >= ROT_DIM // 2) & (lane < ROT_DIM), sn, 0.0)


def _rope_tables(positions):
    n = positions.size
    half = ROT_DIM // 2
    lane = jnp.arange(LANES) % B_HEAD_DIM
    freqs = ROPE_THETA ** (-(lane % half).astype(F32) * 2.0 / ROT_DIM)
    freq_row = jnp.where(lane < ROT_DIM, freqs, 0.0).reshape(1, LANES).astype(F32)
    tm = min(2048, n)
    out = jax.ShapeDtypeStruct((n, LANES), F32)
    spec = pl.BlockSpec((tm, LANES), lambda i: (i, 0))
    return pl.pallas_call(
        _rope_kernel,
        grid=(n // tm,),
        in_specs=[pl.BlockSpec((tm, 1), lambda i: (i, 0)),
                  pl.BlockSpec((1, LANES), lambda i: (0, 0))],
        out_specs=[spec, spec, spec],
        out_shape=[out, out, out],
        compiler_params=_cparams(("arbitrary",)),
        name="rope_tables",
    )(positions.reshape(n, 1), freq_row)


def _inproj_kernel(x_ref, sh_ref, sc_ref, w_ref, cos_ref, s1_ref, s2_ref, lng_ref, lnb_ref,
                   ws_ref, bs_ref, a_ref, s_ref, *rest):
    n_dil = len(DILATIONS)
    q_outs, k_outs, v_outs = rest[:n_dil], rest[n_dil:2 * n_dil], rest[2 * n_dil:3 * n_dil]
    qs_ref, ks_ref, vs_ref = rest[3 * n_dil:]
    tm = x_ref.shape[0]
    h = _ln(x_ref[...]) * (1.0 + sc_ref[0]) + sh_ref[0]
    proj = jnp.dot(h.astype(BF16), w_ref[...], preferred_element_type=F32)

    uv = _gelu_tanh(proj[:, :2 * A_WIDTH])
    u = uv[:, :A_WIDTH]
    v = (_ln(uv[:, A_WIDTH:]) * lng_ref[...] + lnb_ref[...]).astype(BF16)
    row = lax.broadcasted_iota(jnp.int32, (CHUNK, CHUNK), 0)
    col = lax.broadcasted_iota(jnp.int32, (CHUNK, CHUNK), 1)
    head_of_lane = lax.broadcasted_iota(jnp.int32, (CHUNK, A_WIDTH), 1) // A_HEAD_DIM
    w_heads = [jnp.where(col <= row, ws_ref[hd], 0.0).astype(BF16) for hd in range(A_HEADS)]
    for cidx in range(tm // CHUNK):
        rows = slice(cidx * CHUNK, (cidx + 1) * CHUNK)
        vc = v[rows]
        sv = bs_ref[...]
        for hd in range(A_HEADS):
            full = jnp.dot(w_heads[hd], vc, preferred_element_type=F32)
            sv = sv + jnp.where(head_of_lane == hd, full, 0.0)
        a_ref[rows, :] = (u[rows] * sv).astype(a_ref.dtype)

    cos = cos_ref[...]
    s1 = s1_ref[...]
    s2 = s2_ref[...]
    q0 = 2 * A_WIDTH
    k0 = q0 + B_WIDTH
    v0 = k0 + B_WIDTH
    for j in range(B_WIDTH // LANES):
        for base, ref, scale in ((q0, qs_ref, B_HEAD_DIM ** -0.5), (k0, ks_ref, 1.0)):
            xs = proj[:, base + j * LANES: base + (j + 1) * LANES]
            rot = (xs * cos + pltpu.roll(xs, LANES - ROT_DIM // 2, 1) * s1
                   + pltpu.roll(xs, ROT_DIM // 2, 1) * s2)
            ref[j] = rot * scale
        vs_ref[j] = proj[:, v0 + j * LANES:v0 + (j + 1) * LANES]

    for src, outs in ((qs_ref, q_outs), (ks_ref, k_outs), (vs_ref, v_outs)):
        for dil, out in zip(DILATIONS, outs):
            for r in range(dil):
                for j in range(B_WIDTH // LANES):
                    out[0, r, :, j * LANES:(j + 1) * LANES] = (
                        src[j, pl.ds(r, tm // dil, stride=dil), :].astype(out.dtype))

    s_ref[...] = proj[:, v0 + B_WIDTH:].astype(s_ref.dtype)


def _inproj(x2d, mod3, w_in_bf, cos_t, s1_t, s2_t, ln_g, ln_b, ws, bs_full, bsz, t):
    n, d = x2d.shape
    tm = min(ROW_TILE, t)
    nt = t // tm
    pw = w_in_bf.shape[1]
    c_width = pw - 2 * A_WIDTH - 3 * B_WIDTH
    tok = lambda b, i: (b * nt + i, 0)
    const2 = lambda b, i: (0, 0)
    outs = [jax.ShapeDtypeStruct((n, A_WIDTH), BF16),
            jax.ShapeDtypeStruct((t, bsz * c_width), BF16)]
    out_specs = [pl.BlockSpec((tm, A_WIDTH), tok),
                 pl.BlockSpec((tm, c_width), lambda b, i: (i, b))]
    for _ in range(3):
        for dil in DILATIONS:
            outs.append(jax.ShapeDtypeStruct((bsz, dil, t // dil, B_WIDTH), BF16))
            out_specs.append(pl.BlockSpec((1, dil, tm // dil, B_WIDTH), lambda b, i: (b, 0, i, 0)))
    res = pl.pallas_call(
        _inproj_kernel,
        grid=(bsz, nt),
        in_specs=[pl.BlockSpec((tm, d), tok),
                  pl.BlockSpec((1, 1, d), lambda b, i: (b, 0, 0)),
                  pl.BlockSpec((1, 1, d), lambda b, i: (b, 0, 1)),
                  pl.BlockSpec((d, pw), const2),
                  pl.BlockSpec((tm, LANES), tok),
                  pl.BlockSpec((tm, LANES), tok),
                  pl.BlockSpec((tm, LANES), tok),
                  pl.BlockSpec((1, A_WIDTH), const2),
                  pl.BlockSpec((1, A_WIDTH), const2),
                  pl.BlockSpec((A_HEADS, CHUNK, CHUNK), lambda b, i: (0, 0, 0)),
                  pl.BlockSpec((CHUNK, A_WIDTH), const2)],
        out_specs=out_specs,
        out_shape=outs,
        scratch_shapes=[pltpu.VMEM((B_WIDTH // LANES, tm, LANES), F32)] * 3,
        compiler_params=_cparams(("arbitrary", "arbitrary")),
        name="inproj_gmlp_rope",
    )(x2d, mod3, mod3, w_in_bf, cos_t, s1_t, s2_t, ln_g, ln_b, ws, bs_full)
    n_dil = len(DILATIONS)
    a_out, s_in = res[0], res[1]
    return a_out, s_in, res[2:2 + n_dil], res[2 + n_dil:2 + 2 * n_dil], res[2 + 2 * n_dil:]


def _att_kernel(q_ref, kp_ref, kc_ref, vp_ref, vc_ref, o_ref, st_ref, kcat_ref, vaug_ref):
    i = pl.program_id(2)
    blk = q_ref.shape[2]
    n_pairs = B_WIDTH // LANES
    kcat_ref[0:blk, :] = kp_ref[0, 0]
    kcat_ref[blk:, :] = kc_ref[0, 0]
    qi = lax.broadcasted_iota(jnp.int32, (blk, 2 * blk), 0)
    kk = lax.broadcasted_iota(jnp.int32, (blk, 2 * blk), 1)
    mask = (kk >= qi) & (kk <= qi + blk) & ((i > 0) | (kk >= blk))
    lane = lax.broadcasted_iota(jnp.int32, (blk, LANES), 1)
    low_half = lane < B_HEAD_DIM
    stats = jnp.zeros((blk, LANES), F32)
    nt = (((1,), (1,)), ((), ()))
    for pair in range(n_pairs):
        lanes = slice(pair * LANES, (pair + 1) * LANES)
        vaug_ref[pair, 0:blk, 0:LANES] = vp_ref[0, 0, :, lanes]
        vaug_ref[pair, blk:, 0:LANES] = vc_ref[0, 0, :, lanes]
        vaug_ref[pair, :, LANES:] = jnp.ones((2 * blk, LANES), BF16)
        q2 = q_ref[0, 0, :, lanes]
        kslab = kcat_ref[:, lanes]
        res = []
        for hh in range(2):
            hd = 2 * pair + hh
            qm = jnp.where(low_half if hh == 0 else jnp.logical_not(low_half), q2, jnp.zeros_like(q2))
            s = lax.dot_general(qm, kslab, nt, preferred_element_type=F32)
            s = jnp.where(mask, s, NEG_INF)
            m = jnp.max(jnp.maximum(s[:, :blk], s[:, blk:]), axis=1, keepdims=True)
            p = jnp.exp(s - m).astype(BF16)
            ov = jnp.dot(p, vaug_ref[pair], preferred_element_type=F32)
            den = ov[:, LANES:]
            res.append(ov[:, :LANES] / den)
            stats = jnp.where(lane == hd, m, stats)
            stats = jnp.where(lane == B_HEADS + hd, den, stats)
        o_ref[0, 0, :, lanes] = jnp.where(low_half, res[0], res[1]).astype(o_ref.dtype)
    st_ref[0, 0] = stats


def _att_branch(q, k, v, dil):
    bsz, _, rows, _ = q.shape
    nq = rows // ATT_BLK
    cur = lambda b, r, i: (b, r, i, 0)
    prev = lambda b, r, i: (b, r, jnp.maximum(i - 1, 0), 0)
    blk = (1, 1, ATT_BLK, B_WIDTH)
    sblk = (1, 1, ATT_BLK, LANES)
    return pl.pallas_call(
        _att_kernel,
        grid=(bsz, dil, nq),
        in_specs=[pl.BlockSpec(blk, cur), pl.BlockSpec(blk, prev), pl.BlockSpec(blk, cur),
                  pl.BlockSpec(blk, prev), pl.BlockSpec(blk, cur)],
        out_specs=[pl.BlockSpec(blk, cur), pl.BlockSpec(sblk, cur)],
        out_shape=[jax.ShapeDtypeStruct((bsz, dil, rows, B_WIDTH), BF16),
                   jax.ShapeDtypeStruct((bsz, dil, rows, LANES), F32)],
        scratch_shapes=[pltpu.VMEM((2 * ATT_BLK, B_WIDTH), BF16),
                        pltpu.VMEM((B_WIDTH // LANES, 2 * ATT_BLK, 2 * LANES), BF16)],
        compiler_params=_cparams(("arbitrary", "arbitrary", "arbitrary")),
        name=f"dilated_attn_d{dil}",
    )(q, k, k, v, v)


def _ssm_kernel(u_ref, bmat_ref, cmat_ref, are_ref, aim_ref, dskip_ref, gw_ref, gb_ref,
                o_ref, state_ref, xbuf_ref):
    bsz, two_n = state_ref.shape
    n_state = two_n // 2
    steps = u_ref.shape[0] // bsz

    @pl.when(pl.program_id(0) == 0)
    def _():
        state_ref[...] = jnp.zeros_like(state_ref)

    u = u_ref[...]
    xbuf_ref[...] = jnp.dot(u, bmat_ref[...], preferred_element_type=F32)
    a_re = jnp.broadcast_to(are_ref[...], (bsz, n_state))
    a_im = jnp.broadcast_to(aim_ref[...], (bsz, n_state))

    def step(tt, carry):
        x_re, x_im = carry
        r0 = pl.multiple_of(tt * bsz, bsz)
        b_re = xbuf_ref[pl.ds(r0, bsz), :n_state]
        b_im = xbuf_ref[pl.ds(r0, bsz), n_state:]
        n_re = a_re * x_re - a_im * x_im + b_re
        n_im = a_re * x_im + a_im * x_re + b_im
        xbuf_ref[pl.ds(r0, bsz), :n_state] = n_re
        xbuf_ref[pl.ds(r0, bsz), n_state:] = n_im
        return n_re, n_im

    x_re, x_im = lax.fori_loop(0, steps, step, (state_ref[:, :n_state], state_ref[:, n_state:]))
    state_ref[:, :n_state] = x_re
    state_ref[:, n_state:] = x_im

    y = jnp.dot(xbuf_ref[...].astype(BF16), cmat_ref[...], preferred_element_type=F32)
    y = _gelu_tanh(y + dskip_ref[...] * u.astype(F32))
    gate = jax.nn.sigmoid(jnp.dot(y.astype(BF16), gw_ref[...], preferred_element_type=F32) + gb_ref[...])
    o_ref[...] = (y * gate).astype(o_ref.dtype)


def _ssm_weights(lam_re, lam_im, log_dt, b_re, b_im, c_re, c_im):
    g, n_st = lam_re.shape
    cg = b_re.shape[-1]
    dt = jnp.exp(log_dt)[:, None]
    mag = jnp.exp(lam_re * dt)
    ab_re = mag * jnp.cos(lam_im * dt)
    ab_im = mag * jnp.sin(lam_im * dt)
    nr = ab_re - 1.0
    ni = ab_im
    mod2 = lam_re * lam_re + lam_im * lam_im
    f_re = (nr * lam_re + ni * lam_im) / mod2
    f_im = (ni * lam_re - nr * lam_im) / mod2
    bb_re = f_re[..., None] * b_re - f_im[..., None] * b_im
    bb_im = f_re[..., None] * b_im + f_im[..., None] * b_re
    eye = jnp.eye(g, dtype=F32)
    bm_re = jnp.einsum('gnc,gh->gchn', bb_re, eye).reshape(g * cg, g * n_st)
    bm_im = jnp.einsum('gnc,gh->gchn', bb_im, eye).reshape(g * cg, g * n_st)
    bmat = jnp.concatenate([bm_re, bm_im], axis=1)
    cm_re = jnp.einsum('gcn,gh->gnhc', c_re, eye).reshape(g * n_st, g * cg)
    cm_im = jnp.einsum('gcn,gh->gnhc', c_im, eye).reshape(g * n_st, g * cg)
    cmat = jnp.concatenate([cm_re, -cm_im], axis=0)
    return (bmat.astype(BF16), cmat.astype(BF16),
            ab_re.reshape(1, g * n_st), ab_im.reshape(1, g * n_st))


def _ssm(u_tm, bmat, cmat, a_re, a_im, d_skip, glu_w_bf, glu_b, bsz, t):
    cw = bmat.shape[0]
    two_n = bmat.shape[1]
    steps = min(SSM_STEPS, t)
    rows = steps * bsz
    u2 = u_tm.reshape(t * bsz, cw)
    const = lambda i: (0, 0)
    return pl.pallas_call(
        _ssm_kernel,
        grid=(t // steps,),
        in_specs=[pl.BlockSpec((rows, cw), lambda i: (i, 0)),
                  pl.BlockSpec((cw, two_n), const),
                  pl.BlockSpec((two_n, cw), const),
                  pl.BlockSpec((1, two_n // 2), const),
                  pl.BlockSpec((1, two_n // 2), const),
                  pl.BlockSpec((1, cw), const),
                  pl.BlockSpec((cw, cw), const),
                  pl.BlockSpec((1, cw), const)],
        out_specs=pl.BlockSpec((rows, cw), lambda i: (i, 0)),
        out_shape=jax.ShapeDtypeStruct((t * bsz, cw), BF16),
        scratch_shapes=[pltpu.VMEM((bsz, two_n), F32), pltpu.VMEM((rows, two_n), F32)],
        compiler_params=_cparams(("arbitrary",)),
        name="s5_scan_glu",
    )(u2, bmat, cmat, a_re, a_im, d_skip, glu_w_bf, glu_b).reshape(t, bsz * cw)


def _route(sel, scores):
    gs = []
    for g in range(N_EXPERT_GROUPS):
        a, b, c, d = sel[4 * g: 4 * g + 4]
        hi1, lo1 = jnp.maximum(a, b), jnp.minimum(a, b)
        hi2, lo2 = jnp.maximum(c, d), jnp.minimum(c, d)
        gs.append(jnp.maximum(hi1, hi2) + jnp.maximum(jnp.minimum(hi1, hi2), jnp.maximum(lo1, lo2)))
    g_idx = jnp.zeros(gs[0].shape, jnp.int32)
    best = gs[0]
    for g in range(1, N_EXPERT_GROUPS):
        better = gs[g] > best
        g_idx = jnp.where(better, g, g_idx)
        best = jnp.where(better, gs[g], best)

    def pick(rows, j):
        out = rows[j]
        for g in range(1, N_EXPERT_GROUPS):
            out = jnp.where(g_idx == g, rows[4 * g + j], out)
        return out

    v = [pick(sel, j) for j in range(EXPERTS_PER_GROUP)]
    s = [pick(scores, j) for j in range(EXPERTS_PER_GROUP)]
    i1 = jnp.zeros(g_idx.shape, jnp.int32)
    b1, g1 = v[0], s[0]
    for j in range(1, EXPERTS_PER_GROUP):
        better = v[j] > b1
        i1 = jnp.where(better, j, i1)
        b1 = jnp.where(better, v[j], b1)
        g1 = jnp.where(better, s[j], g1)
    i2 = jnp.full(g_idx.shape, -1, jnp.int32)
    b2 = jnp.zeros_like(b1)
    g2 = jnp.zeros_like(g1)
    for j in range(EXPERTS_PER_GROUP):
        better = (i1 != j) & ((i2 < 0) | (v[j] > b2))
        i2 = jnp.where(better, j, i2)
        b2 = jnp.where(better, v[j], b2)
        g2 = jnp.where(better, s[j], g2)
    tot = g1 + g2
    base = g_idx * EXPERTS_PER_GROUP
    return base + i1, base + i2, g1 / tot, g2 / tot


def _natural_order(ref, scr_ref, dil):
    if dil == 1:
        return ref[0, 0].astype(F32)
    per = ref.shape[2]
    n_slab = scr_ref.shape[0]
    for r in range(dil):
        for j in range(n_slab):
            scr_ref[j, pl.ds(r, per, stride=dil), :] = ref[0, r, :, j * LANES:(j + 1) * LANES].astype(F32)
    if n_slab == 1:
        return scr_ref[0]
    return jnp.concatenate([scr_ref[j] for j in range(n_slab)], axis=1)


def _outproj_kernel(a_ref, o1_ref, o2_ref, o3_ref, st1_ref, st2_ref, st3_ref,
                    c_ref, x_ref, g1_ref, sh2_ref, sc2_ref, wout_ref, expand_ref, lng_ref, lnb_ref,
                    rwh_ref, rwl_ref, rb_ref, x1_ref, h2_ref, eidx_ref, gate_ref, count_ref,
                    oscr2_ref, oscr3_ref, sscr2_ref, sscr3_ref, *, alpha):
    o_refs = (o1_ref, o2_ref, o3_ref)
    st_refs = (st1_ref, st2_ref, st3_ref)
    o_scr = (None, oscr2_ref, oscr3_ref)
    st_scr = (None, sscr2_ref, sscr3_ref)
    ms = [_natural_order(st_refs[g], st_scr[g], DILATIONS[g]) for g in range(len(DILATIONS))]
    dens = [pltpu.roll(m, LANES - B_HEADS, 1) for m in ms]
    mx = jnp.maximum(jnp.maximum(ms[0], ms[1]), ms[2])
    ws = [dens[g] * jnp.exp(ms[g] - mx) for g in range(len(DILATIONS))]
    lane = lax.broadcasted_iota(jnp.int32, mx.shape, 1)
    tot = jnp.where(lane < B_HEADS, ws[0] + ws[1] + ws[2], 1.0)
    expand = expand_ref[...]

    def widen(w):
        wn = jnp.where(lane < B_HEADS, w / tot, 0.0)
        hi = wn.astype(BF16)
        lo = (wn - hi.astype(F32)).astype(BF16)
        return (jnp.dot(hi, expand, preferred_element_type=F32)
                + jnp.dot(lo, expand, preferred_element_type=F32))

    b_out = widen(ws[0]) * _natural_order(o_refs[0], o_scr[0], DILATIONS[0])
    for g in range(1, len(DILATIONS)):
        b_out = b_out + widen(ws[g]) * _natural_order(o_refs[g], o_scr[g], DILATIONS[g])
    b_out = b_out.astype(BF16)

    wout = wout_ref
    mix = (jnp.dot(a_ref[...], wout[:A_WIDTH, :], preferred_element_type=F32)
           + jnp.dot(b_out, wout[A_WIDTH:A_WIDTH + B_WIDTH, :], preferred_element_type=F32)
           + jnp.dot(c_ref[...], wout[A_WIDTH + B_WIDTH:, :], preferred_element_type=F32))
    x1 = _ln(alpha * x_ref[...] + (1.0 + g1_ref[0]) * mix) * lng_ref[...] + lnb_ref[...]
    x1_ref[...] = x1
    h2 = _ln(x1) * (1.0 + sc2_ref[0]) + sh2_ref[0]
    h2_ref[...] = h2

    hi = h2.astype(BF16)
    lo = (h2 - hi.astype(F32)).astype(BF16)
    nt = (((1,), (1,)), ((), ()))
    logits = (lax.dot_general(rwh_ref[...], hi, nt, preferred_element_type=F32)
              + lax.dot_general(rwh_ref[...], lo, nt, preferred_element_type=F32)
              + lax.dot_general(rwl_ref[...], hi, nt, preferred_element_type=F32))
    scores = jax.nn.sigmoid(logits)
    sel = scores + rb_ref[...]
    sel_rows = [sel[e:e + 1, :] for e in range(N_EXPERTS)]
    score_rows = [scores[e:e + 1, :] for e in range(N_EXPERTS)]
    e1, e2, gt1, gt2 = _route(sel_rows, score_rows)

    @pl.when((pl.program_id(0) == 0) & (pl.program_id(1) == 0))
    def _():
        count_ref[...] = jnp.zeros_like(count_ref)

    tm = h2.shape[0]
    erow = lax.broadcasted_iota(jnp.int32, (N_EXPERTS, tm), 0)
    oh1 = erow == e1
    oh2 = erow == e2
    both = (oh1 | oh2).astype(BF16)
    earlier = (lax.broadcasted_iota(jnp.int32, (tm, tm), 0)
               < lax.broadcasted_iota(jnp.int32, (tm, tm), 1)).astype(BF16)
    base = count_ref[:, 0:1]
    cnt = jnp.dot(both, earlier, preferred_element_type=F32) + base
    r1 = jnp.sum(jnp.where(oh1, cnt, 0.0), axis=0, keepdims=True)
    r2 = jnp.sum(jnp.where(oh2, cnt, 0.0), axis=0, keepdims=True)
    total = base + jnp.sum(both.astype(F32), axis=1, keepdims=True)
    count_ref[...] = jnp.broadcast_to(total, count_ref.shape)

    eidx_ref[...] = jnp.zeros_like(eidx_ref)
    gate_ref[...] = jnp.zeros_like(gate_ref)
    eidx_ref[0:1, :] = e1
    eidx_ref[1:2, :] = e2
    eidx_ref[2:3, :] = r1.astype(jnp.int32)
    eidx_ref[3:4, :] = r2.astype(jnp.int32)
    gate_ref[0:1, :] = gt1
    gate_ref[1:2, :] = gt2


def _outproj(a_out, att, c_tm, x2d, mod3, w_out_bf, ln_g, ln_b, rw_hi, rw_lo, rbias, bsz, t, alpha):
    n, d = x2d.shape
    tm = min(ROW_TILE, t)
    nt = t // tm
    cw = c_tm.shape[1] // bsz
    tok = lambda b, i: (b * nt + i, 0)
    const2 = lambda b, i: (0, 0)
    (o1, st1), (o2, st2), (o3, st3) = att
    head = jnp.arange(B_WIDTH) // B_HEAD_DIM
    expand = (jnp.arange(LANES)[:, None] == head[None, :]).astype(BF16)
    res_major = lambda dil, w: pl.BlockSpec((1, dil, tm // dil, w), lambda b, i: (b, 0, i, 0))
    modspec = lambda j: pl.BlockSpec((1, 1, d), lambda b, i: (b, 0, j))
    return pl.pallas_call(
        functools.partial(_outproj_kernel, alpha=alpha),
        grid=(bsz, nt),
        in_specs=[pl.BlockSpec((tm, A_WIDTH), tok)]
                 + [res_major(dil, B_WIDTH) for dil in DILATIONS]
                 + [res_major(dil, LANES) for dil in DILATIONS]
                 + [pl.BlockSpec((tm, cw), lambda b, i: (i, b)),
                  pl.BlockSpec((tm, d), tok),
                  modspec(2), modspec(3), modspec(4),
                  pl.BlockSpec((d, d), const2),
                  pl.BlockSpec((LANES, B_WIDTH), const2),
                  pl.BlockSpec((1, d), const2), pl.BlockSpec((1, d), const2),
                  pl.BlockSpec((N_EXPERTS, d), const2), pl.BlockSpec((N_EXPERTS, d), const2),
                  pl.BlockSpec((N_EXPERTS, 1), const2)],
        out_specs=[pl.BlockSpec((tm, d), tok), pl.BlockSpec((tm, d), tok),
                   pl.BlockSpec((SUBLANES, tm), lambda b, i: (0, b * nt + i)),
                   pl.BlockSpec((SUBLANES, tm), lambda b, i: (0, b * nt + i)),
                   pl.BlockSpec((N_EXPERTS, LANES), const2)],
        out_shape=[jax.ShapeDtypeStruct((n, d), F32), jax.ShapeDtypeStruct((n, d), F32),
                   jax.ShapeDtypeStruct((SUBLANES, n), jnp.int32),
                   jax.ShapeDtypeStruct((SUBLANES, n), F32),
                   jax.ShapeDtypeStruct((N_EXPERTS, LANES), F32)],
        scratch_shapes=[pltpu.VMEM((B_WIDTH // LANES, tm, LANES), F32),
                        pltpu.VMEM((B_WIDTH // LANES, tm, LANES), F32),
                        pltpu.VMEM((1, tm, LANES), F32), pltpu.VMEM((1, tm, LANES), F32)],
        compiler_params=_cparams(("arbitrary", "arbitrary")),
        name="outproj_ln_router",
    )(a_out, o1, o2, o3, st1, st2, st3, c_tm, x2d, mod3, mod3, mod3, w_out_bf, expand,
      ln_g, ln_b, rw_hi, rw_lo, rbias)


def _dispatch_plan(eidx, counts, n_tok, blk):
    e = eidx[:TOP_K]
    rank = eidx[TOP_K:2 * TOP_K]
    pcounts = (counts + blk - 1) // blk * blk
    pends = jnp.cumsum(pcounts)
    pstarts = pends - pcounts
    dest = (jnp.take(pstarts, e) + rank).astype(jnp.int32)
    n_blk = n_tok * TOP_K // blk + N_EXPERTS
    blk_start = jnp.arange(n_blk, dtype=jnp.int32) * blk
    blk_e = jnp.minimum(jnp.sum((pends[None, :] <= blk_start[:, None]).astype(jnp.int32), axis=1),
                        N_EXPERTS - 1).astype(jnp.int32)
    n_used = (pends[-1] // blk).astype(jnp.int32).reshape(1)
    return dest, blk_e, n_used, pends.astype(jnp.int32), pcounts.astype(jnp.int32)


def _dispatch_kernel(pend_ref, pcnt_ref, dest_ref, h2_hbm, xs_hbm, zbuf, sem, zsem):
    kk = pl.program_id(0)
    i = pl.program_id(1)
    blk = zbuf.shape[0]
    rows = dest_ref.shape[2]

    @pl.when((kk == 0) & (i == 0))
    def _():
        zbuf[...] = jnp.zeros_like(zbuf)
        for e in range(N_EXPERTS):
            @pl.when(pcnt_ref[e] > 0)
            def _():
                start = pl.multiple_of(pend_ref[e] - blk, blk)
                cp = pltpu.make_async_copy(zbuf, xs_hbm.at[pl.ds(start, blk), :], zsem)
                cp.start()
                cp.wait()
        n_blk = xs_hbm.shape[0] // blk
        for t in range(N_EXPERTS):
            tail = pend_ref[N_EXPERTS - 1] // blk + t

            @pl.when(tail < n_blk)
            def _():
                start = pl.multiple_of(tail * blk, blk)
                cp = pltpu.make_async_copy(zbuf, xs_hbm.at[pl.ds(start, blk), :], zsem)
                cp.start()
                cp.wait()

    base = i * rows

    def issue(s, c):
        pltpu.make_async_copy(h2_hbm.at[pl.ds(base + s, 1), :],
                              xs_hbm.at[pl.ds(dest_ref[0, 0, s], 1), :], sem).start()
        return c
    lax.fori_loop(0, rows, issue, 0, unroll=8)
    pltpu.make_async_copy(h2_hbm.at[pl.ds(0, rows), :], xs_hbm.at[pl.ds(0, rows), :], sem).wait()


def _expert_kernel(blk_e_ref, nused_ref, xs_ref, wg_ref, wu_ref, wd_ref, ys_ref):
    j = pl.program_id(0)

    @pl.when(j < nused_ref[0])
    def _():
        rows = xs_ref[...].astype(BF16)
        gt = jnp.dot(rows, wg_ref[0], preferred_element_type=F32)
        up = jnp.dot(rows, wu_ref[0], preferred_element_type=F32)
        hid = (gt * jax.nn.sigmoid(gt) * up).astype(BF16)
        ys_ref[...] = jnp.dot(hid, wd_ref[0], preferred_element_type=F32)

    @pl.when(j >= nused_ref[0])
    def _():
        ys_ref[...] = jnp.zeros_like(ys_ref)


def _moe(h2, eidx, counts, wg_bf, wu_bf, wd_bf):
    n_tok, d = h2.shape
    de = wg_bf.shape[-1]
    blk = min(MOE_BLK, n_tok)
    dest, blk_e, n_used, pends, pcounts = _dispatch_plan(eidx, counts, n_tok, blk)
    n_blk = blk_e.shape[0]
    cap = n_blk * blk

    rows = min(DISPATCH_ROWS, n_tok)
    nt = n_tok // rows
    xs = pl.pallas_call(
        _dispatch_kernel,
        grid_spec=pltpu.PrefetchScalarGridSpec(
            num_scalar_prefetch=2,
            grid=(TOP_K, nt),
            in_specs=[pl.BlockSpec((1, 1, rows), lambda kk, i, pe, pc: (kk * nt + i, 0, 0),
                                   memory_space=pltpu.SMEM),
                      pl.BlockSpec(memory_space=pl.ANY)],
            out_specs=pl.BlockSpec(memory_space=pl.ANY),
            scratch_shapes=[pltpu.VMEM((blk, d), F32), pltpu.SemaphoreType.DMA, pltpu.SemaphoreType.DMA],
        ),
        out_shape=jax.ShapeDtypeStruct((cap, d), F32),
        compiler_params=_cparams(("arbitrary", "arbitrary")),
        name="moe_dispatch",
    )(pends, pcounts, dest.reshape(TOP_K * nt, 1, rows), h2)

    used = lambda j, be, nu: (jnp.minimum(j, nu[0] - 1), 0)
    ys = pl.pallas_call(
        _expert_kernel,
        grid_spec=pltpu.PrefetchScalarGridSpec(
            num_scalar_prefetch=2,
            grid=(n_blk,),
            in_specs=[pl.BlockSpec((blk, d), used),
                      pl.BlockSpec((1, d, de), lambda j, be, nu: (be[j], 0, 0)),
                      pl.BlockSpec((1, d, de), lambda j, be, nu: (be[j], 0, 0)),
                      pl.BlockSpec((1, de, d), lambda j, be, nu: (be[j], 0, 0))],
            out_specs=pl.BlockSpec((blk, d), lambda j, be, nu: (j, 0)),
        ),
        out_shape=jax.ShapeDtypeStruct((cap, d), F32),
        compiler_params=_cparams(("arbitrary",)),
        name="moe_experts",
    )(blk_e, n_used, xs, wg_bf, wu_bf, wd_bf)
    return ys, dest


def _ffn_ln_kernel(dest_ref, x_ref, gate_ref, g2_ref, lng_ref, lnb_ref, ys_hbm, o_ref, ybuf, sem,
                   *, alpha):
    tm = x_ref.shape[0]

    def issue(s, c):
        pltpu.make_async_copy(ys_hbm.at[pl.ds(dest_ref[0, 0, s], 1), :],
                              ybuf.at[pl.ds(s, 1), :], sem).start()
        return c
    lax.fori_loop(0, TOP_K * tm, issue, 0, unroll=8)
    pltpu.make_async_copy(ys_hbm.at[pl.ds(0, TOP_K * tm), :], ybuf, sem).wait()

    gate = gate_ref[...]
    ffn = ybuf[0:tm, :] * gate[:, 0:1]
    for kk in range(1, TOP_K):
        ffn = ffn + ybuf[kk * tm:(kk + 1) * tm, :] * gate[:, kk:kk + 1]
    o_ref[...] = _ln(alpha * x_ref[...] + (1.0 + g2_ref[0]) * ffn) * lng_ref[...] + lnb_ref[...]


def _ffn_ln(x1, ys, dest, gates, mod3, ln_g, ln_b, bsz, t, alpha):
    n, d = x1.shape
    tm = min(ROW_TILE, t)
    nt = t // tm
    tok = lambda b, i, *_: (b * nt + i, 0)
    const2 = lambda b, i, *_: (0, 0)
    dest_tiles = dest.reshape(TOP_K, n // tm, tm).transpose(1, 0, 2).reshape(n // tm, 1, TOP_K * tm)
    gate_cols = gates[:TOP_K].T
    return pl.pallas_call(
        functools.partial(_ffn_ln_kernel, alpha=alpha),
        grid=(bsz, nt),
        in_specs=[pl.BlockSpec((1, 1, TOP_K * tm), lambda b, i: (b * nt + i, 0, 0),
                               memory_space=pltpu.SMEM),
                  pl.BlockSpec((tm, d), tok), pl.BlockSpec((tm, TOP_K), tok),
                  pl.BlockSpec((1, 1, d), lambda b, i: (b, 0, 5)),
                  pl.BlockSpec((1, d), const2), pl.BlockSpec((1, d), const2),
                  pl.BlockSpec(memory_space=pl.ANY)],
        out_specs=pl.BlockSpec((tm, d), tok),
        out_shape=jax.ShapeDtypeStruct((n, d), F32),
        scratch_shapes=[pltpu.VMEM((TOP_K * tm, d), F32), pltpu.SemaphoreType.DMA],
        compiler_params=_cparams(("arbitrary", "arbitrary")),
        name="ffn_combine_ln",
    )(dest_tiles, x1, gate_cols, mod3, ln_g, ln_b, ys)


def kernel(x, c, positions, ada_w, ada_b, w_in, gm_ln_g, gm_ln_b, gm_ws, gm_bs, ssm_lam_re, ssm_lam_im, ssm_log_dt, ssm_b_re, ssm_b_im, ssm_c_re, ssm_c_im, ssm_d, glu_w, glu_b, w_out, ln1_g, ln1_b, router_w, router_bias, exp_w_gate, exp_w_up, exp_w_down, ln2_g, ln2_b):
    bsz, t, d = x.shape
    depth = ada_w.shape[0]
    alpha = (2.0 * depth) ** 0.25
    n = bsz * t

    mod = _adaln_mod(c.astype(F32), ada_w, ada_b)
    cos_t, s1_t, s2_t = _rope_tables(positions)
    rw_t = router_w.astype(F32).T
    rw_hi = rw_t.astype(BF16)
    rw_lo = (rw_t - rw_hi.astype(F32)).astype(BF16)
    rbias = router_bias.astype(F32).reshape(N_EXPERTS, 1)

    xf = x.astype(F32).reshape(n, d)
    for l in range(depth):
        mod3 = mod[l].reshape(bsz, 1, 6 * d)
        bs_full = jnp.repeat(gm_bs[l].T, A_HEAD_DIM, axis=1)
        a_out, s_in, qs, ks, vs = _inproj(
            xf, mod3, w_in[l].astype(BF16), cos_t, s1_t, s2_t,
            gm_ln_g[l].reshape(1, A_WIDTH), gm_ln_b[l].reshape(1, A_WIDTH), gm_ws[l], bs_full, bsz, t)
        att = [_att_branch(qs[g], ks[g], vs[g], dil) for g, dil in enumerate(DILATIONS)]
        bmat, cmat, a_re, a_im = _ssm_weights(ssm_lam_re[l], ssm_lam_im[l], ssm_log_dt[l],
                                              ssm_b_re[l], ssm_b_im[l], ssm_c_re[l], ssm_c_im[l])
        cw = bmat.shape[0]
        c_out = _ssm(s_in, bmat, cmat, a_re, a_im, ssm_d[l].reshape(1, cw),
                     glu_w[l].astype(BF16), glu_b[l].reshape(1, cw), bsz, t)
        x1, h2, eidx, gates, counts = _outproj(a_out, att, c_out, xf, mod3, w_out[l].astype(BF16),
                                               ln1_g[l].reshape(1, d), ln1_b[l].reshape(1, d),
                                               rw_hi, rw_lo, rbias, bsz, t, alpha)
        ys, dest = _moe(h2, eidx, counts[:, 0].astype(jnp.int32), exp_w_gate[l].astype(BF16),
                        exp_w_up[l].astype(BF16), exp_w_down[l].astype(BF16))
        xf = _ffn_ln(x1, ys, dest, gates, mod3, ln2_g[l].reshape(1, d), ln2_b[l].reshape(1, d),
                     bsz, t, alpha)
    return xf.reshape(bsz, t, d)
```

```python
import functools
import math

import jax
import jax.numpy as jnp
from jax import lax
from jax.experimental import pallas as pl
from jax.experimental.pallas import tpu as pltpu

F32 = jnp.float32
BF16 = jnp.bfloat16

A_HEADS = 4
A_HEAD_DIM = 64
A_WIDTH = A_HEADS * A_HEAD_DIM
CHUNK = 128
B_HEADS = 8
B_HEAD_DIM = 64
B_WIDTH = B_HEADS * B_HEAD_DIM
DILATIONS = (1, 4, 16)
ATT_BLK = 128
ROT_DIM = B_HEAD_DIM // 4
ROPE_THETA = 500000.0
SSM_GROUP = 16
SSM_STATE = 64
N_EXPERTS = 16
N_EXPERT_GROUPS = 4
EXPERTS_PER_GROUP = 4
TOP_K = 2
LN_EPS = 1e-5
NEG_INF = -1e30

LANES = 128
SUBLANES = 8
VMEM_LIMIT = 56 * 1024 * 1024
ROW_TILE = 512
SSM_STEPS = 64
MOE_BLK = 512


def _cparams(sem):
    return pltpu.CompilerParams(dimension_semantics=sem, vmem_limit_bytes=VMEM_LIMIT)


def _ln(x):
    mu = jnp.mean(x, axis=-1, keepdims=True)
    xc = x - mu
    var = jnp.mean(xc * xc, axis=-1, keepdims=True)
    return xc * lax.rsqrt(var + LN_EPS)


def _gelu_tanh(x):
    return 0.5 * x * (1.0 + jnp.tanh(math.sqrt(2.0 / math.pi) * (x + 0.044715 * (x * x * x))))


def _mod_kernel(c_ref, w_ref, b_ref, o_ref):
    c = c_ref[...]
    cond = c * jax.nn.sigmoid(c)
    o_ref[0] = jnp.dot(cond, w_ref[0], precision=lax.Precision.HIGHEST,
                       preferred_element_type=F32) + b_ref[0]


def _adaln_mod(c, ada_w, ada_b):
    depth, d, n = ada_w.shape
    bsz = c.shape[0]
    nb = d
    return pl.pallas_call(
        _mod_kernel,
        grid=(depth, n // nb),
        in_specs=[pl.BlockSpec((bsz, d), lambda l, j: (0, 0)),
                  pl.BlockSpec((1, d, nb), lambda l, j: (l, 0, j)),
                  pl.BlockSpec((1, 1, nb), lambda l, j: (l, 0, j))],
        out_specs=pl.BlockSpec((1, bsz, nb), lambda l, j: (l, 0, j)),
        out_shape=jax.ShapeDtypeStruct((depth, bsz, n), F32),
        compiler_params=_cparams(("arbitrary", "arbitrary")),
        name="adaln_mod",
    )(c, ada_w, ada_b.reshape(depth, 1, n))


def _rope_kernel(pos_ref, freq_ref, cos_ref, s1_ref, s2_ref):
    ang = pos_ref[...].astype(F32) * freq_ref[...]
    lane = lax.broadcasted_iota(jnp.int32, ang.shape, 1) % B_HEAD_DIM
    sn = jnp.sin(ang)
    cos_ref[...] = jnp.cos(ang)
    s1_ref[...] = jnp.where(lane < ROT_DIM // 2, -sn, 0.0)
    s2_ref[...] = jnp.where((lane >= ROT_DIM // 2) & (lane < ROT_DIM), sn, 0.0)


def _rope_tables(positions):
    n = positions.size
    half = ROT_DIM // 2
    lane = jnp.arange(LANES) % B_HEAD_DIM
    freqs = ROPE_THETA ** (-(lane % half).astype(F32) * 2.0 / ROT_DIM)
    freq_row = jnp.where(lane < ROT_DIM, freqs, 0.0).reshape(1, LANES).astype(F32)
    tm = min(2048, n)
    out = jax.ShapeDtypeStruct((n, LANES), F32)
    spec = pl.BlockSpec((tm, LANES), lambda i: (i, 0))
    return pl.pallas_call(
        _rope_kernel,
        grid=(n // tm,),
        in_specs=[pl.BlockSpec((tm, 1), lambda i: (i, 0)),
                  pl.BlockSpec((1, LANES), lambda i: (0, 0))],
        out_specs=[spec, spec, spec],
        out_shape=[out, out, out],
        compiler_params=_cparams(("arbitrary",)),
        name="rope_tables",
    )(positions.reshape(n, 1), freq_row)


def _inproj_kernel(x_ref, sh_ref, sc_ref, w_ref, cos_ref, s1_ref, s2_ref, lng_ref, lnb_ref,
                   ws_ref, bs_ref, a_ref, s_ref, *rest):
    n_dil = len(DILATIONS)
    q_outs, k_outs, v_outs = rest[:n_dil], rest[n_dil:2 * n_dil], rest[2 * n_dil:3 * n_dil]
    qs_ref, ks_ref, vs_ref = rest[3 * n_dil:]
    tm = x_ref.shape[0]
    h = _ln(x_ref[...]) * (1.0 + sc_ref[0]) + sh_ref[0]
    proj = jnp.dot(h.astype(BF16), w_ref[...], preferred_element_type=F32)

    uv = _gelu_tanh(proj[:, :2 * A_WIDTH])
    u = uv[:, :A_WIDTH]
    v = (_ln(uv[:, A_WIDTH:]) * lng_ref[...] + lnb_ref[...]).astype(BF16)
    row = lax.broadcasted_iota(jnp.int32, (CHUNK, CHUNK), 0)
    col = lax.broadcasted_iota(jnp.int32, (CHUNK, CHUNK), 1)
    head_of_lane = lax.broadcasted_iota(jnp.int32, (CHUNK, A_WIDTH), 1) // A_HEAD_DIM
    w_heads = [jnp.where(col <= row, ws_ref[hd], 0.0).astype(BF16) for hd in range(A_HEADS)]
    for cidx in range(tm // CHUNK):
        rows = slice(cidx * CHUNK, (cidx + 1) * CHUNK)
        vc = v[rows]
        sv = bs_ref[...]
        for hd in range(A_HEADS):
            full = jnp.dot(w_heads[hd], vc, preferred_element_type=F32)
            sv = sv + jnp.where(head_of_lane == hd, full, 0.0)
        a_ref[rows, :] = (u[rows] * sv).astype(a_ref.dtype)

    cos = cos_ref[...]
    s1 = s1_ref[...]
    s2 = s2_ref[...]
    q0 = 2 * A_WIDTH
    k0 = q0 + B_WIDTH
    v0 = k0 + B_WIDTH
    for j in range(B_WIDTH // LANES):
        for base, ref, scale in ((q0, qs_ref, B_HEAD_DIM ** -0.5), (k0, ks_ref, 1.0)):
            xs = proj[:, base + j * LANES: base + (j + 1) * LANES]
            rot = (xs * cos + pltpu.roll(xs, LANES - ROT_DIM // 2, 1) * s1
                   + pltpu.roll(xs, ROT_DIM // 2, 1) * s2)
            ref[j] = rot * scale
        vs_ref[j] = proj[:, v0 + j * LANES:v0 + (j + 1) * LANES]

    for src, outs in ((qs_ref, q_outs), (ks_ref, k_outs), (vs_ref, v_outs)):
        for dil, out in zip(DILATIONS, outs):
            for r in range(dil):
                for j in range(B_WIDTH // LANES):
                    out[0, r, :, j * LANES:(j + 1) * LANES] = (
                        src[j, pl.ds(r, tm // dil, stride=dil), :].astype(out.dtype))

    s_ref[...] = proj[:, v0 + B_WIDTH:].astype(s_ref.dtype)


def _inproj(x2d, mod3, w_in_bf, cos_t, s1_t, s2_t, ln_g, ln_b, ws, bs_full, bsz, t):
    n, d = x2d.shape
    tm = min(ROW_TILE, t)
    nt = t // tm
    pw = w_in_bf.shape[1]
    c_width = pw - 2 * A_WIDTH - 3 * B_WIDTH
    tok = lambda b, i: (b * nt + i, 0)
    const2 = lambda b, i: (0, 0)
    outs = [jax.ShapeDtypeStruct((n, A_WIDTH), BF16),
            jax.ShapeDtypeStruct((t, bsz * c_width), BF16)]
    out_specs = [pl.BlockSpec((tm, A_WIDTH), tok),
                 pl.BlockSpec((tm, c_width), lambda b, i: (i, b))]
    for _ in range(3):
        for dil in DILATIONS:
            outs.append(jax.ShapeDtypeStruct((bsz, dil, t // dil, B_WIDTH), BF16))
            out_specs.append(pl.BlockSpec((1, dil, tm // dil, B_WIDTH), lambda b, i: (b, 0, i, 0)))
    res = pl.pallas_call(
        _inproj_kernel,
        grid=(bsz, nt),
        in_specs=[pl.BlockSpec((tm, d), tok),
                  pl.BlockSpec((1, 1, d), lambda b, i: (b, 0, 0)),
                  pl.BlockSpec((1, 1, d), lambda b, i: (b, 0, 1)),
                  pl.BlockSpec((d, pw), const2),
                  pl.BlockSpec((tm, LANES), tok),
                  pl.BlockSpec((tm, LANES), tok),
                  pl.BlockSpec((tm, LANES), tok),
                  pl.BlockSpec((1, A_WIDTH), const2),
                  pl.BlockSpec((1, A_WIDTH), const2),
                  pl.BlockSpec((A_HEADS, CHUNK, CHUNK), lambda b, i: (0, 0, 0)),
                  pl.BlockSpec((CHUNK, A_WIDTH), const2)],
        out_specs=out_specs,
        out_shape=outs,
        scratch_shapes=[pltpu.VMEM((B_WIDTH // LANES, tm, LANES), F32)] * 3,
        compiler_params=_cparams(("arbitrary", "arbitrary")),
        name="inproj_gmlp_rope",
    )(x2d, mod3, mod3, w_in_bf, cos_t, s1_t, s2_t, ln_g, ln_b, ws, bs_full)
    n_dil = len(DILATIONS)
    a_out, s_in = res[0], res[1]
    return a_out, s_in, res[2:2 + n_dil], res[2 + n_dil:2 + 2 * n_dil], res[2 + 2 * n_dil:]


def _att_kernel(q_ref, kp_ref, kc_ref, vp_ref, vc_ref, o_ref, st_ref, kcat_ref, vaug_ref):
    i = pl.program_id(2)
    blk = q_ref.shape[2]
    n_pairs = B_WIDTH // LANES
    kcat_ref[0:blk, :] = kp_ref[0, 0]
    kcat_ref[blk:, :] = kc_ref[0, 0]
    qi = lax.broadcasted_iota(jnp.int32, (blk, 2 * blk), 0)
    kk = lax.broadcasted_iota(jnp.int32, (blk, 2 * blk), 1)
    mask = (kk >= qi) & (kk <= qi + blk) & ((i > 0) | (kk >= blk))
    lane = lax.broadcasted_iota(jnp.int32, (blk, LANES), 1)
    low_half = lane < B_HEAD_DIM
    stats = jnp.zeros((blk, LANES), F32)
    nt = (((1,), (1,)), ((), ()))
    for pair in range(n_pairs):
        lanes = slice(pair * LANES, (pair + 1) * LANES)
        vaug_ref[pair, 0:blk, 0:LANES] = vp_ref[0, 0, :, lanes]
        vaug_ref[pair, blk:, 0:LANES] = vc_ref[0, 0, :, lanes]
        vaug_ref[pair, :, LANES:] = jnp.ones((2 * blk, LANES), BF16)
        q2 = q_ref[0, 0, :, lanes]
        kslab = kcat_ref[:, lanes]
        res = []
        for hh in range(2):
            hd = 2 * pair + hh
            qm = jnp.where(low_half if hh == 0 else jnp.logical_not(low_half), q2, jnp.zeros_like(q2))
            s = lax.dot_general(qm, kslab, nt, preferred_element_type=F32)
            s = jnp.where(mask, s, NEG_INF)
            m = jnp.max(jnp.maximum(s[:, :blk], s[:, blk:]), axis=1, keepdims=True)
            p = jnp.exp(s - m).astype(BF16)
            ov = jnp.dot(p, vaug_ref[pair], preferred_element_type=F32)
            den = ov[:, LANES:]
            res.append(ov[:, :LANES] / den)
            stats = jnp.where(lane == hd, m, stats)
            stats = jnp.where(lane == B_HEADS + hd, den, stats)
        o_ref[0, 0, :, lanes] = jnp.where(low_half, res[0], res[1]).astype(o_ref.dtype)
    st_ref[0, 0] = stats


def _att_branch(q, k, v, dil):
    bsz, _, rows, _ = q.shape
    nq = rows // ATT_BLK
    cur = lambda b, r, i: (b, r, i, 0)
    prev = lambda b, r, i: (b, r, jnp.maximum(i - 1, 0), 0)
    blk = (1, 1, ATT_BLK, B_WIDTH)
    sblk = (1, 1, ATT_BLK, LANES)
    return pl.pallas_call(
        _att_kernel,
        grid=(bsz, dil, nq),
        in_specs=[pl.BlockSpec(blk, cur), pl.BlockSpec(blk, prev), pl.BlockSpec(blk, cur),
                  pl.BlockSpec(blk, prev), pl.BlockSpec(blk, cur)],
        out_specs=[pl.BlockSpec(blk, cur), pl.BlockSpec(sblk, cur)],
        out_shape=[jax.ShapeDtypeStruct((bsz, dil, rows, B_WIDTH), BF16),
                   jax.ShapeDtypeStruct((bsz, dil, rows, LANES), F32)],
        scratch_shapes=[pltpu.VMEM((2 * ATT_BLK, B_WIDTH), BF16),
                        pltpu.VMEM((B_WIDTH // LANES, 2 * ATT_BLK, 2 * LANES), BF16)],
        compiler_params=_cparams(("arbitrary", "arbitrary", "arbitrary")),
        name=f"dilated_attn_d{dil}",
    )(q, k, k, v, v)


def _ssm_kernel(u_ref, bmat_ref, cmat_ref, are_ref, aim_ref, dskip_ref, gw_ref, gb_ref,
                o_ref, state_ref, xbuf_ref):
    bsz, two_n = state_ref.shape
    n_state = two_n // 2
    steps = u_ref.shape[0] // bsz

    @pl.when(pl.program_id(0) == 0)
    def _():
        state_ref[...] = jnp.zeros_like(state_ref)

    u = u_ref[...]
    xbuf_ref[...] = jnp.dot(u, bmat_ref[...], preferred_element_type=F32)
    a_re = jnp.broadcast_to(are_ref[...], (bsz, n_state))
    a_im = jnp.broadcast_to(aim_ref[...], (bsz, n_state))

    def step(tt, carry):
        x_re, x_im = carry
        r0 = pl.multiple_of(tt * bsz, bsz)
        b_re = xbuf_ref[pl.ds(r0, bsz), :n_state]
        b_im = xbuf_ref[pl.ds(r0, bsz), n_state:]
        n_re = a_re * x_re - a_im * x_im + b_re
        n_im = a_re * x_im + a_im * x_re + b_im
        xbuf_ref[pl.ds(r0, bsz), :n_state] = n_re
        xbuf_ref[pl.ds(r0, bsz), n_state:] = n_im
        return n_re, n_im

    x_re, x_im = lax.fori_loop(0, steps, step, (state_ref[:, :n_state], state_ref[:, n_state:]))
    state_ref[:, :n_state] = x_re
    state_ref[:, n_state:] = x_im

    y = jnp.dot(xbuf_ref[...].astype(BF16), cmat_ref[...], preferred_element_type=F32)
    y = _gelu_tanh(y + dskip_ref[...] * u.astype(F32))
    gate = jax.nn.sigmoid(jnp.dot(y.astype(BF16), gw_ref[...], preferred_element_type=F32) + gb_ref[...])
    o_ref[...] = (y * gate).astype(o_ref.dtype)


def _ssm_weights(lam_re, lam_im, log_dt, b_re, b_im, c_re, c_im):
    g, n_st = lam_re.shape
    cg = b_re.shape[-1]
    dt = jnp.exp(log_dt)[:, None]
    mag = jnp.exp(lam_re * dt)
    ab_re = mag * jnp.cos(lam_im * dt)
    ab_im = mag * jnp.sin(lam_im * dt)
    nr = ab_re - 1.0
    ni = ab_im
    mod2 = lam_re * lam_re + lam_im * lam_im
    f_re = (nr * lam_re + ni * lam_im) / mod2
    f_im = (ni * lam_re - nr * lam_im) / mod2
    bb_re = f_re[..., None] * b_re - f_im[..., None] * b_im
    bb_im = f_re[..., None] * b_im + f_im[..., None] * b_re
    eye = jnp.eye(g, dtype=F32)
    bm_re = jnp.einsum('gnc,gh->gchn', bb_re, eye).reshape(g * cg, g * n_st)
    bm_im = jnp.einsum('gnc,gh->gchn', bb_im, eye).reshape(g * cg, g * n_st)
    bmat = jnp.concatenate([bm_re, bm_im], axis=1)
    cm_re = jnp.einsum('gcn,gh->gnhc', c_re, eye).reshape(g * n_st, g * cg)
    cm_im = jnp.einsum('gcn,gh->gnhc', c_im, eye).reshape(g * n_st, g * cg)
    cmat = jnp.concatenate([cm_re, -cm_im], axis=0)
    return (bmat.astype(BF16), cmat.astype(BF16),
            ab_re.reshape(1, g * n_st), ab_im.reshape(1, g * n_st))


def _ssm(u_tm, bmat, cmat, a_re, a_im, d_skip, glu_w_bf, glu_b, bsz, t):
    cw = bmat.shape[0]
    two_n = bmat.shape[1]
    steps = min(SSM_STEPS, t)
    rows = steps * bsz
    u2 = u_tm.reshape(t * bsz, cw)
    const = lambda i: (0, 0)
    return pl.pallas_call(
        _ssm_kernel,
        grid=(t // steps,),
        in_specs=[pl.BlockSpec((rows, cw), lambda i: (i, 0)),
                  pl.BlockSpec((cw, two_n), const),
                  pl.BlockSpec((two_n, cw), const),
                  pl.BlockSpec((1, two_n // 2), const),
                  pl.BlockSpec((1, two_n // 2), const),
                  pl.BlockSpec((1, cw), const),
                  pl.BlockSpec((cw, cw), const),
                  pl.BlockSpec((1, cw), const)],
        out_specs=pl.BlockSpec((rows, cw), lambda i: (i, 0)),
        out_shape=jax.ShapeDtypeStruct((t * bsz, cw), BF16),
        scratch_shapes=[pltpu.VMEM((bsz, two_n), F32), pltpu.VMEM((rows, two_n), F32)],
        compiler_params=_cparams(("arbitrary",)),
        name="s5_scan_glu",
    )(u2, bmat, cmat, a_re, a_im, d_skip, glu_w_bf, glu_b).reshape(t, bsz * cw)


def _route(sel, scores):
    gs = []
    for g in range(N_EXPERT_GROUPS):
        a, b, c, d = sel[4 * g: 4 * g + 4]
        hi1, lo1 = jnp.maximum(a, b), jnp.minimum(a, b)
        hi2, lo2 = jnp.maximum(c, d), jnp.minimum(c, d)
        gs.append(jnp.maximum(hi1, hi2) + jnp.maximum(jnp.minimum(hi1, hi2), jnp.maximum(lo1, lo2)))
    g_idx = jnp.zeros(gs[0].shape, jnp.int32)
    best = gs[0]
    for g in range(1, N_EXPERT_GROUPS):
        better = gs[g] > best
        g_idx = jnp.where(better, g, g_idx)
        best = jnp.where(better, gs[g], best)

    def pick(rows, j):
        out = rows[j]
        for g in range(1, N_EXPERT_GROUPS):
            out = jnp.where(g_idx == g, rows[4 * g + j], out)
        return out

    v = [pick(sel, j) for j in range(EXPERTS_PER_GROUP)]
    s = [pick(scores, j) for j in range(EXPERTS_PER_GROUP)]
    i1 = jnp.zeros(g_idx.shape, jnp.int32)
    b1, g1 = v[0], s[0]
    for j in range(1, EXPERTS_PER_GROUP):
        better = v[j] > b1
        i1 = jnp.where(better, j, i1)
        b1 = jnp.where(better, v[j], b1)
        g1 = jnp.where(better, s[j], g1)
    i2 = jnp.full(g_idx.shape, -1, jnp.int32)
    b2 = jnp.zeros_like(b1)
    g2 = jnp.zeros_like(g1)
    for j in range(EXPERTS_PER_GROUP):
        better = (i1 != j) & ((i2 < 0) | (v[j] > b2))
        i2 = jnp.where(better, j, i2)
        b2 = jnp.where(better, v[j], b2)
        g2 = jnp.where(better, s[j], g2)
    tot = g1 + g2
    base = g_idx * EXPERTS_PER_GROUP
    return base + i1, base + i2, g1 / tot, g2 / tot


def _natural_order(ref, scr_ref, dil):
    if dil == 1:
        return ref[0, 0].astype(F32)
    per = ref.shape[2]
    n_slab = scr_ref.shape[0]
    for r in range(dil):
        for j in range(n_slab):
            scr_ref[j, pl.ds(r, per, stride=dil), :] = ref[0, r, :, j * LANES:(j + 1) * LANES].astype(F32)
    if n_slab == 1:
        return scr_ref[0]
    return jnp.concatenate([scr_ref[j] for j in range(n_slab)], axis=1)


def _outproj_kernel(a_ref, o1_ref, o2_ref, o3_ref, st1_ref, st2_ref, st3_ref,
                    c_ref, x_ref, g1_ref, sh2_ref, sc2_ref, wout_ref, expand_ref, lng_ref, lnb_ref,
                    rwh_ref, rwl_ref, rb_ref, x1_ref, h2_ref, eidx_ref, gate_ref, count_ref,
                    oscr2_ref, oscr3_ref, sscr2_ref, sscr3_ref, *, alpha):
    o_refs = (o1_ref, o2_ref, o3_ref)
    st_refs = (st1_ref, st2_ref, st3_ref)
    o_scr = (None, oscr2_ref, oscr3_ref)
    st_scr = (None, sscr2_ref, sscr3_ref)
    ms = [_natural_order(st_refs[g], st_scr[g], DILATIONS[g]) for g in range(len(DILATIONS))]
    dens = [pltpu.roll(m, LANES - B_HEADS, 1) for m in ms]
    mx = jnp.maximum(jnp.maximum(ms[0], ms[1]), ms[2])
    ws = [dens[g] * jnp.exp(ms[g] - mx) for g in range(len(DILATIONS))]
    lane = lax.broadcasted_iota(jnp.int32, mx.shape, 1)
    tot = jnp.where(lane < B_HEADS, ws[0] + ws[1] + ws[2], 1.0)
    expand = expand_ref[...]

    def widen(w):
        wn = jnp.where(lane < B_HEADS, w / tot, 0.0)
        hi = wn.astype(BF16)
        lo = (wn - hi.astype(F32)).astype(BF16)
        return (jnp.dot(hi, expand, preferred_element_type=F32)
                + jnp.dot(lo, expand, preferred_element_type=F32))

    b_out = widen(ws[0]) * _natural_order(o_refs[0], o_scr[0], DILATIONS[0])
    for g in range(1, len(DILATIONS)):
        b_out = b_out + widen(ws[g]) * _natural_order(o_refs[g], o_scr[g], DILATIONS[g])
    b_out = b_out.astype(BF16)

    wout = wout_ref
    mix = (jnp.dot(a_ref[...], wout[:A_WIDTH, :], preferred_element_type=F32)
           + jnp.dot(b_out, wout[A_WIDTH:A_WIDTH + B_WIDTH, :], preferred_element_type=F32)
           + jnp.dot(c_ref[...], wout[A_WIDTH + B_WIDTH:, :], preferred_element_type=F32))
    x1 = _ln(alpha * x_ref[...] + (1.0 + g1_ref[0]) * mix) * lng_ref[...] + lnb_ref[...]
    x1_ref[...] = x1
    h2 = _ln(x1) * (1.0 + sc2_ref[0]) + sh2_ref[0]
    h2_ref[...] = h2

    hi = h2.astype(BF16)
    lo = (h2 - hi.astype(F32)).astype(BF16)
    nt = (((1,), (1,)), ((), ()))
    logits = (lax.dot_general(rwh_ref[...], hi, nt, preferred_element_type=F32)
              + lax.dot_general(rwh_ref[...], lo, nt, preferred_element_type=F32)
              + lax.dot_general(rwl_ref[...], hi, nt, preferred_element_type=F32))
    scores = jax.nn.sigmoid(logits)
    sel = scores + rb_ref[...]
    sel_rows = [sel[e:e + 1, :] for e in range(N_EXPERTS)]
    score_rows = [scores[e:e + 1, :] for e in range(N_EXPERTS)]
    e1, e2, gt1, gt2 = _route(sel_rows, score_rows)

    @pl.when((pl.program_id(0) == 0) & (pl.program_id(1) == 0))
    def _():
        count_ref[...] = jnp.zeros_like(count_ref)

    tm = h2.shape[0]
    erow = lax.broadcasted_iota(jnp.int32, (N_EXPERTS, tm), 0)
    oh1 = erow == e1
    oh2 = erow == e2
    both = (oh1 | oh2).astype(BF16)
    earlier = (lax.broadcasted_iota(jnp.int32, (tm, tm), 0)
               < lax.broadcasted_iota(jnp.int32, (tm, tm), 1)).astype(BF16)
    base = count_ref[:, 0:1]
    cnt = jnp.dot(both, earlier, preferred_element_type=F32) + base
    r1 = jnp.sum(jnp.where(oh1, cnt, 0.0), axis=0, keepdims=True)
    r2 = jnp.sum(jnp.where(oh2, cnt, 0.0), axis=0, keepdims=True)
    total = base + jnp.sum(both.astype(F32), axis=1, keepdims=True)
    count_ref[...] = jnp.broadcast_to(total, count_ref.shape)

    eidx_ref[...] = jnp.zeros_like(eidx_ref)
    gate_ref[...] = jnp.zeros_like(gate_ref)
    eidx_ref[0:1, :] = e1
    eidx_ref[1:2, :] = e2
    eidx_ref[2:3, :] = r1.astype(jnp.int32)
    eidx_ref[3:4, :] = r2.astype(jnp.int32)
    gate_ref[0:1, :] = gt1
    gate_ref[1:2, :] = gt2


def _outproj(a_out, att, c_tm, x2d, mod3, w_out_bf, ln_g, ln_b, rw_hi, rw_lo, rbias, bsz, t, alpha):
    n, d = x2d.shape
    tm = min(ROW_TILE, t)
    nt = t // tm
    cw = c_tm.shape[1] // bsz
    tok = lambda b, i: (b * nt + i, 0)
    const2 = lambda b, i: (0, 0)
    (o1, st1), (o2, st2), (o3, st3) = att
    head = jnp.arange(B_WIDTH) // B_HEAD_DIM
    expand = (jnp.arange(LANES)[:, None] == head[None, :]).astype(BF16)
    res_major = lambda dil, w: pl.BlockSpec((1, dil, tm // dil, w), lambda b, i: (b, 0, i, 0))
    modspec = lambda j: pl.BlockSpec((1, 1, d), lambda b, i: (b, 0, j))
    return pl.pallas_call(
        functools.partial(_outproj_kernel, alpha=alpha),
        grid=(bsz, nt),
        in_specs=[pl.BlockSpec((tm, A_WIDTH), tok)]
                 + [res_major(dil, B_WIDTH) for dil in DILATIONS]
                 + [res_major(dil, LANES) for dil in DILATIONS]
                 + [pl.BlockSpec((tm, cw), lambda b, i: (i, b)),
                  pl.BlockSpec((tm, d), tok),
                  modspec(2), modspec(3), modspec(4),
                  pl.BlockSpec((d, d), const2),
                  pl.BlockSpec((LANES, B_WIDTH), const2),
                  pl.BlockSpec((1, d), const2), pl.BlockSpec((1, d), const2),
                  pl.BlockSpec((N_EXPERTS, d), const2), pl.BlockSpec((N_EXPERTS, d), const2),
                  pl.BlockSpec((N_EXPERTS, 1), const2)],
        out_specs=[pl.BlockSpec((tm, d), tok), pl.BlockSpec((tm, d), tok),
                   pl.BlockSpec((SUBLANES, tm), lambda b, i: (0, b * nt + i)),
                   pl.BlockSpec((SUBLANES, tm), lambda b, i: (0, b * nt + i)),
                   pl.BlockSpec((N_EXPERTS, LANES), const2)],
        out_shape=[jax.ShapeDtypeStruct((n, d), F32), jax.ShapeDtypeStruct((n, d), F32),
                   jax.ShapeDtypeStruct((SUBLANES, n), jnp.int32),
                   jax.ShapeDtypeStruct((SUBLANES, n), F32),
                   jax.ShapeDtypeStruct((N_EXPERTS, LANES), F32)],
        scratch_shapes=[pltpu.VMEM((B_WIDTH // LANES, tm, LANES), F32),
                        pltpu.VMEM((B_WIDTH // LANES, tm, LANES), F32),
                        pltpu.VMEM((1, tm, LANES), F32), pltpu.VMEM((1, tm, LANES), F32)],
        compiler_params=_cparams(("arbitrary", "arbitrary")),
        name="outproj_ln_router",
    )(a_out, o1, o2, o3, st1, st2, st3, c_tm, x2d, mod3, mod3, mod3, w_out_bf, expand,
      ln_g, ln_b, rw_hi, rw_lo, rbias)


def _dispatch_plan(eidx, counts, n_tok, blk):
    e = eidx[:TOP_K]
    rank = eidx[TOP_K:2 * TOP_K]
    pcounts = (counts + blk - 1) // blk * blk
    pends = jnp.cumsum(pcounts)
    pstarts = pends - pcounts
    seg_start = jnp.zeros_like(e)
    for j in range(N_EXPERTS):
        seg_start = jnp.where(e == j, pstarts[j], seg_start)
    dest = (seg_start + rank).astype(jnp.int32)
    n_blk = n_tok * TOP_K // blk + N_EXPERTS
    blk_start = jnp.arange(n_blk, dtype=jnp.int32) * blk
    blk_e = jnp.minimum(jnp.sum((pends[None, :] <= blk_start[:, None]).astype(jnp.int32), axis=1),
                        N_EXPERTS - 1).astype(jnp.int32)
    n_used = (pends[-1] // blk).astype(jnp.int32).reshape(1)
    return dest, blk_e, n_used, pends.astype(jnp.int32), pcounts.astype(jnp.int32)


def _dispatch_kernel(pend_ref, pcnt_ref, dest_ref, h2_ref, xs_hbm, zbuf, sem, zsem):
    i = pl.program_id(0)
    blk = zbuf.shape[0]
    rows = h2_ref.shape[0]

    @pl.when(i == 0)
    def _():
        zbuf[...] = jnp.zeros_like(zbuf)
        for e in range(N_EXPERTS):
            @pl.when(pcnt_ref[e] > 0)
            def _():
                start = pl.multiple_of(pend_ref[e] - blk, blk)
                cp = pltpu.make_async_copy(zbuf, xs_hbm.at[pl.ds(start, blk), :], zsem)
                cp.start()
                cp.wait()
        n_blk = xs_hbm.shape[0] // blk
        for t in range(N_EXPERTS):
            tail = pend_ref[N_EXPERTS - 1] // blk + t

            @pl.when(tail < n_blk)
            def _():
                start = pl.multiple_of(tail * blk, blk)
                cp = pltpu.make_async_copy(zbuf, xs_hbm.at[pl.ds(start, blk), :], zsem)
                cp.start()
                cp.wait()

    def issue(s, c):
        for kk in range(TOP_K):
            pltpu.make_async_copy(h2_ref.at[pl.ds(s, 1), :],
                                  xs_hbm.at[pl.ds(dest_ref[0, 0, kk * rows + s], 1), :], sem).start()
        return c
    lax.fori_loop(0, rows, issue, 0, unroll=8)
    for kk in range(TOP_K):
        pltpu.make_async_copy(h2_ref, xs_hbm.at[pl.ds(0, rows), :], sem).wait()


def _expert_kernel(blk_e_ref, nused_ref, xs_ref, wg_ref, wu_ref, wd_ref, ys_ref):
    j = pl.program_id(0)

    @pl.when(j < nused_ref[0])
    def _():
        rows = xs_ref[...].astype(BF16)
        gt = jnp.dot(rows, wg_ref[0], preferred_element_type=F32)
        up = jnp.dot(rows, wu_ref[0], preferred_element_type=F32)
        hid = (gt * jax.nn.sigmoid(gt) * up).astype(BF16)
        ys_ref[...] = jnp.dot(hid, wd_ref[0], preferred_element_type=F32)

    @pl.when(j >= nused_ref[0])
    def _():
        ys_ref[...] = jnp.zeros_like(ys_ref)


def _moe(h2, eidx, counts, wg_bf, wu_bf, wd_bf):
    n_tok, d = h2.shape
    de = wg_bf.shape[-1]
    blk = min(MOE_BLK, n_tok)
    dest, blk_e, n_used, pends, pcounts = _dispatch_plan(eidx, counts, n_tok, blk)
    n_blk = blk_e.shape[0]
    cap = n_blk * blk

    rows = min(ROW_TILE, n_tok)
    nt = n_tok // rows
    dest_tiles = dest.reshape(TOP_K, nt, rows).transpose(1, 0, 2).reshape(nt, 1, TOP_K * rows)
    xs = pl.pallas_call(
        _dispatch_kernel,
        grid_spec=pltpu.PrefetchScalarGridSpec(
            num_scalar_prefetch=2,
            grid=(nt,),
            in_specs=[pl.BlockSpec((1, 1, TOP_K * rows), lambda i, pe, pc: (i, 0, 0),
                                   memory_space=pltpu.SMEM),
                      pl.BlockSpec((rows, d), lambda i, pe, pc: (i, 0))],
            out_specs=pl.BlockSpec(memory_space=pl.ANY),
            scratch_shapes=[pltpu.VMEM((blk, d), F32), pltpu.SemaphoreType.DMA, pltpu.SemaphoreType.DMA],
        ),
        out_shape=jax.ShapeDtypeStruct((cap, d), F32),
        compiler_params=_cparams(("arbitrary",)),
        name="moe_dispatch",
    )(pends, pcounts, dest_tiles, h2)

    used = lambda j, be, nu: (jnp.minimum(j, nu[0] - 1), 0)
    ys = pl.pallas_call(
        _expert_kernel,
        grid_spec=pltpu.PrefetchScalarGridSpec(
            num_scalar_prefetch=2,
            grid=(n_blk,),
            in_specs=[pl.BlockSpec((blk, d), used),
                      pl.BlockSpec((1, d, de), lambda j, be, nu: (be[j], 0, 0)),
                      pl.BlockSpec((1, d, de), lambda j, be, nu: (be[j], 0, 0)),
                      pl.BlockSpec((1, de, d), lambda j, be, nu: (be[j], 0, 0))],
            out_specs=pl.BlockSpec((blk, d), lambda j, be, nu: (j, 0)),
        ),
        out_shape=jax.ShapeDtypeStruct((cap, d), F32),
        compiler_params=_cparams(("arbitrary",)),
        name="moe_experts",
    )(blk_e, n_used, xs, wg_bf, wu_bf, wd_bf)
    return ys, dest_tiles


def _ffn_ln_kernel(dest_ref, x_ref, gate_ref, g2_ref, lng_ref, lnb_ref, ys_hbm, o_ref, ybuf, sem,
                   *, alpha):
    tm = x_ref.shape[0]

    def issue(s, c):
        pltpu.make_async_copy(ys_hbm.at[pl.ds(dest_ref[0, 0, s], 1), :],
                              ybuf.at[pl.ds(s, 1), :], sem).start()
        return c
    lax.fori_loop(0, TOP_K * tm, issue, 0, unroll=8)
    pltpu.make_async_copy(ys_hbm.at[pl.ds(0, TOP_K * tm), :], ybuf, sem).wait()

    gate = gate_ref[...]
    ffn = ybuf[0:tm, :] * gate[:, 0:1]
    for kk in range(1, TOP_K):
        ffn = ffn + ybuf[kk * tm:(kk + 1) * tm, :] * gate[:, kk:kk + 1]
    o_ref[...] = _ln(alpha * x_ref[...] + (1.0 + g2_ref[0]) * ffn) * lng_ref[...] + lnb_ref[...]


def _ffn_ln(x1, ys, dest_tiles, gates, mod3, ln_g, ln_b, bsz, t, alpha):
    n, d = x1.shape
    tm = dest_tiles.shape[2] // TOP_K
    nt = t // tm
    tok = lambda b, i, *_: (b * nt + i, 0)
    const2 = lambda b, i, *_: (0, 0)
    gate_cols = gates[:TOP_K].T
    return pl.pallas_call(
        functools.partial(_ffn_ln_kernel, alpha=alpha),
        grid=(bsz, nt),
        in_specs=[pl.BlockSpec((1, 1, TOP_K * tm), lambda b, i: (b * nt + i, 0, 0),
                               memory_space=pltpu.SMEM),
                  pl.BlockSpec((tm, d), tok), pl.BlockSpec((tm, TOP_K), tok),
                  pl.BlockSpec((1, 1, d), lambda b, i: (b, 0, 5)),
                  pl.BlockSpec((1, d), const2), pl.BlockSpec((1, d), const2),
                  pl.BlockSpec(memory_space=pl.ANY)],
        out_specs=pl.BlockSpec((tm, d), tok),
        out_shape=jax.ShapeDtypeStruct((n, d), F32),
        scratch_shapes=[pltpu.VMEM((TOP_K * tm, d), F32), pltpu.SemaphoreType.DMA],
        compiler_params=_cparams(("arbitrary", "arbitrary")),
        name="ffn_combine_ln",
    )(dest_tiles, x1, gate_cols, mod3, ln_g, ln_b, ys)


def kernel(x, c, positions, ada_w, ada_b, w_in, gm_ln_g, gm_ln_b, gm_ws, gm_bs, ssm_lam_re, ssm_lam_im, ssm_log_dt, ssm_b_re, ssm_b_im, ssm_c_re, ssm_c_im, ssm_d, glu_w, glu_b, w_out, ln1_g, ln1_b, router_w, router_bias, exp_w_gate, exp_w_up, exp_w_down, ln2_g, ln2_b):
    bsz, t, d = x.shape
    depth = ada_w.shape[0]
    alpha = (2.0 * depth) ** 0.25
    n = bsz * t

    mod = _adaln_mod(c.astype(F32), ada_w, ada_b)
    cos_t, s1_t, s2_t = _rope_tables(positions)
    rw_t = router_w.astype(F32).T
    rw_hi = rw_t.astype(BF16)
    rw_lo = (rw_t - rw_hi.astype(F32)).astype(BF16)
    rbias = router_bias.astype(F32).reshape(N_EXPERTS, 1)

    xf = x.astype(F32).reshape(n, d)
    for l in range(depth):
        mod3 = mod[l].reshape(bsz, 1, 6 * d)
        bs_full = jnp.repeat(gm_bs[l].T, A_HEAD_DIM, axis=1)
        a_out, s_in, qs, ks, vs = _inproj(
            xf, mod3, w_in[l].astype(BF16), cos_t, s1_t, s2_t,
            gm_ln_g[l].reshape(1, A_WIDTH), gm_ln_b[l].reshape(1, A_WIDTH), gm_ws[l], bs_full, bsz, t)
        att = [_att_branch(qs[g], ks[g], vs[g], dil) for g, dil in enumerate(DILATIONS)]
        bmat, cmat, a_re, a_im = _ssm_weights(ssm_lam_re[l], ssm_lam_im[l], ssm_log_dt[l],
                                              ssm_b_re[l], ssm_b_im[l], ssm_c_re[l], ssm_c_im[l])
        cw = bmat.shape[0]
        c_out = _ssm(s_in, bmat, cmat, a_re, a_im, ssm_d[l].reshape(1, cw),
                     glu_w[l].astype(BF16), glu_b[l].reshape(1, cw), bsz, t)
        x1, h2, eidx, gates, counts = _outproj(a_out, att, c_out, xf, mod3, w_out[l].astype(BF16),
                                               ln1_g[l].reshape(1, d), ln1_b[l].reshape(1, d),
                                               rw_hi, rw_lo, rbias, bsz, t, alpha)
        ys, dest_tiles = _moe(h2, eidx, counts[:, 0].astype(jnp.int32), exp_w_gate[l].astype(BF16),
                              exp_w_up[l].astype(BF16), exp_w_down[l].astype(BF16))
        xf = _ffn_ln(x1, ys, dest_tiles, gates, mod3, ln2_g[l].reshape(1, d), ln2_b[l].reshape(1, d),
                     bsz, t, alpha)
    return xf.reshape(bsz, t, d)
```

```python
import functools
import math

import jax
import jax.numpy as jnp
from jax import lax
from jax.experimental import pallas as pl
from jax.experimental.pallas import tpu as pltpu

F32 = jnp.float32
BF16 = jnp.bfloat16

A_HEADS = 4
A_HEAD_DIM = 64
A_WIDTH = A_HEADS * A_HEAD_DIM
CHUNK = 128
B_HEADS = 8
B_HEAD_DIM = 64
B_WIDTH = B_HEADS * B_HEAD_DIM
DILATIONS = (1, 4, 16)
ATT_BLK = 128
ROT_DIM = B_HEAD_DIM // 4
ROPE_THETA = 500000.0
SSM_GROUP = 16
SSM_STATE = 64
N_EXPERTS = 16
N_EXPERT_GROUPS = 4
EXPERTS_PER_GROUP = 4
TOP_K = 2
LN_EPS = 1e-5
NEG_INF = -1e30

LANES = 128
SUBLANES = 8
VMEM_LIMIT = 56 * 1024 * 1024
ROW_TILE = 512
SSM_STEPS = 64
MOE_BLK = 512
ATT_SUB_BLOCKS = 4


def _cparams(sem):
    return pltpu.CompilerParams(dimension_semantics=sem, vmem_limit_bytes=VMEM_LIMIT)


def _ln(x):
    mu = jnp.mean(x, axis=-1, keepdims=True)
    xc = x - mu
    var = jnp.mean(xc * xc, axis=-1, keepdims=True)
    return xc * lax.rsqrt(var + LN_EPS)


def _gelu_tanh(x):
    return 0.5 * x * (1.0 + jnp.tanh(math.sqrt(2.0 / math.pi) * (x + 0.044715 * (x * x * x))))


def _mod_kernel(c_ref, w_ref, b_ref, o_ref):
    c = c_ref[...]
    cond = c * jax.nn.sigmoid(c)
    o_ref[0] = jnp.dot(cond, w_ref[0], precision=lax.Precision.HIGHEST,
                       preferred_element_type=F32) + b_ref[0]


def _adaln_mod(c, ada_w, ada_b):
    depth, d, n = ada_w.shape
    bsz = c.shape[0]
    nb = d
    return pl.pallas_call(
        _mod_kernel,
        grid=(depth, n // nb),
        in_specs=[pl.BlockSpec((bsz, d), lambda l, j: (0, 0)),
                  pl.BlockSpec((1, d, nb), lambda l, j: (l, 0, j)),
                  pl.BlockSpec((1, 1, nb), lambda l, j: (l, 0, j))],
        out_specs=pl.BlockSpec((1, bsz, nb), lambda l, j: (l, 0, j)),
        out_shape=jax.ShapeDtypeStruct((depth, bsz, n), F32),
        compiler_params=_cparams(("arbitrary", "arbitrary")),
        name="adaln_mod",
    )(c, ada_w, ada_b.reshape(depth, 1, n))


def _rope_kernel(pos_ref, freq_ref, cos_ref, s1_ref, s2_ref):
    ang = pos_ref[...].astype(F32) * freq_ref[...]
    lane = lax.broadcasted_iota(jnp.int32, ang.shape, 1) % B_HEAD_DIM
    sn = jnp.sin(ang)
    cos_ref[...] = jnp.cos(ang)
    s1_ref[...] = jnp.where(lane < ROT_DIM // 2, -sn, 0.0)
    s2_ref[...] = jnp.where((lane >= ROT_DIM // 2) & (lane < ROT_DIM), sn, 0.0)


def _rope_tables(positions):
    n = positions.size
    half = ROT_DIM // 2
    lane = jnp.arange(LANES) % B_HEAD_DIM
    freqs = ROPE_THETA ** (-(lane % half).astype(F32) * 2.0 / ROT_DIM)
    freq_row = jnp.where(lane < ROT_DIM, freqs, 0.0).reshape(1, LANES).astype(F32)
    tm = min(2048, n)
    out = jax.ShapeDtypeStruct((n, LANES), F32)
    spec = pl.BlockSpec((tm, LANES), lambda i: (i, 0))
    return pl.pallas_call(
        _rope_kernel,
        grid=(n // tm,),
        in_specs=[pl.BlockSpec((tm, 1), lambda i: (i, 0)),
                  pl.BlockSpec((1, LANES), lambda i: (0, 0))],
        out_specs=[spec, spec, spec],
        out_shape=[out, out, out],
        compiler_params=_cparams(("arbitrary",)),
        name="rope_tables",
    )(positions.reshape(n, 1), freq_row)


def _inproj_kernel(x_ref, sh_ref, sc_ref, w_ref, cos_ref, s1_ref, s2_ref, lng_ref, lnb_ref,
                   ws_ref, bs_ref, a_ref, s_ref, *rest):
    n_dil = len(DILATIONS)
    q_outs, k_outs, v_outs = rest[:n_dil], rest[n_dil:2 * n_dil], rest[2 * n_dil:3 * n_dil]
    qs_ref, ks_ref, vs_ref = rest[3 * n_dil:]
    tm = x_ref.shape[0]
    h = _ln(x_ref[...]) * (1.0 + sc_ref[0]) + sh_ref[0]
    proj = jnp.dot(h.astype(BF16), w_ref[...], preferred_element_type=F32)

    uv = _gelu_tanh(proj[:, :2 * A_WIDTH])
    u = uv[:, :A_WIDTH]
    v = (_ln(uv[:, A_WIDTH:]) * lng_ref[...] + lnb_ref[...]).astype(BF16)
    row = lax.broadcasted_iota(jnp.int32, (CHUNK, CHUNK), 0)
    col = lax.broadcasted_iota(jnp.int32, (CHUNK, CHUNK), 1)
    head_of_lane = lax.broadcasted_iota(jnp.int32, (CHUNK, A_WIDTH), 1) // A_HEAD_DIM
    w_heads = [jnp.where(col <= row, ws_ref[hd], 0.0).astype(BF16) for hd in range(A_HEADS)]
    for cidx in range(tm // CHUNK):
        rows = slice(cidx * CHUNK, (cidx + 1) * CHUNK)
        vc = v[rows]
        sv = bs_ref[...]
        for hd in range(A_HEADS):
            full = jnp.dot(w_heads[hd], vc, preferred_element_type=F32)
            sv = sv + jnp.where(head_of_lane == hd, full, 0.0)
        a_ref[rows, :] = (u[rows] * sv).astype(a_ref.dtype)

    cos = cos_ref[...]
    s1 = s1_ref[...]
    s2 = s2_ref[...]
    q0 = 2 * A_WIDTH
    k0 = q0 + B_WIDTH
    v0 = k0 + B_WIDTH
    for j in range(B_WIDTH // LANES):
        for base, ref, scale in ((q0, qs_ref, B_HEAD_DIM ** -0.5), (k0, ks_ref, 1.0)):
            xs = proj[:, base + j * LANES: base + (j + 1) * LANES]
            rot = (xs * cos + pltpu.roll(xs, LANES - ROT_DIM // 2, 1) * s1
                   + pltpu.roll(xs, ROT_DIM // 2, 1) * s2)
            ref[j] = rot * scale
        vs_ref[j] = proj[:, v0 + j * LANES:v0 + (j + 1) * LANES]

    for src, outs in ((qs_ref, q_outs), (ks_ref, k_outs), (vs_ref, v_outs)):
        for dil, out in zip(DILATIONS, outs):
            for r in range(dil):
                for j in range(B_WIDTH // LANES):
                    out[0, r, :, j * LANES:(j + 1) * LANES] = (
                        src[j, pl.ds(r, tm // dil, stride=dil), :].astype(out.dtype))

    s_ref[...] = proj[:, v0 + B_WIDTH:].astype(s_ref.dtype)


def _inproj(x2d, mod3, w_in_bf, cos_t, s1_t, s2_t, ln_g, ln_b, ws, bs_full, bsz, t):
    n, d = x2d.shape
    tm = min(ROW_TILE, t)
    nt = t // tm
    pw = w_in_bf.shape[1]
    c_width = pw - 2 * A_WIDTH - 3 * B_WIDTH
    tok = lambda b, i: (b * nt + i, 0)
    const2 = lambda b, i: (0, 0)
    outs = [jax.ShapeDtypeStruct((n, A_WIDTH), BF16),
            jax.ShapeDtypeStruct((t, bsz * c_width), BF16)]
    out_specs = [pl.BlockSpec((tm, A_WIDTH), tok),
                 pl.BlockSpec((tm, c_width), lambda b, i: (i, b))]
    for _ in range(3):
        for dil in DILATIONS:
            outs.append(jax.ShapeDtypeStruct((bsz, dil, t // dil, B_WIDTH), BF16))
            out_specs.append(pl.BlockSpec((1, dil, tm // dil, B_WIDTH), lambda b, i: (b, 0, i, 0)))
    res = pl.pallas_call(
        _inproj_kernel,
        grid=(bsz, nt),
        in_specs=[pl.BlockSpec((tm, d), tok),
                  pl.BlockSpec((1, 1, d), lambda b, i: (b, 0, 0)),
                  pl.BlockSpec((1, 1, d), lambda b, i: (b, 0, 1)),
                  pl.BlockSpec((d, pw), const2),
                  pl.BlockSpec((tm, LANES), tok),
                  pl.BlockSpec((tm, LANES), tok),
                  pl.BlockSpec((tm, LANES), tok),
                  pl.BlockSpec((1, A_WIDTH), const2),
                  pl.BlockSpec((1, A_WIDTH), const2),
                  pl.BlockSpec((A_HEADS, CHUNK, CHUNK), lambda b, i: (0, 0, 0)),
                  pl.BlockSpec((CHUNK, A_WIDTH), const2)],
        out_specs=out_specs,
        out_shape=outs,
        scratch_shapes=[pltpu.VMEM((B_WIDTH // LANES, tm, LANES), F32)] * 3,
        compiler_params=_cparams(("arbitrary", "arbitrary")),
        name="inproj_gmlp_rope",
    )(x2d, mod3, mod3, w_in_bf, cos_t, s1_t, s2_t, ln_g, ln_b, ws, bs_full)
    n_dil = len(DILATIONS)
    a_out, s_in = res[0], res[1]
    return a_out, s_in, res[2:2 + n_dil], res[2 + n_dil:2 + 2 * n_dil], res[2 + 2 * n_dil:]


def _att_kernel(q_ref, k_ref, v_ref, o_ref, st_ref, kcat_ref, vaug_ref):
    i = pl.program_id(2)
    blk = ATT_BLK
    n_sub = q_ref.shape[2] // blk
    n_pairs = B_WIDTH // LANES
    last = slice(n_sub * blk, (n_sub + 1) * blk)

    @pl.when(i == 0)
    def _():
        kcat_ref[0:blk, :] = jnp.zeros((blk, B_WIDTH), BF16)
        vaug_ref[:, 0:blk, 0:LANES] = jnp.zeros((n_pairs, blk, LANES), BF16)
        vaug_ref[:, :, LANES:] = jnp.ones((n_pairs, (n_sub + 1) * blk, LANES), BF16)

    @pl.when(i > 0)
    def _():
        kcat_ref[0:blk, :] = kcat_ref[last, :]
        vaug_ref[:, 0:blk, 0:LANES] = vaug_ref[:, last, 0:LANES]

    kcat_ref[blk:, :] = k_ref[0, 0]
    for pair in range(n_pairs):
        vaug_ref[pair, blk:, 0:LANES] = v_ref[0, 0, :, pair * LANES:(pair + 1) * LANES]

    qi = lax.broadcasted_iota(jnp.int32, (blk, 2 * blk), 0)
    kk = lax.broadcasted_iota(jnp.int32, (blk, 2 * blk), 1)
    band = (kk >= qi) & (kk <= qi + blk)
    first = band & ((i > 0) | (kk >= blk))
    lane = lax.broadcasted_iota(jnp.int32, (blk, LANES), 1)
    low_half = lane < B_HEAD_DIM
    nt = (((1,), (1,)), ((), ()))
    for sub in range(n_sub):
        rows = slice(sub * blk, (sub + 1) * blk)
        keys = slice(sub * blk, (sub + 2) * blk)
        mask = first if sub == 0 else band
        stats = jnp.zeros((blk, LANES), F32)
        for pair in range(n_pairs):
            lanes = slice(pair * LANES, (pair + 1) * LANES)
            q2 = q_ref[0, 0, rows, lanes]
            kslab = kcat_ref[keys, lanes]
            res = []
            for hh in range(2):
                hd = 2 * pair + hh
                qm = jnp.where(low_half if hh == 0 else jnp.logical_not(low_half), q2, jnp.zeros_like(q2))
                s = lax.dot_general(qm, kslab, nt, preferred_element_type=F32)
                s = jnp.where(mask, s, NEG_INF)
                m = jnp.max(jnp.maximum(s[:, :blk], s[:, blk:]), axis=1, keepdims=True)
                p = jnp.exp(s - m).astype(BF16)
                ov = jnp.dot(p, vaug_ref[pair, keys, :], preferred_element_type=F32)
                den = ov[:, LANES:]
                res.append(ov[:, :LANES] / den)
                stats = jnp.where(lane == hd, m, stats)
                stats = jnp.where(lane == B_HEADS + hd, den, stats)
            o_ref[0, 0, rows, lanes] = jnp.where(low_half, res[0], res[1]).astype(o_ref.dtype)
        st_ref[0, 0, rows, :] = stats


def _att_branch(q, k, v, dil):
    bsz, _, rows, _ = q.shape
    n_sub = min(ATT_SUB_BLOCKS, rows // ATT_BLK)
    step_rows = n_sub * ATT_BLK
    cur = lambda b, r, i: (b, r, i, 0)
    blk = (1, 1, step_rows, B_WIDTH)
    sblk = (1, 1, step_rows, LANES)
    return pl.pallas_call(
        _att_kernel,
        grid=(bsz, dil, rows // step_rows),
        in_specs=[pl.BlockSpec(blk, cur), pl.BlockSpec(blk, cur), pl.BlockSpec(blk, cur)],
        out_specs=[pl.BlockSpec(blk, cur), pl.BlockSpec(sblk, cur)],
        out_shape=[jax.ShapeDtypeStruct((bsz, dil, rows, B_WIDTH), BF16),
                   jax.ShapeDtypeStruct((bsz, dil, rows, LANES), F32)],
        scratch_shapes=[pltpu.VMEM((step_rows + ATT_BLK, B_WIDTH), BF16),
                        pltpu.VMEM((B_WIDTH // LANES, step_rows + ATT_BLK, 2 * LANES), BF16)],
        compiler_params=_cparams(("arbitrary", "arbitrary", "arbitrary")),
        name=f"dilated_attn_d{dil}",
    )(q, k, v)


def _ssm_kernel(u_ref, bmat_ref, cmat_ref, are_ref, aim_ref, dskip_ref, gw_ref, gb_ref,
                o_ref, state_ref, xbuf_ref):
    bsz, two_n = state_ref.shape
    n_state = two_n // 2
    steps = u_ref.shape[0] // bsz

    @pl.when(pl.program_id(0) == 0)
    def _():
        state_ref[...] = jnp.zeros_like(state_ref)

    u = u_ref[...]
    xbuf_ref[...] = jnp.dot(u, bmat_ref[...], preferred_element_type=F32)
    a_re = jnp.broadcast_to(are_ref[...], (bsz, n_state))
    a_im = jnp.broadcast_to(aim_ref[...], (bsz, n_state))

    def step(tt, carry):
        x_re, x_im = carry
        r0 = pl.multiple_of(tt * bsz, bsz)
        b_re = xbuf_ref[pl.ds(r0, bsz), :n_state]
        b_im = xbuf_ref[pl.ds(r0, bsz), n_state:]
        n_re = a_re * x_re - a_im * x_im + b_re
        n_im = a_re * x_im + a_im * x_re + b_im
        xbuf_ref[pl.ds(r0, bsz), :n_state] = n_re
        xbuf_ref[pl.ds(r0, bsz), n_state:] = n_im
        return n_re, n_im

    x_re, x_im = lax.fori_loop(0, steps, step, (state_ref[:, :n_state], state_ref[:, n_state:]))
    state_ref[:, :n_state] = x_re
    state_ref[:, n_state:] = x_im

    y = jnp.dot(xbuf_ref[...].astype(BF16), cmat_ref[...], preferred_element_type=F32)
    y = _gelu_tanh(y + dskip_ref[...] * u.astype(F32))
    gate = jax.nn.sigmoid(jnp.dot(y.astype(BF16), gw_ref[...], preferred_element_type=F32) + gb_ref[...])
    o_ref[...] = (y * gate).astype(o_ref.dtype)


def _ssm_weights(lam_re, lam_im, log_dt, b_re, b_im, c_re, c_im):
    g, n_st = lam_re.shape
    cg = b_re.shape[-1]
    dt = jnp.exp(log_dt)[:, None]
    mag = jnp.exp(lam_re * dt)
    ab_re = mag * jnp.cos(lam_im * dt)
    ab_im = mag * jnp.sin(lam_im * dt)
    nr = ab_re - 1.0
    ni = ab_im
    mod2 = lam_re * lam_re + lam_im * lam_im
    f_re = (nr * lam_re + ni * lam_im) / mod2
    f_im = (ni * lam_re - nr * lam_im) / mod2
    bb_re = f_re[..., None] * b_re - f_im[..., None] * b_im
    bb_im = f_re[..., None] * b_im + f_im[..., None] * b_re
    eye = jnp.eye(g, dtype=F32)
    bm_re = jnp.einsum('gnc,gh->gchn', bb_re, eye).reshape(g * cg, g * n_st)
    bm_im = jnp.einsum('gnc,gh->gchn', bb_im, eye).reshape(g * cg, g * n_st)
    bmat = jnp.concatenate([bm_re, bm_im], axis=1)
    cm_re = jnp.einsum('gcn,gh->gnhc', c_re, eye).reshape(g * n_st, g * cg)
    cm_im = jnp.einsum('gcn,gh->gnhc', c_im, eye).reshape(g * n_st, g * cg)
    cmat = jnp.concatenate([cm_re, -cm_im], axis=0)
    return (bmat.astype(BF16), cmat.astype(BF16),
            ab_re.reshape(1, g * n_st), ab_im.reshape(1, g * n_st))


def _ssm(u_tm, bmat, cmat, a_re, a_im, d_skip, glu_w_bf, glu_b, bsz, t):
    cw = bmat.shape[0]
    two_n = bmat.shape[1]
    steps = min(SSM_STEPS, t)
    rows = steps * bsz
    u2 = u_tm.reshape(t * bsz, cw)
    const = lambda i: (0, 0)
    return pl.pallas_call(
        _ssm_kernel,
        grid=(t // steps,),
        in_specs=[pl.BlockSpec((rows, cw), lambda i: (i, 0)),
                  pl.BlockSpec((cw, two_n), const),
                  pl.BlockSpec((two_n, cw), const),
                  pl.BlockSpec((1, two_n // 2), const),
                  pl.BlockSpec((1, two_n // 2), const),
                  pl.BlockSpec((1, cw), const),
                  pl.BlockSpec((cw, cw), const),
                  pl.BlockSpec((1, cw), const)],
        out_specs=pl.BlockSpec((rows, cw), lambda i: (i, 0)),
        out_shape=jax.ShapeDtypeStruct((t * bsz, cw), BF16),
        scratch_shapes=[pltpu.VMEM((bsz, two_n), F32), pltpu.VMEM((rows, two_n), F32)],
        compiler_params=_cparams(("arbitrary",)),
        name="s5_scan_glu",
    )(u2, bmat, cmat, a_re, a_im, d_skip, glu_w_bf, glu_b).reshape(t, bsz * cw)


def _route(sel, scores):
    gs = []
    for g in range(N_EXPERT_GROUPS):
        a, b, c, d = sel[4 * g: 4 * g + 4]
        hi1, lo1 = jnp.maximum(a, b), jnp.minimum(a, b)
        hi2, lo2 = jnp.maximum(c, d), jnp.minimum(c, d)
        gs.append(jnp.maximum(hi1, hi2) + jnp.maximum(jnp.minimum(hi1, hi2), jnp.maximum(lo1, lo2)))
    g_idx = jnp.zeros(gs[0].shape, jnp.int32)
    best = gs[0]
    for g in range(1, N_EXPERT_GROUPS):
        better = gs[g] > best
        g_idx = jnp.where(better, g, g_idx)
        best = jnp.where(better, gs[g], best)

    def pick(rows, j):
        out = rows[j]
        for g in range(1, N_EXPERT_GROUPS):
            out = jnp.where(g_idx == g, rows[4 * g + j], out)
        return out

    v = [pick(sel, j) for j in range(EXPERTS_PER_GROUP)]
    s = [pick(scores, j) for j in range(EXPERTS_PER_GROUP)]
    i1 = jnp.zeros(g_idx.shape, jnp.int32)
    b1, g1 = v[0], s[0]
    for j in range(1, EXPERTS_PER_GROUP):
        better = v[j] > b1
        i1 = jnp.where(better, j, i1)
        b1 = jnp.where(better, v[j], b1)
        g1 = jnp.where(better, s[j], g1)
    i2 = jnp.full(g_idx.shape, -1, jnp.int32)
    b2 = jnp.zeros_like(b1)
    g2 = jnp.zeros_like(g1)
    for j in range(EXPERTS_PER_GROUP):
        better = (i1 != j) & ((i2 < 0) | (v[j] > b2))
        i2 = jnp.where(better, j, i2)
        b2 = jnp.where(better, v[j], b2)
        g2 = jnp.where(better, s[j], g2)
    tot = g1 + g2
    base = g_idx * EXPERTS_PER_GROUP
    return base + i1, base + i2, g1 / tot, g2 / tot


def _natural_order(ref, scr_ref, dil):
    if dil == 1:
        return ref[0, 0].astype(F32)
    per = ref.shape[2]
    n_slab = scr_ref.shape[0]
    for r in range(dil):
        for j in range(n_slab):
            scr_ref[j, pl.ds(r, per, stride=dil), :] = ref[0, r, :, j * LANES:(j + 1) * LANES].astype(F32)
    if n_slab == 1:
        return scr_ref[0]
    return jnp.concatenate([scr_ref[j] for j in range(n_slab)], axis=1)


def _outproj_kernel(a_ref, o1_ref, o2_ref, o3_ref, st1_ref, st2_ref, st3_ref,
                    c_ref, x_ref, g1_ref, sh2_ref, sc2_ref, wout_ref, expand_ref, lng_ref, lnb_ref,
                    rwh_ref, rwl_ref, rb_ref, x1_ref, h2_ref, eidx_ref, gate_ref, count_ref,
                    oscr2_ref, oscr3_ref, sscr2_ref, sscr3_ref, *, alpha):
    o_refs = (o1_ref, o2_ref, o3_ref)
    st_refs = (st1_ref, st2_ref, st3_ref)
    o_scr = (None, oscr2_ref, oscr3_ref)
    st_scr = (None, sscr2_ref, sscr3_ref)
    ms = [_natural_order(st_refs[g], st_scr[g], DILATIONS[g]) for g in range(len(DILATIONS))]
    dens = [pltpu.roll(m, LANES - B_HEADS, 1) for m in ms]
    mx = jnp.maximum(jnp.maximum(ms[0], ms[1]), ms[2])
    ws = [dens[g] * jnp.exp(ms[g] - mx) for g in range(len(DILATIONS))]
    lane = lax.broadcasted_iota(jnp.int32, mx.shape, 1)
    tot = jnp.where(lane < B_HEADS, ws[0] + ws[1] + ws[2], 1.0)
    expand = expand_ref[...]

    def widen(w):
        wn = jnp.where(lane < B_HEADS, w / tot, 0.0)
        hi = wn.astype(BF16)
        lo = (wn - hi.astype(F32)).astype(BF16)
        return (jnp.dot(hi, expand, preferred_element_type=F32)
                + jnp.dot(lo, expand, preferred_element_type=F32))

    b_out = widen(ws[0]) * _natural_order(o_refs[0], o_scr[0], DILATIONS[0])
    for g in range(1, len(DILATIONS)):
        b_out = b_out + widen(ws[g]) * _natural_order(o_refs[g], o_scr[g], DILATIONS[g])
    b_out = b_out.astype(BF16)

    wout = wout_ref
    mix = (jnp.dot(a_ref[...], wout[:A_WIDTH, :], preferred_element_type=F32)
           + jnp.dot(b_out, wout[A_WIDTH:A_WIDTH + B_WIDTH, :], preferred_element_type=F32)
           + jnp.dot(c_ref[...], wout[A_WIDTH + B_WIDTH:, :], preferred_element_type=F32))
    x1 = _ln(alpha * x_ref[...] + (1.0 + g1_ref[0]) * mix) * lng_ref[...] + lnb_ref[...]
    x1_ref[...] = x1
    h2 = _ln(x1) * (1.0 + sc2_ref[0]) + sh2_ref[0]
    h2_ref[...] = h2

    hi = h2.astype(BF16)
    lo = (h2 - hi.astype(F32)).astype(BF16)
    nt = (((1,), (1,)), ((), ()))
    logits = (lax.dot_general(rwh_ref[...], hi, nt, preferred_element_type=F32)
              + lax.dot_general(rwh_ref[...], lo, nt, preferred_element_type=F32)
              + lax.dot_general(rwl_ref[...], hi, nt, preferred_element_type=F32))
    scores = jax.nn.sigmoid(logits)
    sel = scores + rb_ref[...]
    sel_rows = [sel[e:e + 1, :] for e in range(N_EXPERTS)]
    score_rows = [scores[e:e + 1, :] for e in range(N_EXPERTS)]
    e1, e2, gt1, gt2 = _route(sel_rows, score_rows)

    @pl.when((pl.program_id(0) == 0) & (pl.program_id(1) == 0))
    def _():
        count_ref[...] = jnp.zeros_like(count_ref)

    tm = h2.shape[0]
    erow = lax.broadcasted_iota(jnp.int32, (N_EXPERTS, tm), 0)
    oh1 = erow == e1
    oh2 = erow == e2
    both = (oh1 | oh2).astype(BF16)
    earlier = (lax.broadcasted_iota(jnp.int32, (tm, tm), 0)
               < lax.broadcasted_iota(jnp.int32, (tm, tm), 1)).astype(BF16)
    base = count_ref[:, 0:1]
    cnt = jnp.dot(both, earlier, preferred_element_type=F32) + base
    r1 = jnp.sum(jnp.where(oh1, cnt, 0.0), axis=0, keepdims=True)
    r2 = jnp.sum(jnp.where(oh2, cnt, 0.0), axis=0, keepdims=True)
    total = base + jnp.sum(both.astype(F32), axis=1, keepdims=True)
    count_ref[...] = jnp.broadcast_to(total, count_ref.shape)

    eidx_ref[...] = jnp.zeros_like(eidx_ref)
    gate_ref[...] = jnp.zeros_like(gate_ref)
    eidx_ref[0:1, :] = e1
    eidx_ref[1:2, :] = e2
    eidx_ref[2:3, :] = r1.astype(jnp.int32)
    eidx_ref[3:4, :] = r2.astype(jnp.int32)
    gate_ref[0:1, :] = gt1
    gate_ref[1:2, :] = gt2


def _outproj(a_out, att, c_tm, x2d, mod3, w_out_bf, ln_g, ln_b, rw_hi, rw_lo, rbias, bsz, t, alpha):
    n, d = x2d.shape
    tm = min(ROW_TILE, t)
    nt = t // tm
    cw = c_tm.shape[1] // bsz
    tok = lambda b, i: (b * nt + i, 0)
    const2 = lambda b, i: (0, 0)
    (o1, st1), (o2, st2), (o3, st3) = att
    head = jnp.arange(B_WIDTH) // B_HEAD_DIM
    expand = (jnp.arange(LANES)[:, None] == head[None, :]).astype(BF16)
    res_major = lambda dil, w: pl.BlockSpec((1, dil, tm // dil, w), lambda b, i: (b, 0, i, 0))
    modspec = lambda j: pl.BlockSpec((1, 1, d), lambda b, i: (b, 0, j))
    return pl.pallas_call(
        functools.partial(_outproj_kernel, alpha=alpha),
        grid=(bsz, nt),
        in_specs=[pl.BlockSpec((tm, A_WIDTH), tok)]
                 + [res_major(dil, B_WIDTH) for dil in DILATIONS]
                 + [res_major(dil, LANES) for dil in DILATIONS]
                 + [pl.BlockSpec((tm, cw), lambda b, i: (i, b)),
                  pl.BlockSpec((tm, d), tok),
                  modspec(2), modspec(3), modspec(4),
                  pl.BlockSpec((d, d), const2),
                  pl.BlockSpec((LANES, B_WIDTH), const2),
                  pl.BlockSpec((1, d), const2), pl.BlockSpec((1, d), const2),
                  pl.BlockSpec((N_EXPERTS, d), const2), pl.BlockSpec((N_EXPERTS, d), const2),
                  pl.BlockSpec((N_EXPERTS, 1), const2)],
        out_specs=[pl.BlockSpec((tm, d), tok), pl.BlockSpec((tm, d), tok),
                   pl.BlockSpec((SUBLANES, tm), lambda b, i: (0, b * nt + i)),
                   pl.BlockSpec((SUBLANES, tm), lambda b, i: (0, b * nt + i)),
                   pl.BlockSpec((N_EXPERTS, LANES), const2)],
        out_shape=[jax.ShapeDtypeStruct((n, d), F32), jax.ShapeDtypeStruct((n, d), F32),
                   jax.ShapeDtypeStruct((SUBLANES, n), jnp.int32),
                   jax.ShapeDtypeStruct((SUBLANES, n), F32),
                   jax.ShapeDtypeStruct((N_EXPERTS, LANES), F32)],
        scratch_shapes=[pltpu.VMEM((B_WIDTH // LANES, tm, LANES), F32),
                        pltpu.VMEM((B_WIDTH // LANES, tm, LANES), F32),
                        pltpu.VMEM((1, tm, LANES), F32), pltpu.VMEM((1, tm, LANES), F32)],
        compiler_params=_cparams(("arbitrary", "arbitrary")),
        name="outproj_ln_router",
    )(a_out, o1, o2, o3, st1, st2, st3, c_tm, x2d, mod3, mod3, mod3, w_out_bf, expand,
      ln_g, ln_b, rw_hi, rw_lo, rbias)


def _dispatch_plan(eidx, counts, n_tok, blk):
    e = eidx[:TOP_K]
    rank = eidx[TOP_K:2 * TOP_K]
    pcounts = (counts + blk - 1) // blk * blk
    pends = jnp.cumsum(pcounts)
    pstarts = pends - pcounts
    seg_start = jnp.zeros_like(e)
    for j in range(N_EXPERTS):
        seg_start = jnp.where(e == j, pstarts[j], seg_start)
    dest = (seg_start + rank).astype(jnp.int32)
    n_blk = n_tok * TOP_K // blk + N_EXPERTS
    blk_start = jnp.arange(n_blk, dtype=jnp.int32) * blk
    blk_e = jnp.minimum(jnp.sum((pends[None, :] <= blk_start[:, None]).astype(jnp.int32), axis=1),
                        N_EXPERTS - 1).astype(jnp.int32)
    n_used = (pends[-1] // blk).astype(jnp.int32).reshape(1)
    return dest, blk_e, n_used, pends.astype(jnp.int32), pcounts.astype(jnp.int32)


def _dispatch_kernel(pend_ref, pcnt_ref, dest_ref, h2_ref, xs_hbm, zbuf, sem, zsem):
    i = pl.program_id(0)
    blk = zbuf.shape[0]
    rows = h2_ref.shape[0]

    @pl.when(i == 0)
    def _():
        zbuf[...] = jnp.zeros_like(zbuf)
        for e in range(N_EXPERTS):
            @pl.when(pcnt_ref[e] > 0)
            def _():
                start = pl.multiple_of(pend_ref[e] - blk, blk)
                cp = pltpu.make_async_copy(zbuf, xs_hbm.at[pl.ds(start, blk), :], zsem)
                cp.start()
                cp.wait()
        n_blk = xs_hbm.shape[0] // blk
        for t in range(N_EXPERTS):
            tail = pend_ref[N_EXPERTS - 1] // blk + t

            @pl.when(tail < n_blk)
            def _():
                start = pl.multiple_of(tail * blk, blk)
                cp = pltpu.make_async_copy(zbuf, xs_hbm.at[pl.ds(start, blk), :], zsem)
                cp.start()
                cp.wait()

    def issue(s, c):
        for kk in range(TOP_K):
            pltpu.make_async_copy(h2_ref.at[pl.ds(s, 1), :],
                                  xs_hbm.at[pl.ds(dest_ref[0, 0, kk * rows + s], 1), :], sem).start()
        return c
    lax.fori_loop(0, rows, issue, 0, unroll=8)
    for kk in range(TOP_K):
        pltpu.make_async_copy(h2_ref, xs_hbm.at[pl.ds(0, rows), :], sem).wait()


def _expert_kernel(blk_e_ref, nused_ref, xs_ref, wg_ref, wu_ref, wd_ref, ys_ref):
    j = pl.program_id(0)

    @pl.when(j < nused_ref[0])
    def _():
        rows = xs_ref[...].astype(BF16)
        gt = jnp.dot(rows, wg_ref[0], preferred_element_type=F32)
        up = jnp.dot(rows, wu_ref[0], preferred_element_type=F32)
        hid = (gt * jax.nn.sigmoid(gt) * up).astype(BF16)
        ys_ref[...] = jnp.dot(hid, wd_ref[0], preferred_element_type=F32)

    @pl.when(j >= nused_ref[0])
    def _():
        ys_ref[...] = jnp.zeros_like(ys_ref)


def _moe(h2, eidx, counts, wg_bf, wu_bf, wd_bf):
    n_tok, d = h2.shape
    de = wg_bf.shape[-1]
    blk = min(MOE_BLK, n_tok)
    dest, blk_e, n_used, pends, pcounts = _dispatch_plan(eidx, counts, n_tok, blk)
    n_blk = blk_e.shape[0]
    cap = n_blk * blk

    rows = min(ROW_TILE, n_tok)
    nt = n_tok // rows
    dest_tiles = dest.reshape(TOP_K, nt, rows).transpose(1, 0, 2).reshape(nt, 1, TOP_K * rows)
    xs = pl.pallas_call(
        _dispatch_kernel,
        grid_spec=pltpu.PrefetchScalarGridSpec(
            num_scalar_prefetch=2,
            grid=(nt,),
            in_specs=[pl.BlockSpec((1, 1, TOP_K * rows), lambda i, pe, pc: (i, 0, 0),
                                   memory_space=pltpu.SMEM),
                      pl.BlockSpec((rows, d), lambda i, pe, pc: (i, 0))],
            out_specs=pl.BlockSpec(memory_space=pl.ANY),
            scratch_shapes=[pltpu.VMEM((blk, d), F32), pltpu.SemaphoreType.DMA, pltpu.SemaphoreType.DMA],
        ),
        out_shape=jax.ShapeDtypeStruct((cap, d), F32),
        compiler_params=_cparams(("arbitrary",)),
        name="moe_dispatch",
    )(pends, pcounts, dest_tiles, h2)

    used = lambda j, be, nu: (jnp.minimum(j, nu[0] - 1), 0)
    ys = pl.pallas_call(
        _expert_kernel,
        grid_spec=pltpu.PrefetchScalarGridSpec(
            num_scalar_prefetch=2,
            grid=(n_blk,),
            in_specs=[pl.BlockSpec((blk, d), used),
                      pl.BlockSpec((1, d, de), lambda j, be, nu: (be[j], 0, 0)),
                      pl.BlockSpec((1, d, de), lambda j, be, nu: (be[j], 0, 0)),
                      pl.BlockSpec((1, de, d), lambda j, be, nu: (be[j], 0, 0))],
            out_specs=pl.BlockSpec((blk, d), lambda j, be, nu: (j, 0)),
        ),
        out_shape=jax.ShapeDtypeStruct((cap, d), F32),
        compiler_params=_cparams(("arbitrary",)),
        name="moe_experts",
    )(blk_e, n_used, xs, wg_bf, wu_bf, wd_bf)
    return ys, dest_tiles


def _ffn_ln_kernel(dest_ref, x_ref, gate_ref, g2_ref, lng_ref, lnb_ref, ys_hbm, o_ref, ybuf, sem,
                   *, alpha):
    tm = x_ref.shape[0]

    def issue(s, c):
        pltpu.make_async_copy(ys_hbm.at[pl.ds(dest_ref[0, 0, s], 1), :],
                              ybuf.at[pl.ds(s, 1), :], sem).start()
        return c
    lax.fori_loop(0, TOP_K * tm, issue, 0, unroll=8)
    pltpu.make_async_copy(ys_hbm.at[pl.ds(0, TOP_K * tm), :], ybuf, sem).wait()

    gate = gate_ref[...]
    ffn = ybuf[0:tm, :] * gate[:, 0:1]
    for kk in range(1, TOP_K):
        ffn = ffn + ybuf[kk * tm:(kk + 1) * tm, :] * gate[:, kk:kk + 1]
    o_ref[...] = _ln(alpha * x_ref[...] + (1.0 + g2_ref[0]) * ffn) * lng_ref[...] + lnb_ref[...]


def _ffn_ln(x1, ys, dest_tiles, gates, mod3, ln_g, ln_b, bsz, t, alpha):
    n, d = x1.shape
    tm = dest_tiles.shape[2] // TOP_K
    nt = t // tm
    tok = lambda b, i, *_: (b * nt + i, 0)
    const2 = lambda b, i, *_: (0, 0)
    gate_cols = gates[:TOP_K].T
    return pl.pallas_call(
        functools.partial(_ffn_ln_kernel, alpha=alpha),
        grid=(bsz, nt),
        in_specs=[pl.BlockSpec((1, 1, TOP_K * tm), lambda b, i: (b * nt + i, 0, 0),
                               memory_space=pltpu.SMEM),
                  pl.BlockSpec((tm, d), tok), pl.BlockSpec((tm, TOP_K), tok),
                  pl.BlockSpec((1, 1, d), lambda b, i: (b, 0, 5)),
                  pl.BlockSpec((1, d), const2), pl.BlockSpec((1, d), const2),
                  pl.BlockSpec(memory_space=pl.ANY)],
        out_specs=pl.BlockSpec((tm, d), tok),
        out_shape=jax.ShapeDtypeStruct((n, d), F32),
        scratch_shapes=[pltpu.VMEM((TOP_K * tm, d), F32), pltpu.SemaphoreType.DMA],
        compiler_params=_cparams(("arbitrary", "arbitrary")),
        name="ffn_combine_ln",
    )(dest_tiles, x1, gate_cols, mod3, ln_g, ln_b, ys)


def kernel(x, c, positions, ada_w, ada_b, w_in, gm_ln_g, gm_ln_b, gm_ws, gm_bs, ssm_lam_re, ssm_lam_im, ssm_log_dt, ssm_b_re, ssm_b_im, ssm_c_re, ssm_c_im, ssm_d, glu_w, glu_b, w_out, ln1_g, ln1_b, router_w, router_bias, exp_w_gate, exp_w_up, exp_w_down, ln2_g, ln2_b):
    bsz, t, d = x.shape
    depth = ada_w.shape[0]
    alpha = (2.0 * depth) ** 0.25
    n = bsz * t

    mod = _adaln_mod(c.astype(F32), ada_w, ada_b)
    cos_t, s1_t, s2_t = _rope_tables(positions)
    rw_t = router_w.astype(F32).T
    rw_hi = rw_t.astype(BF16)
    rw_lo = (rw_t - rw_hi.astype(F32)).astype(BF16)
    rbias = router_bias.astype(F32).reshape(N_EXPERTS, 1)

    xf = x.astype(F32).reshape(n, d)
    for l in range(depth):
        mod3 = mod[l].reshape(bsz, 1, 6 * d)
        bs_full = jnp.repeat(gm_bs[l].T, A_HEAD_DIM, axis=1)
        a_out, s_in, qs, ks, vs = _inproj(
            xf, mod3, w_in[l].astype(BF16), cos_t, s1_t, s2_t,
            gm_ln_g[l].reshape(1, A_WIDTH), gm_ln_b[l].reshape(1, A_WIDTH), gm_ws[l], bs_full, bsz, t)
        att = [_att_branch(qs[g], ks[g], vs[g], dil) for g, dil in enumerate(DILATIONS)]
        bmat, cmat, a_re, a_im = _ssm_weights(ssm_lam_re[l], ssm_lam_im[l], ssm_log_dt[l],
                                              ssm_b_re[l], ssm_b_im[l], ssm_c_re[l], ssm_c_im[l])
        cw = bmat.shape[0]
        c_out = _ssm(s_in, bmat, cmat, a_re, a_im, ssm_d[l].reshape(1, cw),
                     glu_w[l].astype(BF16), glu_b[l].reshape(1, cw), bsz, t)
        x1, h2, eidx, gates, counts = _outproj(a_out, att, c_out, xf, mod3, w_out[l].astype(BF16),
                                               ln1_g[l].reshape(1, d), ln1_b[l].reshape(1, d),
                                               rw_hi, rw_lo, rbias, bsz, t, alpha)
        ys, dest_tiles = _moe(h2, eidx, counts[:, 0].astype(jnp.int32), exp_w_gate[l].astype(BF16),
                              exp_w_up[l].astype(BF16), exp_w_down[l].astype(BF16))
        xf = _ffn_ln(x1, ys, dest_tiles, gates, mod3, ln2_g[l].reshape(1, d), ln2_b[l].reshape(1, d),
                     bsz, t, alpha)
    return xf.reshape(bsz, t, d)
```

```python
import functools
import math

import jax
import jax.numpy as jnp
from jax import lax
from jax.experimental import pallas as pl
from jax.experimental.pallas import tpu as pltpu

F32 = jnp.float32
BF16 = jnp.bfloat16

A_HEADS = 4
A_HEAD_DIM = 64
A_WIDTH = A_HEADS * A_HEAD_DIM
CHUNK = 128
B_HEADS = 8
B_HEAD_DIM = 64
B_WIDTH = B_HEADS * B_HEAD_DIM
DILATIONS = (1, 4, 16)
ATT_BLK = 128
ROT_DIM = B_HEAD_DIM // 4
ROPE_THETA = 500000.0
SSM_GROUP = 16
SSM_STATE = 64
N_EXPERTS = 16
N_EXPERT_GROUPS = 4
EXPERTS_PER_GROUP = 4
TOP_K = 2
LN_EPS = 1e-5
NEG_INF = -1e30

LANES = 128
SUBLANES = 8
VMEM_LIMIT = 56 * 1024 * 1024
ROW_TILE = 512
SSM_STEPS = 64
MOE_BLK = 512
ATT_SUB_BLOCKS = 4


def _cparams(sem):
    return pltpu.CompilerParams(dimension_semantics=sem, vmem_limit_bytes=VMEM_LIMIT)


def _ln(x):
    mu = jnp.mean(x, axis=-1, keepdims=True)
    xc = x - mu
    var = jnp.mean(xc * xc, axis=-1, keepdims=True)
    return xc * lax.rsqrt(var + LN_EPS)


def _store_tile_rows(ref, val):
    n_slab = val.shape[1] // LANES
    for j in range(n_slab):
        ref[pl.ds(j, val.shape[0], stride=n_slab), :] = val[:, j * LANES:(j + 1) * LANES]


def _load_tile_rows(ref, start, rows, n_slab):
    return jnp.concatenate([ref[pl.ds(start * n_slab + j, rows, stride=n_slab), :] for j in range(n_slab)],
                           axis=1)


def _gelu_tanh(x):
    return 0.5 * x * (1.0 + jnp.tanh(math.sqrt(2.0 / math.pi) * (x + 0.044715 * (x * x * x))))


def _mod_kernel(c_ref, w_ref, b_ref, o_ref):
    c = c_ref[...]
    cond = c * jax.nn.sigmoid(c)
    o_ref[0] = jnp.dot(cond, w_ref[0], precision=lax.Precision.HIGHEST,
                       preferred_element_type=F32) + b_ref[0]


def _adaln_mod(c, ada_w, ada_b):
    depth, d, n = ada_w.shape
    bsz = c.shape[0]
    nb = d
    return pl.pallas_call(
        _mod_kernel,
        grid=(depth, n // nb),
        in_specs=[pl.BlockSpec((bsz, d), lambda l, j: (0, 0)),
                  pl.BlockSpec((1, d, nb), lambda l, j: (l, 0, j)),
                  pl.BlockSpec((1, 1, nb), lambda l, j: (l, 0, j))],
        out_specs=pl.BlockSpec((1, bsz, nb), lambda l, j: (l, 0, j)),
        out_shape=jax.ShapeDtypeStruct((depth, bsz, n), F32),
        compiler_params=_cparams(("arbitrary", "arbitrary")),
        name="adaln_mod",
    )(c, ada_w, ada_b.reshape(depth, 1, n))


def _rope_kernel(pos_ref, freq_ref, cos_ref, s1_ref, s2_ref):
    ang = pos_ref[...].astype(F32) * freq_ref[...]
    lane = lax.broadcasted_iota(jnp.int32, ang.shape, 1) % B_HEAD_DIM
    sn = jnp.sin(ang)
    cos_ref[...] = jnp.cos(ang)
    s1_ref[...] = jnp.where(lane < ROT_DIM // 2, -sn, 0.0)
    s2_ref[...] = jnp.where((lane >= ROT_DIM // 2) & (lane < ROT_DIM), sn, 0.0)


def _rope_tables(positions):
    n = positions.size
    half = ROT_DIM // 2
    lane = jnp.arange(LANES) % B_HEAD_DIM
    freqs = ROPE_THETA ** (-(lane % half).astype(F32) * 2.0 / ROT_DIM)
    freq_row = jnp.where(lane < ROT_DIM, freqs, 0.0).reshape(1, LANES).astype(F32)
    tm = min(2048, n)
    out = jax.ShapeDtypeStruct((n, LANES), F32)
    spec = pl.BlockSpec((tm, LANES), lambda i: (i, 0))
    return pl.pallas_call(
        _rope_kernel,
        grid=(n // tm,),
        in_specs=[pl.BlockSpec((tm, 1), lambda i: (i, 0)),
                  pl.BlockSpec((1, LANES), lambda i: (0, 0))],
        out_specs=[spec, spec, spec],
        out_shape=[out, out, out],
        compiler_params=_cparams(("arbitrary",)),
        name="rope_tables",
    )(positions.reshape(n, 1), freq_row)


def _inproj_kernel(x_ref, sh_ref, sc_ref, w_ref, cos_ref, s1_ref, s2_ref, lng_ref, lnb_ref,
                   ws_ref, bs_ref, a_ref, s_ref, *rest):
    n_dil = len(DILATIONS)
    q_outs, k_outs, v_outs = rest[:n_dil], rest[n_dil:2 * n_dil], rest[2 * n_dil:3 * n_dil]
    qs_ref, ks_ref, vs_ref = rest[3 * n_dil:]
    tm = x_ref.shape[0]
    h = _ln(x_ref[...]) * (1.0 + sc_ref[0]) + sh_ref[0]
    proj = jnp.dot(h.astype(BF16), w_ref[...], preferred_element_type=F32)

    uv = _gelu_tanh(proj[:, :2 * A_WIDTH])
    u = uv[:, :A_WIDTH]
    v = (_ln(uv[:, A_WIDTH:]) * lng_ref[...] + lnb_ref[...]).astype(BF16)
    row = lax.broadcasted_iota(jnp.int32, (CHUNK, CHUNK), 0)
    col = lax.broadcasted_iota(jnp.int32, (CHUNK, CHUNK), 1)
    head_of_lane = lax.broadcasted_iota(jnp.int32, (CHUNK, A_WIDTH), 1) // A_HEAD_DIM
    w_heads = [jnp.where(col <= row, ws_ref[hd], 0.0).astype(BF16) for hd in range(A_HEADS)]
    for cidx in range(tm // CHUNK):
        rows = slice(cidx * CHUNK, (cidx + 1) * CHUNK)
        vc = v[rows]
        sv = bs_ref[...]
        for hd in range(A_HEADS):
            full = jnp.dot(w_heads[hd], vc, preferred_element_type=F32)
            sv = sv + jnp.where(head_of_lane == hd, full, 0.0)
        a_ref[rows, :] = (u[rows] * sv).astype(a_ref.dtype)

    cos = cos_ref[...]
    s1 = s1_ref[...]
    s2 = s2_ref[...]
    q0 = 2 * A_WIDTH
    k0 = q0 + B_WIDTH
    v0 = k0 + B_WIDTH
    for j in range(B_WIDTH // LANES):
        for base, ref, scale in ((q0, qs_ref, B_HEAD_DIM ** -0.5), (k0, ks_ref, 1.0)):
            xs = proj[:, base + j * LANES: base + (j + 1) * LANES]
            rot = (xs * cos + pltpu.roll(xs, LANES - ROT_DIM // 2, 1) * s1
                   + pltpu.roll(xs, ROT_DIM // 2, 1) * s2)
            ref[j] = rot * scale
        vs_ref[j] = proj[:, v0 + j * LANES:v0 + (j + 1) * LANES]

    for src, outs in ((qs_ref, q_outs), (ks_ref, k_outs), (vs_ref, v_outs)):
        for dil, out in zip(DILATIONS, outs):
            for r in range(dil):
                for j in range(B_WIDTH // LANES):
                    out[0, r, :, j * LANES:(j + 1) * LANES] = (
                        src[j, pl.ds(r, tm // dil, stride=dil), :].astype(out.dtype))

    s_ref[...] = proj[:, v0 + B_WIDTH:].astype(s_ref.dtype)


def _inproj(x2d, mod3, w_in_bf, cos_t, s1_t, s2_t, ln_g, ln_b, ws, bs_full, bsz, t):
    n, d = x2d.shape
    tm = min(ROW_TILE, t)
    nt = t // tm
    pw = w_in_bf.shape[1]
    c_width = pw - 2 * A_WIDTH - 3 * B_WIDTH
    tok = lambda b, i: (b * nt + i, 0)
    const2 = lambda b, i: (0, 0)
    outs = [jax.ShapeDtypeStruct((n, A_WIDTH), BF16),
            jax.ShapeDtypeStruct((t, bsz * c_width), BF16)]
    out_specs = [pl.BlockSpec((tm, A_WIDTH), tok),
                 pl.BlockSpec((tm, c_width), lambda b, i: (i, b))]
    for _ in range(3):
        for dil in DILATIONS:
            outs.append(jax.ShapeDtypeStruct((bsz, dil, t // dil, B_WIDTH), BF16))
            out_specs.append(pl.BlockSpec((1, dil, tm // dil, B_WIDTH), lambda b, i: (b, 0, i, 0)))
    res = pl.pallas_call(
        _inproj_kernel,
        grid=(bsz, nt),
        in_specs=[pl.BlockSpec((tm, d), tok),
                  pl.BlockSpec((1, 1, d), lambda b, i: (b, 0, 0)),
                  pl.BlockSpec((1, 1, d), lambda b, i: (b, 0, 1)),
                  pl.BlockSpec((d, pw), const2),
                  pl.BlockSpec((tm, LANES), tok),
                  pl.BlockSpec((tm, LANES), tok),
                  pl.BlockSpec((tm, LANES), tok),
                  pl.BlockSpec((1, A_WIDTH), const2),
                  pl.BlockSpec((1, A_WIDTH), const2),
                  pl.BlockSpec((A_HEADS, CHUNK, CHUNK), lambda b, i: (0, 0, 0)),
                  pl.BlockSpec((CHUNK, A_WIDTH), const2)],
        out_specs=out_specs,
        out_shape=outs,
        scratch_shapes=[pltpu.VMEM((B_WIDTH // LANES, tm, LANES), F32)] * 3,
        compiler_params=_cparams(("arbitrary", "arbitrary")),
        name="inproj_gmlp_rope",
    )(x2d, mod3, mod3, w_in_bf, cos_t, s1_t, s2_t, ln_g, ln_b, ws, bs_full)
    n_dil = len(DILATIONS)
    a_out, s_in = res[0], res[1]
    return a_out, s_in, res[2:2 + n_dil], res[2 + n_dil:2 + 2 * n_dil], res[2 + 2 * n_dil:]


def _att_kernel(q_ref, k_ref, v_ref, o_ref, st_ref, kcat_ref, vaug_ref):
    i = pl.program_id(2)
    blk = ATT_BLK
    n_sub = q_ref.shape[2] // blk
    n_pairs = B_WIDTH // LANES
    last = slice(n_sub * blk, (n_sub + 1) * blk)

    @pl.when(i == 0)
    def _():
        kcat_ref[0:blk, :] = jnp.zeros((blk, B_WIDTH), BF16)
        vaug_ref[:, 0:blk, 0:LANES] = jnp.zeros((n_pairs, blk, LANES), BF16)
        vaug_ref[:, :, LANES:] = jnp.ones((n_pairs, (n_sub + 1) * blk, LANES), BF16)

    @pl.when(i > 0)
    def _():
        kcat_ref[0:blk, :] = kcat_ref[last, :]
        vaug_ref[:, 0:blk, 0:LANES] = vaug_ref[:, last, 0:LANES]

    kcat_ref[blk:, :] = k_ref[0, 0]
    for pair in range(n_pairs):
        vaug_ref[pair, blk:, 0:LANES] = v_ref[0, 0, :, pair * LANES:(pair + 1) * LANES]

    qi = lax.broadcasted_iota(jnp.int32, (blk, 2 * blk), 0)
    kk = lax.broadcasted_iota(jnp.int32, (blk, 2 * blk), 1)
    band = (kk >= qi) & (kk <= qi + blk)
    first = band & ((i > 0) | (kk >= blk))
    lane = lax.broadcasted_iota(jnp.int32, (blk, LANES), 1)
    low_half = lane < B_HEAD_DIM
    nt = (((1,), (1,)), ((), ()))
    for sub in range(n_sub):
        rows = slice(sub * blk, (sub + 1) * blk)
        keys = slice(sub * blk, (sub + 2) * blk)
        mask = first if sub == 0 else band
        stats = jnp.zeros((blk, LANES), F32)
        for pair in range(n_pairs):
            lanes = slice(pair * LANES, (pair + 1) * LANES)
            q2 = q_ref[0, 0, rows, lanes]
            kslab = kcat_ref[keys, lanes]
            res = []
            for hh in range(2):
                hd = 2 * pair + hh
                qm = jnp.where(low_half if hh == 0 else jnp.logical_not(low_half), q2, jnp.zeros_like(q2))
                s = lax.dot_general(qm, kslab, nt, preferred_element_type=F32)
                s = jnp.where(mask, s, NEG_INF)
                m = jnp.max(jnp.maximum(s[:, :blk], s[:, blk:]), axis=1, keepdims=True)
                p = jnp.exp(s - m).astype(BF16)
                ov = jnp.dot(p, vaug_ref[pair, keys, :], preferred_element_type=F32)
                den = ov[:, LANES:]
                res.append(ov[:, :LANES] / den)
                stats = jnp.where(lane == hd, m, stats)
                stats = jnp.where(lane == B_HEADS + hd, den, stats)
            o_ref[0, 0, rows, lanes] = jnp.where(low_half, res[0], res[1]).astype(o_ref.dtype)
        st_ref[0, 0, rows, :] = stats


def _att_branch(q, k, v, dil):
    bsz, _, rows, _ = q.shape
    n_sub = min(ATT_SUB_BLOCKS, rows // ATT_BLK)
    step_rows = n_sub * ATT_BLK
    cur = lambda b, r, i: (b, r, i, 0)
    blk = (1, 1, step_rows, B_WIDTH)
    sblk = (1, 1, step_rows, LANES)
    return pl.pallas_call(
        _att_kernel,
        grid=(bsz, dil, rows // step_rows),
        in_specs=[pl.BlockSpec(blk, cur), pl.BlockSpec(blk, cur), pl.BlockSpec(blk, cur)],
        out_specs=[pl.BlockSpec(blk, cur), pl.BlockSpec(sblk, cur)],
        out_shape=[jax.ShapeDtypeStruct((bsz, dil, rows, B_WIDTH), BF16),
                   jax.ShapeDtypeStruct((bsz, dil, rows, LANES), F32)],
        scratch_shapes=[pltpu.VMEM((step_rows + ATT_BLK, B_WIDTH), BF16),
                        pltpu.VMEM((B_WIDTH // LANES, step_rows + ATT_BLK, 2 * LANES), BF16)],
        compiler_params=_cparams(("arbitrary", "arbitrary", "arbitrary")),
        name=f"dilated_attn_d{dil}",
    )(q, k, v)


def _ssm_kernel(u_ref, bmat_ref, cmat_ref, are_ref, aim_ref, dskip_ref, gw_ref, gb_ref,
                o_ref, state_ref, xbuf_ref):
    bsz, two_n = state_ref.shape
    n_state = two_n // 2
    steps = u_ref.shape[0] // bsz

    @pl.when(pl.program_id(0) == 0)
    def _():
        state_ref[...] = jnp.zeros_like(state_ref)

    u = u_ref[...]
    xbuf_ref[...] = jnp.dot(u, bmat_ref[...], preferred_element_type=F32)
    a_re = jnp.broadcast_to(are_ref[...], (bsz, n_state))
    a_im = jnp.broadcast_to(aim_ref[...], (bsz, n_state))

    def step(tt, carry):
        x_re, x_im = carry
        r0 = pl.multiple_of(tt * bsz, bsz)
        b_re = xbuf_ref[pl.ds(r0, bsz), :n_state]
        b_im = xbuf_ref[pl.ds(r0, bsz), n_state:]
        n_re = a_re * x_re - a_im * x_im + b_re
        n_im = a_re * x_im + a_im * x_re + b_im
        xbuf_ref[pl.ds(r0, bsz), :n_state] = n_re
        xbuf_ref[pl.ds(r0, bsz), n_state:] = n_im
        return n_re, n_im

    x_re, x_im = lax.fori_loop(0, steps, step, (state_ref[:, :n_state], state_ref[:, n_state:]))
    state_ref[:, :n_state] = x_re
    state_ref[:, n_state:] = x_im

    y = jnp.dot(xbuf_ref[...].astype(BF16), cmat_ref[...], preferred_element_type=F32)
    y = _gelu_tanh(y + dskip_ref[...] * u.astype(F32))
    gate = jax.nn.sigmoid(jnp.dot(y.astype(BF16), gw_ref[...], preferred_element_type=F32) + gb_ref[...])
    o_ref[...] = (y * gate).astype(o_ref.dtype)


def _ssm_weights(lam_re, lam_im, log_dt, b_re, b_im, c_re, c_im):
    g, n_st = lam_re.shape
    cg = b_re.shape[-1]
    dt = jnp.exp(log_dt)[:, None]
    mag = jnp.exp(lam_re * dt)
    ab_re = mag * jnp.cos(lam_im * dt)
    ab_im = mag * jnp.sin(lam_im * dt)
    nr = ab_re - 1.0
    ni = ab_im
    mod2 = lam_re * lam_re + lam_im * lam_im
    f_re = (nr * lam_re + ni * lam_im) / mod2
    f_im = (ni * lam_re - nr * lam_im) / mod2
    bb_re = f_re[..., None] * b_re - f_im[..., None] * b_im
    bb_im = f_re[..., None] * b_im + f_im[..., None] * b_re
    eye = jnp.eye(g, dtype=F32)
    bm_re = jnp.einsum('gnc,gh->gchn', bb_re, eye).reshape(g * cg, g * n_st)
    bm_im = jnp.einsum('gnc,gh->gchn', bb_im, eye).reshape(g * cg, g * n_st)
    bmat = jnp.concatenate([bm_re, bm_im], axis=1)
    cm_re = jnp.einsum('gcn,gh->gnhc', c_re, eye).reshape(g * n_st, g * cg)
    cm_im = jnp.einsum('gcn,gh->gnhc', c_im, eye).reshape(g * n_st, g * cg)
    cmat = jnp.concatenate([cm_re, -cm_im], axis=0)
    return (bmat.astype(BF16), cmat.astype(BF16),
            ab_re.reshape(1, g * n_st), ab_im.reshape(1, g * n_st))


def _ssm(u_tm, bmat, cmat, a_re, a_im, d_skip, glu_w_bf, glu_b, bsz, t):
    cw = bmat.shape[0]
    two_n = bmat.shape[1]
    steps = min(SSM_STEPS, t)
    rows = steps * bsz
    u2 = u_tm.reshape(t * bsz, cw)
    const = lambda i: (0, 0)
    return pl.pallas_call(
        _ssm_kernel,
        grid=(t // steps,),
        in_specs=[pl.BlockSpec((rows, cw), lambda i: (i, 0)),
                  pl.BlockSpec((cw, two_n), const),
                  pl.BlockSpec((two_n, cw), const),
                  pl.BlockSpec((1, two_n // 2), const),
                  pl.BlockSpec((1, two_n // 2), const),
                  pl.BlockSpec((1, cw), const),
                  pl.BlockSpec((cw, cw), const),
                  pl.BlockSpec((1, cw), const)],
        out_specs=pl.BlockSpec((rows, cw), lambda i: (i, 0)),
        out_shape=jax.ShapeDtypeStruct((t * bsz, cw), BF16),
        scratch_shapes=[pltpu.VMEM((bsz, two_n), F32), pltpu.VMEM((rows, two_n), F32)],
        compiler_params=_cparams(("arbitrary",)),
        name="s5_scan_glu",
    )(u2, bmat, cmat, a_re, a_im, d_skip, glu_w_bf, glu_b).reshape(t, bsz * cw)


def _route(sel, scores):
    gs = []
    for g in range(N_EXPERT_GROUPS):
        a, b, c, d = sel[4 * g: 4 * g + 4]
        hi1, lo1 = jnp.maximum(a, b), jnp.minimum(a, b)
        hi2, lo2 = jnp.maximum(c, d), jnp.minimum(c, d)
        gs.append(jnp.maximum(hi1, hi2) + jnp.maximum(jnp.minimum(hi1, hi2), jnp.maximum(lo1, lo2)))
    g_idx = jnp.zeros(gs[0].shape, jnp.int32)
    best = gs[0]
    for g in range(1, N_EXPERT_GROUPS):
        better = gs[g] > best
        g_idx = jnp.where(better, g, g_idx)
        best = jnp.where(better, gs[g], best)

    def pick(rows, j):
        out = rows[j]
        for g in range(1, N_EXPERT_GROUPS):
            out = jnp.where(g_idx == g, rows[4 * g + j], out)
        return out

    v = [pick(sel, j) for j in range(EXPERTS_PER_GROUP)]
    s = [pick(scores, j) for j in range(EXPERTS_PER_GROUP)]
    i1 = jnp.zeros(g_idx.shape, jnp.int32)
    b1, g1 = v[0], s[0]
    for j in range(1, EXPERTS_PER_GROUP):
        better = v[j] > b1
        i1 = jnp.where(better, j, i1)
        b1 = jnp.where(better, v[j], b1)
        g1 = jnp.where(better, s[j], g1)
    i2 = jnp.full(g_idx.shape, -1, jnp.int32)
    b2 = jnp.zeros_like(b1)
    g2 = jnp.zeros_like(g1)
    for j in range(EXPERTS_PER_GROUP):
        better = (i1 != j) & ((i2 < 0) | (v[j] > b2))
        i2 = jnp.where(better, j, i2)
        b2 = jnp.where(better, v[j], b2)
        g2 = jnp.where(better, s[j], g2)
    tot = g1 + g2
    base = g_idx * EXPERTS_PER_GROUP
    return base + i1, base + i2, g1 / tot, g2 / tot


def _natural_order(ref, scr_ref, dil):
    if dil == 1:
        return ref[0, 0].astype(F32)
    per = ref.shape[2]
    n_slab = scr_ref.shape[0]
    for r in range(dil):
        for j in range(n_slab):
            scr_ref[j, pl.ds(r, per, stride=dil), :] = ref[0, r, :, j * LANES:(j + 1) * LANES].astype(F32)
    if n_slab == 1:
        return scr_ref[0]
    return jnp.concatenate([scr_ref[j] for j in range(n_slab)], axis=1)


def _outproj_kernel(a_ref, o1_ref, o2_ref, o3_ref, st1_ref, st2_ref, st3_ref,
                    c_ref, x_ref, g1_ref, sh2_ref, sc2_ref, wout_ref, expand_ref, lng_ref, lnb_ref,
                    rwh_ref, rwl_ref, rb_ref, x1_ref, h2_ref, eidx_ref, gate_ref, count_ref,
                    oscr2_ref, oscr3_ref, sscr2_ref, sscr3_ref, *, alpha):
    o_refs = (o1_ref, o2_ref, o3_ref)
    st_refs = (st1_ref, st2_ref, st3_ref)
    o_scr = (None, oscr2_ref, oscr3_ref)
    st_scr = (None, sscr2_ref, sscr3_ref)
    ms = [_natural_order(st_refs[g], st_scr[g], DILATIONS[g]) for g in range(len(DILATIONS))]
    dens = [pltpu.roll(m, LANES - B_HEADS, 1) for m in ms]
    mx = jnp.maximum(jnp.maximum(ms[0], ms[1]), ms[2])
    ws = [dens[g] * jnp.exp(ms[g] - mx) for g in range(len(DILATIONS))]
    lane = lax.broadcasted_iota(jnp.int32, mx.shape, 1)
    tot = jnp.where(lane < B_HEADS, ws[0] + ws[1] + ws[2], 1.0)
    expand = expand_ref[...]

    def widen(w):
        wn = jnp.where(lane < B_HEADS, w / tot, 0.0)
        hi = wn.astype(BF16)
        lo = (wn - hi.astype(F32)).astype(BF16)
        return (jnp.dot(hi, expand, preferred_element_type=F32)
                + jnp.dot(lo, expand, preferred_element_type=F32))

    b_out = widen(ws[0]) * _natural_order(o_refs[0], o_scr[0], DILATIONS[0])
    for g in range(1, len(DILATIONS)):
        b_out = b_out + widen(ws[g]) * _natural_order(o_refs[g], o_scr[g], DILATIONS[g])
    b_out = b_out.astype(BF16)

    wout = wout_ref
    mix = (jnp.dot(a_ref[...], wout[:A_WIDTH, :], preferred_element_type=F32)
           + jnp.dot(b_out, wout[A_WIDTH:A_WIDTH + B_WIDTH, :], preferred_element_type=F32)
           + jnp.dot(c_ref[...], wout[A_WIDTH + B_WIDTH:, :], preferred_element_type=F32))
    x1 = _ln(alpha * x_ref[...] + (1.0 + g1_ref[0]) * mix) * lng_ref[...] + lnb_ref[...]
    x1_ref[...] = x1
    h2 = _ln(x1) * (1.0 + sc2_ref[0]) + sh2_ref[0]
    _store_tile_rows(h2_ref, h2)

    hi = h2.astype(BF16)
    lo = (h2 - hi.astype(F32)).astype(BF16)
    nt = (((1,), (1,)), ((), ()))
    logits = (lax.dot_general(rwh_ref[...], hi, nt, preferred_element_type=F32)
              + lax.dot_general(rwh_ref[...], lo, nt, preferred_element_type=F32)
              + lax.dot_general(rwl_ref[...], hi, nt, preferred_element_type=F32))
    scores = jax.nn.sigmoid(logits)
    sel = scores + rb_ref[...]
    sel_rows = [sel[e:e + 1, :] for e in range(N_EXPERTS)]
    score_rows = [scores[e:e + 1, :] for e in range(N_EXPERTS)]
    e1, e2, gt1, gt2 = _route(sel_rows, score_rows)

    @pl.when((pl.program_id(0) == 0) & (pl.program_id(1) == 0))
    def _():
        count_ref[...] = jnp.zeros_like(count_ref)

    tm = h2.shape[0]
    erow = lax.broadcasted_iota(jnp.int32, (N_EXPERTS, tm), 0)
    oh1 = erow == e1
    oh2 = erow == e2
    both = (oh1 | oh2).astype(BF16)
    earlier = (lax.broadcasted_iota(jnp.int32, (tm, tm), 0)
               < lax.broadcasted_iota(jnp.int32, (tm, tm), 1)).astype(BF16)
    base = count_ref[:, 0:1]
    cnt = jnp.dot(both, earlier, preferred_element_type=F32) + base
    r1 = jnp.sum(jnp.where(oh1, cnt, 0.0), axis=0, keepdims=True)
    r2 = jnp.sum(jnp.where(oh2, cnt, 0.0), axis=0, keepdims=True)
    total = base + jnp.sum(both.astype(F32), axis=1, keepdims=True)
    count_ref[...] = jnp.broadcast_to(total, count_ref.shape)

    eidx_ref[...] = jnp.zeros_like(eidx_ref)
    gate_ref[...] = jnp.zeros_like(gate_ref)
    eidx_ref[0:1, :] = e1
    eidx_ref[1:2, :] = e2
    eidx_ref[2:3, :] = r1.astype(jnp.int32)
    eidx_ref[3:4, :] = r2.astype(jnp.int32)
    gate_ref[0:1, :] = gt1
    gate_ref[1:2, :] = gt2


def _outproj(a_out, att, c_tm, x2d, mod3, w_out_bf, ln_g, ln_b, rw_hi, rw_lo, rbias, bsz, t, alpha):
    n, d = x2d.shape
    tm = min(ROW_TILE, t)
    nt = t // tm
    cw = c_tm.shape[1] // bsz
    tok = lambda b, i: (b * nt + i, 0)
    const2 = lambda b, i: (0, 0)
    (o1, st1), (o2, st2), (o3, st3) = att
    head = jnp.arange(B_WIDTH) // B_HEAD_DIM
    expand = (jnp.arange(LANES)[:, None] == head[None, :]).astype(BF16)
    res_major = lambda dil, w: pl.BlockSpec((1, dil, tm // dil, w), lambda b, i: (b, 0, i, 0))
    modspec = lambda j: pl.BlockSpec((1, 1, d), lambda b, i: (b, 0, j))
    return pl.pallas_call(
        functools.partial(_outproj_kernel, alpha=alpha),
        grid=(bsz, nt),
        in_specs=[pl.BlockSpec((tm, A_WIDTH), tok)]
                 + [res_major(dil, B_WIDTH) for dil in DILATIONS]
                 + [res_major(dil, LANES) for dil in DILATIONS]
                 + [pl.BlockSpec((tm, cw), lambda b, i: (i, b)),
                  pl.BlockSpec((tm, d), tok),
                  modspec(2), modspec(3), modspec(4),
                  pl.BlockSpec((d, d), const2),
                  pl.BlockSpec((LANES, B_WIDTH), const2),
                  pl.BlockSpec((1, d), const2), pl.BlockSpec((1, d), const2),
                  pl.BlockSpec((N_EXPERTS, d), const2), pl.BlockSpec((N_EXPERTS, d), const2),
                  pl.BlockSpec((N_EXPERTS, 1), const2)],
        out_specs=[pl.BlockSpec((tm, d), tok), pl.BlockSpec((tm * (d // LANES), LANES), tok),
                   pl.BlockSpec((SUBLANES, tm), lambda b, i: (0, b * nt + i)),
                   pl.BlockSpec((SUBLANES, tm), lambda b, i: (0, b * nt + i)),
                   pl.BlockSpec((N_EXPERTS, LANES), const2)],
        out_shape=[jax.ShapeDtypeStruct((n, d), F32),
                   jax.ShapeDtypeStruct((n * (d // LANES), LANES), F32),

                   jax.ShapeDtypeStruct((SUBLANES, n), jnp.int32),
                   jax.ShapeDtypeStruct((SUBLANES, n), F32),
                   jax.ShapeDtypeStruct((N_EXPERTS, LANES), F32)],
        scratch_shapes=[pltpu.VMEM((B_WIDTH // LANES, tm, LANES), F32),
                        pltpu.VMEM((B_WIDTH // LANES, tm, LANES), F32),
                        pltpu.VMEM((1, tm, LANES), F32), pltpu.VMEM((1, tm, LANES), F32)],
        compiler_params=_cparams(("arbitrary", "arbitrary")),
        name="outproj_ln_router",
    )(a_out, o1, o2, o3, st1, st2, st3, c_tm, x2d, mod3, mod3, mod3, w_out_bf, expand,
      ln_g, ln_b, rw_hi, rw_lo, rbias)


def _dispatch_plan(eidx, counts, n_tok, blk):
    e = eidx[:TOP_K]
    rank = eidx[TOP_K:2 * TOP_K]
    pcounts = (counts + blk - 1) // blk * blk
    pends = jnp.cumsum(pcounts)
    pstarts = pends - pcounts
    seg_start = jnp.zeros_like(e)
    for j in range(N_EXPERTS):
        seg_start = jnp.where(e == j, pstarts[j], seg_start)
    dest = (seg_start + rank).astype(jnp.int32)
    n_blk = n_tok * TOP_K // blk + N_EXPERTS
    blk_start = jnp.arange(n_blk, dtype=jnp.int32) * blk
    blk_e = jnp.minimum(jnp.sum((pends[None, :] <= blk_start[:, None]).astype(jnp.int32), axis=1),
                        N_EXPERTS - 1).astype(jnp.int32)
    n_used = (pends[-1] // blk).astype(jnp.int32).reshape(1)
    return dest, blk_e, n_used, pends.astype(jnp.int32), pcounts.astype(jnp.int32)


def _dispatch_kernel(pend_ref, pcnt_ref, dest_ref, h2_ref, xs_hbm, zbuf, sem, zsem, *, blk, n_slab):
    i = pl.program_id(0)
    rows = h2_ref.shape[0] // n_slab
    blk_rows = blk * n_slab

    @pl.when(i == 0)
    def _():
        zbuf[...] = jnp.zeros_like(zbuf)
        for e in range(N_EXPERTS):
            @pl.when(pcnt_ref[e] > 0)
            def _():
                start = pl.multiple_of((pend_ref[e] - blk) * n_slab, blk_rows)
                cp = pltpu.make_async_copy(zbuf, xs_hbm.at[pl.ds(start, blk_rows), :], zsem)
                cp.start()
                cp.wait()
        n_blk = xs_hbm.shape[0] // blk_rows
        for t in range(N_EXPERTS):
            tail = pend_ref[N_EXPERTS - 1] // blk + t

            @pl.when(tail < n_blk)
            def _():
                start = pl.multiple_of(tail * blk_rows, blk_rows)
                cp = pltpu.make_async_copy(zbuf, xs_hbm.at[pl.ds(start, blk_rows), :], zsem)
                cp.start()
                cp.wait()

    def issue(s, c):
        src = h2_ref.at[pl.ds(pl.multiple_of(s * n_slab, n_slab), n_slab), :]
        for kk in range(TOP_K):
            slot = pl.multiple_of(dest_ref[0, 0, kk * rows + s] * n_slab, n_slab)
            pltpu.make_async_copy(src, xs_hbm.at[pl.ds(slot, n_slab), :], sem).start()
        return c
    lax.fori_loop(0, rows, issue, 0, unroll=8)
    for kk in range(TOP_K):
        pltpu.make_async_copy(h2_ref, xs_hbm.at[pl.ds(0, rows * n_slab), :], sem).wait()


def _expert_kernel(blk_e_ref, nused_ref, xs_ref, wg_ref, wu_ref, wd_ref, ys_ref, *, blk, n_slab):
    j = pl.program_id(0)

    @pl.when(j < nused_ref[0])
    def _():
        rows = _load_tile_rows(xs_ref, 0, blk, n_slab).astype(BF16)
        gt = jnp.dot(rows, wg_ref[0], preferred_element_type=F32)
        up = jnp.dot(rows, wu_ref[0], preferred_element_type=F32)
        hid = (gt * jax.nn.sigmoid(gt) * up).astype(BF16)
        _store_tile_rows(ys_ref, jnp.dot(hid, wd_ref[0], preferred_element_type=F32))

    @pl.when(j >= nused_ref[0])
    def _():
        ys_ref[...] = jnp.zeros_like(ys_ref)


def _moe(h2, eidx, counts, wg_bf, wu_bf, wd_bf):
    d, de = wg_bf.shape[-2:]
    n_slab = d // LANES
    n_tok = h2.shape[0] // n_slab
    blk = min(MOE_BLK, n_tok)
    dest, blk_e, n_used, pends, pcounts = _dispatch_plan(eidx, counts, n_tok, blk)
    n_blk = blk_e.shape[0]
    cap = n_blk * blk

    rows = min(ROW_TILE, n_tok)
    nt = n_tok // rows
    dest_tiles = dest.reshape(TOP_K, nt, rows).transpose(1, 0, 2).reshape(nt, 1, TOP_K * rows)
    xs = pl.pallas_call(
        functools.partial(_dispatch_kernel, blk=blk, n_slab=n_slab),
        grid_spec=pltpu.PrefetchScalarGridSpec(
            num_scalar_prefetch=2,
            grid=(nt,),
            in_specs=[pl.BlockSpec((1, 1, TOP_K * rows), lambda i, pe, pc: (i, 0, 0),
                                   memory_space=pltpu.SMEM),
                      pl.BlockSpec((rows * n_slab, LANES), lambda i, pe, pc: (i, 0))],
            out_specs=pl.BlockSpec(memory_space=pl.ANY),
            scratch_shapes=[pltpu.VMEM((blk * n_slab, LANES), F32),
                            pltpu.SemaphoreType.DMA, pltpu.SemaphoreType.DMA],
        ),
        out_shape=jax.ShapeDtypeStruct((cap * n_slab, LANES), F32),
        compiler_params=_cparams(("arbitrary",)),
        name="moe_dispatch",
    )(pends, pcounts, dest_tiles, h2)

    used = lambda j, be, nu: (jnp.minimum(j, nu[0] - 1), 0)
    ys = pl.pallas_call(
        functools.partial(_expert_kernel, blk=blk, n_slab=n_slab),
        grid_spec=pltpu.PrefetchScalarGridSpec(
            num_scalar_prefetch=2,
            grid=(n_blk,),
            in_specs=[pl.BlockSpec((blk * n_slab, LANES), used),
                      pl.BlockSpec((1, d, de), lambda j, be, nu: (be[j], 0, 0)),
                      pl.BlockSpec((1, d, de), lambda j, be, nu: (be[j], 0, 0)),
                      pl.BlockSpec((1, de, d), lambda j, be, nu: (be[j], 0, 0))],
            out_specs=pl.BlockSpec((blk * n_slab, LANES), lambda j, be, nu: (j, 0)),
        ),
        out_shape=jax.ShapeDtypeStruct((cap * n_slab, LANES), F32),
        compiler_params=_cparams(("arbitrary",)),
        name="moe_experts",
    )(blk_e, n_used, xs, wg_bf, wu_bf, wd_bf)
    return ys, dest_tiles


def _ffn_ln_kernel(dest_ref, next_ref, x_ref, gate_ref, g2_ref, lng_ref, lnb_ref, ys_hbm, o_ref, ybuf, sem,
                   *, alpha):
    tm, d = x_ref.shape
    n_slab = d // LANES
    g = pl.program_id(0) * pl.num_programs(1) + pl.program_id(1)
    n_tiles = pl.num_programs(0) * pl.num_programs(1)
    slot = g % 2

    def gather(idx_ref, buf_slot):
        def issue(s, c):
            src = pl.multiple_of(idx_ref[0, 0, s] * n_slab, n_slab)
            dst = pl.multiple_of(s * n_slab, n_slab)
            pltpu.make_async_copy(ys_hbm.at[pl.ds(src, n_slab), :],
                                  ybuf.at[buf_slot, pl.ds(dst, n_slab), :], sem.at[buf_slot]).start()
            return c
        lax.fori_loop(0, TOP_K * tm, issue, 0, unroll=8)

    @pl.when(g == 0)
    def _():
        gather(dest_ref, 0)

    @pl.when(g + 1 < n_tiles)
    def _():
        gather(next_ref, 1 - slot)

    pltpu.make_async_copy(ys_hbm.at[pl.ds(0, TOP_K * tm * n_slab), :], ybuf.at[slot], sem.at[slot]).wait()

    gate = gate_ref[...]
    buf = ybuf.at[slot]
    ffn = _load_tile_rows(buf, 0, tm, n_slab) * gate[:, 0:1]
    for kk in range(1, TOP_K):
        ffn = ffn + _load_tile_rows(buf, kk * tm, tm, n_slab) * gate[:, kk:kk + 1]
    o_ref[...] = _ln(alpha * x_ref[...] + (1.0 + g2_ref[0]) * ffn) * lng_ref[...] + lnb_ref[...]


def _ffn_ln(x1, ys, dest_tiles, gates, mod3, ln_g, ln_b, bsz, t, alpha):
    n, d = x1.shape
    n_slab = d // LANES
    tm = dest_tiles.shape[2] // TOP_K
    nt = t // tm
    n_tiles = bsz * nt
    tok = lambda b, i, *_: (b * nt + i, 0)
    const2 = lambda b, i, *_: (0, 0)
    gate_cols = gates[:TOP_K].T
    idx_blk = (1, 1, TOP_K * tm)
    return pl.pallas_call(
        functools.partial(_ffn_ln_kernel, alpha=alpha),
        grid=(bsz, nt),
        in_specs=[pl.BlockSpec(idx_blk, lambda b, i: (b * nt + i, 0, 0), memory_space=pltpu.SMEM),
                  pl.BlockSpec(idx_blk, lambda b, i: (jnp.minimum(b * nt + i + 1, n_tiles - 1), 0, 0),
                               memory_space=pltpu.SMEM),
                  pl.BlockSpec((tm, d), tok), pl.BlockSpec((tm, TOP_K), tok),
                  pl.BlockSpec((1, 1, d), lambda b, i: (b, 0, 5)),
                  pl.BlockSpec((1, d), const2), pl.BlockSpec((1, d), const2),
                  pl.BlockSpec(memory_space=pl.ANY)],
        out_specs=pl.BlockSpec((tm, d), tok),
        out_shape=jax.ShapeDtypeStruct((n, d), F32),
        scratch_shapes=[pltpu.VMEM((2, TOP_K * tm * n_slab, LANES), F32), pltpu.SemaphoreType.DMA((2,))],
        compiler_params=_cparams(("arbitrary", "arbitrary")),
        name="ffn_combine_ln",
    )(dest_tiles, dest_tiles, x1, gate_cols, mod3, ln_g, ln_b, ys)


def kernel(x, c, positions, ada_w, ada_b, w_in, gm_ln_g, gm_ln_b, gm_ws, gm_bs, ssm_lam_re, ssm_lam_im, ssm_log_dt, ssm_b_re, ssm_b_im, ssm_c_re, ssm_c_im, ssm_d, glu_w, glu_b, w_out, ln1_g, ln1_b, router_w, router_bias, exp_w_gate, exp_w_up, exp_w_down, ln2_g, ln2_b):
    bsz, t, d = x.shape
    depth = ada_w.shape[0]
    alpha = (2.0 * depth) ** 0.25
    n = bsz * t

    mod = _adaln_mod(c.astype(F32), ada_w, ada_b)
    cos_t, s1_t, s2_t = _rope_tables(positions)
    rw_t = router_w.astype(F32).T
    rw_hi = rw_t.astype(BF16)
    rw_lo = (rw_t - rw_hi.astype(F32)).astype(BF16)
    rbias = router_bias.astype(F32).reshape(N_EXPERTS, 1)

    xf = x.astype(F32).reshape(n, d)
    for l in range(depth):
        mod3 = mod[l].reshape(bsz, 1, 6 * d)
        bs_full = jnp.repeat(gm_bs[l].T, A_HEAD_DIM, axis=1)
        a_out, s_in, qs, ks, vs = _inproj(
            xf, mod3, w_in[l].astype(BF16), cos_t, s1_t, s2_t,
            gm_ln_g[l].reshape(1, A_WIDTH), gm_ln_b[l].reshape(1, A_WIDTH), gm_ws[l], bs_full, bsz, t)
        att = [_att_branch(qs[g], ks[g], vs[g], dil) for g, dil in enumerate(DILATIONS)]
        bmat, cmat, a_re, a_im = _ssm_weights(ssm_lam_re[l], ssm_lam_im[l], ssm_log_dt[l],
                                              ssm_b_re[l], ssm_b_im[l], ssm_c_re[l], ssm_c_im[l])
        cw = bmat.shape[0]
        c_out = _ssm(s_in, bmat, cmat, a_re, a_im, ssm_d[l].reshape(1, cw),
                     glu_w[l].astype(BF16), glu_b[l].reshape(1, cw), bsz, t)
        x1, h2, eidx, gates, counts = _outproj(a_out, att, c_out, xf, mod3, w_out[l].astype(BF16),
                                               ln1_g[l].reshape(1, d), ln1_b[l].reshape(1, d),
                                               rw_hi, rw_lo, rbias, bsz, t, alpha)
        ys, dest_tiles = _moe(h2, eidx, counts[:, 0].astype(jnp.int32), exp_w_gate[l].astype(BF16),
                              exp_w_up[l].astype(BF16), exp_w_down[l].astype(BF16))
        xf = _ffn_ln(x1, ys, dest_tiles, gates, mod3, ln2_g[l].reshape(1, d), ln2_b[l].reshape(1, d),
                     bsz, t, alpha)
    return xf.reshape(bsz, t, d)
```

```python
import functools
import math

import jax
import jax.numpy as jnp
from jax import lax
from jax.experimental import pallas as pl
from jax.experimental.pallas import tpu as pltpu

F32 = jnp.float32
BF16 = jnp.bfloat16

A_HEADS = 4
A_HEAD_DIM = 64
A_WIDTH = A_HEADS * A_HEAD_DIM
CHUNK = 128
B_HEADS = 8
B_HEAD_DIM = 64
B_WIDTH = B_HEADS * B_HEAD_DIM
DILATIONS = (1, 4, 16)
ATT_BLK = 128
ROT_DIM = B_HEAD_DIM // 4
ROPE_THETA = 500000.0
SSM_GROUP = 16
SSM_STATE = 64
N_EXPERTS = 16
N_EXPERT_GROUPS = 4
EXPERTS_PER_GROUP = 4
TOP_K = 2
LN_EPS = 1e-5
NEG_INF = -1e30

LANES = 128
SUBLANES = 8
VMEM_LIMIT = 56 * 1024 * 1024
ROW_TILE = 512
SSM_STEPS = 64
MOE_BLK = 512
ATT_SUB_BLOCKS = 4


def _cparams(sem):
    return pltpu.CompilerParams(dimension_semantics=sem, vmem_limit_bytes=VMEM_LIMIT)


def _ln(x):
    mu = jnp.mean(x, axis=-1, keepdims=True)
    xc = x - mu
    var = jnp.mean(xc * xc, axis=-1, keepdims=True)
    return xc * lax.rsqrt(var + LN_EPS)


def _store_tile_rows(ref, val):
    n_slab = val.shape[1] // LANES
    for j in range(n_slab):
        ref[pl.ds(j, val.shape[0], stride=n_slab), :] = val[:, j * LANES:(j + 1) * LANES]


def _load_tile_rows(ref, start, rows, n_slab):
    return jnp.concatenate([ref[pl.ds(start * n_slab + j, rows, stride=n_slab), :] for j in range(n_slab)],
                           axis=1)


def _gelu_tanh(x):
    return 0.5 * x * (1.0 + jnp.tanh(math.sqrt(2.0 / math.pi) * (x + 0.044715 * (x * x * x))))


def _mod_kernel(c_ref, w_ref, b_ref, o_ref):
    c = c_ref[...]
    cond = c * jax.nn.sigmoid(c)
    o_ref[0] = jnp.dot(cond, w_ref[0], precision=lax.Precision.HIGHEST,
                       preferred_element_type=F32) + b_ref[0]


def _adaln_mod(c, ada_w, ada_b):
    depth, d, n = ada_w.shape
    bsz = c.shape[0]
    nb = d
    return pl.pallas_call(
        _mod_kernel,
        grid=(depth, n // nb),
        in_specs=[pl.BlockSpec((bsz, d), lambda l, j: (0, 0)),
                  pl.BlockSpec((1, d, nb), lambda l, j: (l, 0, j)),
                  pl.BlockSpec((1, 1, nb), lambda l, j: (l, 0, j))],
        out_specs=pl.BlockSpec((1, bsz, nb), lambda l, j: (l, 0, j)),
        out_shape=jax.ShapeDtypeStruct((depth, bsz, n), F32),
        compiler_params=_cparams(("arbitrary", "arbitrary")),
        name="adaln_mod",
    )(c, ada_w, ada_b.reshape(depth, 1, n))


def _rope_kernel(pos_ref, freq_ref, cos_ref, s1_ref, s2_ref):
    ang = pos_ref[...].astype(F32) * freq_ref[...]
    lane = lax.broadcasted_iota(jnp.int32, ang.shape, 1) % B_HEAD_DIM
    sn = jnp.sin(ang)
    cos_ref[...] = jnp.cos(ang)
    s1_ref[...] = jnp.where(lane < ROT_DIM // 2, -sn, 0.0)
    s2_ref[...] = jnp.where((lane >= ROT_DIM // 2) & (lane < ROT_DIM), sn, 0.0)


def _rope_tables(positions):
    n = positions.size
    half = ROT_DIM // 2
    lane = jnp.arange(LANES) % B_HEAD_DIM
    freqs = ROPE_THETA ** (-(lane % half).astype(F32) * 2.0 / ROT_DIM)
    freq_row = jnp.where(lane < ROT_DIM, freqs, 0.0).reshape(1, LANES).astype(F32)
    tm = min(2048, n)
    out = jax.ShapeDtypeStruct((n, LANES), F32)
    spec = pl.BlockSpec((tm, LANES), lambda i: (i, 0))
    return pl.pallas_call(
        _rope_kernel,
        grid=(n // tm,),
        in_specs=[pl.BlockSpec((tm, 1), lambda i: (i, 0)),
                  pl.BlockSpec((1, LANES), lambda i: (0, 0))],
        out_specs=[spec, spec, spec],
        out_shape=[out, out, out],
        compiler_params=_cparams(("arbitrary",)),
        name="rope_tables",
    )(positions.reshape(n, 1), freq_row)


def _inproj_kernel(x_ref, sh_ref, sc_ref, w_ref, cos_ref, s1_ref, s2_ref, lng_ref, lnb_ref,
                   ws_ref, bs_ref, a_ref, s_ref, qn_ref, kn_ref, vn_ref, *rest):
    copy_dils = [dil for dil in DILATIONS if _dilation_layout(dil) == "copy"]
    n_copy = len(copy_dils)
    q_outs, k_outs, v_outs = rest[:n_copy], rest[n_copy:2 * n_copy], rest[2 * n_copy:3 * n_copy]
    qs_ref, ks_ref, vs_ref, qt_ref, kt_ref, vt_ref = rest[3 * n_copy:]
    tm = x_ref.shape[0]
    h = _ln(x_ref[...]) * (1.0 + sc_ref[0]) + sh_ref[0]
    proj = jnp.dot(h.astype(BF16), w_ref[...], preferred_element_type=F32)

    uv = _gelu_tanh(proj[:, :2 * A_WIDTH])
    u = uv[:, :A_WIDTH]
    v = (_ln(uv[:, A_WIDTH:]) * lng_ref[...] + lnb_ref[...]).astype(BF16)
    row = lax.broadcasted_iota(jnp.int32, (CHUNK, CHUNK), 0)
    col = lax.broadcasted_iota(jnp.int32, (CHUNK, CHUNK), 1)
    head_of_lane = lax.broadcasted_iota(jnp.int32, (CHUNK, A_WIDTH), 1) // A_HEAD_DIM
    w_heads = [jnp.where(col <= row, ws_ref[hd], 0.0).astype(BF16) for hd in range(A_HEADS)]
    for cidx in range(tm // CHUNK):
        rows = slice(cidx * CHUNK, (cidx + 1) * CHUNK)
        vc = v[rows]
        sv = bs_ref[...]
        for hd in range(A_HEADS):
            full = jnp.dot(w_heads[hd], vc, preferred_element_type=F32)
            sv = sv + jnp.where(head_of_lane == hd, full, 0.0)
        a_ref[rows, :] = (u[rows] * sv).astype(a_ref.dtype)

    cos = cos_ref[...]
    s1 = s1_ref[...]
    s2 = s2_ref[...]
    q0 = 2 * A_WIDTH
    k0 = q0 + B_WIDTH
    v0 = k0 + B_WIDTH
    for j in range(B_WIDTH // LANES):
        lanes = slice(j * LANES, (j + 1) * LANES)
        for base, ref, nat, scale in ((q0, qs_ref, qn_ref, B_HEAD_DIM ** -0.5), (k0, ks_ref, kn_ref, 1.0)):
            xs = proj[:, base + j * LANES: base + (j + 1) * LANES]
            rot = (xs * cos + pltpu.roll(xs, LANES - ROT_DIM // 2, 1) * s1
                   + pltpu.roll(xs, ROT_DIM // 2, 1) * s2) * scale
            nat[:, lanes] = rot.astype(nat.dtype)
            ref[j] = rot
        vs = proj[:, v0 + j * LANES:v0 + (j + 1) * LANES]
        vn_ref[:, lanes] = vs.astype(vn_ref.dtype)
        vs_ref[j] = vs

    for src, stage, outs in ((qs_ref, qt_ref, q_outs), (ks_ref, kt_ref, k_outs), (vs_ref, vt_ref, v_outs)):
        bufs = (src, stage)
        prev = 1
        for idx, (dil, out) in enumerate(zip(copy_dils, outs)):
            chained = prev > 1 and dil % prev == 0
            base_dil = prev if chained else 1
            source = bufs[idx % 2] if chained or idx == 0 else None
            assert source is not None, "copy dilations must form a divisibility chain"
            step = dil // base_dil
            per = tm // dil
            keep = idx + 1 < len(copy_dils)
            for r_prev in range(base_dil):
                for q in range(step):
                    r = r_prev + base_dil * q
                    start = r_prev * (tm // base_dil) + q
                    for j in range(B_WIDTH // LANES):
                        val = source[j, pl.ds(start, per, stride=step), :]
                        out[0, r, :, j * LANES:(j + 1) * LANES] = val.astype(out.dtype)
                        if keep:
                            bufs[(idx + 1) % 2][j, r * per:(r + 1) * per, :] = val
            prev = dil

    s_ref[...] = proj[:, v0 + B_WIDTH:].astype(s_ref.dtype)


def _inproj(x2d, mod3, w_in_bf, cos_t, s1_t, s2_t, ln_g, ln_b, ws, bs_full, bsz, t):
    n, d = x2d.shape
    tm = min(ROW_TILE, t)
    nt = t // tm
    pw = w_in_bf.shape[1]
    c_width = pw - 2 * A_WIDTH - 3 * B_WIDTH
    tok = lambda b, i: (b * nt + i, 0)
    const2 = lambda b, i: (0, 0)
    outs = [jax.ShapeDtypeStruct((n, A_WIDTH), BF16),
            jax.ShapeDtypeStruct((t, bsz * c_width), BF16)]
    out_specs = [pl.BlockSpec((tm, A_WIDTH), tok),
                 pl.BlockSpec((tm, c_width), lambda b, i: (i, b))]
    for _ in range(3):
        outs.append(jax.ShapeDtypeStruct((n, B_WIDTH), BF16))
        out_specs.append(pl.BlockSpec((tm, B_WIDTH), tok))
    copy_dils = [dil for dil in DILATIONS if _dilation_layout(dil) == "copy"]
    for _ in range(3):
        for dil in copy_dils:
            outs.append(jax.ShapeDtypeStruct((bsz, dil, t // dil, B_WIDTH), BF16))
            out_specs.append(pl.BlockSpec((1, dil, tm // dil, B_WIDTH), lambda b, i: (b, 0, i, 0)))
    res = pl.pallas_call(
        _inproj_kernel,
        grid=(bsz, nt),
        in_specs=[pl.BlockSpec((tm, d), tok),
                  pl.BlockSpec((1, 1, d), lambda b, i: (b, 0, 0)),
                  pl.BlockSpec((1, 1, d), lambda b, i: (b, 0, 1)),
                  pl.BlockSpec((d, pw), const2),
                  pl.BlockSpec((tm, LANES), tok),
                  pl.BlockSpec((tm, LANES), tok),
                  pl.BlockSpec((tm, LANES), tok),
                  pl.BlockSpec((1, A_WIDTH), const2),
                  pl.BlockSpec((1, A_WIDTH), const2),
                  pl.BlockSpec((A_HEADS, CHUNK, CHUNK), lambda b, i: (0, 0, 0)),
                  pl.BlockSpec((CHUNK, A_WIDTH), const2)],
        out_specs=out_specs,
        out_shape=outs,
        scratch_shapes=[pltpu.VMEM((B_WIDTH // LANES, tm, LANES), F32)] * 6,
        compiler_params=_cparams(("arbitrary", "arbitrary")),
        name="inproj_gmlp_rope",
    )(x2d, mod3, mod3, w_in_bf, cos_t, s1_t, s2_t, ln_g, ln_b, ws, bs_full)
    a_out, s_in = res[0], res[1]
    natural = res[2:5]
    n_copy = len(copy_dils)
    copies = {dil: tuple(res[5 + z * n_copy + g] for z in range(3)) for g, dil in enumerate(copy_dils)}
    qkv = [copies[dil] if dil in copies else natural for dil in DILATIONS]
    return a_out, s_in, qkv


def _att_kernel(q_ref, k_ref, v_ref, o_ref, st_ref, kcat_ref, vaug_ref):
    i = pl.program_id(2)
    blk = ATT_BLK
    n_res = q_ref.shape[1]
    n_sub = q_ref.shape[2] // blk
    n_pairs = B_WIDTH // LANES
    last = slice(n_sub * blk, (n_sub + 1) * blk)

    @pl.when(i == 0)
    def _():
        kcat_ref[:, 0:blk, :] = jnp.zeros((n_res, blk, B_WIDTH), BF16)
        vaug_ref[:, :, 0:blk, 0:LANES] = jnp.zeros((n_res, n_pairs, blk, LANES), BF16)
        vaug_ref[:, :, :, LANES:] = jnp.ones((n_res, n_pairs, (n_sub + 1) * blk, LANES), BF16)

    @pl.when(i > 0)
    def _():
        kcat_ref[:, 0:blk, :] = kcat_ref[:, last, :]
        vaug_ref[:, :, 0:blk, 0:LANES] = vaug_ref[:, :, last, 0:LANES]

    for rr in range(n_res):
        kcat_ref[rr, blk:, :] = k_ref[0, rr]
        for pair in range(n_pairs):
            vaug_ref[rr, pair, blk:, 0:LANES] = v_ref[0, rr, :, pair * LANES:(pair + 1) * LANES]

    qi = lax.broadcasted_iota(jnp.int32, (blk, 2 * blk), 0)
    kk = lax.broadcasted_iota(jnp.int32, (blk, 2 * blk), 1)
    band = (kk >= qi) & (kk <= qi + blk)
    first = band & ((i > 0) | (kk >= blk))
    lane = lax.broadcasted_iota(jnp.int32, (blk, LANES), 1)
    low_half = lane < B_HEAD_DIM
    nt = (((1,), (1,)), ((), ()))
    for rr, sub in [(rr, sub) for rr in range(n_res) for sub in range(n_sub)]:
        rows = slice(sub * blk, (sub + 1) * blk)
        keys = slice(sub * blk, (sub + 2) * blk)
        mask = first if sub == 0 else band
        stats = jnp.zeros((blk, LANES), F32)
        for pair in range(n_pairs):
            lanes = slice(pair * LANES, (pair + 1) * LANES)
            q2 = q_ref[0, rr, rows, lanes]
            kslab = kcat_ref[rr, keys, lanes]
            res = []
            for hh in range(2):
                hd = 2 * pair + hh
                qm = jnp.where(low_half if hh == 0 else jnp.logical_not(low_half), q2, jnp.zeros_like(q2))
                s = lax.dot_general(qm, kslab, nt, preferred_element_type=F32)
                s = jnp.where(mask, s, NEG_INF)
                m = jnp.max(jnp.maximum(s[:, :blk], s[:, blk:]), axis=1, keepdims=True)
                p = jnp.exp(s - m).astype(BF16)
                ov = jnp.dot(p, vaug_ref[rr, pair, keys, :], preferred_element_type=F32)
                den = ov[:, LANES:]
                res.append(ov[:, :LANES] / den)
                stats = jnp.where(lane == hd, m, stats)
                stats = jnp.where(lane == B_HEADS + hd, den, stats)
            o_ref[0, rr, rows, lanes] = jnp.where(low_half, res[0], res[1]).astype(o_ref.dtype)
        st_ref[0, rr, rows, :] = stats


def _dilation_layout(dil):
    return "natural" if dil == 1 else "copy"


def _att_branch(q, k, v, dil, bsz, t):
    rows = t // dil
    n_sub = min(ATT_SUB_BLOCKS, rows // ATT_BLK)
    step_rows = n_sub * ATT_BLK
    n_steps = rows // step_rows
    n_res = math.gcd(dil, max(1, ATT_SUB_BLOCKS // n_sub)) if n_steps == 1 else 1
    shape = lambda w: (bsz, dil, rows, w)
    blk = lambda w: pl.BlockSpec((1, n_res, step_rows, w), lambda b, r, i: (b, r, i, 0))
    q, k, v = (z.reshape(shape(B_WIDTH)) for z in (q, k, v))
    return pl.pallas_call(
        _att_kernel,
        grid=(bsz, dil // n_res, n_steps),
        in_specs=[blk(B_WIDTH), blk(B_WIDTH), blk(B_WIDTH)],
        out_specs=[blk(B_WIDTH), blk(LANES)],
        out_shape=[jax.ShapeDtypeStruct(shape(B_WIDTH), BF16),
                   jax.ShapeDtypeStruct(shape(LANES), F32)],
        scratch_shapes=[pltpu.VMEM((n_res, step_rows + ATT_BLK, B_WIDTH), BF16),
                        pltpu.VMEM((n_res, B_WIDTH // LANES, step_rows + ATT_BLK, 2 * LANES), BF16)],
        compiler_params=_cparams(("arbitrary", "arbitrary", "arbitrary")),
        name=f"dilated_attn_d{dil}",
    )(q, k, v)


def _ssm_kernel(u_ref, bmat_ref, cmat_ref, are_ref, aim_ref, dskip_ref, gw_ref, gb_ref,
                o_ref, state_ref, xbuf_ref):
    bsz, two_n = state_ref.shape
    n_state = two_n // 2
    steps = u_ref.shape[0] // bsz

    @pl.when(pl.program_id(0) == 0)
    def _():
        state_ref[...] = jnp.zeros_like(state_ref)

    u = u_ref[...]
    xbuf_ref[...] = jnp.dot(u, bmat_ref[...], preferred_element_type=F32)
    a_re = jnp.broadcast_to(are_ref[...], (bsz, n_state))
    a_im = jnp.broadcast_to(aim_ref[...], (bsz, n_state))

    def step(tt, carry):
        x_re, x_im = carry
        r0 = pl.multiple_of(tt * bsz, bsz)
        b_re = xbuf_ref[pl.ds(r0, bsz), :n_state]
        b_im = xbuf_ref[pl.ds(r0, bsz), n_state:]
        n_re = a_re * x_re - a_im * x_im + b_re
        n_im = a_re * x_im + a_im * x_re + b_im
        xbuf_ref[pl.ds(r0, bsz), :n_state] = n_re
        xbuf_ref[pl.ds(r0, bsz), n_state:] = n_im
        return n_re, n_im

    x_re, x_im = lax.fori_loop(0, steps, step, (state_ref[:, :n_state], state_ref[:, n_state:]))
    state_ref[:, :n_state] = x_re
    state_ref[:, n_state:] = x_im

    y = jnp.dot(xbuf_ref[...].astype(BF16), cmat_ref[...], preferred_element_type=F32)
    y = _gelu_tanh(y + dskip_ref[...] * u.astype(F32))
    gate = jax.nn.sigmoid(jnp.dot(y.astype(BF16), gw_ref[...], preferred_element_type=F32) + gb_ref[...])
    o_ref[...] = (y * gate).astype(o_ref.dtype)


def _ssm_weights(lam_re, lam_im, log_dt, b_re, b_im, c_re, c_im):
    g, n_st = lam_re.shape
    cg = b_re.shape[-1]
    dt = jnp.exp(log_dt)[:, None]
    mag = jnp.exp(lam_re * dt)
    ab_re = mag * jnp.cos(lam_im * dt)
    ab_im = mag * jnp.sin(lam_im * dt)
    nr = ab_re - 1.0
    ni = ab_im
    mod2 = lam_re * lam_re + lam_im * lam_im
    f_re = (nr * lam_re + ni * lam_im) / mod2
    f_im = (ni * lam_re - nr * lam_im) / mod2
    bb_re = f_re[..., None] * b_re - f_im[..., None] * b_im
    bb_im = f_re[..., None] * b_im + f_im[..., None] * b_re
    eye = jnp.eye(g, dtype=F32)
    bm_re = jnp.einsum('gnc,gh->gchn', bb_re, eye).reshape(g * cg, g * n_st)
    bm_im = jnp.einsum('gnc,gh->gchn', bb_im, eye).reshape(g * cg, g * n_st)
    bmat = jnp.concatenate([bm_re, bm_im], axis=1)
    cm_re = jnp.einsum('gcn,gh->gnhc', c_re, eye).reshape(g * n_st, g * cg)
    cm_im = jnp.einsum('gcn,gh->gnhc', c_im, eye).reshape(g * n_st, g * cg)
    cmat = jnp.concatenate([cm_re, -cm_im], axis=0)
    return (bmat.astype(BF16), cmat.astype(BF16),
            ab_re.reshape(1, g * n_st), ab_im.reshape(1, g * n_st))


def _ssm(u_tm, bmat, cmat, a_re, a_im, d_skip, glu_w_bf, glu_b, bsz, t):
    cw = bmat.shape[0]
    two_n = bmat.shape[1]
    steps = min(SSM_STEPS, t)
    rows = steps * bsz
    u2 = u_tm.reshape(t * bsz, cw)
    const = lambda i: (0, 0)
    return pl.pallas_call(
        _ssm_kernel,
        grid=(t // steps,),
        in_specs=[pl.BlockSpec((rows, cw), lambda i: (i, 0)),
                  pl.BlockSpec((cw, two_n), const),
                  pl.BlockSpec((two_n, cw), const),
                  pl.BlockSpec((1, two_n // 2), const),
                  pl.BlockSpec((1, two_n // 2), const),
                  pl.BlockSpec((1, cw), const),
                  pl.BlockSpec((cw, cw), const),
                  pl.BlockSpec((1, cw), const)],
        out_specs=pl.BlockSpec((rows, cw), lambda i: (i, 0)),
        out_shape=jax.ShapeDtypeStruct((t * bsz, cw), BF16),
        scratch_shapes=[pltpu.VMEM((bsz, two_n), F32), pltpu.VMEM((rows, two_n), F32)],
        compiler_params=_cparams(("arbitrary",)),
        name="s5_scan_glu",
    )(u2, bmat, cmat, a_re, a_im, d_skip, glu_w_bf, glu_b).reshape(t, bsz * cw)


def _route(sel, scores):
    gs = []
    for g in range(N_EXPERT_GROUPS):
        a, b, c, d = sel[4 * g: 4 * g + 4]
        hi1, lo1 = jnp.maximum(a, b), jnp.minimum(a, b)
        hi2, lo2 = jnp.maximum(c, d), jnp.minimum(c, d)
        gs.append(jnp.maximum(hi1, hi2) + jnp.maximum(jnp.minimum(hi1, hi2), jnp.maximum(lo1, lo2)))
    g_idx = jnp.zeros(gs[0].shape, jnp.int32)
    best = gs[0]
    for g in range(1, N_EXPERT_GROUPS):
        better = gs[g] > best
        g_idx = jnp.where(better, g, g_idx)
        best = jnp.where(better, gs[g], best)

    def pick(rows, j):
        out = rows[j]
        for g in range(1, N_EXPERT_GROUPS):
            out = jnp.where(g_idx == g, rows[4 * g + j], out)
        return out

    v = [pick(sel, j) for j in range(EXPERTS_PER_GROUP)]
    s = [pick(scores, j) for j in range(EXPERTS_PER_GROUP)]
    i1 = jnp.zeros(g_idx.shape, jnp.int32)
    b1, g1 = v[0], s[0]
    for j in range(1, EXPERTS_PER_GROUP):
        better = v[j] > b1
        i1 = jnp.where(better, j, i1)
        b1 = jnp.where(better, v[j], b1)
        g1 = jnp.where(better, s[j], g1)
    i2 = jnp.full(g_idx.shape, -1, jnp.int32)
    b2 = jnp.zeros_like(b1)
    g2 = jnp.zeros_like(g1)
    for j in range(EXPERTS_PER_GROUP):
        better = (i1 != j) & ((i2 < 0) | (v[j] > b2))
        i2 = jnp.where(better, j, i2)
        b2 = jnp.where(better, v[j], b2)
        g2 = jnp.where(better, s[j], g2)
    tot = g1 + g2
    base = g_idx * EXPERTS_PER_GROUP
    return base + i1, base + i2, g1 / tot, g2 / tot


def _natural_order(ref, scr_ref, dil):
    if _dilation_layout(dil) != "copy":
        return ref[...].astype(F32)
    per = ref.shape[2]
    n_slab = scr_ref.shape[0]
    for r in range(dil):
        for j in range(n_slab):
            scr_ref[j, pl.ds(r, per, stride=dil), :] = ref[0, r, :, j * LANES:(j + 1) * LANES].astype(F32)
    if n_slab == 1:
        return scr_ref[0]
    return jnp.concatenate([scr_ref[j] for j in range(n_slab)], axis=1)


def _outproj_kernel(a_ref, o1_ref, o2_ref, o3_ref, st1_ref, st2_ref, st3_ref,
                    c_ref, x_ref, g1_ref, sh2_ref, sc2_ref, wout_ref, expand_ref, lng_ref, lnb_ref,
                    rwh_ref, rwl_ref, rb_ref, x1_ref, h2_ref, eidx_ref, gate_ref, count_ref,
                    oscr2_ref, oscr3_ref, sscr2_ref, sscr3_ref, *, alpha):
    o_refs = (o1_ref, o2_ref, o3_ref)
    st_refs = (st1_ref, st2_ref, st3_ref)
    o_scr = (None, oscr2_ref, oscr3_ref)
    st_scr = (None, sscr2_ref, sscr3_ref)
    ms = [_natural_order(st_refs[g], st_scr[g], DILATIONS[g]) for g in range(len(DILATIONS))]
    dens = [pltpu.roll(m, LANES - B_HEADS, 1) for m in ms]
    mx = jnp.maximum(jnp.maximum(ms[0], ms[1]), ms[2])
    ws = [dens[g] * jnp.exp(ms[g] - mx) for g in range(len(DILATIONS))]
    lane = lax.broadcasted_iota(jnp.int32, mx.shape, 1)
    tot = jnp.where(lane < B_HEADS, ws[0] + ws[1] + ws[2], 1.0)
    expand = expand_ref[...]

    def widen(w):
        wn = jnp.where(lane < B_HEADS, w / tot, 0.0)
        hi = wn.astype(BF16)
        lo = (wn - hi.astype(F32)).astype(BF16)
        return (jnp.dot(hi, expand, preferred_element_type=F32)
                + jnp.dot(lo, expand, preferred_element_type=F32))

    b_out = widen(ws[0]) * _natural_order(o_refs[0], o_scr[0], DILATIONS[0])
    for g in range(1, len(DILATIONS)):
        b_out = b_out + widen(ws[g]) * _natural_order(o_refs[g], o_scr[g], DILATIONS[g])
    b_out = b_out.astype(BF16)

    wout = wout_ref
    mix = (jnp.dot(a_ref[...], wout[:A_WIDTH, :], preferred_element_type=F32)
           + jnp.dot(b_out, wout[A_WIDTH:A_WIDTH + B_WIDTH, :], preferred_element_type=F32)
           + jnp.dot(c_ref[...], wout[A_WIDTH + B_WIDTH:, :], preferred_element_type=F32))
    x1 = _ln(alpha * x_ref[...] + (1.0 + g1_ref[0]) * mix) * lng_ref[...] + lnb_ref[...]
    x1_ref[...] = x1
    h2 = _ln(x1) * (1.0 + sc2_ref[0]) + sh2_ref[0]
    _store_tile_rows(h2_ref, h2)

    hi = h2.astype(BF16)
    lo = (h2 - hi.astype(F32)).astype(BF16)
    nt = (((1,), (1,)), ((), ()))
    logits = (lax.dot_general(rwh_ref[...], hi, nt, preferred_element_type=F32)
              + lax.dot_general(rwh_ref[...], lo, nt, preferred_element_type=F32)
              + lax.dot_general(rwl_ref[...], hi, nt, preferred_element_type=F32))
    scores = jax.nn.sigmoid(logits)
    sel = scores + rb_ref[...]
    sel_rows = [sel[e:e + 1, :] for e in range(N_EXPERTS)]
    score_rows = [scores[e:e + 1, :] for e in range(N_EXPERTS)]
    e1, e2, gt1, gt2 = _route(sel_rows, score_rows)

    @pl.when((pl.program_id(0) == 0) & (pl.program_id(1) == 0))
    def _():
        count_ref[...] = jnp.zeros_like(count_ref)

    tm = h2.shape[0]
    erow = lax.broadcasted_iota(jnp.int32, (N_EXPERTS, tm), 0)
    oh1 = erow == e1
    oh2 = erow == e2
    both = (oh1 | oh2).astype(BF16)
    earlier = (lax.broadcasted_iota(jnp.int32, (tm, tm), 0)
               < lax.broadcasted_iota(jnp.int32, (tm, tm), 1)).astype(BF16)
    base = count_ref[:, 0:1]
    cnt = jnp.dot(both, earlier, preferred_element_type=F32) + base
    r1 = jnp.sum(jnp.where(oh1, cnt, 0.0), axis=0, keepdims=True)
    r2 = jnp.sum(jnp.where(oh2, cnt, 0.0), axis=0, keepdims=True)
    total = base + jnp.sum(both.astype(F32), axis=1, keepdims=True)
    count_ref[...] = jnp.broadcast_to(total, count_ref.shape)

    eidx_ref[...] = jnp.zeros_like(eidx_ref)
    gate_ref[...] = jnp.zeros_like(gate_ref)
    eidx_ref[0:1, :] = e1
    eidx_ref[1:2, :] = e2
    eidx_ref[2:3, :] = r1.astype(jnp.int32)
    eidx_ref[3:4, :] = r2.astype(jnp.int32)
    gate_ref[0:1, :] = gt1
    gate_ref[1:2, :] = gt2


def _outproj(a_out, att, c_tm, x2d, mod3, w_out_bf, ln_g, ln_b, rw_hi, rw_lo, rbias, bsz, t, alpha):
    n, d = x2d.shape
    tm = min(ROW_TILE, t)
    nt = t // tm
    cw = c_tm.shape[1] // bsz
    tok = lambda b, i: (b * nt + i, 0)
    const2 = lambda b, i: (0, 0)
    head = jnp.arange(B_WIDTH) // B_HEAD_DIM
    expand = (jnp.arange(LANES)[:, None] == head[None, :]).astype(BF16)

    def branch_input(z, dil, w):
        if _dilation_layout(dil) == "copy":
            return z, pl.BlockSpec((1, dil, tm // dil, w), lambda b, i: (b, 0, i, 0))
        return z.reshape(n, w), pl.BlockSpec((tm, w), tok)

    o_in = [branch_input(o, dil, B_WIDTH) for (o, _), dil in zip(att, DILATIONS)]
    st_in = [branch_input(st, dil, LANES) for (_, st), dil in zip(att, DILATIONS)]
    (o1, st1), (o2, st2), (o3, st3) = [(o_in[g][0], st_in[g][0]) for g in range(len(DILATIONS))]
    modspec = lambda j: pl.BlockSpec((1, 1, d), lambda b, i: (b, 0, j))
    return pl.pallas_call(
        functools.partial(_outproj_kernel, alpha=alpha),
        grid=(bsz, nt),
        in_specs=[pl.BlockSpec((tm, A_WIDTH), tok)]
                 + [spec for _, spec in o_in]
                 + [spec for _, spec in st_in]
                 + [pl.BlockSpec((tm, cw), lambda b, i: (i, b)),
                  pl.BlockSpec((tm, d), tok),
                  modspec(2), modspec(3), modspec(4),
                  pl.BlockSpec((d, d), const2),
                  pl.BlockSpec((LANES, B_WIDTH), const2),
                  pl.BlockSpec((1, d), const2), pl.BlockSpec((1, d), const2),
                  pl.BlockSpec((N_EXPERTS, d), const2), pl.BlockSpec((N_EXPERTS, d), const2),
                  pl.BlockSpec((N_EXPERTS, 1), const2)],
        out_specs=[pl.BlockSpec((tm, d), tok), pl.BlockSpec((tm * (d // LANES), LANES), tok),
                   pl.BlockSpec((SUBLANES, tm), lambda b, i: (0, b * nt + i)),
                   pl.BlockSpec((SUBLANES, tm), lambda b, i: (0, b * nt + i)),
                   pl.BlockSpec((N_EXPERTS, LANES), const2)],
        out_shape=[jax.ShapeDtypeStruct((n, d), F32),
                   jax.ShapeDtypeStruct((n * (d // LANES), LANES), F32),

                   jax.ShapeDtypeStruct((SUBLANES, n), jnp.int32),
                   jax.ShapeDtypeStruct((SUBLANES, n), F32),
                   jax.ShapeDtypeStruct((N_EXPERTS, LANES), F32)],
        scratch_shapes=[pltpu.VMEM((B_WIDTH // LANES, tm, LANES), F32),
                        pltpu.VMEM((B_WIDTH // LANES, tm, LANES), F32),
                        pltpu.VMEM((1, tm, LANES), F32), pltpu.VMEM((1, tm, LANES), F32)],
        compiler_params=_cparams(("arbitrary", "arbitrary")),
        name="outproj_ln_router",
    )(a_out, o1, o2, o3, st1, st2, st3, c_tm, x2d, mod3, mod3, mod3, w_out_bf, expand,
      ln_g, ln_b, rw_hi, rw_lo, rbias)


def _dispatch_plan(eidx, counts, n_tok, blk):
    e = eidx[:TOP_K]
    rank = eidx[TOP_K:2 * TOP_K]
    pcounts = (counts + blk - 1) // blk * blk
    pends = jnp.cumsum(pcounts)
    pstarts = pends - pcounts
    seg_start = jnp.zeros_like(e)
    for j in range(N_EXPERTS):
        seg_start = jnp.where(e == j, pstarts[j], seg_start)
    dest = (seg_start + rank).astype(jnp.int32)
    n_blk = n_tok * TOP_K // blk + N_EXPERTS
    blk_start = jnp.arange(n_blk, dtype=jnp.int32) * blk
    blk_e = jnp.minimum(jnp.sum((pends[None, :] <= blk_start[:, None]).astype(jnp.int32), axis=1),
                        N_EXPERTS - 1).astype(jnp.int32)
    n_used = (pends[-1] // blk).astype(jnp.int32).reshape(1)
    return dest, blk_e, n_used, pends.astype(jnp.int32), pcounts.astype(jnp.int32)


def _dispatch_kernel(pend_ref, pcnt_ref, dest_ref, h2_ref, xs_hbm, zbuf, sem, zsem, *, blk, n_slab):
    i = pl.program_id(0)
    rows = h2_ref.shape[0] // n_slab
    blk_rows = blk * n_slab

    @pl.when(i == 0)
    def _():
        zbuf[...] = jnp.zeros_like(zbuf)
        for e in range(N_EXPERTS):
            @pl.when(pcnt_ref[e] > 0)
            def _():
                start = pl.multiple_of((pend_ref[e] - blk) * n_slab, blk_rows)
                cp = pltpu.make_async_copy(zbuf, xs_hbm.at[pl.ds(start, blk_rows), :], zsem)
                cp.start()
                cp.wait()
        n_blk = xs_hbm.shape[0] // blk_rows
        for t in range(N_EXPERTS):
            tail = pend_ref[N_EXPERTS - 1] // blk + t

            @pl.when(tail < n_blk)
            def _():
                start = pl.multiple_of(tail * blk_rows, blk_rows)
                cp = pltpu.make_async_copy(zbuf, xs_hbm.at[pl.ds(start, blk_rows), :], zsem)
                cp.start()
                cp.wait()

    def issue(s, c):
        src = h2_ref.at[pl.ds(pl.multiple_of(s * n_slab, n_slab), n_slab), :]
        for kk in range(TOP_K):
            slot = pl.multiple_of(dest_ref[0, 0, kk * rows + s] * n_slab, n_slab)
            pltpu.make_async_copy(src, xs_hbm.at[pl.ds(slot, n_slab), :], sem).start()
        return c
    lax.fori_loop(0, rows, issue, 0, unroll=8)
    for kk in range(TOP_K):
        pltpu.make_async_copy(h2_ref, xs_hbm.at[pl.ds(0, rows * n_slab), :], sem).wait()


def _expert_kernel(blk_e_ref, nused_ref, xs_ref, wg_ref, wu_ref, wd_ref, ys_ref, wg_bf, wu_bf, wd_bf,
                   *, blk, n_slab):
    j = pl.program_id(0)

    @pl.when((j == 0) | (blk_e_ref[j] != blk_e_ref[jnp.maximum(j - 1, 0)]))
    def _():
        wg_bf[...] = wg_ref[0, 0].astype(BF16)
        wu_bf[...] = wu_ref[0, 0].astype(BF16)
        wd_bf[...] = wd_ref[0, 0].astype(BF16)

    @pl.when(j < nused_ref[0])
    def _():
        rows = _load_tile_rows(xs_ref, 0, blk, n_slab).astype(BF16)
        gt = jnp.dot(rows, wg_bf[...], preferred_element_type=F32)
        up = jnp.dot(rows, wu_bf[...], preferred_element_type=F32)
        hid = (gt * jax.nn.sigmoid(gt) * up).astype(BF16)
        _store_tile_rows(ys_ref, jnp.dot(hid, wd_bf[...], preferred_element_type=F32))

    @pl.when(j >= nused_ref[0])
    def _():
        ys_ref[...] = jnp.zeros_like(ys_ref)


def _moe(h2, eidx, counts, w_gate, w_up, w_down, layer):
    d, de = w_gate.shape[-2:]
    n_slab = d // LANES
    n_tok = h2.shape[0] // n_slab
    blk = min(MOE_BLK, n_tok)
    dest, blk_e, n_used, pends, pcounts = _dispatch_plan(eidx, counts, n_tok, blk)
    n_blk = blk_e.shape[0]
    cap = n_blk * blk

    rows = min(ROW_TILE, n_tok)
    nt = n_tok // rows
    dest_tiles = dest.reshape(TOP_K, nt, rows).transpose(1, 0, 2).reshape(nt, 1, TOP_K * rows)
    xs = pl.pallas_call(
        functools.partial(_dispatch_kernel, blk=blk, n_slab=n_slab),
        grid_spec=pltpu.PrefetchScalarGridSpec(
            num_scalar_prefetch=2,
            grid=(nt,),
            in_specs=[pl.BlockSpec((1, 1, TOP_K * rows), lambda i, pe, pc: (i, 0, 0),
                                   memory_space=pltpu.SMEM),
                      pl.BlockSpec((rows * n_slab, LANES), lambda i, pe, pc: (i, 0))],
            out_specs=pl.BlockSpec(memory_space=pl.ANY),
            scratch_shapes=[pltpu.VMEM((blk * n_slab, LANES), F32),
                            pltpu.SemaphoreType.DMA, pltpu.SemaphoreType.DMA],
        ),
        out_shape=jax.ShapeDtypeStruct((cap * n_slab, LANES), F32),
        compiler_params=_cparams(("arbitrary",)),
        name="moe_dispatch",
    )(pends, pcounts, dest_tiles, h2)

    used = lambda j, be, nu: (jnp.minimum(j, nu[0] - 1), 0)
    ys = pl.pallas_call(
        functools.partial(_expert_kernel, blk=blk, n_slab=n_slab),
        grid_spec=pltpu.PrefetchScalarGridSpec(
            num_scalar_prefetch=2,
            grid=(n_blk,),
            in_specs=[pl.BlockSpec((blk * n_slab, LANES), used),
                      pl.BlockSpec((1, 1, d, de), lambda j, be, nu: (layer, be[j], 0, 0)),
                      pl.BlockSpec((1, 1, d, de), lambda j, be, nu: (layer, be[j], 0, 0)),
                      pl.BlockSpec((1, 1, de, d), lambda j, be, nu: (layer, be[j], 0, 0))],
            out_specs=pl.BlockSpec((blk * n_slab, LANES), lambda j, be, nu: (j, 0)),
            scratch_shapes=[pltpu.VMEM((d, de), BF16), pltpu.VMEM((d, de), BF16), pltpu.VMEM((de, d), BF16)],
        ),
        out_shape=jax.ShapeDtypeStruct((cap * n_slab, LANES), F32),
        compiler_params=_cparams(("arbitrary",)),
        name="moe_experts",
    )(blk_e, n_used, xs, w_gate, w_up, w_down)
    return ys, dest_tiles


def _ffn_ln_kernel(dest_ref, next_ref, x_ref, gate_ref, g2_ref, lng_ref, lnb_ref, ys_hbm, o_ref, ybuf, sem,
                   *, alpha):
    tm, d = x_ref.shape
    n_slab = d // LANES
    g = pl.program_id(0) * pl.num_programs(1) + pl.program_id(1)
    n_tiles = pl.num_programs(0) * pl.num_programs(1)
    slot = g % 2

    def gather(idx_ref, buf_slot):
        def issue(s, c):
            src = pl.multiple_of(idx_ref[0, 0, s] * n_slab, n_slab)
            dst = pl.multiple_of(s * n_slab, n_slab)
            pltpu.make_async_copy(ys_hbm.at[pl.ds(src, n_slab), :],
                                  ybuf.at[buf_slot, pl.ds(dst, n_slab), :], sem.at[buf_slot]).start()
            return c
        lax.fori_loop(0, TOP_K * tm, issue, 0, unroll=8)

    @pl.when(g == 0)
    def _():
        gather(dest_ref, 0)

    @pl.when(g + 1 < n_tiles)
    def _():
        gather(next_ref, 1 - slot)

    pltpu.make_async_copy(ys_hbm.at[pl.ds(0, TOP_K * tm * n_slab), :], ybuf.at[slot], sem.at[slot]).wait()

    gate = gate_ref[...]
    buf = ybuf.at[slot]
    ffn = _load_tile_rows(buf, 0, tm, n_slab) * gate[:, 0:1]
    for kk in range(1, TOP_K):
        ffn = ffn + _load_tile_rows(buf, kk * tm, tm, n_slab) * gate[:, kk:kk + 1]
    o_ref[...] = _ln(alpha * x_ref[...] + (1.0 + g2_ref[0]) * ffn) * lng_ref[...] + lnb_ref[...]


def _ffn_ln(x1, ys, dest_tiles, gates, mod3, ln_g, ln_b, bsz, t, alpha):
    n, d = x1.shape
    n_slab = d // LANES
    tm = dest_tiles.shape[2] // TOP_K
    nt = t // tm
    n_tiles = bsz * nt
    tok = lambda b, i, *_: (b * nt + i, 0)
    const2 = lambda b, i, *_: (0, 0)
    gate_cols = gates[:TOP_K].T
    idx_blk = (1, 1, TOP_K * tm)
    return pl.pallas_call(
        functools.partial(_ffn_ln_kernel, alpha=alpha),
        grid=(bsz, nt),
        in_specs=[pl.BlockSpec(idx_blk, lambda b, i: (b * nt + i, 0, 0), memory_space=pltpu.SMEM),
                  pl.BlockSpec(idx_blk, lambda b, i: (jnp.minimum(b * nt + i + 1, n_tiles - 1), 0, 0),
                               memory_space=pltpu.SMEM),
                  pl.BlockSpec((tm, d), tok), pl.BlockSpec((tm, TOP_K), tok),
                  pl.BlockSpec((1, 1, d), lambda b, i: (b, 0, 5)),
                  pl.BlockSpec((1, d), const2), pl.BlockSpec((1, d), const2),
                  pl.BlockSpec(memory_space=pl.ANY)],
        out_specs=pl.BlockSpec((tm, d), tok),
        out_shape=jax.ShapeDtypeStruct((n, d), F32),
        scratch_shapes=[pltpu.VMEM((2, TOP_K * tm * n_slab, LANES), F32), pltpu.SemaphoreType.DMA((2,))],
        compiler_params=_cparams(("arbitrary", "arbitrary")),
        name="ffn_combine_ln",
    )(dest_tiles, dest_tiles, x1, gate_cols, mod3, ln_g, ln_b, ys)


def kernel(x, c, positions, ada_w, ada_b, w_in, gm_ln_g, gm_ln_b, gm_ws, gm_bs, ssm_lam_re, ssm_lam_im, ssm_log_dt, ssm_b_re, ssm_b_im, ssm_c_re, ssm_c_im, ssm_d, glu_w, glu_b, w_out, ln1_g, ln1_b, router_w, router_bias, exp_w_gate, exp_w_up, exp_w_down, ln2_g, ln2_b):
    bsz, t, d = x.shape
    depth = ada_w.shape[0]
    alpha = (2.0 * depth) ** 0.25
    n = bsz * t

    mod = _adaln_mod(c.astype(F32), ada_w, ada_b)
    cos_t, s1_t, s2_t = _rope_tables(positions)
    rw_t = router_w.astype(F32).T
    rw_hi = rw_t.astype(BF16)
    rw_lo = (rw_t - rw_hi.astype(F32)).astype(BF16)
    rbias = router_bias.astype(F32).reshape(N_EXPERTS, 1)

    xf = x.astype(F32).reshape(n, d)
    for l in range(depth):
        mod3 = mod[l].reshape(bsz, 1, 6 * d)
        bs_full = jnp.repeat(gm_bs[l].T, A_HEAD_DIM, axis=1)
        a_out, s_in, qkv = _inproj(
            xf, mod3, w_in[l].astype(BF16), cos_t, s1_t, s2_t,
            gm_ln_g[l].reshape(1, A_WIDTH), gm_ln_b[l].reshape(1, A_WIDTH), gm_ws[l], bs_full, bsz, t)
        att = [_att_branch(*qkv[g], dil, bsz, t) for g, dil in enumerate(DILATIONS)]
        bmat, cmat, a_re, a_im = _ssm_weights(ssm_lam_re[l], ssm_lam_im[l], ssm_log_dt[l],
                                              ssm_b_re[l], ssm_b_im[l], ssm_c_re[l], ssm_c_im[l])
        cw = bmat.shape[0]
        c_out = _ssm(s_in, bmat, cmat, a_re, a_im, ssm_d[l].reshape(1, cw),
                     glu_w[l].astype(BF16), glu_b[l].reshape(1, cw), bsz, t)
        x1, h2, eidx, gates, counts = _outproj(a_out, att, c_out, xf, mod3, w_out[l].astype(BF16),
                                               ln1_g[l].reshape(1, d), ln1_b[l].reshape(1, d),
                                               rw_hi, rw_lo, rbias, bsz, t, alpha)
        ys, dest_tiles = _moe(h2, eidx, counts[:, 0].astype(jnp.int32),
                              exp_w_gate.astype(F32), exp_w_up.astype(F32), exp_w_down.astype(F32), l)
        xf = _ffn_ln(x1, ys, dest_tiles, gates, mod3, ln2_g[l].reshape(1, d), ln2_b[l].reshape(1, d),
                     bsz, t, alpha)
    return xf.reshape(bsz, t, d)
```

```python
import functools
import math

import jax
import jax.numpy as jnp
from jax import lax
from jax.experimental import pallas as pl
from jax.experimental.pallas import tpu as pltpu

F32 = jnp.float32
BF16 = jnp.bfloat16

A_HEADS = 4
A_HEAD_DIM = 64
A_WIDTH = A_HEADS * A_HEAD_DIM
CHUNK = 128
B_HEADS = 8
B_HEAD_DIM = 64
B_WIDTH = B_HEADS * B_HEAD_DIM
DILATIONS = (1, 4, 16)
ATT_BLK = 128
ROT_DIM = B_HEAD_DIM // 4
ROPE_THETA = 500000.0
SSM_GROUP = 16
SSM_STATE = 64
N_EXPERTS = 16
N_EXPERT_GROUPS = 4
EXPERTS_PER_GROUP = 4
TOP_K = 2
LN_EPS = 1e-5
NEG_INF = -1e30

LANES = 128
SUBLANES = 8
VMEM_LIMIT = 56 * 1024 * 1024
ROW_TILE = 512
SSM_STEPS = 128
MOE_BLK = 512
MOE_ROW_TILE = 1024
ATT_SUB_BLOCKS = 8


def _cparams(sem):
    return pltpu.CompilerParams(dimension_semantics=sem, vmem_limit_bytes=VMEM_LIMIT)


def _ln(x):
    mu = jnp.mean(x, axis=-1, keepdims=True)
    xc = x - mu
    var = jnp.mean(xc * xc, axis=-1, keepdims=True)
    return xc * lax.rsqrt(var + LN_EPS)


def _store_tile_rows(ref, val):
    n_slab = val.shape[1] // LANES
    for j in range(n_slab):
        ref[pl.ds(j, val.shape[0], stride=n_slab), :] = val[:, j * LANES:(j + 1) * LANES]


def _load_tile_rows(ref, start, rows, n_slab):
    return jnp.concatenate([ref[pl.ds(start * n_slab + j, rows, stride=n_slab), :] for j in range(n_slab)],
                           axis=1)


def _gelu_tanh(x):
    return 0.5 * x * (1.0 + jnp.tanh(math.sqrt(2.0 / math.pi) * (x + 0.044715 * (x * x * x))))


def _mod_kernel(c_ref, w_ref, b_ref, o_ref):
    c = c_ref[...]
    cond = c * jax.nn.sigmoid(c)
    o_ref[0] = jnp.dot(cond, w_ref[0], precision=lax.Precision.HIGHEST,
                       preferred_element_type=F32) + b_ref[0]


def _adaln_mod(c, ada_w, ada_b):
    depth, d, n = ada_w.shape
    bsz = c.shape[0]
    nb = d
    return pl.pallas_call(
        _mod_kernel,
        grid=(depth, n // nb),
        in_specs=[pl.BlockSpec((bsz, d), lambda l, j: (0, 0)),
                  pl.BlockSpec((1, d, nb), lambda l, j: (l, 0, j)),
                  pl.BlockSpec((1, 1, nb), lambda l, j: (l, 0, j))],
        out_specs=pl.BlockSpec((1, bsz, nb), lambda l, j: (l, 0, j)),
        out_shape=jax.ShapeDtypeStruct((depth, bsz, n), F32),
        compiler_params=_cparams(("arbitrary", "arbitrary")),
        name="adaln_mod",
    )(c, ada_w, ada_b.reshape(depth, 1, n))


def _rope_kernel(pos_ref, freq_ref, cos_ref, s1_ref, s2_ref):
    ang = pos_ref[...].astype(F32) * freq_ref[...]
    lane = lax.broadcasted_iota(jnp.int32, ang.shape, 1) % B_HEAD_DIM
    sn = jnp.sin(ang)
    cos_ref[...] = jnp.cos(ang)
    s1_ref[...] = jnp.where(lane < ROT_DIM // 2, -sn, 0.0)
    s2_ref[...] = jnp.where((lane >= ROT_DIM // 2) & (lane < ROT_DIM), sn, 0.0)


def _rope_tables(positions):
    n = positions.size
    half = ROT_DIM // 2
    lane = jnp.arange(LANES) % B_HEAD_DIM
    freqs = ROPE_THETA ** (-(lane % half).astype(F32) * 2.0 / ROT_DIM)
    freq_row = jnp.where(lane < ROT_DIM, freqs, 0.0).reshape(1, LANES).astype(F32)
    tm = min(2048, n)
    out = jax.ShapeDtypeStruct((n, LANES), F32)
    spec = pl.BlockSpec((tm, LANES), lambda i: (i, 0))
    return pl.pallas_call(
        _rope_kernel,
        grid=(n // tm,),
        in_specs=[pl.BlockSpec((tm, 1), lambda i: (i, 0)),
                  pl.BlockSpec((1, LANES), lambda i: (0, 0))],
        out_specs=[spec, spec, spec],
        out_shape=[out, out, out],
        compiler_params=_cparams(("arbitrary",)),
        name="rope_tables",
    )(positions.reshape(n, 1), freq_row)


def _inproj_kernel(x_ref, sh_ref, sc_ref, w_ref, cos_ref, s1_ref, s2_ref, lng_ref, lnb_ref,
                   ws_ref, bs_ref, a_ref, s_ref, qn_ref, kn_ref, vn_ref, *rest):
    copy_dils = [dil for dil in DILATIONS if _dilation_layout(dil) == "copy"]
    n_copy = len(copy_dils)
    q_outs, k_outs, v_outs = rest[:n_copy], rest[n_copy:2 * n_copy], rest[2 * n_copy:3 * n_copy]
    qs_ref, ks_ref, vs_ref, qt_ref, kt_ref, vt_ref = rest[3 * n_copy:]
    tm = x_ref.shape[0]
    h = _ln(x_ref[...]) * (1.0 + sc_ref[0]) + sh_ref[0]
    proj = jnp.dot(h.astype(BF16), w_ref[...], preferred_element_type=F32)

    uv = _gelu_tanh(proj[:, :2 * A_WIDTH])
    u = uv[:, :A_WIDTH]
    v = (_ln(uv[:, A_WIDTH:]) * lng_ref[...] + lnb_ref[...]).astype(BF16)
    row = lax.broadcasted_iota(jnp.int32, (CHUNK, CHUNK), 0)
    col = lax.broadcasted_iota(jnp.int32, (CHUNK, CHUNK), 1)
    head_of_lane = lax.broadcasted_iota(jnp.int32, (CHUNK, A_WIDTH), 1) // A_HEAD_DIM
    w_heads = [jnp.where(col <= row, ws_ref[hd], 0.0).astype(BF16) for hd in range(A_HEADS)]
    for cidx in range(tm // CHUNK):
        rows = slice(cidx * CHUNK, (cidx + 1) * CHUNK)
        vc = v[rows]
        sv = bs_ref[...]
        for hd in range(A_HEADS):
            full = jnp.dot(w_heads[hd], vc, preferred_element_type=F32)
            sv = sv + jnp.where(head_of_lane == hd, full, 0.0)
        a_ref[rows, :] = (u[rows] * sv).astype(a_ref.dtype)

    cos = cos_ref[...]
    s1 = s1_ref[...]
    s2 = s2_ref[...]
    q0 = 2 * A_WIDTH
    k0 = q0 + B_WIDTH
    v0 = k0 + B_WIDTH
    for j in range(B_WIDTH // LANES):
        lanes = slice(j * LANES, (j + 1) * LANES)
        for base, ref, nat, scale in ((q0, qs_ref, qn_ref, B_HEAD_DIM ** -0.5), (k0, ks_ref, kn_ref, 1.0)):
            xs = proj[:, base + j * LANES: base + (j + 1) * LANES]
            rot = (xs * cos + pltpu.roll(xs, LANES - ROT_DIM // 2, 1) * s1
                   + pltpu.roll(xs, ROT_DIM // 2, 1) * s2) * scale
            nat[:, lanes] = rot.astype(nat.dtype)
            ref[j] = rot
        vs = proj[:, v0 + j * LANES:v0 + (j + 1) * LANES]
        vn_ref[:, lanes] = vs.astype(vn_ref.dtype)
        vs_ref[j] = vs

    for src, stage, outs in ((qs_ref, qt_ref, q_outs), (ks_ref, kt_ref, k_outs), (vs_ref, vt_ref, v_outs)):
        bufs = (src, stage)
        prev = 1
        for idx, (dil, out) in enumerate(zip(copy_dils, outs)):
            chained = prev > 1 and dil % prev == 0
            base_dil = prev if chained else 1
            source = bufs[idx % 2] if chained or idx == 0 else None
            assert source is not None, "copy dilations must form a divisibility chain"
            step = dil // base_dil
            per = tm // dil
            keep = idx + 1 < len(copy_dils)
            for r_prev in range(base_dil):
                for q in range(step):
                    r = r_prev + base_dil * q
                    start = r_prev * (tm // base_dil) + q
                    for j in range(B_WIDTH // LANES):
                        val = source[j, pl.ds(start, per, stride=step), :]
                        out[0, r, :, j * LANES:(j + 1) * LANES] = val.astype(out.dtype)
                        if keep:
                            bufs[(idx + 1) % 2][j, r * per:(r + 1) * per, :] = val
            prev = dil

    s_ref[...] = proj[:, v0 + B_WIDTH:].astype(s_ref.dtype)


def _inproj(x2d, mod3, w_in_bf, cos_t, s1_t, s2_t, ln_g, ln_b, ws, bs_full, bsz, t):
    n, d = x2d.shape
    tm = min(ROW_TILE, t)
    nt = t // tm
    pw = w_in_bf.shape[1]
    c_width = pw - 2 * A_WIDTH - 3 * B_WIDTH
    tok = lambda b, i: (b * nt + i, 0)
    const2 = lambda b, i: (0, 0)
    outs = [jax.ShapeDtypeStruct((n, A_WIDTH), BF16),
            jax.ShapeDtypeStruct((t, bsz * c_width), BF16)]
    out_specs = [pl.BlockSpec((tm, A_WIDTH), tok),
                 pl.BlockSpec((tm, c_width), lambda b, i: (i, b))]
    for _ in range(3):
        outs.append(jax.ShapeDtypeStruct((n, B_WIDTH), BF16))
        out_specs.append(pl.BlockSpec((tm, B_WIDTH), tok))
    copy_dils = [dil for dil in DILATIONS if _dilation_layout(dil) == "copy"]
    for _ in range(3):
        for dil in copy_dils:
            outs.append(jax.ShapeDtypeStruct((bsz, dil, t // dil, B_WIDTH), BF16))
            out_specs.append(pl.BlockSpec((1, dil, tm // dil, B_WIDTH), lambda b, i: (b, 0, i, 0)))
    res = pl.pallas_call(
        _inproj_kernel,
        grid=(bsz, nt),
        in_specs=[pl.BlockSpec((tm, d), tok),
                  pl.BlockSpec((1, 1, d), lambda b, i: (b, 0, 0)),
                  pl.BlockSpec((1, 1, d), lambda b, i: (b, 0, 1)),
                  pl.BlockSpec((d, pw), const2),
                  pl.BlockSpec((tm, LANES), tok),
                  pl.BlockSpec((tm, LANES), tok),
                  pl.BlockSpec((tm, LANES), tok),
                  pl.BlockSpec((1, A_WIDTH), const2),
                  pl.BlockSpec((1, A_WIDTH), const2),
                  pl.BlockSpec((A_HEADS, CHUNK, CHUNK), lambda b, i: (0, 0, 0)),
                  pl.BlockSpec((CHUNK, A_WIDTH), const2)],
        out_specs=out_specs,
        out_shape=outs,
        scratch_shapes=[pltpu.VMEM((B_WIDTH // LANES, tm, LANES), F32)] * 6,
        compiler_params=_cparams(("arbitrary", "arbitrary")),
        name="inproj_gmlp_rope",
    )(x2d, mod3, mod3, w_in_bf, cos_t, s1_t, s2_t, ln_g, ln_b, ws, bs_full)
    a_out, s_in = res[0], res[1]
    natural = res[2:5]
    n_copy = len(copy_dils)
    copies = {dil: tuple(res[5 + z * n_copy + g] for z in range(3)) for g, dil in enumerate(copy_dils)}
    qkv = [copies[dil] if dil in copies else natural for dil in DILATIONS]
    return a_out, s_in, qkv


def _att_kernel(q_ref, k_ref, v_ref, o_ref, st_ref, kcat_ref, vaug_ref):
    i = pl.program_id(2)
    blk = ATT_BLK
    n_res = q_ref.shape[1]
    n_sub = q_ref.shape[2] // blk
    n_pairs = B_WIDTH // LANES
    last = slice(n_sub * blk, (n_sub + 1) * blk)

    @pl.when(i == 0)
    def _():
        kcat_ref[:, 0:blk, :] = jnp.zeros((n_res, blk, B_WIDTH), BF16)
        vaug_ref[:, :, 0:blk, 0:LANES] = jnp.zeros((n_res, n_pairs, blk, LANES), BF16)
        vaug_ref[:, :, :, LANES:] = jnp.ones((n_res, n_pairs, (n_sub + 1) * blk, LANES), BF16)

    @pl.when(i > 0)
    def _():
        kcat_ref[:, 0:blk, :] = kcat_ref[:, last, :]
        vaug_ref[:, :, 0:blk, 0:LANES] = vaug_ref[:, :, last, 0:LANES]

    for rr in range(n_res):
        kcat_ref[rr, blk:, :] = k_ref[0, rr]
        for pair in range(n_pairs):
            vaug_ref[rr, pair, blk:, 0:LANES] = v_ref[0, rr, :, pair * LANES:(pair + 1) * LANES]

    qi = lax.broadcasted_iota(jnp.int32, (blk, 2 * blk), 0)
    kk = lax.broadcasted_iota(jnp.int32, (blk, 2 * blk), 1)
    band = (kk >= qi) & (kk <= qi + blk)
    first = band & ((i > 0) | (kk >= blk))
    lane = lax.broadcasted_iota(jnp.int32, (blk, LANES), 1)
    low_half = lane < B_HEAD_DIM
    nt = (((1,), (1,)), ((), ()))
    for rr, sub in [(rr, sub) for rr in range(n_res) for sub in range(n_sub)]:
        rows = slice(sub * blk, (sub + 1) * blk)
        keys = slice(sub * blk, (sub + 2) * blk)
        mask = first if sub == 0 else band
        stats = jnp.zeros((blk, LANES), F32)
        for pair in range(n_pairs):
            lanes = slice(pair * LANES, (pair + 1) * LANES)
            q2 = q_ref[0, rr, rows, lanes]
            kslab = kcat_ref[rr, keys, lanes]
            res = []
            for hh in range(2):
                hd = 2 * pair + hh
                qm = jnp.where(low_half if hh == 0 else jnp.logical_not(low_half), q2, jnp.zeros_like(q2))
                s = lax.dot_general(qm, kslab, nt, preferred_element_type=F32)
                s = jnp.where(mask, s, NEG_INF)
                m = jnp.max(jnp.maximum(s[:, :blk], s[:, blk:]), axis=1, keepdims=True)
                p = jnp.exp(s - m).astype(BF16)
                ov = jnp.dot(p, vaug_ref[rr, pair, keys, :], preferred_element_type=F32)
                den = ov[:, LANES:]
                res.append(ov[:, :LANES] / den)
                stats = jnp.where(lane == hd, m, stats)
                stats = jnp.where(lane == B_HEADS + hd, den, stats)
            o_ref[0, rr, rows, lanes] = jnp.where(low_half, res[0], res[1]).astype(o_ref.dtype)
        st_ref[0, rr, rows, :] = stats


def _dilation_layout(dil):
    return "natural" if dil == 1 else "copy"


def _att_branch(q, k, v, dil, bsz, t):
    rows = t // dil
    n_sub = min(ATT_SUB_BLOCKS, rows // ATT_BLK)
    step_rows = n_sub * ATT_BLK
    n_steps = rows // step_rows
    n_res = math.gcd(dil, max(1, ATT_SUB_BLOCKS // n_sub)) if n_steps == 1 else 1
    shape = lambda w: (bsz, dil, rows, w)
    blk = lambda w: pl.BlockSpec((1, n_res, step_rows, w), lambda b, r, i: (b, r, i, 0))
    q, k, v = (z.reshape(shape(B_WIDTH)) for z in (q, k, v))
    return pl.pallas_call(
        _att_kernel,
        grid=(bsz, dil // n_res, n_steps),
        in_specs=[blk(B_WIDTH), blk(B_WIDTH), blk(B_WIDTH)],
        out_specs=[blk(B_WIDTH), blk(LANES)],
        out_shape=[jax.ShapeDtypeStruct(shape(B_WIDTH), BF16),
                   jax.ShapeDtypeStruct(shape(LANES), F32)],
        scratch_shapes=[pltpu.VMEM((n_res, step_rows + ATT_BLK, B_WIDTH), BF16),
                        pltpu.VMEM((n_res, B_WIDTH // LANES, step_rows + ATT_BLK, 2 * LANES), BF16)],
        compiler_params=_cparams(("arbitrary", "arbitrary", "arbitrary")),
        name=f"dilated_attn_d{dil}",
    )(q, k, v)


def _ssm_kernel(u_ref, bmat_ref, cmat_ref, are_ref, aim_ref, dskip_ref, gw_ref, gb_ref,
                o_ref, state_ref, xbuf_ref):
    bsz, two_n = state_ref.shape
    n_state = two_n // 2
    steps = u_ref.shape[0] // bsz

    @pl.when(pl.program_id(0) == 0)
    def _():
        state_ref[...] = jnp.zeros_like(state_ref)

    u = u_ref[...]
    xbuf_ref[...] = jnp.dot(u, bmat_ref[...], preferred_element_type=F32)
    a_re = jnp.broadcast_to(are_ref[...], (bsz, n_state))
    a_im = jnp.broadcast_to(aim_ref[...], (bsz, n_state))

    def step(tt, carry):
        x_re, x_im = carry
        r0 = pl.multiple_of(tt * bsz, bsz)
        b_re = xbuf_ref[pl.ds(r0, bsz), :n_state]
        b_im = xbuf_ref[pl.ds(r0, bsz), n_state:]
        n_re = a_re * x_re - a_im * x_im + b_re
        n_im = a_re * x_im + a_im * x_re + b_im
        xbuf_ref[pl.ds(r0, bsz), :n_state] = n_re
        xbuf_ref[pl.ds(r0, bsz), n_state:] = n_im
        return n_re, n_im

    x_re, x_im = lax.fori_loop(0, steps, step, (state_ref[:, :n_state], state_ref[:, n_state:]))
    state_ref[:, :n_state] = x_re
    state_ref[:, n_state:] = x_im

    y = jnp.dot(xbuf_ref[...].astype(BF16), cmat_ref[...], preferred_element_type=F32)
    y = _gelu_tanh(y + dskip_ref[...] * u.astype(F32))
    gate = jax.nn.sigmoid(jnp.dot(y.astype(BF16), gw_ref[...], preferred_element_type=F32) + gb_ref[...])
    o_ref[...] = (y * gate).astype(o_ref.dtype)


def _ssm_weights(lam_re, lam_im, log_dt, b_re, b_im, c_re, c_im):
    g, n_st = lam_re.shape
    cg = b_re.shape[-1]
    dt = jnp.exp(log_dt)[:, None]
    mag = jnp.exp(lam_re * dt)
    ab_re = mag * jnp.cos(lam_im * dt)
    ab_im = mag * jnp.sin(lam_im * dt)
    nr = ab_re - 1.0
    ni = ab_im
    mod2 = lam_re * lam_re + lam_im * lam_im
    f_re = (nr * lam_re + ni * lam_im) / mod2
    f_im = (ni * lam_re - nr * lam_im) / mod2
    bb_re = f_re[..., None] * b_re - f_im[..., None] * b_im
    bb_im = f_re[..., None] * b_im + f_im[..., None] * b_re
    eye = jnp.eye(g, dtype=F32)
    bm_re = jnp.einsum('gnc,gh->gchn', bb_re, eye).reshape(g * cg, g * n_st)
    bm_im = jnp.einsum('gnc,gh->gchn', bb_im, eye).reshape(g * cg, g * n_st)
    bmat = jnp.concatenate([bm_re, bm_im], axis=1)
    cm_re = jnp.einsum('gcn,gh->gnhc', c_re, eye).reshape(g * n_st, g * cg)
    cm_im = jnp.einsum('gcn,gh->gnhc', c_im, eye).reshape(g * n_st, g * cg)
    cmat = jnp.concatenate([cm_re, -cm_im], axis=0)
    return (bmat.astype(BF16), cmat.astype(BF16),
            ab_re.reshape(1, g * n_st), ab_im.reshape(1, g * n_st))


def _ssm(u_tm, bmat, cmat, a_re, a_im, d_skip, glu_w_bf, glu_b, bsz, t):
    cw = bmat.shape[0]
    two_n = bmat.shape[1]
    steps = min(SSM_STEPS, t)
    rows = steps * bsz
    u2 = u_tm.reshape(t * bsz, cw)
    const = lambda i: (0, 0)
    return pl.pallas_call(
        _ssm_kernel,
        grid=(t // steps,),
        in_specs=[pl.BlockSpec((rows, cw), lambda i: (i, 0)),
                  pl.BlockSpec((cw, two_n), const),
                  pl.BlockSpec((two_n, cw), const),
                  pl.BlockSpec((1, two_n // 2), const),
                  pl.BlockSpec((1, two_n // 2), const),
                  pl.BlockSpec((1, cw), const),
                  pl.BlockSpec((cw, cw), const),
                  pl.BlockSpec((1, cw), const)],
        out_specs=pl.BlockSpec((rows, cw), lambda i: (i, 0)),
        out_shape=jax.ShapeDtypeStruct((t * bsz, cw), BF16),
        scratch_shapes=[pltpu.VMEM((bsz, two_n), F32), pltpu.VMEM((rows, two_n), F32)],
        compiler_params=_cparams(("arbitrary",)),
        name="s5_scan_glu",
    )(u2, bmat, cmat, a_re, a_im, d_skip, glu_w_bf, glu_b).reshape(t, bsz * cw)


def _route(sel, scores):
    gs = []
    for g in range(N_EXPERT_GROUPS):
        a, b, c, d = sel[4 * g: 4 * g + 4]
        hi1, lo1 = jnp.maximum(a, b), jnp.minimum(a, b)
        hi2, lo2 = jnp.maximum(c, d), jnp.minimum(c, d)
        gs.append(jnp.maximum(hi1, hi2) + jnp.maximum(jnp.minimum(hi1, hi2), jnp.maximum(lo1, lo2)))
    g_idx = jnp.zeros(gs[0].shape, jnp.int32)
    best = gs[0]
    for g in range(1, N_EXPERT_GROUPS):
        better = gs[g] > best
        g_idx = jnp.where(better, g, g_idx)
        best = jnp.where(better, gs[g], best)

    def pick(rows, j):
        out = rows[j]
        for g in range(1, N_EXPERT_GROUPS):
            out = jnp.where(g_idx == g, rows[4 * g + j], out)
        return out

    v = [pick(sel, j) for j in range(EXPERTS_PER_GROUP)]
    s = [pick(scores, j) for j in range(EXPERTS_PER_GROUP)]
    i1 = jnp.zeros(g_idx.shape, jnp.int32)
    b1, g1 = v[0], s[0]
    for j in range(1, EXPERTS_PER_GROUP):
        better = v[j] > b1
        i1 = jnp.where(better, j, i1)
        b1 = jnp.where(better, v[j], b1)
        g1 = jnp.where(better, s[j], g1)
    i2 = jnp.full(g_idx.shape, -1, jnp.int32)
    b2 = jnp.zeros_like(b1)
    g2 = jnp.zeros_like(g1)
    for j in range(EXPERTS_PER_GROUP):
        better = (i1 != j) & ((i2 < 0) | (v[j] > b2))
        i2 = jnp.where(better, j, i2)
        b2 = jnp.where(better, v[j], b2)
        g2 = jnp.where(better, s[j], g2)
    tot = g1 + g2
    base = g_idx * EXPERTS_PER_GROUP
    return base + i1, base + i2, g1 / tot, g2 / tot


def _natural_order(ref, scr_ref, dil):
    if _dilation_layout(dil) != "copy":
        return ref[...].astype(F32)
    per = ref.shape[2]
    n_slab = scr_ref.shape[0]
    for r in range(dil):
        for j in range(n_slab):
            scr_ref[j, pl.ds(r, per, stride=dil), :] = ref[0, r, :, j * LANES:(j + 1) * LANES].astype(F32)
    if n_slab == 1:
        return scr_ref[0]
    return jnp.concatenate([scr_ref[j] for j in range(n_slab)], axis=1)


def _outproj_kernel(a_ref, o1_ref, o2_ref, o3_ref, st1_ref, st2_ref, st3_ref,
                    c_ref, x_ref, g1_ref, sh2_ref, sc2_ref, wout_ref, expand_ref, lng_ref, lnb_ref,
                    rwh_ref, rwl_ref, rb_ref, x1_ref, h2_ref, eidx_ref, gate_ref, count_ref,
                    oscr2_ref, oscr3_ref, sscr2_ref, sscr3_ref, *, alpha):
    o_refs = (o1_ref, o2_ref, o3_ref)
    st_refs = (st1_ref, st2_ref, st3_ref)
    o_scr = (None, oscr2_ref, oscr3_ref)
    st_scr = (None, sscr2_ref, sscr3_ref)
    ms = [_natural_order(st_refs[g], st_scr[g], DILATIONS[g]) for g in range(len(DILATIONS))]
    dens = [pltpu.roll(m, LANES - B_HEADS, 1) for m in ms]
    mx = jnp.maximum(jnp.maximum(ms[0], ms[1]), ms[2])
    ws = [dens[g] * jnp.exp(ms[g] - mx) for g in range(len(DILATIONS))]
    lane = lax.broadcasted_iota(jnp.int32, mx.shape, 1)
    tot = jnp.where(lane < B_HEADS, ws[0] + ws[1] + ws[2], 1.0)
    expand = expand_ref[...]

    def widen(w):
        wn = jnp.where(lane < B_HEADS, w / tot, 0.0)
        hi = wn.astype(BF16)
        lo = (wn - hi.astype(F32)).astype(BF16)
        return (jnp.dot(hi, expand, preferred_element_type=F32)
                + jnp.dot(lo, expand, preferred_element_type=F32))

    b_out = widen(ws[0]) * _natural_order(o_refs[0], o_scr[0], DILATIONS[0])
    for g in range(1, len(DILATIONS)):
        b_out = b_out + widen(ws[g]) * _natural_order(o_refs[g], o_scr[g], DILATIONS[g])
    b_out = b_out.astype(BF16)

    wout = wout_ref
    mix = (jnp.dot(a_ref[...], wout[:A_WIDTH, :], preferred_element_type=F32)
           + jnp.dot(b_out, wout[A_WIDTH:A_WIDTH + B_WIDTH, :], preferred_element_type=F32)
           + jnp.dot(c_ref[...], wout[A_WIDTH + B_WIDTH:, :], preferred_element_type=F32))
    x1 = _ln(alpha * x_ref[...] + (1.0 + g1_ref[0]) * mix) * lng_ref[...] + lnb_ref[...]
    x1_ref[...] = x1
    h2 = _ln(x1) * (1.0 + sc2_ref[0]) + sh2_ref[0]
    _store_tile_rows(h2_ref, h2)

    hi = h2.astype(BF16)
    lo = (h2 - hi.astype(F32)).astype(BF16)
    nt = (((1,), (1,)), ((), ()))
    logits = (lax.dot_general(rwh_ref[...], hi, nt, preferred_element_type=F32)
              + lax.dot_general(rwh_ref[...], lo, nt, preferred_element_type=F32)
              + lax.dot_general(rwl_ref[...], hi, nt, preferred_element_type=F32))
    scores = jax.nn.sigmoid(logits)
    sel = scores + rb_ref[...]
    sel_rows = [sel[e:e + 1, :] for e in range(N_EXPERTS)]
    score_rows = [scores[e:e + 1, :] for e in range(N_EXPERTS)]
    e1, e2, gt1, gt2 = _route(sel_rows, score_rows)

    @pl.when((pl.program_id(0) == 0) & (pl.program_id(1) == 0))
    def _():
        count_ref[...] = jnp.zeros_like(count_ref)

    tm = h2.shape[0]
    erow = lax.broadcasted_iota(jnp.int32, (N_EXPERTS, tm), 0)
    oh1 = erow == e1
    oh2 = erow == e2
    both = (oh1 | oh2).astype(BF16)
    earlier = (lax.broadcasted_iota(jnp.int32, (tm, tm), 0)
               < lax.broadcasted_iota(jnp.int32, (tm, tm), 1)).astype(BF16)
    base = count_ref[:, 0:1]
    cnt = jnp.dot(both, earlier, preferred_element_type=F32) + base
    r1 = jnp.sum(jnp.where(oh1, cnt, 0.0), axis=0, keepdims=True)
    r2 = jnp.sum(jnp.where(oh2, cnt, 0.0), axis=0, keepdims=True)
    total = base + jnp.sum(both.astype(F32), axis=1, keepdims=True)
    count_ref[...] = jnp.broadcast_to(total, count_ref.shape)

    eidx_ref[...] = jnp.zeros_like(eidx_ref)
    gate_ref[...] = jnp.zeros_like(gate_ref)
    eidx_ref[0:1, :] = e1
    eidx_ref[1:2, :] = e2
    eidx_ref[2:3, :] = r1.astype(jnp.int32)
    eidx_ref[3:4, :] = r2.astype(jnp.int32)
    gate_ref[0:1, :] = gt1
    gate_ref[1:2, :] = gt2


def _outproj(a_out, att, c_tm, x2d, mod3, w_out_bf, ln_g, ln_b, rw_hi, rw_lo, rbias, bsz, t, alpha):
    n, d = x2d.shape
    tm = min(ROW_TILE, t)
    nt = t // tm
    cw = c_tm.shape[1] // bsz
    tok = lambda b, i: (b * nt + i, 0)
    const2 = lambda b, i: (0, 0)
    head = jnp.arange(B_WIDTH) // B_HEAD_DIM
    expand = (jnp.arange(LANES)[:, None] == head[None, :]).astype(BF16)

    def branch_input(z, dil, w):
        if _dilation_layout(dil) == "copy":
            return z, pl.BlockSpec((1, dil, tm // dil, w), lambda b, i: (b, 0, i, 0))
        return z.reshape(n, w), pl.BlockSpec((tm, w), tok)

    o_in = [branch_input(o, dil, B_WIDTH) for (o, _), dil in zip(att, DILATIONS)]
    st_in = [branch_input(st, dil, LANES) for (_, st), dil in zip(att, DILATIONS)]
    (o1, st1), (o2, st2), (o3, st3) = [(o_in[g][0], st_in[g][0]) for g in range(len(DILATIONS))]
    modspec = lambda j: pl.BlockSpec((1, 1, d), lambda b, i: (b, 0, j))
    return pl.pallas_call(
        functools.partial(_outproj_kernel, alpha=alpha),
        grid=(bsz, nt),
        in_specs=[pl.BlockSpec((tm, A_WIDTH), tok)]
                 + [spec for _, spec in o_in]
                 + [spec for _, spec in st_in]
                 + [pl.BlockSpec((tm, cw), lambda b, i: (i, b)),
                  pl.BlockSpec((tm, d), tok),
                  modspec(2), modspec(3), modspec(4),
                  pl.BlockSpec((d, d), const2),
                  pl.BlockSpec((LANES, B_WIDTH), const2),
                  pl.BlockSpec((1, d), const2), pl.BlockSpec((1, d), const2),
                  pl.BlockSpec((N_EXPERTS, d), const2), pl.BlockSpec((N_EXPERTS, d), const2),
                  pl.BlockSpec((N_EXPERTS, 1), const2)],
        out_specs=[pl.BlockSpec((tm, d), tok), pl.BlockSpec((tm * (d // LANES), LANES), tok),
                   pl.BlockSpec((SUBLANES, tm), lambda b, i: (0, b * nt + i)),
                   pl.BlockSpec((SUBLANES, tm), lambda b, i: (0, b * nt + i)),
                   pl.BlockSpec((N_EXPERTS, LANES), const2)],
        out_shape=[jax.ShapeDtypeStruct((n, d), F32),
                   jax.ShapeDtypeStruct((n * (d // LANES), LANES), F32),

                   jax.ShapeDtypeStruct((SUBLANES, n), jnp.int32),
                   jax.ShapeDtypeStruct((SUBLANES, n), F32),
                   jax.ShapeDtypeStruct((N_EXPERTS, LANES), F32)],
        scratch_shapes=[pltpu.VMEM((B_WIDTH // LANES, tm, LANES), F32),
                        pltpu.VMEM((B_WIDTH // LANES, tm, LANES), F32),
                        pltpu.VMEM((1, tm, LANES), F32), pltpu.VMEM((1, tm, LANES), F32)],
        compiler_params=_cparams(("arbitrary", "arbitrary")),
        name="outproj_ln_router",
    )(a_out, o1, o2, o3, st1, st2, st3, c_tm, x2d, mod3, mod3, mod3, w_out_bf, expand,
      ln_g, ln_b, rw_hi, rw_lo, rbias)


def _dispatch_plan(eidx, counts, n_tok, blk):
    e = eidx[:TOP_K]
    rank = eidx[TOP_K:2 * TOP_K]
    pcounts = (counts + blk - 1) // blk * blk
    pends = jnp.cumsum(pcounts)
    pstarts = pends - pcounts
    seg_start = jnp.zeros_like(e)
    for j in range(N_EXPERTS):
        seg_start = jnp.where(e == j, pstarts[j], seg_start)
    dest = (seg_start + rank).astype(jnp.int32)
    n_blk = n_tok * TOP_K // blk + N_EXPERTS
    blk_start = jnp.arange(n_blk, dtype=jnp.int32) * blk
    blk_e = jnp.minimum(jnp.sum((pends[None, :] <= blk_start[:, None]).astype(jnp.int32), axis=1),
                        N_EXPERTS - 1).astype(jnp.int32)
    n_used = (pends[-1] // blk).astype(jnp.int32).reshape(1)
    return dest, blk_e, n_used, pends.astype(jnp.int32), pcounts.astype(jnp.int32)


def _dispatch_kernel(pend_ref, pcnt_ref, dest_ref, h2_ref, xs_hbm, zbuf, sem, zsem, *, blk, n_slab):
    i = pl.program_id(0)
    rows = h2_ref.shape[0] // n_slab
    blk_rows = blk * n_slab

    @pl.when(i == 0)
    def _():
        zbuf[...] = jnp.zeros_like(zbuf)
        for e in range(N_EXPERTS):
            @pl.when(pcnt_ref[e] > 0)
            def _():
                start = pl.multiple_of((pend_ref[e] - blk) * n_slab, blk_rows)
                cp = pltpu.make_async_copy(zbuf, xs_hbm.at[pl.ds(start, blk_rows), :], zsem)
                cp.start()
                cp.wait()
        n_blk = xs_hbm.shape[0] // blk_rows
        for t in range(N_EXPERTS):
            tail = pend_ref[N_EXPERTS - 1] // blk + t

            @pl.when(tail < n_blk)
            def _():
                start = pl.multiple_of(tail * blk_rows, blk_rows)
                cp = pltpu.make_async_copy(zbuf, xs_hbm.at[pl.ds(start, blk_rows), :], zsem)
                cp.start()
                cp.wait()

    def issue(s, c):
        src = h2_ref.at[pl.ds(pl.multiple_of(s * n_slab, n_slab), n_slab), :]
        for kk in range(TOP_K):
            slot = pl.multiple_of(dest_ref[0, 0, kk * rows + s] * n_slab, n_slab)
            pltpu.make_async_copy(src, xs_hbm.at[pl.ds(slot, n_slab), :], sem).start()
        return c
    lax.fori_loop(0, rows, issue, 0, unroll=8)
    for kk in range(TOP_K):
        pltpu.make_async_copy(h2_ref, xs_hbm.at[pl.ds(0, rows * n_slab), :], sem).wait()


def _expert_kernel(blk_e_ref, nused_ref, xs_ref, wg_ref, wu_ref, wd_ref, ys_ref, wg_bf, wu_bf, wd_bf,
                   *, blk, n_slab):
    j = pl.program_id(0)

    @pl.when((j == 0) | (blk_e_ref[j] != blk_e_ref[jnp.maximum(j - 1, 0)]))
    def _():
        wg_bf[...] = wg_ref[0, 0].astype(BF16)
        wu_bf[...] = wu_ref[0, 0].astype(BF16)
        wd_bf[...] = wd_ref[0, 0].astype(BF16)

    @pl.when(j < nused_ref[0])
    def _():
        rows = _load_tile_rows(xs_ref, 0, blk, n_slab).astype(BF16)
        gt = jnp.dot(rows, wg_bf[...], preferred_element_type=F32)
        up = jnp.dot(rows, wu_bf[...], preferred_element_type=F32)
        hid = (gt * jax.nn.sigmoid(gt) * up).astype(BF16)
        _store_tile_rows(ys_ref, jnp.dot(hid, wd_bf[...], preferred_element_type=F32))

    @pl.when(j >= nused_ref[0])
    def _():
        ys_ref[...] = jnp.zeros_like(ys_ref)


def _moe(h2, eidx, counts, w_gate, w_up, w_down, layer, rows):
    d, de = w_gate.shape[-2:]
    n_slab = d // LANES
    n_tok = h2.shape[0] // n_slab
    blk = min(MOE_BLK, n_tok)
    dest, blk_e, n_used, pends, pcounts = _dispatch_plan(eidx, counts, n_tok, blk)
    n_blk = blk_e.shape[0]
    cap = n_blk * blk

    nt = n_tok // rows
    dest_tiles = dest.reshape(TOP_K, nt, rows).transpose(1, 0, 2).reshape(nt, 1, TOP_K * rows)
    xs = pl.pallas_call(
        functools.partial(_dispatch_kernel, blk=blk, n_slab=n_slab),
        grid_spec=pltpu.PrefetchScalarGridSpec(
            num_scalar_prefetch=2,
            grid=(nt,),
            in_specs=[pl.BlockSpec((1, 1, TOP_K * rows), lambda i, pe, pc: (i, 0, 0),
                                   memory_space=pltpu.SMEM),
                      pl.BlockSpec((rows * n_slab, LANES), lambda i, pe, pc: (i, 0))],
            out_specs=pl.BlockSpec(memory_space=pl.ANY),
            scratch_shapes=[pltpu.VMEM((blk * n_slab, LANES), F32),
                            pltpu.SemaphoreType.DMA, pltpu.SemaphoreType.DMA],
        ),
        out_shape=jax.ShapeDtypeStruct((cap * n_slab, LANES), F32),
        compiler_params=_cparams(("arbitrary",)),
        name="moe_dispatch",
    )(pends, pcounts, dest_tiles, h2)

    used = lambda j, be, nu: (jnp.minimum(j, nu[0] - 1), 0)
    ys = pl.pallas_call(
        functools.partial(_expert_kernel, blk=blk, n_slab=n_slab),
        grid_spec=pltpu.PrefetchScalarGridSpec(
            num_scalar_prefetch=2,
            grid=(n_blk,),
            in_specs=[pl.BlockSpec((blk * n_slab, LANES), used),
                      pl.BlockSpec((1, 1, d, de), lambda j, be, nu: (layer, be[j], 0, 0)),
                      pl.BlockSpec((1, 1, d, de), lambda j, be, nu: (layer, be[j], 0, 0)),
                      pl.BlockSpec((1, 1, de, d), lambda j, be, nu: (layer, be[j], 0, 0))],
            out_specs=pl.BlockSpec((blk * n_slab, LANES), lambda j, be, nu: (j, 0)),
            scratch_shapes=[pltpu.VMEM((d, de), BF16), pltpu.VMEM((d, de), BF16), pltpu.VMEM((de, d), BF16)],
        ),
        out_shape=jax.ShapeDtypeStruct((cap * n_slab, LANES), F32),
        compiler_params=_cparams(("arbitrary",)),
        name="moe_experts",
    )(blk_e, n_used, xs, w_gate, w_up, w_down)
    return ys, dest_tiles


def _ffn_ln_kernel(dest_ref, next_ref, x_ref, gate_ref, g2_ref, lng_ref, lnb_ref, ys_hbm, o_ref, ybuf, sem,
                   *, alpha):
    tm, d = x_ref.shape
    n_slab = d // LANES
    g = pl.program_id(0) * pl.num_programs(1) + pl.program_id(1)
    n_tiles = pl.num_programs(0) * pl.num_programs(1)
    slot = g % 2

    def gather(idx_ref, buf_slot):
        def issue(s, c):
            src = pl.multiple_of(idx_ref[0, 0, s] * n_slab, n_slab)
            dst = pl.multiple_of(s * n_slab, n_slab)
            pltpu.make_async_copy(ys_hbm.at[pl.ds(src, n_slab), :],
                                  ybuf.at[buf_slot, pl.ds(dst, n_slab), :], sem.at[buf_slot]).start()
            return c
        lax.fori_loop(0, TOP_K * tm, issue, 0, unroll=8)

    @pl.when(g == 0)
    def _():
        gather(dest_ref, 0)

    @pl.when(g + 1 < n_tiles)
    def _():
        gather(next_ref, 1 - slot)

    pltpu.make_async_copy(ys_hbm.at[pl.ds(0, TOP_K * tm * n_slab), :], ybuf.at[slot], sem.at[slot]).wait()

    gate = gate_ref[...]
    buf = ybuf.at[slot]
    ffn = _load_tile_rows(buf, 0, tm, n_slab) * gate[:, 0:1]
    for kk in range(1, TOP_K):
        ffn = ffn + _load_tile_rows(buf, kk * tm, tm, n_slab) * gate[:, kk:kk + 1]
    o_ref[...] = _ln(alpha * x_ref[...] + (1.0 + g2_ref[0]) * ffn) * lng_ref[...] + lnb_ref[...]


def _ffn_ln(x1, ys, dest_tiles, gates, mod3, ln_g, ln_b, bsz, t, alpha):
    n, d = x1.shape
    n_slab = d // LANES
    tm = dest_tiles.shape[2] // TOP_K
    nt = t // tm
    n_tiles = bsz * nt
    tok = lambda b, i, *_: (b * nt + i, 0)
    const2 = lambda b, i, *_: (0, 0)
    gate_cols = gates[:TOP_K].T
    idx_blk = (1, 1, TOP_K * tm)
    return pl.pallas_call(
        functools.partial(_ffn_ln_kernel, alpha=alpha),
        grid=(bsz, nt),
        in_specs=[pl.BlockSpec(idx_blk, lambda b, i: (b * nt + i, 0, 0), memory_space=pltpu.SMEM),
                  pl.BlockSpec(idx_blk, lambda b, i: (jnp.minimum(b * nt + i + 1, n_tiles - 1), 0, 0),
                               memory_space=pltpu.SMEM),
                  pl.BlockSpec((tm, d), tok), pl.BlockSpec((tm, TOP_K), tok),
                  pl.BlockSpec((1, 1, d), lambda b, i: (b, 0, 5)),
                  pl.BlockSpec((1, d), const2), pl.BlockSpec((1, d), const2),
                  pl.BlockSpec(memory_space=pl.ANY)],
        out_specs=pl.BlockSpec((tm, d), tok),
        out_shape=jax.ShapeDtypeStruct((n, d), F32),
        scratch_shapes=[pltpu.VMEM((2, TOP_K * tm * n_slab, LANES), F32), pltpu.SemaphoreType.DMA((2,))],
        compiler_params=_cparams(("arbitrary", "arbitrary")),
        name="ffn_combine_ln",
    )(dest_tiles, dest_tiles, x1, gate_cols, mod3, ln_g, ln_b, ys)


def kernel(x, c, positions, ada_w, ada_b, w_in, gm_ln_g, gm_ln_b, gm_ws, gm_bs, ssm_lam_re, ssm_lam_im, ssm_log_dt, ssm_b_re, ssm_b_im, ssm_c_re, ssm_c_im, ssm_d, glu_w, glu_b, w_out, ln1_g, ln1_b, router_w, router_bias, exp_w_gate, exp_w_up, exp_w_down, ln2_g, ln2_b):
    bsz, t, d = x.shape
    depth = ada_w.shape[0]
    alpha = (2.0 * depth) ** 0.25
    n = bsz * t

    mod = _adaln_mod(c.astype(F32), ada_w, ada_b)
    cos_t, s1_t, s2_t = _rope_tables(positions)
    rw_t = router_w.astype(F32).T
    rw_hi = rw_t.astype(BF16)
    rw_lo = (rw_t - rw_hi.astype(F32)).astype(BF16)
    rbias = router_bias.astype(F32).reshape(N_EXPERTS, 1)

    xf = x.astype(F32).reshape(n, d)
    for l in range(depth):
        mod3 = mod[l].reshape(bsz, 1, 6 * d)
        bs_full = jnp.repeat(gm_bs[l].T, A_HEAD_DIM, axis=1)
        a_out, s_in, qkv = _inproj(
            xf, mod3, w_in[l].astype(BF16), cos_t, s1_t, s2_t,
            gm_ln_g[l].reshape(1, A_WIDTH), gm_ln_b[l].reshape(1, A_WIDTH), gm_ws[l], bs_full, bsz, t)
        att = [_att_branch(*qkv[g], dil, bsz, t) for g, dil in enumerate(DILATIONS)]
        bmat, cmat, a_re, a_im = _ssm_weights(ssm_lam_re[l], ssm_lam_im[l], ssm_log_dt[l],
                                              ssm_b_re[l], ssm_b_im[l], ssm_c_re[l], ssm_c_im[l])
        cw = bmat.shape[0]
        c_out = _ssm(s_in, bmat, cmat, a_re, a_im, ssm_d[l].reshape(1, cw),
                     glu_w[l].astype(BF16), glu_b[l].reshape(1, cw), bsz, t)
        x1, h2, eidx, gates, counts = _outproj(a_out, att, c_out, xf, mod3, w_out[l].astype(BF16),
                                               ln1_g[l].reshape(1, d), ln1_b[l].reshape(1, d),
                                               rw_hi, rw_lo, rbias, bsz, t, alpha)
        ys, dest_tiles = _moe(h2, eidx, counts[:, 0].astype(jnp.int32),
                              exp_w_gate.astype(F32), exp_w_up.astype(F32), exp_w_down.astype(F32), l,
                              min(MOE_ROW_TILE, t))
        xf = _ffn_ln(x1, ys, dest_tiles, gates, mod3, ln2_g[l].reshape(1, d), ln2_b[l].reshape(1, d),
                     bsz, t, alpha)
    return xf.reshape(bsz, t, d)
```

```python
import functools
import math

import jax
import jax.numpy as jnp
from jax import lax
from jax.experimental import pallas as pl
from jax.experimental.pallas import tpu as pltpu

F32 = jnp.float32
BF16 = jnp.bfloat16

A_HEADS = 4
A_HEAD_DIM = 64
A_WIDTH = A_HEADS * A_HEAD_DIM
CHUNK = 128
B_HEADS = 8
B_HEAD_DIM = 64
B_WIDTH = B_HEADS * B_HEAD_DIM
DILATIONS = (1, 4, 16)
ATT_BLK = 128
ROT_DIM = B_HEAD_DIM // 4
ROPE_THETA = 500000.0
SSM_GROUP = 16
SSM_STATE = 64
N_EXPERTS = 16
N_EXPERT_GROUPS = 4
EXPERTS_PER_GROUP = 4
TOP_K = 2
LN_EPS = 1e-5
NEG_INF = -1e30

LANES = 128
SUBLANES = 8
VMEM_LIMIT = 56 * 1024 * 1024
ROW_TILE = 512
SSM_STEPS = 128
MOE_BLK = 512
MOE_ROW_TILE = 1024
ATT_SUB_BLOCKS = 8


def _cparams(sem):
    return pltpu.CompilerParams(dimension_semantics=sem, vmem_limit_bytes=VMEM_LIMIT)


def _ln(x):
    mu = jnp.mean(x, axis=-1, keepdims=True)
    xc = x - mu
    var = jnp.mean(xc * xc, axis=-1, keepdims=True)
    return xc * lax.rsqrt(var + LN_EPS)


def _store_tile_rows(ref, val):
    n_slab = val.shape[1] // LANES
    for j in range(n_slab):
        ref[pl.ds(j, val.shape[0], stride=n_slab), :] = val[:, j * LANES:(j + 1) * LANES]


def _load_tile_rows(ref, start, rows, n_slab):
    return jnp.concatenate([ref[pl.ds(start * n_slab + j, rows, stride=n_slab), :] for j in range(n_slab)],
                           axis=1)


def _gelu_tanh(x):
    return 0.5 * x * (1.0 + jnp.tanh(math.sqrt(2.0 / math.pi) * (x + 0.044715 * (x * x * x))))


def _mod_kernel(c_ref, w_ref, b_ref, o_ref):
    c = c_ref[...]
    cond = c * jax.nn.sigmoid(c)
    o_ref[0] = jnp.dot(cond, w_ref[0], precision=lax.Precision.HIGHEST,
                       preferred_element_type=F32) + b_ref[0]


def _adaln_mod(c, ada_w, ada_b):
    depth, d, n = ada_w.shape
    bsz = c.shape[0]
    nb = d
    return pl.pallas_call(
        _mod_kernel,
        grid=(depth, n // nb),
        in_specs=[pl.BlockSpec((bsz, d), lambda l, j: (0, 0)),
                  pl.BlockSpec((1, d, nb), lambda l, j: (l, 0, j)),
                  pl.BlockSpec((1, 1, nb), lambda l, j: (l, 0, j))],
        out_specs=pl.BlockSpec((1, bsz, nb), lambda l, j: (l, 0, j)),
        out_shape=jax.ShapeDtypeStruct((depth, bsz, n), F32),
        compiler_params=_cparams(("arbitrary", "arbitrary")),
        name="adaln_mod",
    )(c, ada_w, ada_b.reshape(depth, 1, n))


def _rope_kernel(pos_ref, freq_ref, cos_ref, s1_ref, s2_ref):
    ang = pos_ref[...].astype(F32) * freq_ref[...]
    lane = lax.broadcasted_iota(jnp.int32, ang.shape, 1) % B_HEAD_DIM
    sn = jnp.sin(ang)
    cos_ref[...] = jnp.cos(ang)
    s1_ref[...] = jnp.where(lane < ROT_DIM // 2, -sn, 0.0)
    s2_ref[...] = jnp.where((lane >= ROT_DIM // 2) & (lane < ROT_DIM), sn, 0.0)


def _rope_tables(positions):
    n = positions.size
    half = ROT_DIM // 2
    lane = jnp.arange(LANES) % B_HEAD_DIM
    freqs = ROPE_THETA ** (-(lane % half).astype(F32) * 2.0 / ROT_DIM)
    freq_row = jnp.where(lane < ROT_DIM, freqs, 0.0).reshape(1, LANES).astype(F32)
    tm = min(2048, n)
    out = jax.ShapeDtypeStruct((n, LANES), F32)
    spec = pl.BlockSpec((tm, LANES), lambda i: (i, 0))
    return pl.pallas_call(
        _rope_kernel,
        grid=(n // tm,),
        in_specs=[pl.BlockSpec((tm, 1), lambda i: (i, 0)),
                  pl.BlockSpec((1, LANES), lambda i: (0, 0))],
        out_specs=[spec, spec, spec],
        out_shape=[out, out, out],
        compiler_params=_cparams(("arbitrary",)),
        name="rope_tables",
    )(positions.reshape(n, 1), freq_row)


def _inproj_kernel(x_ref, sh_ref, sc_ref, w_ref, cos_ref, s1_ref, s2_ref, lng_ref, lnb_ref,
                   ws_ref, bs_ref, a_ref, s_ref, qn_ref, kn_ref, vn_ref, *rest):
    copy_dils = [dil for dil in DILATIONS if _dilation_layout(dil) == "copy"]
    n_copy = len(copy_dils)
    q_outs, k_outs, v_outs = rest[:n_copy], rest[n_copy:2 * n_copy], rest[2 * n_copy:3 * n_copy]
    qs_ref, ks_ref, vs_ref, qt_ref, kt_ref, vt_ref = rest[3 * n_copy:]
    tm = x_ref.shape[0]
    h = _ln(x_ref[...]) * (1.0 + sc_ref[0]) + sh_ref[0]
    proj = jnp.dot(h.astype(BF16), w_ref[...], preferred_element_type=F32)

    uv = _gelu_tanh(proj[:, :2 * A_WIDTH])
    u = uv[:, :A_WIDTH]
    v = (_ln(uv[:, A_WIDTH:]) * lng_ref[...] + lnb_ref[...]).astype(BF16)
    row = lax.broadcasted_iota(jnp.int32, (CHUNK, CHUNK), 0)
    col = lax.broadcasted_iota(jnp.int32, (CHUNK, CHUNK), 1)
    head_of_lane = lax.broadcasted_iota(jnp.int32, (CHUNK, A_WIDTH), 1) // A_HEAD_DIM
    w_heads = [jnp.where(col <= row, ws_ref[hd], 0.0).astype(BF16) for hd in range(A_HEADS)]
    for cidx in range(tm // CHUNK):
        rows = slice(cidx * CHUNK, (cidx + 1) * CHUNK)
        vc = v[rows]
        sv = bs_ref[...]
        for hd in range(A_HEADS):
            full = jnp.dot(w_heads[hd], vc, preferred_element_type=F32)
            sv = sv + jnp.where(head_of_lane == hd, full, 0.0)
        a_ref[rows, :] = (u[rows] * sv).astype(a_ref.dtype)

    cos = cos_ref[...]
    s1 = s1_ref[...]
    s2 = s2_ref[...]
    q0 = 2 * A_WIDTH
    k0 = q0 + B_WIDTH
    v0 = k0 + B_WIDTH
    for j in range(B_WIDTH // LANES):
        lanes = slice(j * LANES, (j + 1) * LANES)
        for base, ref, nat, scale in ((q0, qs_ref, qn_ref, B_HEAD_DIM ** -0.5), (k0, ks_ref, kn_ref, 1.0)):
            xs = proj[:, base + j * LANES: base + (j + 1) * LANES]
            rot = (xs * cos + pltpu.roll(xs, LANES - ROT_DIM // 2, 1) * s1
                   + pltpu.roll(xs, ROT_DIM // 2, 1) * s2) * scale
            nat[:, lanes] = rot.astype(nat.dtype)
            ref[j] = rot
        vs = proj[:, v0 + j * LANES:v0 + (j + 1) * LANES]
        vn_ref[:, lanes] = vs.astype(vn_ref.dtype)
        vs_ref[j] = vs

    for src, stage, outs in ((qs_ref, qt_ref, q_outs), (ks_ref, kt_ref, k_outs), (vs_ref, vt_ref, v_outs)):
        bufs = (src, stage)
        prev = 1
        for idx, (dil, out) in enumerate(zip(copy_dils, outs)):
            chained = prev > 1 and dil % prev == 0
            base_dil = prev if chained else 1
            source = bufs[idx % 2] if chained or idx == 0 else None
            assert source is not None, "copy dilations must form a divisibility chain"
            step = dil // base_dil
            per = tm // dil
            keep = idx + 1 < len(copy_dils)
            for r_prev in range(base_dil):
                for q in range(step):
                    r = r_prev + base_dil * q
                    start = r_prev * (tm // base_dil) + q
                    for j in range(B_WIDTH // LANES):
                        val = source[j, pl.ds(start, per, stride=step), :]
                        out[0, r, :, j * LANES:(j + 1) * LANES] = val.astype(out.dtype)
                        if keep:
                            bufs[(idx + 1) % 2][j, r * per:(r + 1) * per, :] = val
            prev = dil

    s_ref[...] = proj[:, v0 + B_WIDTH:].astype(s_ref.dtype)


def _inproj(x2d, mod3, w_in_bf, cos_t, s1_t, s2_t, ln_g, ln_b, ws, bs_full, bsz, t):
    n, d = x2d.shape
    tm = min(ROW_TILE, t)
    nt = t // tm
    pw = w_in_bf.shape[1]
    c_width = pw - 2 * A_WIDTH - 3 * B_WIDTH
    tok = lambda b, i: (b * nt + i, 0)
    const2 = lambda b, i: (0, 0)
    outs = [jax.ShapeDtypeStruct((n, A_WIDTH), BF16),
            jax.ShapeDtypeStruct((t, bsz * c_width), BF16)]
    out_specs = [pl.BlockSpec((tm, A_WIDTH), tok),
                 pl.BlockSpec((tm, c_width), lambda b, i: (i, b))]
    for _ in range(3):
        outs.append(jax.ShapeDtypeStruct((n, B_WIDTH), BF16))
        out_specs.append(pl.BlockSpec((tm, B_WIDTH), tok))
    copy_dils = [dil for dil in DILATIONS if _dilation_layout(dil) == "copy"]
    for _ in range(3):
        for dil in copy_dils:
            outs.append(jax.ShapeDtypeStruct((bsz, dil, t // dil, B_WIDTH), BF16))
            out_specs.append(pl.BlockSpec((1, dil, tm // dil, B_WIDTH), lambda b, i: (b, 0, i, 0)))
    res = pl.pallas_call(
        _inproj_kernel,
        grid=(bsz, nt),
        in_specs=[pl.BlockSpec((tm, d), tok),
                  pl.BlockSpec((1, 1, d), lambda b, i: (b, 0, 0)),
                  pl.BlockSpec((1, 1, d), lambda b, i: (b, 0, 1)),
                  pl.BlockSpec((d, pw), const2),
                  pl.BlockSpec((tm, LANES), tok),
                  pl.BlockSpec((tm, LANES), tok),
                  pl.BlockSpec((tm, LANES), tok),
                  pl.BlockSpec((1, A_WIDTH), const2),
                  pl.BlockSpec((1, A_WIDTH), const2),
                  pl.BlockSpec((A_HEADS, CHUNK, CHUNK), lambda b, i: (0, 0, 0)),
                  pl.BlockSpec((CHUNK, A_WIDTH), const2)],
        out_specs=out_specs,
        out_shape=outs,
        scratch_shapes=[pltpu.VMEM((B_WIDTH // LANES, tm, LANES), F32)] * 6,
        compiler_params=_cparams(("arbitrary", "arbitrary")),
        name="inproj_gmlp_rope",
    )(x2d, mod3, mod3, w_in_bf, cos_t, s1_t, s2_t, ln_g, ln_b, ws, bs_full)
    a_out, s_in = res[0], res[1]
    natural = res[2:5]
    n_copy = len(copy_dils)
    copies = {dil: tuple(res[5 + z * n_copy + g] for z in range(3)) for g, dil in enumerate(copy_dils)}
    qkv = [copies[dil] if dil in copies else natural for dil in DILATIONS]
    return a_out, s_in, qkv


def _att_kernel(q_ref, k_ref, v_ref, o_ref, st_ref, kcat_ref, vaug_ref):
    i = pl.program_id(2)
    blk = ATT_BLK
    n_res = q_ref.shape[1]
    n_sub = q_ref.shape[2] // blk
    n_pairs = B_WIDTH // LANES
    last = slice(n_sub * blk, (n_sub + 1) * blk)

    @pl.when(i == 0)
    def _():
        kcat_ref[:, 0:blk, :] = jnp.zeros((n_res, blk, B_WIDTH), BF16)
        vaug_ref[:, :, 0:blk, 0:LANES] = jnp.zeros((n_res, n_pairs, blk, LANES), BF16)
        vaug_ref[:, :, :, LANES:] = jnp.ones((n_res, n_pairs, (n_sub + 1) * blk, LANES), BF16)

    @pl.when(i > 0)
    def _():
        kcat_ref[:, 0:blk, :] = kcat_ref[:, last, :]
        vaug_ref[:, :, 0:blk, 0:LANES] = vaug_ref[:, :, last, 0:LANES]

    for rr in range(n_res):
        kcat_ref[rr, blk:, :] = k_ref[0, rr]
        for pair in range(n_pairs):
            vaug_ref[rr, pair, blk:, 0:LANES] = v_ref[0, rr, :, pair * LANES:(pair + 1) * LANES]

    qi = lax.broadcasted_iota(jnp.int32, (blk, 2 * blk), 0)
    kk = lax.broadcasted_iota(jnp.int32, (blk, 2 * blk), 1)
    band = (kk >= qi) & (kk <= qi + blk)
    first = band & ((i > 0) | (kk >= blk))
    lane = lax.broadcasted_iota(jnp.int32, (blk, LANES), 1)
    low_half = lane < B_HEAD_DIM
    nt = (((1,), (1,)), ((), ()))
    for rr, sub in [(rr, sub) for rr in range(n_res) for sub in range(n_sub)]:
        rows = slice(sub * blk, (sub + 1) * blk)
        keys = slice(sub * blk, (sub + 2) * blk)
        mask = first if sub == 0 else band
        stats = jnp.zeros((blk, LANES), F32)
        for pair in range(n_pairs):
            lanes = slice(pair * LANES, (pair + 1) * LANES)
            q2 = q_ref[0, rr, rows, lanes]
            kslab = kcat_ref[rr, keys, lanes]
            res = []
            for hh in range(2):
                hd = 2 * pair + hh
                qm = jnp.where(low_half if hh == 0 else jnp.logical_not(low_half), q2, jnp.zeros_like(q2))
                s = lax.dot_general(qm, kslab, nt, preferred_element_type=F32)
                s = jnp.where(mask, s, NEG_INF)
                m = jnp.max(jnp.maximum(s[:, :blk], s[:, blk:]), axis=1, keepdims=True)
                p = jnp.exp(s - m).astype(BF16)
                ov = jnp.dot(p, vaug_ref[rr, pair, keys, :], preferred_element_type=F32)
                den = ov[:, LANES:]
                res.append(ov[:, :LANES] / den)
                stats = jnp.where(lane == hd, m, stats)
                stats = jnp.where(lane == B_HEADS + hd, den, stats)
            o_ref[0, rr, rows, lanes] = jnp.where(low_half, res[0], res[1]).astype(o_ref.dtype)
        st_ref[0, rr, rows, :] = stats


def _dilation_layout(dil):
    return "natural" if dil == 1 else "copy"


def _att_branch(q, k, v, dil, bsz, t):
    rows = t // dil
    n_sub = min(ATT_SUB_BLOCKS, rows // ATT_BLK)
    step_rows = n_sub * ATT_BLK
    n_steps = rows // step_rows
    n_res = math.gcd(dil, max(1, ATT_SUB_BLOCKS // n_sub)) if n_steps == 1 else 1
    shape = lambda w: (bsz, dil, rows, w)
    blk = lambda w: pl.BlockSpec((1, n_res, step_rows, w), lambda b, r, i: (b, r, i, 0))
    q, k, v = (z.reshape(shape(B_WIDTH)) for z in (q, k, v))
    return pl.pallas_call(
        _att_kernel,
        grid=(bsz, dil // n_res, n_steps),
        in_specs=[blk(B_WIDTH), blk(B_WIDTH), blk(B_WIDTH)],
        out_specs=[blk(B_WIDTH), blk(LANES)],
        out_shape=[jax.ShapeDtypeStruct(shape(B_WIDTH), BF16),
                   jax.ShapeDtypeStruct(shape(LANES), F32)],
        scratch_shapes=[pltpu.VMEM((n_res, step_rows + ATT_BLK, B_WIDTH), BF16),
                        pltpu.VMEM((n_res, B_WIDTH // LANES, step_rows + ATT_BLK, 2 * LANES), BF16)],
        compiler_params=_cparams(("arbitrary", "arbitrary", "arbitrary")),
        name=f"dilated_attn_d{dil}",
    )(q, k, v)


def _ssm_kernel(u_ref, bmat_ref, cmat_ref, are_ref, aim_ref, dskip_ref, gw_ref, gb_ref,
                o_ref, state_ref, xbuf_ref, rows_ref):
    bsz, two_n = state_ref.shape
    n_state = two_n // 2
    steps = u_ref.shape[0]
    cw = u_ref.shape[1] // bsz
    n_slab = cw // LANES

    @pl.when(pl.program_id(0) == 0)
    def _():
        state_ref[...] = jnp.zeros_like(state_ref)

    for b in range(bsz):
        for j in range(n_slab):
            rows_ref[j, pl.ds(b, steps, stride=bsz), :] = (
                u_ref[:, b * cw + j * LANES:b * cw + (j + 1) * LANES].astype(F32))
    u32 = jnp.concatenate([rows_ref[j] for j in range(n_slab)], axis=1)
    xbuf_ref[...] = jnp.dot(u32.astype(BF16), bmat_ref[...], preferred_element_type=F32)
    a_re = jnp.broadcast_to(are_ref[...], (bsz, n_state))
    a_im = jnp.broadcast_to(aim_ref[...], (bsz, n_state))

    def step(tt, carry):
        x_re, x_im = carry
        r0 = pl.multiple_of(tt * bsz, bsz)
        b_re = xbuf_ref[pl.ds(r0, bsz), :n_state]
        b_im = xbuf_ref[pl.ds(r0, bsz), n_state:]
        n_re = a_re * x_re - a_im * x_im + b_re
        n_im = a_re * x_im + a_im * x_re + b_im
        xbuf_ref[pl.ds(r0, bsz), :n_state] = n_re
        xbuf_ref[pl.ds(r0, bsz), n_state:] = n_im
        return n_re, n_im

    x_re, x_im = lax.fori_loop(0, steps, step, (state_ref[:, :n_state], state_ref[:, n_state:]))
    state_ref[:, :n_state] = x_re
    state_ref[:, n_state:] = x_im

    y = jnp.dot(xbuf_ref[...].astype(BF16), cmat_ref[...], preferred_element_type=F32)
    y = _gelu_tanh(y + dskip_ref[...] * u32)
    gate = jax.nn.sigmoid(jnp.dot(y.astype(BF16), gw_ref[...], preferred_element_type=F32) + gb_ref[...])
    out = y * gate
    for j in range(n_slab):
        rows_ref[j] = out[:, j * LANES:(j + 1) * LANES]
    for b in range(bsz):
        for j in range(n_slab):
            o_ref[:, b * cw + j * LANES:b * cw + (j + 1) * LANES] = (
                rows_ref[j, pl.ds(b, steps, stride=bsz), :].astype(o_ref.dtype))


def _ssm_weights(lam_re, lam_im, log_dt, b_re, b_im, c_re, c_im):
    g, n_st = lam_re.shape
    cg = b_re.shape[-1]
    dt = jnp.exp(log_dt)[:, None]
    mag = jnp.exp(lam_re * dt)
    ab_re = mag * jnp.cos(lam_im * dt)
    ab_im = mag * jnp.sin(lam_im * dt)
    nr = ab_re - 1.0
    ni = ab_im
    mod2 = lam_re * lam_re + lam_im * lam_im
    f_re = (nr * lam_re + ni * lam_im) / mod2
    f_im = (ni * lam_re - nr * lam_im) / mod2
    bb_re = f_re[..., None] * b_re - f_im[..., None] * b_im
    bb_im = f_re[..., None] * b_im + f_im[..., None] * b_re
    eye = jnp.eye(g, dtype=F32)
    bm_re = jnp.einsum('gnc,gh->gchn', bb_re, eye).reshape(g * cg, g * n_st)
    bm_im = jnp.einsum('gnc,gh->gchn', bb_im, eye).reshape(g * cg, g * n_st)
    bmat = jnp.concatenate([bm_re, bm_im], axis=1)
    cm_re = jnp.einsum('gcn,gh->gnhc', c_re, eye).reshape(g * n_st, g * cg)
    cm_im = jnp.einsum('gcn,gh->gnhc', c_im, eye).reshape(g * n_st, g * cg)
    cmat = jnp.concatenate([cm_re, -cm_im], axis=0)
    return (bmat.astype(BF16), cmat.astype(BF16),
            ab_re.reshape(1, g * n_st), ab_im.reshape(1, g * n_st))


def _ssm(u_tm, bmat, cmat, a_re, a_im, d_skip, glu_w_bf, glu_b, bsz, t):
    cw = bmat.shape[0]
    two_n = bmat.shape[1]
    steps = min(SSM_STEPS, t)
    rows = steps * bsz
    const = lambda i: (0, 0)
    return pl.pallas_call(
        _ssm_kernel,
        grid=(t // steps,),
        in_specs=[pl.BlockSpec((steps, bsz * cw), lambda i: (i, 0)),
                  pl.BlockSpec((cw, two_n), const),
                  pl.BlockSpec((two_n, cw), const),
                  pl.BlockSpec((1, two_n // 2), const),
                  pl.BlockSpec((1, two_n // 2), const),
                  pl.BlockSpec((1, cw), const),
                  pl.BlockSpec((cw, cw), const),
                  pl.BlockSpec((1, cw), const)],
        out_specs=pl.BlockSpec((steps, bsz * cw), lambda i: (i, 0)),
        out_shape=jax.ShapeDtypeStruct((t, bsz * cw), BF16),
        scratch_shapes=[pltpu.VMEM((bsz, two_n), F32), pltpu.VMEM((rows, two_n), F32),
                        pltpu.VMEM((cw // LANES, rows, LANES), F32)],
        compiler_params=_cparams(("arbitrary",)),
        name="s5_scan_glu",
    )(u_tm, bmat, cmat, a_re, a_im, d_skip, glu_w_bf, glu_b)


def _route(sel, scores):
    gs = []
    for g in range(N_EXPERT_GROUPS):
        a, b, c, d = sel[4 * g: 4 * g + 4]
        hi1, lo1 = jnp.maximum(a, b), jnp.minimum(a, b)
        hi2, lo2 = jnp.maximum(c, d), jnp.minimum(c, d)
        gs.append(jnp.maximum(hi1, hi2) + jnp.maximum(jnp.minimum(hi1, hi2), jnp.maximum(lo1, lo2)))
    g_idx = jnp.zeros(gs[0].shape, jnp.int32)
    best = gs[0]
    for g in range(1, N_EXPERT_GROUPS):
        better = gs[g] > best
        g_idx = jnp.where(better, g, g_idx)
        best = jnp.where(better, gs[g], best)

    def pick(rows, j):
        out = rows[j]
        for g in range(1, N_EXPERT_GROUPS):
            out = jnp.where(g_idx == g, rows[4 * g + j], out)
        return out

    v = [pick(sel, j) for j in range(EXPERTS_PER_GROUP)]
    s = [pick(scores, j) for j in range(EXPERTS_PER_GROUP)]
    i1 = jnp.zeros(g_idx.shape, jnp.int32)
    b1, g1 = v[0], s[0]
    for j in range(1, EXPERTS_PER_GROUP):
        better = v[j] > b1
        i1 = jnp.where(better, j, i1)
        b1 = jnp.where(better, v[j], b1)
        g1 = jnp.where(better, s[j], g1)
    i2 = jnp.full(g_idx.shape, -1, jnp.int32)
    b2 = jnp.zeros_like(b1)
    g2 = jnp.zeros_like(g1)
    for j in range(EXPERTS_PER_GROUP):
        better = (i1 != j) & ((i2 < 0) | (v[j] > b2))
        i2 = jnp.where(better, j, i2)
        b2 = jnp.where(better, v[j], b2)
        g2 = jnp.where(better, s[j], g2)
    tot = g1 + g2
    base = g_idx * EXPERTS_PER_GROUP
    return base + i1, base + i2, g1 / tot, g2 / tot


def _natural_order(ref, scr_ref, dil):
    if _dilation_layout(dil) != "copy":
        return ref[...].astype(F32)
    per = ref.shape[2]
    n_slab = scr_ref.shape[0]
    for r in range(dil):
        for j in range(n_slab):
            scr_ref[j, pl.ds(r, per, stride=dil), :] = ref[0, r, :, j * LANES:(j + 1) * LANES].astype(F32)
    if n_slab == 1:
        return scr_ref[0]
    return jnp.concatenate([scr_ref[j] for j in range(n_slab)], axis=1)


def _outproj_kernel(a_ref, o1_ref, o2_ref, o3_ref, st1_ref, st2_ref, st3_ref,
                    c_ref, x_ref, g1_ref, sh2_ref, sc2_ref, wout_ref, expand_ref, lng_ref, lnb_ref,
                    rwh_ref, rwl_ref, rb_ref, x1_ref, h2_ref, eidx_ref, gate_ref, count_ref,
                    oscr2_ref, oscr3_ref, sscr2_ref, sscr3_ref, *, alpha):
    o_refs = (o1_ref, o2_ref, o3_ref)
    st_refs = (st1_ref, st2_ref, st3_ref)
    o_scr = (None, oscr2_ref, oscr3_ref)
    st_scr = (None, sscr2_ref, sscr3_ref)
    ms = [_natural_order(st_refs[g], st_scr[g], DILATIONS[g]) for g in range(len(DILATIONS))]
    dens = [pltpu.roll(m, LANES - B_HEADS, 1) for m in ms]
    mx = jnp.maximum(jnp.maximum(ms[0], ms[1]), ms[2])
    ws = [dens[g] * jnp.exp(ms[g] - mx) for g in range(len(DILATIONS))]
    lane = lax.broadcasted_iota(jnp.int32, mx.shape, 1)
    tot = jnp.where(lane < B_HEADS, ws[0] + ws[1] + ws[2], 1.0)
    expand = expand_ref[...]

    def widen(w):
        wn = jnp.where(lane < B_HEADS, w / tot, 0.0)
        hi = wn.astype(BF16)
        lo = (wn - hi.astype(F32)).astype(BF16)
        return (jnp.dot(hi, expand, preferred_element_type=F32)
                + jnp.dot(lo, expand, preferred_element_type=F32))

    b_out = widen(ws[0]) * _natural_order(o_refs[0], o_scr[0], DILATIONS[0])
    for g in range(1, len(DILATIONS)):
        b_out = b_out + widen(ws[g]) * _natural_order(o_refs[g], o_scr[g], DILATIONS[g])
    b_out = b_out.astype(BF16)

    wout = wout_ref
    mix = (jnp.dot(a_ref[...], wout[:A_WIDTH, :], preferred_element_type=F32)
           + jnp.dot(b_out, wout[A_WIDTH:A_WIDTH + B_WIDTH, :], preferred_element_type=F32)
           + jnp.dot(c_ref[...], wout[A_WIDTH + B_WIDTH:, :], preferred_element_type=F32))
    x1 = _ln(alpha * x_ref[...] + (1.0 + g1_ref[0]) * mix) * lng_ref[...] + lnb_ref[...]
    x1_ref[...] = x1
    h2 = _ln(x1) * (1.0 + sc2_ref[0]) + sh2_ref[0]
    _store_tile_rows(h2_ref, h2)

    hi = h2.astype(BF16)
    lo = (h2 - hi.astype(F32)).astype(BF16)
    nt = (((1,), (1,)), ((), ()))
    logits = (lax.dot_general(rwh_ref[...], hi, nt, preferred_element_type=F32)
              + lax.dot_general(rwh_ref[...], lo, nt, preferred_element_type=F32)
              + lax.dot_general(rwl_ref[...], hi, nt, preferred_element_type=F32))
    scores = jax.nn.sigmoid(logits)
    sel = scores + rb_ref[...]
    sel_rows = [sel[e:e + 1, :] for e in range(N_EXPERTS)]
    score_rows = [scores[e:e + 1, :] for e in range(N_EXPERTS)]
    e1, e2, gt1, gt2 = _route(sel_rows, score_rows)

    @pl.when((pl.program_id(0) == 0) & (pl.program_id(1) == 0))
    def _():
        count_ref[...] = jnp.zeros_like(count_ref)

    tm = h2.shape[0]
    erow = lax.broadcasted_iota(jnp.int32, (N_EXPERTS, tm), 0)
    oh1 = erow == e1
    oh2 = erow == e2
    both = (oh1 | oh2).astype(BF16)
    earlier = (lax.broadcasted_iota(jnp.int32, (tm, tm), 0)
               < lax.broadcasted_iota(jnp.int32, (tm, tm), 1)).astype(BF16)
    base = count_ref[:, 0:1]
    cnt = jnp.dot(both, earlier, preferred_element_type=F32) + base
    r1 = jnp.sum(jnp.where(oh1, cnt, 0.0), axis=0, keepdims=True)
    r2 = jnp.sum(jnp.where(oh2, cnt, 0.0), axis=0, keepdims=True)
    total = base + jnp.sum(both.astype(F32), axis=1, keepdims=True)
    count_ref[...] = jnp.broadcast_to(total, count_ref.shape)

    eidx_ref[...] = jnp.zeros_like(eidx_ref)
    gate_ref[...] = jnp.zeros_like(gate_ref)
    eidx_ref[0:1, :] = e1
    eidx_ref[1:2, :] = e2
    eidx_ref[2:3, :] = r1.astype(jnp.int32)
    eidx_ref[3:4, :] = r2.astype(jnp.int32)
    gate_ref[0:1, :] = gt1
    gate_ref[1:2, :] = gt2


def _outproj(a_out, att, c_tm, x2d, mod3, w_out_bf, ln_g, ln_b, rw_hi, rw_lo, rbias, bsz, t, alpha):
    n, d = x2d.shape
    tm = min(ROW_TILE, t)
    nt = t // tm
    cw = c_tm.shape[1] // bsz
    tok = lambda b, i: (b * nt + i, 0)
    const2 = lambda b, i: (0, 0)
    head = jnp.arange(B_WIDTH) // B_HEAD_DIM
    expand = (jnp.arange(LANES)[:, None] == head[None, :]).astype(BF16)

    def branch_input(z, dil, w):
        if _dilation_layout(dil) == "copy":
            return z, pl.BlockSpec((1, dil, tm // dil, w), lambda b, i: (b, 0, i, 0))
        return z.reshape(n, w), pl.BlockSpec((tm, w), tok)

    o_in = [branch_input(o, dil, B_WIDTH) for (o, _), dil in zip(att, DILATIONS)]
    st_in = [branch_input(st, dil, LANES) for (_, st), dil in zip(att, DILATIONS)]
    (o1, st1), (o2, st2), (o3, st3) = [(o_in[g][0], st_in[g][0]) for g in range(len(DILATIONS))]
    modspec = lambda j: pl.BlockSpec((1, 1, d), lambda b, i: (b, 0, j))
    return pl.pallas_call(
        functools.partial(_outproj_kernel, alpha=alpha),
        grid=(bsz, nt),
        in_specs=[pl.BlockSpec((tm, A_WIDTH), tok)]
                 + [spec for _, spec in o_in]
                 + [spec for _, spec in st_in]
                 + [pl.BlockSpec((tm, cw), lambda b, i: (i, b)),
                  pl.BlockSpec((tm, d), tok),
                  modspec(2), modspec(3), modspec(4),
                  pl.BlockSpec((d, d), const2),
                  pl.BlockSpec((LANES, B_WIDTH), const2),
                  pl.BlockSpec((1, d), const2), pl.BlockSpec((1, d), const2),
                  pl.BlockSpec((N_EXPERTS, d), const2), pl.BlockSpec((N_EXPERTS, d), const2),
                  pl.BlockSpec((N_EXPERTS, 1), const2)],
        out_specs=[pl.BlockSpec((tm, d), tok), pl.BlockSpec((tm * (d // LANES), LANES), tok),
                   pl.BlockSpec((SUBLANES, tm), lambda b, i: (0, b * nt + i)),
                   pl.BlockSpec((SUBLANES, tm), lambda b, i: (0, b * nt + i)),
                   pl.BlockSpec((N_EXPERTS, LANES), const2)],
        out_shape=[jax.ShapeDtypeStruct((n, d), F32),
                   jax.ShapeDtypeStruct((n * (d // LANES), LANES), F32),

                   jax.ShapeDtypeStruct((SUBLANES, n), jnp.int32),
                   jax.ShapeDtypeStruct((SUBLANES, n), F32),
                   jax.ShapeDtypeStruct((N_EXPERTS, LANES), F32)],
        scratch_shapes=[pltpu.VMEM((B_WIDTH // LANES, tm, LANES), F32),
                        pltpu.VMEM((B_WIDTH // LANES, tm, LANES), F32),
                        pltpu.VMEM((1, tm, LANES), F32), pltpu.VMEM((1, tm, LANES), F32)],
        compiler_params=_cparams(("arbitrary", "arbitrary")),
        name="outproj_ln_router",
    )(a_out, o1, o2, o3, st1, st2, st3, c_tm, x2d, mod3, mod3, mod3, w_out_bf, expand,
      ln_g, ln_b, rw_hi, rw_lo, rbias)


def _dispatch_plan(eidx, counts, n_tok, blk):
    e = eidx[:TOP_K]
    rank = eidx[TOP_K:2 * TOP_K]
    pcounts = (counts + blk - 1) // blk * blk
    pends = jnp.cumsum(pcounts)
    pstarts = pends - pcounts
    seg_start = jnp.zeros_like(e)
    for j in range(N_EXPERTS):
        seg_start = jnp.where(e == j, pstarts[j], seg_start)
    dest = (seg_start + rank).astype(jnp.int32)
    n_blk = n_tok * TOP_K // blk + N_EXPERTS
    blk_start = jnp.arange(n_blk, dtype=jnp.int32) * blk
    blk_e = jnp.minimum(jnp.sum((pends[None, :] <= blk_start[:, None]).astype(jnp.int32), axis=1),
                        N_EXPERTS - 1).astype(jnp.int32)
    n_used = (pends[-1] // blk).astype(jnp.int32).reshape(1)
    return dest, blk_e, n_used, pends.astype(jnp.int32), pcounts.astype(jnp.int32)


def _dispatch_kernel(pend_ref, pcnt_ref, dest_ref, h2_ref, xs_hbm, zbuf, sem, zsem, *, blk, n_slab):
    i = pl.program_id(0)
    rows = h2_ref.shape[0] // n_slab
    blk_rows = blk * n_slab

    @pl.when(i == 0)
    def _():
        zbuf[...] = jnp.zeros_like(zbuf)
        for e in range(N_EXPERTS):
            @pl.when(pcnt_ref[e] > 0)
            def _():
                start = pl.multiple_of((pend_ref[e] - blk) * n_slab, blk_rows)
                cp = pltpu.make_async_copy(zbuf, xs_hbm.at[pl.ds(start, blk_rows), :], zsem)
                cp.start()
                cp.wait()
        n_blk = xs_hbm.shape[0] // blk_rows
        for t in range(N_EXPERTS):
            tail = pend_ref[N_EXPERTS - 1] // blk + t

            @pl.when(tail < n_blk)
            def _():
                start = pl.multiple_of(tail * blk_rows, blk_rows)
                cp = pltpu.make_async_copy(zbuf, xs_hbm.at[pl.ds(start, blk_rows), :], zsem)
                cp.start()
                cp.wait()

    def issue(s, c):
        src = h2_ref.at[pl.ds(pl.multiple_of(s * n_slab, n_slab), n_slab), :]
        for kk in range(TOP_K):
            slot = pl.multiple_of(dest_ref[0, 0, kk * rows + s] * n_slab, n_slab)
            pltpu.make_async_copy(src, xs_hbm.at[pl.ds(slot, n_slab), :], sem).start()
        return c
    lax.fori_loop(0, rows, issue, 0, unroll=8)
    for kk in range(TOP_K):
        pltpu.make_async_copy(h2_ref, xs_hbm.at[pl.ds(0, rows * n_slab), :], sem).wait()


def _expert_kernel(blk_e_ref, nused_ref, xs_ref, wg_ref, wu_ref, wd_ref, ys_ref, wg_bf, wu_bf, wd_bf,
                   *, blk, n_slab):
    j = pl.program_id(0)

    @pl.when((j == 0) | (blk_e_ref[j] != blk_e_ref[jnp.maximum(j - 1, 0)]))
    def _():
        wg_bf[...] = wg_ref[0, 0].astype(BF16)
        wu_bf[...] = wu_ref[0, 0].astype(BF16)
        wd_bf[...] = wd_ref[0, 0].astype(BF16)

    @pl.when(j < nused_ref[0])
    def _():
        rows = _load_tile_rows(xs_ref, 0, blk, n_slab).astype(BF16)
        gt = jnp.dot(rows, wg_bf[...], preferred_element_type=F32)
        up = jnp.dot(rows, wu_bf[...], preferred_element_type=F32)
        hid = (gt * jax.nn.sigmoid(gt) * up).astype(BF16)
        _store_tile_rows(ys_ref, jnp.dot(hid, wd_bf[...], preferred_element_type=F32))

    @pl.when(j >= nused_ref[0])
    def _():
        ys_ref[...] = jnp.zeros_like(ys_ref)


def _moe(h2, eidx, counts, w_gate, w_up, w_down, layer, rows):
    d, de = w_gate.shape[-2:]
    n_slab = d // LANES
    n_tok = h2.shape[0] // n_slab
    blk = min(MOE_BLK, n_tok)
    dest, blk_e, n_used, pends, pcounts = _dispatch_plan(eidx, counts, n_tok, blk)
    n_blk = blk_e.shape[0]
    cap = n_blk * blk

    nt = n_tok // rows
    dest_tiles = dest.reshape(TOP_K, nt, rows).transpose(1, 0, 2).reshape(nt, 1, TOP_K * rows)
    xs = pl.pallas_call(
        functools.partial(_dispatch_kernel, blk=blk, n_slab=n_slab),
        grid_spec=pltpu.PrefetchScalarGridSpec(
            num_scalar_prefetch=2,
            grid=(nt,),
            in_specs=[pl.BlockSpec((1, 1, TOP_K * rows), lambda i, pe, pc: (i, 0, 0),
                                   memory_space=pltpu.SMEM),
                      pl.BlockSpec((rows * n_slab, LANES), lambda i, pe, pc: (i, 0))],
            out_specs=pl.BlockSpec(memory_space=pl.ANY),
            scratch_shapes=[pltpu.VMEM((blk * n_slab, LANES), F32),
                            pltpu.SemaphoreType.DMA, pltpu.SemaphoreType.DMA],
        ),
        out_shape=jax.ShapeDtypeStruct((cap * n_slab, LANES), F32),
        compiler_params=_cparams(("arbitrary",)),
        name="moe_dispatch",
    )(pends, pcounts, dest_tiles, h2)

    used = lambda j, be, nu: (jnp.minimum(j, nu[0] - 1), 0)
    ys = pl.pallas_call(
        functools.partial(_expert_kernel, blk=blk, n_slab=n_slab),
        grid_spec=pltpu.PrefetchScalarGridSpec(
            num_scalar_prefetch=2,
            grid=(n_blk,),
            in_specs=[pl.BlockSpec((blk * n_slab, LANES), used),
                      pl.BlockSpec((1, 1, d, de), lambda j, be, nu: (layer, be[j], 0, 0)),
                      pl.BlockSpec((1, 1, d, de), lambda j, be, nu: (layer, be[j], 0, 0)),
                      pl.BlockSpec((1, 1, de, d), lambda j, be, nu: (layer, be[j], 0, 0))],
            out_specs=pl.BlockSpec((blk * n_slab, LANES), lambda j, be, nu: (j, 0)),
            scratch_shapes=[pltpu.VMEM((d, de), BF16), pltpu.VMEM((d, de), BF16), pltpu.VMEM((de, d), BF16)],
        ),
        out_shape=jax.ShapeDtypeStruct((cap * n_slab, LANES), F32),
        compiler_params=_cparams(("arbitrary",)),
        name="moe_experts",
    )(blk_e, n_used, xs, w_gate, w_up, w_down)
    return ys, dest


def _ffn_ln_kernel(dest_ref, next_ref, x_ref, gate_ref, g2_ref, lng_ref, lnb_ref, ys_hbm, o_ref, ybuf, sem,
                   *, alpha):
    tm, d = x_ref.shape
    n_slab = d // LANES
    g = pl.program_id(0) * pl.num_programs(1) + pl.program_id(1)
    n_tiles = pl.num_programs(0) * pl.num_programs(1)
    slot = g % 2

    def gather(idx_ref, buf_slot):
        def issue(s, c):
            src = pl.multiple_of(idx_ref[0, 0, s] * n_slab, n_slab)
            dst = pl.multiple_of(s * n_slab, n_slab)
            pltpu.make_async_copy(ys_hbm.at[pl.ds(src, n_slab), :],
                                  ybuf.at[buf_slot, pl.ds(dst, n_slab), :], sem.at[buf_slot]).start()
            return c
        lax.fori_loop(0, TOP_K * tm, issue, 0, unroll=8)

    @pl.when(g == 0)
    def _():
        gather(dest_ref, 0)

    @pl.when(g + 1 < n_tiles)
    def _():
        gather(next_ref, 1 - slot)

    pltpu.make_async_copy(ys_hbm.at[pl.ds(0, TOP_K * tm * n_slab), :], ybuf.at[slot], sem.at[slot]).wait()

    gate = gate_ref[...]
    buf = ybuf.at[slot]
    ffn = _load_tile_rows(buf, 0, tm, n_slab) * gate[:, 0:1]
    for kk in range(1, TOP_K):
        ffn = ffn + _load_tile_rows(buf, kk * tm, tm, n_slab) * gate[:, kk:kk + 1]
    o_ref[...] = _ln(alpha * x_ref[...] + (1.0 + g2_ref[0]) * ffn) * lng_ref[...] + lnb_ref[...]


def _ffn_ln(x1, ys, dest, gates, mod3, ln_g, ln_b, bsz, t, alpha):
    n, d = x1.shape
    n_slab = d // LANES
    tm = min(ROW_TILE, t)
    nt = t // tm
    n_tiles = bsz * nt
    dest_tiles = dest.reshape(TOP_K, n_tiles, tm).transpose(1, 0, 2).reshape(n_tiles, 1, TOP_K * tm)
    tok = lambda b, i, *_: (b * nt + i, 0)
    const2 = lambda b, i, *_: (0, 0)
    gate_cols = gates[:TOP_K].T
    idx_blk = (1, 1, TOP_K * tm)
    return pl.pallas_call(
        functools.partial(_ffn_ln_kernel, alpha=alpha),
        grid=(bsz, nt),
        in_specs=[pl.BlockSpec(idx_blk, lambda b, i: (b * nt + i, 0, 0), memory_space=pltpu.SMEM),
                  pl.BlockSpec(idx_blk, lambda b, i: (jnp.minimum(b * nt + i + 1, n_tiles - 1), 0, 0),
                               memory_space=pltpu.SMEM),
                  pl.BlockSpec((tm, d), tok), pl.BlockSpec((tm, TOP_K), tok),
                  pl.BlockSpec((1, 1, d), lambda b, i: (b, 0, 5)),
                  pl.BlockSpec((1, d), const2), pl.BlockSpec((1, d), const2),
                  pl.BlockSpec(memory_space=pl.ANY)],
        out_specs=pl.BlockSpec((tm, d), tok),
        out_shape=jax.ShapeDtypeStruct((n, d), F32),
        scratch_shapes=[pltpu.VMEM((2, TOP_K * tm * n_slab, LANES), F32), pltpu.SemaphoreType.DMA((2,))],
        compiler_params=_cparams(("arbitrary", "arbitrary")),
        name="ffn_combine_ln",
    )(dest_tiles, dest_tiles, x1, gate_cols, mod3, ln_g, ln_b, ys)


def kernel(x, c, positions, ada_w, ada_b, w_in, gm_ln_g, gm_ln_b, gm_ws, gm_bs, ssm_lam_re, ssm_lam_im, ssm_log_dt, ssm_b_re, ssm_b_im, ssm_c_re, ssm_c_im, ssm_d, glu_w, glu_b, w_out, ln1_g, ln1_b, router_w, router_bias, exp_w_gate, exp_w_up, exp_w_down, ln2_g, ln2_b):
    bsz, t, d = x.shape
    depth = ada_w.shape[0]
    alpha = (2.0 * depth) ** 0.25
    n = bsz * t

    mod = _adaln_mod(c.astype(F32), ada_w, ada_b)
    cos_t, s1_t, s2_t = _rope_tables(positions)
    rw_t = router_w.astype(F32).T
    rw_hi = rw_t.astype(BF16)
    rw_lo = (rw_t - rw_hi.astype(F32)).astype(BF16)
    rbias = router_bias.astype(F32).reshape(N_EXPERTS, 1)

    xf = x.astype(F32).reshape(n, d)
    for l in range(depth):
        mod3 = mod[l].reshape(bsz, 1, 6 * d)
        bs_full = jnp.repeat(gm_bs[l].T, A_HEAD_DIM, axis=1)
        a_out, s_in, qkv = _inproj(
            xf, mod3, w_in[l].astype(BF16), cos_t, s1_t, s2_t,
            gm_ln_g[l].reshape(1, A_WIDTH), gm_ln_b[l].reshape(1, A_WIDTH), gm_ws[l], bs_full, bsz, t)
        att = [_att_branch(*qkv[g], dil, bsz, t) for g, dil in enumerate(DILATIONS)]
        bmat, cmat, a_re, a_im = _ssm_weights(ssm_lam_re[l], ssm_lam_im[l], ssm_log_dt[l],
                                              ssm_b_re[l], ssm_b_im[l], ssm_c_re[l], ssm_c_im[l])
        cw = bmat.shape[0]
        c_out = _ssm(s_in, bmat, cmat, a_re, a_im, ssm_d[l].reshape(1, cw),
                     glu_w[l].astype(BF16), glu_b[l].reshape(1, cw), bsz, t)
        x1, h2, eidx, gates, counts = _outproj(a_out, att, c_out, xf, mod3, w_out[l].astype(BF16),
                                               ln1_g[l].reshape(1, d), ln1_b[l].reshape(1, d),
                                               rw_hi, rw_lo, rbias, bsz, t, alpha)
        ys, dest = _moe(h2, eidx, counts[:, 0].astype(jnp.int32),
                              exp_w_gate.astype(F32), exp_w_up.astype(F32), exp_w_down.astype(F32), l,
                              min(MOE_ROW_TILE, t))
        xf = _ffn_ln(x1, ys, dest, gates, mod3, ln2_g[l].reshape(1, d), ln2_b[l].reshape(1, d),
                     bsz, t, alpha)
    return xf.reshape(bsz, t, d)
```

```python
import functools
import math

import jax
import jax.numpy as jnp
from jax import lax
from jax.experimental import pallas as pl
from jax.experimental.pallas import tpu as pltpu

F32 = jnp.float32
BF16 = jnp.bfloat16

A_HEADS = 4
A_HEAD_DIM = 64
A_WIDTH = A_HEADS * A_HEAD_DIM
CHUNK = 128
B_HEADS = 8
B_HEAD_DIM = 64
B_WIDTH = B_HEADS * B_HEAD_DIM
DILATIONS = (1, 4, 16)
ATT_BLK = 128
ROT_DIM = B_HEAD_DIM // 4
ROPE_THETA = 500000.0
SSM_GROUP = 16
SSM_STATE = 64
N_EXPERTS = 16
N_EXPERT_GROUPS = 4
EXPERTS_PER_GROUP = 4
TOP_K = 2
LN_EPS = 1e-5
NEG_INF = -1e30
Q_SCALE = B_HEAD_DIM ** -0.5 * math.log2(math.e)

LANES = 128
SUBLANES = 8
VMEM_LIMIT = 56 * 1024 * 1024
ROW_TILE = 512
SSM_STEPS = 128
MOE_BLK = 512
MOE_ROW_TILE = 1024
ATT_SUB_BLOCKS = 8


def _cparams(sem):
    return pltpu.CompilerParams(dimension_semantics=sem, vmem_limit_bytes=VMEM_LIMIT)


def _ln(x):
    mu = jnp.mean(x, axis=-1, keepdims=True)
    xc = x - mu
    var = jnp.mean(xc * xc, axis=-1, keepdims=True)
    return xc * lax.rsqrt(var + LN_EPS)


def _store_tile_rows(ref, val):
    n_slab = val.shape[1] // LANES
    for j in range(n_slab):
        ref[pl.ds(j, val.shape[0], stride=n_slab), :] = val[:, j * LANES:(j + 1) * LANES]


def _load_tile_rows(ref, start, rows, n_slab):
    return jnp.concatenate([ref[pl.ds(start * n_slab + j, rows, stride=n_slab), :] for j in range(n_slab)],
                           axis=1)


def _gelu_tanh(x):
    return 0.5 * x * (1.0 + jnp.tanh(math.sqrt(2.0 / math.pi) * (x + 0.044715 * (x * x * x))))


def _mod_kernel(c_ref, w_ref, b_ref, o_ref):
    c = c_ref[...]
    cond = c * jax.nn.sigmoid(c)
    o_ref[0] = jnp.dot(cond, w_ref[0], precision=lax.Precision.HIGHEST,
                       preferred_element_type=F32) + b_ref[0]


def _adaln_mod(c, ada_w, ada_b):
    depth, d, n = ada_w.shape
    bsz = c.shape[0]
    nb = d
    return pl.pallas_call(
        _mod_kernel,
        grid=(depth, n // nb),
        in_specs=[pl.BlockSpec((bsz, d), lambda l, j: (0, 0)),
                  pl.BlockSpec((1, d, nb), lambda l, j: (l, 0, j)),
                  pl.BlockSpec((1, 1, nb), lambda l, j: (l, 0, j))],
        out_specs=pl.BlockSpec((1, bsz, nb), lambda l, j: (l, 0, j)),
        out_shape=jax.ShapeDtypeStruct((depth, bsz, n), F32),
        compiler_params=_cparams(("arbitrary", "arbitrary")),
        name="adaln_mod",
    )(c, ada_w, ada_b.reshape(depth, 1, n))


def _rope_kernel(pos_ref, freq_ref, cos_ref, s1_ref, s2_ref):
    ang = pos_ref[...].astype(F32) * freq_ref[...]
    lane = lax.broadcasted_iota(jnp.int32, ang.shape, 1) % B_HEAD_DIM
    sn = jnp.sin(ang)
    cos_ref[...] = jnp.cos(ang)
    s1_ref[...] = jnp.where(lane < ROT_DIM // 2, -sn, 0.0)
    s2_ref[...] = jnp.where((lane >= ROT_DIM // 2) & (lane < ROT_DIM), sn, 0.0)


def _rope_tables(positions):
    n = positions.size
    half = ROT_DIM // 2
    lane = jnp.arange(LANES) % B_HEAD_DIM
    freqs = ROPE_THETA ** (-(lane % half).astype(F32) * 2.0 / ROT_DIM)
    freq_row = jnp.where(lane < ROT_DIM, freqs, 0.0).reshape(1, LANES).astype(F32)
    tm = min(2048, n)
    out = jax.ShapeDtypeStruct((n, LANES), F32)
    spec = pl.BlockSpec((tm, LANES), lambda i: (i, 0))
    return pl.pallas_call(
        _rope_kernel,
        grid=(n // tm,),
        in_specs=[pl.BlockSpec((tm, 1), lambda i: (i, 0)),
                  pl.BlockSpec((1, LANES), lambda i: (0, 0))],
        out_specs=[spec, spec, spec],
        out_shape=[out, out, out],
        compiler_params=_cparams(("arbitrary",)),
        name="rope_tables",
    )(positions.reshape(n, 1), freq_row)


def _inproj_kernel(x_ref, sh_ref, sc_ref, w_ref, cos_ref, s1_ref, s2_ref, lng_ref, lnb_ref,
                   ws_ref, bs_ref, a_ref, s_ref, qn_ref, kn_ref, vn_ref, *rest):
    copy_dils = [dil for dil in DILATIONS if _dilation_layout(dil) == "copy"]
    n_copy = len(copy_dils)
    q_outs, k_outs, v_outs = rest[:n_copy], rest[n_copy:2 * n_copy], rest[2 * n_copy:3 * n_copy]
    qs_ref, ks_ref, vs_ref, qt_ref, kt_ref, vt_ref = rest[3 * n_copy:]
    tm = x_ref.shape[0]
    h = _ln(x_ref[...]) * (1.0 + sc_ref[0]) + sh_ref[0]
    proj = jnp.dot(h.astype(BF16), w_ref[...], preferred_element_type=F32)

    uv = _gelu_tanh(proj[:, :2 * A_WIDTH])
    u = uv[:, :A_WIDTH]
    v = (_ln(uv[:, A_WIDTH:]) * lng_ref[...] + lnb_ref[...]).astype(BF16)
    row = lax.broadcasted_iota(jnp.int32, (CHUNK, CHUNK), 0)
    col = lax.broadcasted_iota(jnp.int32, (CHUNK, CHUNK), 1)
    head_of_lane = lax.broadcasted_iota(jnp.int32, (CHUNK, A_WIDTH), 1) // A_HEAD_DIM
    w_heads = [jnp.where(col <= row, ws_ref[hd], 0.0).astype(BF16) for hd in range(A_HEADS)]
    for cidx in range(tm // CHUNK):
        rows = slice(cidx * CHUNK, (cidx + 1) * CHUNK)
        vc = v[rows]
        sv = bs_ref[...]
        for hd in range(A_HEADS):
            full = jnp.dot(w_heads[hd], vc, preferred_element_type=F32)
            sv = sv + jnp.where(head_of_lane == hd, full, 0.0)
        a_ref[rows, :] = (u[rows] * sv).astype(a_ref.dtype)

    cos = cos_ref[...]
    s1 = s1_ref[...]
    s2 = s2_ref[...]
    q0 = 2 * A_WIDTH
    k0 = q0 + B_WIDTH
    v0 = k0 + B_WIDTH
    for j in range(B_WIDTH // LANES):
        lanes = slice(j * LANES, (j + 1) * LANES)
        for base, ref, nat, scale in ((q0, qs_ref, qn_ref, Q_SCALE), (k0, ks_ref, kn_ref, 1.0)):
            xs = proj[:, base + j * LANES: base + (j + 1) * LANES]
            rot = (xs * cos + pltpu.roll(xs, LANES - ROT_DIM // 2, 1) * s1
                   + pltpu.roll(xs, ROT_DIM // 2, 1) * s2) * scale
            nat[:, lanes] = rot.astype(nat.dtype)
            ref[j] = rot
        vs = proj[:, v0 + j * LANES:v0 + (j + 1) * LANES]
        vn_ref[:, lanes] = vs.astype(vn_ref.dtype)
        vs_ref[j] = vs

    for src, stage, outs in ((qs_ref, qt_ref, q_outs), (ks_ref, kt_ref, k_outs), (vs_ref, vt_ref, v_outs)):
        bufs = (src, stage)
        prev = 1
        for idx, (dil, out) in enumerate(zip(copy_dils, outs)):
            chained = prev > 1 and dil % prev == 0
            base_dil = prev if chained else 1
            source = bufs[idx % 2] if chained or idx == 0 else None
            assert source is not None, "copy dilations must form a divisibility chain"
            step = dil // base_dil
            per = tm // dil
            keep = idx + 1 < len(copy_dils)
            for r_prev in range(base_dil):
                for q in range(step):
                    r = r_prev + base_dil * q
                    start = r_prev * (tm // base_dil) + q
                    for j in range(B_WIDTH // LANES):
                        val = source[j, pl.ds(start, per, stride=step), :]
                        out[0, r, :, j * LANES:(j + 1) * LANES] = val.astype(out.dtype)
                        if keep:
                            bufs[(idx + 1) % 2][j, r * per:(r + 1) * per, :] = val
            prev = dil

    s_ref[...] = proj[:, v0 + B_WIDTH:].astype(s_ref.dtype)


def _inproj(x2d, mod3, w_in_bf, cos_t, s1_t, s2_t, ln_g, ln_b, ws, bs_full, bsz, t):
    n, d = x2d.shape
    tm = min(ROW_TILE, t)
    nt = t // tm
    pw = w_in_bf.shape[1]
    c_width = pw - 2 * A_WIDTH - 3 * B_WIDTH
    tok = lambda b, i: (b * nt + i, 0)
    const2 = lambda b, i: (0, 0)
    outs = [jax.ShapeDtypeStruct((n, A_WIDTH), BF16),
            jax.ShapeDtypeStruct((t, bsz * c_width), BF16)]
    out_specs = [pl.BlockSpec((tm, A_WIDTH), tok),
                 pl.BlockSpec((tm, c_width), lambda b, i: (i, b))]
    for _ in range(3):
        outs.append(jax.ShapeDtypeStruct((n, B_WIDTH), BF16))
        out_specs.append(pl.BlockSpec((tm, B_WIDTH), tok))
    copy_dils = [dil for dil in DILATIONS if _dilation_layout(dil) == "copy"]
    for _ in range(3):
        for dil in copy_dils:
            outs.append(jax.ShapeDtypeStruct((bsz, dil, t // dil, B_WIDTH), BF16))
            out_specs.append(pl.BlockSpec((1, dil, tm // dil, B_WIDTH), lambda b, i: (b, 0, i, 0)))
    res = pl.pallas_call(
        _inproj_kernel,
        grid=(bsz, nt),
        in_specs=[pl.BlockSpec((tm, d), tok),
                  pl.BlockSpec((1, 1, d), lambda b, i: (b, 0, 0)),
                  pl.BlockSpec((1, 1, d), lambda b, i: (b, 0, 1)),
                  pl.BlockSpec((d, pw), const2),
                  pl.BlockSpec((tm, LANES), tok),
                  pl.BlockSpec((tm, LANES), tok),
                  pl.BlockSpec((tm, LANES), tok),
                  pl.BlockSpec((1, A_WIDTH), const2),
                  pl.BlockSpec((1, A_WIDTH), const2),
                  pl.BlockSpec((A_HEADS, CHUNK, CHUNK), lambda b, i: (0, 0, 0)),
                  pl.BlockSpec((CHUNK, A_WIDTH), const2)],
        out_specs=out_specs,
        out_shape=outs,
        scratch_shapes=[pltpu.VMEM((B_WIDTH // LANES, tm, LANES), F32)] * 6,
        compiler_params=_cparams(("arbitrary", "arbitrary")),
        name="inproj_gmlp_rope",
    )(x2d, mod3, mod3, w_in_bf, cos_t, s1_t, s2_t, ln_g, ln_b, ws, bs_full)
    a_out, s_in = res[0], res[1]
    natural = res[2:5]
    n_copy = len(copy_dils)
    copies = {dil: tuple(res[5 + z * n_copy + g] for z in range(3)) for g, dil in enumerate(copy_dils)}
    qkv = [copies[dil] if dil in copies else natural for dil in DILATIONS]
    return a_out, s_in, qkv


def _att_kernel(q_ref, k_ref, v_ref, o_ref, st_ref, kcat_ref, vaug_ref):
    i = pl.program_id(2)
    blk = ATT_BLK
    n_res = q_ref.shape[1]
    n_sub = q_ref.shape[2] // blk
    n_pairs = B_WIDTH // LANES
    last = slice(n_sub * blk, (n_sub + 1) * blk)

    @pl.when(i == 0)
    def _():
        kcat_ref[:, 0:blk, :] = jnp.zeros((n_res, blk, B_WIDTH), BF16)
        vaug_ref[:, :, 0:blk, 0:LANES] = jnp.zeros((n_res, n_pairs, blk, LANES), BF16)
        vaug_ref[:, :, :, LANES:] = jnp.ones((n_res, n_pairs, (n_sub + 1) * blk, LANES), BF16)

    @pl.when(i > 0)
    def _():
        kcat_ref[:, 0:blk, :] = kcat_ref[:, last, :]
        vaug_ref[:, :, 0:blk, 0:LANES] = vaug_ref[:, :, last, 0:LANES]

    for rr in range(n_res):
        kcat_ref[rr, blk:, :] = k_ref[0, rr]
        for pair in range(n_pairs):
            vaug_ref[rr, pair, blk:, 0:LANES] = v_ref[0, rr, :, pair * LANES:(pair + 1) * LANES]

    qi = lax.broadcasted_iota(jnp.int32, (blk, 2 * blk), 0)
    kk = lax.broadcasted_iota(jnp.int32, (blk, 2 * blk), 1)
    band = (kk >= qi) & (kk <= qi + blk)
    first = band & ((i > 0) | (kk >= blk))
    lane = lax.broadcasted_iota(jnp.int32, (blk, LANES), 1)
    low_half = lane < B_HEAD_DIM
    nt = (((1,), (1,)), ((), ()))
    for rr, sub in [(rr, sub) for rr in range(n_res) for sub in range(n_sub)]:
        rows = slice(sub * blk, (sub + 1) * blk)
        keys = slice(sub * blk, (sub + 2) * blk)
        mask = first if sub == 0 else band
        stats = jnp.zeros((blk, LANES), F32)
        for pair in range(n_pairs):
            lanes = slice(pair * LANES, (pair + 1) * LANES)
            q2 = q_ref[0, rr, rows, lanes]
            kslab = kcat_ref[rr, keys, lanes]
            res = []
            for hh in range(2):
                hd = 2 * pair + hh
                qm = jnp.where(low_half if hh == 0 else jnp.logical_not(low_half), q2, jnp.zeros_like(q2))
                s = lax.dot_general(qm, kslab, nt, preferred_element_type=F32)
                s = jnp.where(mask, s, NEG_INF)
                m = jnp.max(jnp.maximum(s[:, :blk], s[:, blk:]), axis=1, keepdims=True)
                p = jnp.exp2(s - m).astype(BF16)
                ov = jnp.dot(p, vaug_ref[rr, pair, keys, :], preferred_element_type=F32)
                den = ov[:, LANES:]
                res.append(ov[:, :LANES] / den)
                stats = jnp.where(lane == hd, m, stats)
                stats = jnp.where(lane == B_HEADS + hd, den, stats)
            o_ref[0, rr, rows, lanes] = jnp.where(low_half, res[0], res[1]).astype(o_ref.dtype)
        st_ref[0, rr, rows, :] = stats


def _dilation_layout(dil):
    return "natural" if dil == 1 else "copy"


def _att_branch(q, k, v, dil, bsz, t):
    rows = t // dil
    n_sub = min(ATT_SUB_BLOCKS, rows // ATT_BLK)
    step_rows = n_sub * ATT_BLK
    n_steps = rows // step_rows
    n_res = math.gcd(dil, max(1, ATT_SUB_BLOCKS // n_sub)) if n_steps == 1 else 1
    shape = lambda w: (bsz, dil, rows, w)
    blk = lambda w: pl.BlockSpec((1, n_res, step_rows, w), lambda b, r, i: (b, r, i, 0))
    q, k, v = (z.reshape(shape(B_WIDTH)) for z in (q, k, v))
    return pl.pallas_call(
        _att_kernel,
        grid=(bsz, dil // n_res, n_steps),
        in_specs=[blk(B_WIDTH), blk(B_WIDTH), blk(B_WIDTH)],
        out_specs=[blk(B_WIDTH), blk(LANES)],
        out_shape=[jax.ShapeDtypeStruct(shape(B_WIDTH), BF16),
                   jax.ShapeDtypeStruct(shape(LANES), F32)],
        scratch_shapes=[pltpu.VMEM((n_res, step_rows + ATT_BLK, B_WIDTH), BF16),
                        pltpu.VMEM((n_res, B_WIDTH // LANES, step_rows + ATT_BLK, 2 * LANES), BF16)],
        compiler_params=_cparams(("arbitrary", "arbitrary", "arbitrary")),
        name=f"dilated_attn_d{dil}",
    )(q, k, v)


def _ssm_kernel(u_ref, bmat_ref, cmat_ref, are_ref, aim_ref, dskip_ref, gw_ref, gb_ref,
                o_ref, state_ref, xbuf_ref, rows_ref):
    bsz, two_n = state_ref.shape
    n_state = two_n // 2
    steps = u_ref.shape[0]
    cw = u_ref.shape[1] // bsz
    n_slab = cw // LANES

    @pl.when(pl.program_id(0) == 0)
    def _():
        state_ref[...] = jnp.zeros_like(state_ref)

    for b in range(bsz):
        for j in range(n_slab):
            rows_ref[j, pl.ds(b, steps, stride=bsz), :] = (
                u_ref[:, b * cw + j * LANES:b * cw + (j + 1) * LANES].astype(F32))
    u32 = jnp.concatenate([rows_ref[j] for j in range(n_slab)], axis=1)
    xbuf_ref[...] = jnp.dot(u32.astype(BF16), bmat_ref[...], preferred_element_type=F32)
    a_re = jnp.broadcast_to(are_ref[...], (bsz, n_state))
    a_im = jnp.broadcast_to(aim_ref[...], (bsz, n_state))

    def step(tt, carry):
        x_re, x_im = carry
        r0 = pl.multiple_of(tt * bsz, bsz)
        b_re = xbuf_ref[pl.ds(r0, bsz), :n_state]
        b_im = xbuf_ref[pl.ds(r0, bsz), n_state:]
        n_re = a_re * x_re - a_im * x_im + b_re
        n_im = a_re * x_im + a_im * x_re + b_im
        xbuf_ref[pl.ds(r0, bsz), :n_state] = n_re
        xbuf_ref[pl.ds(r0, bsz), n_state:] = n_im
        return n_re, n_im

    x_re, x_im = lax.fori_loop(0, steps, step, (state_ref[:, :n_state], state_ref[:, n_state:]))
    state_ref[:, :n_state] = x_re
    state_ref[:, n_state:] = x_im

    y = jnp.dot(xbuf_ref[...].astype(BF16), cmat_ref[...], preferred_element_type=F32)
    y = _gelu_tanh(y + dskip_ref[...] * u32)
    gate = jax.nn.sigmoid(jnp.dot(y.astype(BF16), gw_ref[...], preferred_element_type=F32) + gb_ref[...])
    out = y * gate
    for j in range(n_slab):
        rows_ref[j] = out[:, j * LANES:(j + 1) * LANES]
    for b in range(bsz):
        for j in range(n_slab):
            o_ref[:, b * cw + j * LANES:b * cw + (j + 1) * LANES] = (
                rows_ref[j, pl.ds(b, steps, stride=bsz), :].astype(o_ref.dtype))


def _ssm_weights(lam_re, lam_im, log_dt, b_re, b_im, c_re, c_im):
    g, n_st = lam_re.shape
    cg = b_re.shape[-1]
    dt = jnp.exp(log_dt)[:, None]
    mag = jnp.exp(lam_re * dt)
    ab_re = mag * jnp.cos(lam_im * dt)
    ab_im = mag * jnp.sin(lam_im * dt)
    nr = ab_re - 1.0
    ni = ab_im
    mod2 = lam_re * lam_re + lam_im * lam_im
    f_re = (nr * lam_re + ni * lam_im) / mod2
    f_im = (ni * lam_re - nr * lam_im) / mod2
    bb_re = f_re[..., None] * b_re - f_im[..., None] * b_im
    bb_im = f_re[..., None] * b_im + f_im[..., None] * b_re
    eye = jnp.eye(g, dtype=F32)
    bm_re = jnp.einsum('gnc,gh->gchn', bb_re, eye).reshape(g * cg, g * n_st)
    bm_im = jnp.einsum('gnc,gh->gchn', bb_im, eye).reshape(g * cg, g * n_st)
    bmat = jnp.concatenate([bm_re, bm_im], axis=1)
    cm_re = jnp.einsum('gcn,gh->gnhc', c_re, eye).reshape(g * n_st, g * cg)
    cm_im = jnp.einsum('gcn,gh->gnhc', c_im, eye).reshape(g * n_st, g * cg)
    cmat = jnp.concatenate([cm_re, -cm_im], axis=0)
    return (bmat.astype(BF16), cmat.astype(BF16),
            ab_re.reshape(1, g * n_st), ab_im.reshape(1, g * n_st))


def _ssm(u_tm, bmat, cmat, a_re, a_im, d_skip, glu_w_bf, glu_b, bsz, t):
    cw = bmat.shape[0]
    two_n = bmat.shape[1]
    steps = min(SSM_STEPS, t)
    rows = steps * bsz
    const = lambda i: (0, 0)
    return pl.pallas_call(
        _ssm_kernel,
        grid=(t // steps,),
        in_specs=[pl.BlockSpec((steps, bsz * cw), lambda i: (i, 0)),
                  pl.BlockSpec((cw, two_n), const),
                  pl.BlockSpec((two_n, cw), const),
                  pl.BlockSpec((1, two_n // 2), const),
                  pl.BlockSpec((1, two_n // 2), const),
                  pl.BlockSpec((1, cw), const),
                  pl.BlockSpec((cw, cw), const),
                  pl.BlockSpec((1, cw), const)],
        out_specs=pl.BlockSpec((steps, bsz * cw), lambda i: (i, 0)),
        out_shape=jax.ShapeDtypeStruct((t, bsz * cw), BF16),
        scratch_shapes=[pltpu.VMEM((bsz, two_n), F32), pltpu.VMEM((rows, two_n), F32),
                        pltpu.VMEM((cw // LANES, rows, LANES), F32)],
        compiler_params=_cparams(("arbitrary",)),
        name="s5_scan_glu",
    )(u_tm, bmat, cmat, a_re, a_im, d_skip, glu_w_bf, glu_b)


def _route(sel, scores):
    gs = []
    for g in range(N_EXPERT_GROUPS):
        a, b, c, d = sel[4 * g: 4 * g + 4]
        hi1, lo1 = jnp.maximum(a, b), jnp.minimum(a, b)
        hi2, lo2 = jnp.maximum(c, d), jnp.minimum(c, d)
        gs.append(jnp.maximum(hi1, hi2) + jnp.maximum(jnp.minimum(hi1, hi2), jnp.maximum(lo1, lo2)))
    g_idx = jnp.zeros(gs[0].shape, jnp.int32)
    best = gs[0]
    for g in range(1, N_EXPERT_GROUPS):
        better = gs[g] > best
        g_idx = jnp.where(better, g, g_idx)
        best = jnp.where(better, gs[g], best)

    def pick(rows, j):
        out = rows[j]
        for g in range(1, N_EXPERT_GROUPS):
            out = jnp.where(g_idx == g, rows[4 * g + j], out)
        return out

    v = [pick(sel, j) for j in range(EXPERTS_PER_GROUP)]
    s = [pick(scores, j) for j in range(EXPERTS_PER_GROUP)]
    i1 = jnp.zeros(g_idx.shape, jnp.int32)
    b1, g1 = v[0], s[0]
    for j in range(1, EXPERTS_PER_GROUP):
        better = v[j] > b1
        i1 = jnp.where(better, j, i1)
        b1 = jnp.where(better, v[j], b1)
        g1 = jnp.where(better, s[j], g1)
    i2 = jnp.full(g_idx.shape, -1, jnp.int32)
    b2 = jnp.zeros_like(b1)
    g2 = jnp.zeros_like(g1)
    for j in range(EXPERTS_PER_GROUP):
        better = (i1 != j) & ((i2 < 0) | (v[j] > b2))
        i2 = jnp.where(better, j, i2)
        b2 = jnp.where(better, v[j], b2)
        g2 = jnp.where(better, s[j], g2)
    tot = g1 + g2
    base = g_idx * EXPERTS_PER_GROUP
    return base + i1, base + i2, g1 / tot, g2 / tot


def _natural_order(ref, scr_ref, dil):
    if _dilation_layout(dil) != "copy":
        return ref[...].astype(F32)
    per = ref.shape[2]
    n_slab = scr_ref.shape[0]
    for r in range(dil):
        for j in range(n_slab):
            scr_ref[j, pl.ds(r, per, stride=dil), :] = ref[0, r, :, j * LANES:(j + 1) * LANES].astype(F32)
    if n_slab == 1:
        return scr_ref[0]
    return jnp.concatenate([scr_ref[j] for j in range(n_slab)], axis=1)


def _outproj_kernel(a_ref, o1_ref, o2_ref, o3_ref, st1_ref, st2_ref, st3_ref,
                    c_ref, x_ref, g1_ref, sh2_ref, sc2_ref, wout_ref, expand_ref, lng_ref, lnb_ref,
                    rwh_ref, rwl_ref, rb_ref, x1_ref, h2_ref, eidx_ref, gate_ref, count_ref,
                    oscr2_ref, oscr3_ref, sscr2_ref, sscr3_ref, *, alpha):
    o_refs = (o1_ref, o2_ref, o3_ref)
    st_refs = (st1_ref, st2_ref, st3_ref)
    o_scr = (None, oscr2_ref, oscr3_ref)
    st_scr = (None, sscr2_ref, sscr3_ref)
    ms = [_natural_order(st_refs[g], st_scr[g], DILATIONS[g]) for g in range(len(DILATIONS))]
    dens = [pltpu.roll(m, LANES - B_HEADS, 1) for m in ms]
    mx = jnp.maximum(jnp.maximum(ms[0], ms[1]), ms[2])
    ws = [dens[g] * jnp.exp2(ms[g] - mx) for g in range(len(DILATIONS))]
    lane = lax.broadcasted_iota(jnp.int32, mx.shape, 1)
    tot = jnp.where(lane < B_HEADS, ws[0] + ws[1] + ws[2], 1.0)
    expand = expand_ref[...]

    def widen(w):
        wn = jnp.where(lane < B_HEADS, w / tot, 0.0)
        hi = wn.astype(BF16)
        lo = (wn - hi.astype(F32)).astype(BF16)
        return (jnp.dot(hi, expand, preferred_element_type=F32)
                + jnp.dot(lo, expand, preferred_element_type=F32))

    b_out = widen(ws[0]) * _natural_order(o_refs[0], o_scr[0], DILATIONS[0])
    for g in range(1, len(DILATIONS)):
        b_out = b_out + widen(ws[g]) * _natural_order(o_refs[g], o_scr[g], DILATIONS[g])
    b_out = b_out.astype(BF16)

    wout = wout_ref
    mix = (jnp.dot(a_ref[...], wout[:A_WIDTH, :], preferred_element_type=F32)
           + jnp.dot(b_out, wout[A_WIDTH:A_WIDTH + B_WIDTH, :], preferred_element_type=F32)
           + jnp.dot(c_ref[...], wout[A_WIDTH + B_WIDTH:, :], preferred_element_type=F32))
    x1 = _ln(alpha * x_ref[...] + (1.0 + g1_ref[0]) * mix) * lng_ref[...] + lnb_ref[...]
    x1_ref[...] = x1
    h2 = _ln(x1) * (1.0 + sc2_ref[0]) + sh2_ref[0]
    _store_tile_rows(h2_ref, h2)

    hi = h2.astype(BF16)
    lo = (h2 - hi.astype(F32)).astype(BF16)
    nt = (((1,), (1,)), ((), ()))
    logits = (lax.dot_general(rwh_ref[...], hi, nt, preferred_element_type=F32)
              + lax.dot_general(rwh_ref[...], lo, nt, preferred_element_type=F32)
              + lax.dot_general(rwl_ref[...], hi, nt, preferred_element_type=F32))
    scores = jax.nn.sigmoid(logits)
    sel = scores + rb_ref[...]
    sel_rows = [sel[e:e + 1, :] for e in range(N_EXPERTS)]
    score_rows = [scores[e:e + 1, :] for e in range(N_EXPERTS)]
    e1, e2, gt1, gt2 = _route(sel_rows, score_rows)

    @pl.when((pl.program_id(0) == 0) & (pl.program_id(1) == 0))
    def _():
        count_ref[...] = jnp.zeros_like(count_ref)

    tm = h2.shape[0]
    erow = lax.broadcasted_iota(jnp.int32, (N_EXPERTS, tm), 0)
    oh1 = erow == e1
    oh2 = erow == e2
    both = (oh1 | oh2).astype(BF16)
    earlier = (lax.broadcasted_iota(jnp.int32, (tm, tm), 0)
               < lax.broadcasted_iota(jnp.int32, (tm, tm), 1)).astype(BF16)
    base = count_ref[:, 0:1]
    cnt = jnp.dot(both, earlier, preferred_element_type=F32) + base
    r1 = jnp.sum(jnp.where(oh1, cnt, 0.0), axis=0, keepdims=True)
    r2 = jnp.sum(jnp.where(oh2, cnt, 0.0), axis=0, keepdims=True)
    total = base + jnp.sum(both.astype(F32), axis=1, keepdims=True)
    count_ref[...] = jnp.broadcast_to(total, count_ref.shape)

    eidx_ref[...] = jnp.zeros_like(eidx_ref)
    gate_ref[...] = jnp.zeros_like(gate_ref)
    eidx_ref[0:1, :] = e1
    eidx_ref[1:2, :] = e2
    eidx_ref[2:3, :] = r1.astype(jnp.int32)
    eidx_ref[3:4, :] = r2.astype(jnp.int32)
    gate_ref[0:1, :] = gt1
    gate_ref[1:2, :] = gt2


def _outproj(a_out, att, c_tm, x2d, mod3, w_out_bf, ln_g, ln_b, rw_hi, rw_lo, rbias, bsz, t, alpha):
    n, d = x2d.shape
    tm = min(ROW_TILE, t)
    nt = t // tm
    cw = c_tm.shape[1] // bsz
    tok = lambda b, i: (b * nt + i, 0)
    const2 = lambda b, i: (0, 0)
    head = jnp.arange(B_WIDTH) // B_HEAD_DIM
    expand = (jnp.arange(LANES)[:, None] == head[None, :]).astype(BF16)

    def branch_input(z, dil, w):
        if _dilation_layout(dil) == "copy":
            return z, pl.BlockSpec((1, dil, tm // dil, w), lambda b, i: (b, 0, i, 0))
        return z.reshape(n, w), pl.BlockSpec((tm, w), tok)

    o_in = [branch_input(o, dil, B_WIDTH) for (o, _), dil in zip(att, DILATIONS)]
    st_in = [branch_input(st, dil, LANES) for (_, st), dil in zip(att, DILATIONS)]
    (o1, st1), (o2, st2), (o3, st3) = [(o_in[g][0], st_in[g][0]) for g in range(len(DILATIONS))]
    modspec = lambda j: pl.BlockSpec((1, 1, d), lambda b, i: (b, 0, j))
    return pl.pallas_call(
        functools.partial(_outproj_kernel, alpha=alpha),
        grid=(bsz, nt),
        in_specs=[pl.BlockSpec((tm, A_WIDTH), tok)]
                 + [spec for _, spec in o_in]
                 + [spec for _, spec in st_in]
                 + [pl.BlockSpec((tm, cw), lambda b, i: (i, b)),
                  pl.BlockSpec((tm, d), tok),
                  modspec(2), modspec(3), modspec(4),
                  pl.BlockSpec((d, d), const2),
                  pl.BlockSpec((LANES, B_WIDTH), const2),
                  pl.BlockSpec((1, d), const2), pl.BlockSpec((1, d), const2),
                  pl.BlockSpec((N_EXPERTS, d), const2), pl.BlockSpec((N_EXPERTS, d), const2),
                  pl.BlockSpec((N_EXPERTS, 1), const2)],
        out_specs=[pl.BlockSpec((tm, d), tok), pl.BlockSpec((tm * (d // LANES), LANES), tok),
                   pl.BlockSpec((SUBLANES, tm), lambda b, i: (0, b * nt + i)),
                   pl.BlockSpec((SUBLANES, tm), lambda b, i: (0, b * nt + i)),
                   pl.BlockSpec((N_EXPERTS, LANES), const2)],
        out_shape=[jax.ShapeDtypeStruct((n, d), F32),
                   jax.ShapeDtypeStruct((n * (d // LANES), LANES), F32),

                   jax.ShapeDtypeStruct((SUBLANES, n), jnp.int32),
                   jax.ShapeDtypeStruct((SUBLANES, n), F32),
                   jax.ShapeDtypeStruct((N_EXPERTS, LANES), F32)],
        scratch_shapes=[pltpu.VMEM((B_WIDTH // LANES, tm, LANES), F32),
                        pltpu.VMEM((B_WIDTH // LANES, tm, LANES), F32),
                        pltpu.VMEM((1, tm, LANES), F32), pltpu.VMEM((1, tm, LANES), F32)],
        compiler_params=_cparams(("arbitrary", "arbitrary")),
        name="outproj_ln_router",
    )(a_out, o1, o2, o3, st1, st2, st3, c_tm, x2d, mod3, mod3, mod3, w_out_bf, expand,
      ln_g, ln_b, rw_hi, rw_lo, rbias)


def _dispatch_plan(eidx, counts, n_tok, blk):
    e = eidx[:TOP_K]
    rank = eidx[TOP_K:2 * TOP_K]
    pcounts = (counts + blk - 1) // blk * blk
    pends = jnp.cumsum(pcounts)
    pstarts = pends - pcounts
    seg_start = jnp.zeros_like(e)
    for j in range(N_EXPERTS):
        seg_start = jnp.where(e == j, pstarts[j], seg_start)
    dest = (seg_start + rank).astype(jnp.int32)
    n_blk = n_tok * TOP_K // blk + N_EXPERTS
    blk_start = jnp.arange(n_blk, dtype=jnp.int32) * blk
    blk_e = jnp.minimum(jnp.sum((pends[None, :] <= blk_start[:, None]).astype(jnp.int32), axis=1),
                        N_EXPERTS - 1).astype(jnp.int32)
    n_used = (pends[-1] // blk).astype(jnp.int32).reshape(1)
    return dest, blk_e, n_used, pends.astype(jnp.int32), pcounts.astype(jnp.int32)


def _dispatch_kernel(pend_ref, pcnt_ref, dest_ref, h2_ref, xs_hbm, zbuf, sem, zsem, *, blk, n_slab):
    i = pl.program_id(0)
    rows = h2_ref.shape[0] // n_slab
    blk_rows = blk * n_slab

    @pl.when(i == 0)
    def _():
        zbuf[...] = jnp.zeros_like(zbuf)
        for e in range(N_EXPERTS):
            @pl.when(pcnt_ref[e] > 0)
            def _():
                start = pl.multiple_of((pend_ref[e] - blk) * n_slab, blk_rows)
                cp = pltpu.make_async_copy(zbuf, xs_hbm.at[pl.ds(start, blk_rows), :], zsem)
                cp.start()
                cp.wait()
        n_blk = xs_hbm.shape[0] // blk_rows
        for t in range(N_EXPERTS):
            tail = pend_ref[N_EXPERTS - 1] // blk + t

            @pl.when(tail < n_blk)
            def _():
                start = pl.multiple_of(tail * blk_rows, blk_rows)
                cp = pltpu.make_async_copy(zbuf, xs_hbm.at[pl.ds(start, blk_rows), :], zsem)
                cp.start()
                cp.wait()

    def issue(s, c):
        src = h2_ref.at[pl.ds(pl.multiple_of(s * n_slab, n_slab), n_slab), :]
        for kk in range(TOP_K):
            slot = pl.multiple_of(dest_ref[0, 0, kk * rows + s] * n_slab, n_slab)
            pltpu.make_async_copy(src, xs_hbm.at[pl.ds(slot, n_slab), :], sem).start()
        return c
    lax.fori_loop(0, rows, issue, 0, unroll=8)
    for kk in range(TOP_K):
        pltpu.make_async_copy(h2_ref, xs_hbm.at[pl.ds(0, rows * n_slab), :], sem).wait()


def _expert_kernel(blk_e_ref, nused_ref, xs_ref, wg_ref, wu_ref, wd_ref, ys_ref, wg_bf, wu_bf, wd_bf,
                   *, blk, n_slab):
    j = pl.program_id(0)

    @pl.when((j == 0) | (blk_e_ref[j] != blk_e_ref[jnp.maximum(j - 1, 0)]))
    def _():
        wg_bf[...] = wg_ref[0, 0].astype(BF16)
        wu_bf[...] = wu_ref[0, 0].astype(BF16)
        wd_bf[...] = wd_ref[0, 0].astype(BF16)

    @pl.when(j < nused_ref[0])
    def _():
        rows = _load_tile_rows(xs_ref, 0, blk, n_slab).astype(BF16)
        gt = jnp.dot(rows, wg_bf[...], preferred_element_type=F32)
        up = jnp.dot(rows, wu_bf[...], preferred_element_type=F32)
        hid = (gt * jax.nn.sigmoid(gt) * up).astype(BF16)
        _store_tile_rows(ys_ref, jnp.dot(hid, wd_bf[...], preferred_element_type=F32))

    @pl.when(j >= nused_ref[0])
    def _():
        ys_ref[...] = jnp.zeros_like(ys_ref)


def _moe(h2, eidx, counts, w_gate, w_up, w_down, layer, rows):
    d, de = w_gate.shape[-2:]
    n_slab = d // LANES
    n_tok = h2.shape[0] // n_slab
    blk = min(MOE_BLK, n_tok)
    dest, blk_e, n_used, pends, pcounts = _dispatch_plan(eidx, counts, n_tok, blk)
    n_blk = blk_e.shape[0]
    cap = n_blk * blk

    nt = n_tok // rows
    dest_tiles = dest.reshape(TOP_K, nt, rows).transpose(1, 0, 2).reshape(nt, 1, TOP_K * rows)
    xs = pl.pallas_call(
        functools.partial(_dispatch_kernel, blk=blk, n_slab=n_slab),
        grid_spec=pltpu.PrefetchScalarGridSpec(
            num_scalar_prefetch=2,
            grid=(nt,),
            in_specs=[pl.BlockSpec((1, 1, TOP_K * rows), lambda i, pe, pc: (i, 0, 0),
                                   memory_space=pltpu.SMEM),
                      pl.BlockSpec((rows * n_slab, LANES), lambda i, pe, pc: (i, 0))],
            out_specs=pl.BlockSpec(memory_space=pl.ANY),
            scratch_shapes=[pltpu.VMEM((blk * n_slab, LANES), F32),
                            pltpu.SemaphoreType.DMA, pltpu.SemaphoreType.DMA],
        ),
        out_shape=jax.ShapeDtypeStruct((cap * n_slab, LANES), F32),
        compiler_params=_cparams(("arbitrary",)),
        name="moe_dispatch",
    )(pends, pcounts, dest_tiles, h2)

    used = lambda j, be, nu: (jnp.minimum(j, nu[0] - 1), 0)
    ys = pl.pallas_call(
        functools.partial(_expert_kernel, blk=blk, n_slab=n_slab),
        grid_spec=pltpu.PrefetchScalarGridSpec(
            num_scalar_prefetch=2,
            grid=(n_blk,),
            in_specs=[pl.BlockSpec((blk * n_slab, LANES), used),
                      pl.BlockSpec((1, 1, d, de), lambda j, be, nu: (layer, be[j], 0, 0)),
                      pl.BlockSpec((1, 1, d, de), lambda j, be, nu: (layer, be[j], 0, 0)),
                      pl.BlockSpec((1, 1, de, d), lambda j, be, nu: (layer, be[j], 0, 0))],
            out_specs=pl.BlockSpec((blk * n_slab, LANES), lambda j, be, nu: (j, 0)),
            scratch_shapes=[pltpu.VMEM((d, de), BF16), pltpu.VMEM((d, de), BF16), pltpu.VMEM((de, d), BF16)],
        ),
        out_shape=jax.ShapeDtypeStruct((cap * n_slab, LANES), F32),
        compiler_params=_cparams(("arbitrary",)),
        name="moe_experts",
    )(blk_e, n_used, xs, w_gate, w_up, w_down)
    return ys, dest


def _ffn_ln_kernel(dest_ref, next_ref, x_ref, gate_ref, g2_ref, lng_ref, lnb_ref, ys_hbm, o_ref, ybuf, sem,
                   *, alpha):
    tm, d = x_ref.shape
    n_slab = d // LANES
    g = pl.program_id(0) * pl.num_programs(1) + pl.program_id(1)
    n_tiles = pl.num_programs(0) * pl.num_programs(1)
    slot = g % 2

    def gather(idx_ref, buf_slot):
        def issue(s, c):
            src = pl.multiple_of(idx_ref[0, 0, s] * n_slab, n_slab)
            dst = pl.multiple_of(s * n_slab, n_slab)
            pltpu.make_async_copy(ys_hbm.at[pl.ds(src, n_slab), :],
                                  ybuf.at[buf_slot, pl.ds(dst, n_slab), :], sem.at[buf_slot]).start()
            return c
        lax.fori_loop(0, TOP_K * tm, issue, 0, unroll=8)

    @pl.when(g == 0)
    def _():
        gather(dest_ref, 0)

    @pl.when(g + 1 < n_tiles)
    def _():
        gather(next_ref, 1 - slot)

    pltpu.make_async_copy(ys_hbm.at[pl.ds(0, TOP_K * tm * n_slab), :], ybuf.at[slot], sem.at[slot]).wait()

    gate = gate_ref[...]
    buf = ybuf.at[slot]
    ffn = _load_tile_rows(buf, 0, tm, n_slab) * gate[:, 0:1]
    for kk in range(1, TOP_K):
        ffn = ffn + _load_tile_rows(buf, kk * tm, tm, n_slab) * gate[:, kk:kk + 1]
    o_ref[...] = _ln(alpha * x_ref[...] + (1.0 + g2_ref[0]) * ffn) * lng_ref[...] + lnb_ref[...]


def _ffn_ln(x1, ys, dest, gates, mod3, ln_g, ln_b, bsz, t, alpha):
    n, d = x1.shape
    n_slab = d // LANES
    tm = min(ROW_TILE, t)
    nt = t // tm
    n_tiles = bsz * nt
    dest_tiles = dest.reshape(TOP_K, n_tiles, tm).transpose(1, 0, 2).reshape(n_tiles, 1, TOP_K * tm)
    tok = lambda b, i, *_: (b * nt + i, 0)
    const2 = lambda b, i, *_: (0, 0)
    gate_cols = gates[:TOP_K].T
    idx_blk = (1, 1, TOP_K * tm)
    return pl.pallas_call(
        functools.partial(_ffn_ln_kernel, alpha=alpha),
        grid=(bsz, nt),
        in_specs=[pl.BlockSpec(idx_blk, lambda b, i: (b * nt + i, 0, 0), memory_space=pltpu.SMEM),
                  pl.BlockSpec(idx_blk, lambda b, i: (jnp.minimum(b * nt + i + 1, n_tiles - 1), 0, 0),
                               memory_space=pltpu.SMEM),
                  pl.BlockSpec((tm, d), tok), pl.BlockSpec((tm, TOP_K), tok),
                  pl.BlockSpec((1, 1, d), lambda b, i: (b, 0, 5)),
                  pl.BlockSpec((1, d), const2), pl.BlockSpec((1, d), const2),
                  pl.BlockSpec(memory_space=pl.ANY)],
        out_specs=pl.BlockSpec((tm, d), tok),
        out_shape=jax.ShapeDtypeStruct((n, d), F32),
        scratch_shapes=[pltpu.VMEM((2, TOP_K * tm * n_slab, LANES), F32), pltpu.SemaphoreType.DMA((2,))],
        compiler_params=_cparams(("arbitrary", "arbitrary")),
        name="ffn_combine_ln",
    )(dest_tiles, dest_tiles, x1, gate_cols, mod3, ln_g, ln_b, ys)


def kernel(x, c, positions, ada_w, ada_b, w_in, gm_ln_g, gm_ln_b, gm_ws, gm_bs, ssm_lam_re, ssm_lam_im, ssm_log_dt, ssm_b_re, ssm_b_im, ssm_c_re, ssm_c_im, ssm_d, glu_w, glu_b, w_out, ln1_g, ln1_b, router_w, router_bias, exp_w_gate, exp_w_up, exp_w_down, ln2_g, ln2_b):
    bsz, t, d = x.shape
    depth = ada_w.shape[0]
    alpha = (2.0 * depth) ** 0.25
    n = bsz * t

    mod = _adaln_mod(c.astype(F32), ada_w, ada_b)
    cos_t, s1_t, s2_t = _rope_tables(positions)
    rw_t = router_w.astype(F32).T
    rw_hi = rw_t.astype(BF16)
    rw_lo = (rw_t - rw_hi.astype(F32)).astype(BF16)
    rbias = router_bias.astype(F32).reshape(N_EXPERTS, 1)

    xf = x.astype(F32).reshape(n, d)
    for l in range(depth):
        mod3 = mod[l].reshape(bsz, 1, 6 * d)
        bs_full = jnp.repeat(gm_bs[l].T, A_HEAD_DIM, axis=1)
        a_out, s_in, qkv = _inproj(
            xf, mod3, w_in[l].astype(BF16), cos_t, s1_t, s2_t,
            gm_ln_g[l].reshape(1, A_WIDTH), gm_ln_b[l].reshape(1, A_WIDTH), gm_ws[l], bs_full, bsz, t)
        att = [_att_branch(*qkv[g], dil, bsz, t) for g, dil in enumerate(DILATIONS)]
        bmat, cmat, a_re, a_im = _ssm_weights(ssm_lam_re[l], ssm_lam_im[l], ssm_log_dt[l],
                                              ssm_b_re[l], ssm_b_im[l], ssm_c_re[l], ssm_c_im[l])
        cw = bmat.shape[0]
        c_out = _ssm(s_in, bmat, cmat, a_re, a_im, ssm_d[l].reshape(1, cw),
                     glu_w[l].astype(BF16), glu_b[l].reshape(1, cw), bsz, t)
        x1, h2, eidx, gates, counts = _outproj(a_out, att, c_out, xf, mod3, w_out[l].astype(BF16),
                                               ln1_g[l].reshape(1, d), ln1_b[l].reshape(1, d),
                                               rw_hi, rw_lo, rbias, bsz, t, alpha)
        ys, dest = _moe(h2, eidx, counts[:, 0].astype(jnp.int32),
                              exp_w_gate.astype(F32), exp_w_up.astype(F32), exp_w_down.astype(F32), l,
                              min(MOE_ROW_TILE, t))
        xf = _ffn_ln(x1, ys, dest, gates, mod3, ln2_g[l].reshape(1, d), ln2_b[l].reshape(1, d),
                     bsz, t, alpha)
    return xf.reshape(bsz, t, d)
```

```python
import functools
import math

import jax
import jax.numpy as jnp
from jax import lax
from jax.experimental import pallas as pl
from jax.experimental.pallas import tpu as pltpu

F32 = jnp.float32
BF16 = jnp.bfloat16

A_HEADS = 4
A_HEAD_DIM = 64
A_WIDTH = A_HEADS * A_HEAD_DIM
CHUNK = 128
B_HEADS = 8
B_HEAD_DIM = 64
B_WIDTH = B_HEADS * B_HEAD_DIM
DILATIONS = (1, 4, 16)
ATT_BLK = 128
ROT_DIM = B_HEAD_DIM // 4
ROPE_THETA = 500000.0
SSM_GROUP = 16
SSM_STATE = 64
N_EXPERTS = 16
N_EXPERT_GROUPS = 4
EXPERTS_PER_GROUP = 4
TOP_K = 2
LN_EPS = 1e-5
NEG_INF = -1e30
Q_SCALE = B_HEAD_DIM ** -0.5 * math.log2(math.e)

LANES = 128
SUBLANES = 8
VMEM_LIMIT = 56 * 1024 * 1024
ROW_TILE = 512
SSM_STEPS = 128
MOE_BLK = 512
ATT_SUB_BLOCKS = 8


def _cparams(sem):
    return pltpu.CompilerParams(dimension_semantics=sem, vmem_limit_bytes=VMEM_LIMIT)


def _ln(x):
    mu = jnp.mean(x, axis=-1, keepdims=True)
    xc = x - mu
    var = jnp.mean(xc * xc, axis=-1, keepdims=True)
    return xc * lax.rsqrt(var + LN_EPS)


def _store_tile_rows(ref, val):
    n_slab = val.shape[1] // LANES
    for j in range(n_slab):
        ref[pl.ds(j, val.shape[0], stride=n_slab), :] = val[:, j * LANES:(j + 1) * LANES]


def _load_tile_rows(ref, start, rows, n_slab):
    return jnp.concatenate([ref[pl.ds(start * n_slab + j, rows, stride=n_slab), :] for j in range(n_slab)],
                           axis=1)


def _gelu_tanh(x):
    return 0.5 * x * (1.0 + jnp.tanh(math.sqrt(2.0 / math.pi) * (x + 0.044715 * (x * x * x))))


def _mod_kernel(c_ref, w_ref, b_ref, o_ref):
    c = c_ref[...]
    cond = c * jax.nn.sigmoid(c)
    o_ref[0] = jnp.dot(cond, w_ref[0], precision=lax.Precision.HIGHEST,
                       preferred_element_type=F32) + b_ref[0]


def _adaln_mod(c, ada_w, ada_b):
    depth, d, n = ada_w.shape
    bsz = c.shape[0]
    nb = d
    return pl.pallas_call(
        _mod_kernel,
        grid=(depth, n // nb),
        in_specs=[pl.BlockSpec((bsz, d), lambda l, j: (0, 0)),
                  pl.BlockSpec((1, d, nb), lambda l, j: (l, 0, j)),
                  pl.BlockSpec((1, 1, nb), lambda l, j: (l, 0, j))],
        out_specs=pl.BlockSpec((1, bsz, nb), lambda l, j: (l, 0, j)),
        out_shape=jax.ShapeDtypeStruct((depth, bsz, n), F32),
        compiler_params=_cparams(("arbitrary", "arbitrary")),
        name="adaln_mod",
    )(c, ada_w, ada_b.reshape(depth, 1, n))


def _rope_kernel(pos_ref, freq_ref, cos_ref, s1_ref, s2_ref):
    ang = pos_ref[...].astype(F32) * freq_ref[...]
    lane = lax.broadcasted_iota(jnp.int32, ang.shape, 1) % B_HEAD_DIM
    sn = jnp.sin(ang)
    cos_ref[...] = jnp.cos(ang)
    s1_ref[...] = jnp.where(lane < ROT_DIM // 2, -sn, 0.0)
    s2_ref[...] = jnp.where((lane >= ROT_DIM // 2) & (lane < ROT_DIM), sn, 0.0)


def _rope_tables(positions):
    n = positions.size
    half = ROT_DIM // 2
    lane = jnp.arange(LANES) % B_HEAD_DIM
    freqs = ROPE_THETA ** (-(lane % half).astype(F32) * 2.0 / ROT_DIM)
    freq_row = jnp.where(lane < ROT_DIM, freqs, 0.0).reshape(1, LANES).astype(F32)
    tm = min(2048, n)
    out = jax.ShapeDtypeStruct((n, LANES), F32)
    spec = pl.BlockSpec((tm, LANES), lambda i: (i, 0))
    return pl.pallas_call(
        _rope_kernel,
        grid=(n // tm,),
        in_specs=[pl.BlockSpec((tm, 1), lambda i: (i, 0)),
                  pl.BlockSpec((1, LANES), lambda i: (0, 0))],
        out_specs=[spec, spec, spec],
        out_shape=[out, out, out],
        compiler_params=_cparams(("arbitrary",)),
        name="rope_tables",
    )(positions.reshape(n, 1), freq_row)


def _inproj_kernel(x_ref, sh_ref, sc_ref, w_ref, cos_ref, s1_ref, s2_ref, lng_ref, lnb_ref,
                   ws_ref, bs_ref, a_ref, s_ref, qn_ref, kn_ref, vn_ref, *rest):
    copy_dils = [dil for dil in DILATIONS if _dilation_layout(dil) == "copy"]
    n_copy = len(copy_dils)
    q_outs, k_outs, v_outs = rest[:n_copy], rest[n_copy:2 * n_copy], rest[2 * n_copy:3 * n_copy]
    qs_ref, ks_ref, vs_ref, qt_ref, kt_ref, vt_ref = rest[3 * n_copy:]
    tm = x_ref.shape[0]
    h = _ln(x_ref[...]) * (1.0 + sc_ref[0]) + sh_ref[0]
    proj = jnp.dot(h.astype(BF16), w_ref[...], preferred_element_type=F32)

    uv = _gelu_tanh(proj[:, :2 * A_WIDTH])
    u = uv[:, :A_WIDTH]
    v = (_ln(uv[:, A_WIDTH:]) * lng_ref[...] + lnb_ref[...]).astype(BF16)
    row = lax.broadcasted_iota(jnp.int32, (CHUNK, CHUNK), 0)
    col = lax.broadcasted_iota(jnp.int32, (CHUNK, CHUNK), 1)
    head_of_lane = lax.broadcasted_iota(jnp.int32, (CHUNK, A_WIDTH), 1) // A_HEAD_DIM
    w_heads = [jnp.where(col <= row, ws_ref[hd], 0.0).astype(BF16) for hd in range(A_HEADS)]
    for cidx in range(tm // CHUNK):
        rows = slice(cidx * CHUNK, (cidx + 1) * CHUNK)
        vc = v[rows]
        sv = bs_ref[...]
        for hd in range(A_HEADS):
            full = jnp.dot(w_heads[hd], vc, preferred_element_type=F32)
            sv = sv + jnp.where(head_of_lane == hd, full, 0.0)
        a_ref[rows, :] = (u[rows] * sv).astype(a_ref.dtype)

    cos = cos_ref[...]
    s1 = s1_ref[...]
    s2 = s2_ref[...]
    q0 = 2 * A_WIDTH
    k0 = q0 + B_WIDTH
    v0 = k0 + B_WIDTH
    for j in range(B_WIDTH // LANES):
        lanes = slice(j * LANES, (j + 1) * LANES)
        for base, ref, nat, scale in ((q0, qs_ref, qn_ref, Q_SCALE), (k0, ks_ref, kn_ref, 1.0)):
            xs = proj[:, base + j * LANES: base + (j + 1) * LANES]
            rot = (xs * cos + pltpu.roll(xs, LANES - ROT_DIM // 2, 1) * s1
                   + pltpu.roll(xs, ROT_DIM // 2, 1) * s2) * scale
            nat[:, lanes] = rot.astype(nat.dtype)
            ref[j] = rot
        vs = proj[:, v0 + j * LANES:v0 + (j + 1) * LANES]
        vn_ref[:, lanes] = vs.astype(vn_ref.dtype)
        vs_ref[j] = vs

    for src, stage, outs in ((qs_ref, qt_ref, q_outs), (ks_ref, kt_ref, k_outs), (vs_ref, vt_ref, v_outs)):
        bufs = (src, stage)
        prev = 1
        for idx, (dil, out) in enumerate(zip(copy_dils, outs)):
            chained = prev > 1 and dil % prev == 0
            base_dil = prev if chained else 1
            source = bufs[idx % 2] if chained or idx == 0 else None
            assert source is not None, "copy dilations must form a divisibility chain"
            step = dil // base_dil
            per = tm // dil
            keep = idx + 1 < len(copy_dils)
            for r_prev in range(base_dil):
                for q in range(step):
                    r = r_prev + base_dil * q
                    start = r_prev * (tm // base_dil) + q
                    for j in range(B_WIDTH // LANES):
                        val = source[j, pl.ds(start, per, stride=step), :]
                        out[0, r, :, j * LANES:(j + 1) * LANES] = val.astype(out.dtype)
                        if keep:
                            bufs[(idx + 1) % 2][j, r * per:(r + 1) * per, :] = val
            prev = dil

    s_ref[...] = proj[:, v0 + B_WIDTH:].astype(s_ref.dtype)


def _inproj(x2d, mod3, w_in_bf, cos_t, s1_t, s2_t, ln_g, ln_b, ws, bs_full, bsz, t):
    n, d = x2d.shape
    tm = min(ROW_TILE, t)
    nt = t // tm
    pw = w_in_bf.shape[1]
    c_width = pw - 2 * A_WIDTH - 3 * B_WIDTH
    tok = lambda b, i: (b * nt + i, 0)
    const2 = lambda b, i: (0, 0)
    outs = [jax.ShapeDtypeStruct((n, A_WIDTH), BF16),
            jax.ShapeDtypeStruct((t, bsz * c_width), BF16)]
    out_specs = [pl.BlockSpec((tm, A_WIDTH), tok),
                 pl.BlockSpec((tm, c_width), lambda b, i: (i, b))]
    for _ in range(3):
        outs.append(jax.ShapeDtypeStruct((n, B_WIDTH), BF16))
        out_specs.append(pl.BlockSpec((tm, B_WIDTH), tok))
    copy_dils = [dil for dil in DILATIONS if _dilation_layout(dil) == "copy"]
    for _ in range(3):
        for dil in copy_dils:
            outs.append(jax.ShapeDtypeStruct((bsz, dil, t // dil, B_WIDTH), BF16))
            out_specs.append(pl.BlockSpec((1, dil, tm // dil, B_WIDTH), lambda b, i: (b, 0, i, 0)))
    res = pl.pallas_call(
        _inproj_kernel,
        grid=(bsz, nt),
        in_specs=[pl.BlockSpec((tm, d), tok),
                  pl.BlockSpec((1, 1, d), lambda b, i: (b, 0, 0)),
                  pl.BlockSpec((1, 1, d), lambda b, i: (b, 0, 1)),
                  pl.BlockSpec((d, pw), const2),
                  pl.BlockSpec((tm, LANES), tok),
                  pl.BlockSpec((tm, LANES), tok),
                  pl.BlockSpec((tm, LANES), tok),
                  pl.BlockSpec((1, A_WIDTH), const2),
                  pl.BlockSpec((1, A_WIDTH), const2),
                  pl.BlockSpec((A_HEADS, CHUNK, CHUNK), lambda b, i: (0, 0, 0)),
                  pl.BlockSpec((CHUNK, A_WIDTH), const2)],
        out_specs=out_specs,
        out_shape=outs,
        scratch_shapes=[pltpu.VMEM((B_WIDTH // LANES, tm, LANES), F32)] * 6,
        compiler_params=_cparams(("arbitrary", "arbitrary")),
        name="inproj_gmlp_rope",
    )(x2d, mod3, mod3, w_in_bf, cos_t, s1_t, s2_t, ln_g, ln_b, ws, bs_full)
    a_out, s_in = res[0], res[1]
    natural = res[2:5]
    n_copy = len(copy_dils)
    copies = {dil: tuple(res[5 + z * n_copy + g] for z in range(3)) for g, dil in enumerate(copy_dils)}
    qkv = [copies[dil] if dil in copies else natural for dil in DILATIONS]
    return a_out, s_in, qkv


def _att_kernel(q_ref, k_ref, v_ref, o_ref, st_ref, kcat_ref, vaug_ref):
    i = pl.program_id(2)
    blk = ATT_BLK
    n_res = q_ref.shape[1]
    n_sub = q_ref.shape[2] // blk
    n_pairs = B_WIDTH // LANES
    last = slice(n_sub * blk, (n_sub + 1) * blk)

    @pl.when(i == 0)
    def _():
        kcat_ref[:, 0:blk, :] = jnp.zeros((n_res, blk, B_WIDTH), BF16)
        vaug_ref[:, :, 0:blk, 0:LANES] = jnp.zeros((n_res, n_pairs, blk, LANES), BF16)
        vaug_ref[:, :, :, LANES:] = jnp.ones((n_res, n_pairs, (n_sub + 1) * blk, LANES), BF16)

    @pl.when(i > 0)
    def _():
        kcat_ref[:, 0:blk, :] = kcat_ref[:, last, :]
        vaug_ref[:, :, 0:blk, 0:LANES] = vaug_ref[:, :, last, 0:LANES]

    for rr in range(n_res):
        kcat_ref[rr, blk:, :] = k_ref[0, rr]
        for pair in range(n_pairs):
            vaug_ref[rr, pair, blk:, 0:LANES] = v_ref[0, rr, :, pair * LANES:(pair + 1) * LANES]

    qi = lax.broadcasted_iota(jnp.int32, (blk, 2 * blk), 0)
    kk = lax.broadcasted_iota(jnp.int32, (blk, 2 * blk), 1)
    band = (kk >= qi) & (kk <= qi + blk)
    first = band & ((i > 0) | (kk >= blk))
    lane = lax.broadcasted_iota(jnp.int32, (blk, LANES), 1)
    low_half = lane < B_HEAD_DIM
    nt = (((1,), (1,)), ((), ()))
    for rr, sub in [(rr, sub) for rr in range(n_res) for sub in range(n_sub)]:
        rows = slice(sub * blk, (sub + 1) * blk)
        keys = slice(sub * blk, (sub + 2) * blk)
        mask = first if sub == 0 else band
        stats = jnp.zeros((blk, LANES), F32)
        for pair in range(n_pairs):
            lanes = slice(pair * LANES, (pair + 1) * LANES)
            q2 = q_ref[0, rr, rows, lanes]
            kslab = kcat_ref[rr, keys, lanes]
            res = []
            for hh in range(2):
                hd = 2 * pair + hh
                qm = jnp.where(low_half if hh == 0 else jnp.logical_not(low_half), q2, jnp.zeros_like(q2))
                s = lax.dot_general(qm, kslab, nt, preferred_element_type=F32)
                s = jnp.where(mask, s, NEG_INF)
                m = jnp.max(jnp.maximum(s[:, :blk], s[:, blk:]), axis=1, keepdims=True)
                p = jnp.exp2(s - m).astype(BF16)
                ov = jnp.dot(p, vaug_ref[rr, pair, keys, :], preferred_element_type=F32)
                den = ov[:, LANES:]
                res.append(ov[:, :LANES] / den)
                stats = jnp.where(lane == hd, m, stats)
                stats = jnp.where(lane == B_HEADS + hd, den, stats)
            o_ref[0, rr, rows, lanes] = jnp.where(low_half, res[0], res[1]).astype(o_ref.dtype)
        st_ref[0, rr, rows, :] = stats


def _dilation_layout(dil):
    return "natural" if dil == 1 else "copy"


def _att_branch(q, k, v, dil, bsz, t):
    rows = t // dil
    n_sub = min(ATT_SUB_BLOCKS, rows // ATT_BLK)
    step_rows = n_sub * ATT_BLK
    n_steps = rows // step_rows
    n_res = math.gcd(dil, max(1, ATT_SUB_BLOCKS // n_sub)) if n_steps == 1 else 1
    shape = lambda w: (bsz, dil, rows, w)
    blk = lambda w: pl.BlockSpec((1, n_res, step_rows, w), lambda b, r, i: (b, r, i, 0))
    q, k, v = (z.reshape(shape(B_WIDTH)) for z in (q, k, v))
    return pl.pallas_call(
        _att_kernel,
        grid=(bsz, dil // n_res, n_steps),
        in_specs=[blk(B_WIDTH), blk(B_WIDTH), blk(B_WIDTH)],
        out_specs=[blk(B_WIDTH), blk(LANES)],
        out_shape=[jax.ShapeDtypeStruct(shape(B_WIDTH), BF16),
                   jax.ShapeDtypeStruct(shape(LANES), F32)],
        scratch_shapes=[pltpu.VMEM((n_res, step_rows + ATT_BLK, B_WIDTH), BF16),
                        pltpu.VMEM((n_res, B_WIDTH // LANES, step_rows + ATT_BLK, 2 * LANES), BF16)],
        compiler_params=_cparams(("arbitrary", "arbitrary", "arbitrary")),
        name=f"dilated_attn_d{dil}",
    )(q, k, v)


def _ssm_kernel(u_ref, bmat_ref, cmat_ref, are_ref, aim_ref, dskip_ref, gw_ref, gb_ref,
                o_ref, state_ref, xbuf_ref, rows_ref):
    bsz, two_n = state_ref.shape
    n_state = two_n // 2
    steps = u_ref.shape[0]
    cw = u_ref.shape[1] // bsz
    n_slab = cw // LANES

    @pl.when(pl.program_id(0) == 0)
    def _():
        state_ref[...] = jnp.zeros_like(state_ref)

    for b in range(bsz):
        for j in range(n_slab):
            rows_ref[j, pl.ds(b, steps, stride=bsz), :] = (
                u_ref[:, b * cw + j * LANES:b * cw + (j + 1) * LANES].astype(F32))
    u32 = jnp.concatenate([rows_ref[j] for j in range(n_slab)], axis=1)
    xbuf_ref[...] = jnp.dot(u32.astype(BF16), bmat_ref[...], preferred_element_type=F32)
    a_re = jnp.broadcast_to(are_ref[...], (bsz, n_state))
    a_im = jnp.broadcast_to(aim_ref[...], (bsz, n_state))

    def step(tt, carry):
        x_re, x_im = carry
        r0 = pl.multiple_of(tt * bsz, bsz)
        b_re = xbuf_ref[pl.ds(r0, bsz), :n_state]
        b_im = xbuf_ref[pl.ds(r0, bsz), n_state:]
        n_re = a_re * x_re - a_im * x_im + b_re
        n_im = a_re * x_im + a_im * x_re + b_im
        xbuf_ref[pl.ds(r0, bsz), :n_state] = n_re
        xbuf_ref[pl.ds(r0, bsz), n_state:] = n_im
        return n_re, n_im

    x_re, x_im = lax.fori_loop(0, steps, step, (state_ref[:, :n_state], state_ref[:, n_state:]))
    state_ref[:, :n_state] = x_re
    state_ref[:, n_state:] = x_im

    y = jnp.dot(xbuf_ref[...].astype(BF16), cmat_ref[...], preferred_element_type=F32)
    y = _gelu_tanh(y + dskip_ref[...] * u32)
    gate = jax.nn.sigmoid(jnp.dot(y.astype(BF16), gw_ref[...], preferred_element_type=F32) + gb_ref[...])
    out = y * gate
    for j in range(n_slab):
        rows_ref[j] = out[:, j * LANES:(j + 1) * LANES]
    for b in range(bsz):
        for j in range(n_slab):
            o_ref[:, b * cw + j * LANES:b * cw + (j + 1) * LANES] = (
                rows_ref[j, pl.ds(b, steps, stride=bsz), :].astype(o_ref.dtype))


def _ssm_weights(lam_re, lam_im, log_dt, b_re, b_im, c_re, c_im):
    g, n_st = lam_re.shape
    cg = b_re.shape[-1]
    dt = jnp.exp(log_dt)[:, None]
    mag = jnp.exp(lam_re * dt)
    ab_re = mag * jnp.cos(lam_im * dt)
    ab_im = mag * jnp.sin(lam_im * dt)
    nr = ab_re - 1.0
    ni = ab_im
    mod2 = lam_re * lam_re + lam_im * lam_im
    f_re = (nr * lam_re + ni * lam_im) / mod2
    f_im = (ni * lam_re - nr * lam_im) / mod2
    bb_re = f_re[..., None] * b_re - f_im[..., None] * b_im
    bb_im = f_re[..., None] * b_im + f_im[..., None] * b_re
    eye = jnp.eye(g, dtype=F32)
    bm_re = jnp.einsum('gnc,gh->gchn', bb_re, eye).reshape(g * cg, g * n_st)
    bm_im = jnp.einsum('gnc,gh->gchn', bb_im, eye).reshape(g * cg, g * n_st)
    bmat = jnp.concatenate([bm_re, bm_im], axis=1)
    cm_re = jnp.einsum('gcn,gh->gnhc', c_re, eye).reshape(g * n_st, g * cg)
    cm_im = jnp.einsum('gcn,gh->gnhc', c_im, eye).reshape(g * n_st, g * cg)
    cmat = jnp.concatenate([cm_re, -cm_im], axis=0)
    return (bmat.astype(BF16), cmat.astype(BF16),
            ab_re.reshape(1, g * n_st), ab_im.reshape(1, g * n_st))


def _ssm(u_tm, bmat, cmat, a_re, a_im, d_skip, glu_w_bf, glu_b, bsz, t):
    cw = bmat.shape[0]
    two_n = bmat.shape[1]
    steps = min(SSM_STEPS, t)
    rows = steps * bsz
    const = lambda i: (0, 0)
    return pl.pallas_call(
        _ssm_kernel,
        grid=(t // steps,),
        in_specs=[pl.BlockSpec((steps, bsz * cw), lambda i: (i, 0)),
                  pl.BlockSpec((cw, two_n), const),
                  pl.BlockSpec((two_n, cw), const),
                  pl.BlockSpec((1, two_n // 2), const),
                  pl.BlockSpec((1, two_n // 2), const),
                  pl.BlockSpec((1, cw), const),
                  pl.BlockSpec((cw, cw), const),
                  pl.BlockSpec((1, cw), const)],
        out_specs=pl.BlockSpec((steps, bsz * cw), lambda i: (i, 0)),
        out_shape=jax.ShapeDtypeStruct((t, bsz * cw), BF16),
        scratch_shapes=[pltpu.VMEM((bsz, two_n), F32), pltpu.VMEM((rows, two_n), F32),
                        pltpu.VMEM((cw // LANES, rows, LANES), F32)],
        compiler_params=_cparams(("arbitrary",)),
        name="s5_scan_glu",
    )(u_tm, bmat, cmat, a_re, a_im, d_skip, glu_w_bf, glu_b)


def _route(sel, scores):
    gs = []
    for g in range(N_EXPERT_GROUPS):
        a, b, c, d = sel[4 * g: 4 * g + 4]
        hi1, lo1 = jnp.maximum(a, b), jnp.minimum(a, b)
        hi2, lo2 = jnp.maximum(c, d), jnp.minimum(c, d)
        gs.append(jnp.maximum(hi1, hi2) + jnp.maximum(jnp.minimum(hi1, hi2), jnp.maximum(lo1, lo2)))
    g_idx = jnp.zeros(gs[0].shape, jnp.int32)
    best = gs[0]
    for g in range(1, N_EXPERT_GROUPS):
        better = gs[g] > best
        g_idx = jnp.where(better, g, g_idx)
        best = jnp.where(better, gs[g], best)

    def pick(rows, j):
        out = rows[j]
        for g in range(1, N_EXPERT_GROUPS):
            out = jnp.where(g_idx == g, rows[4 * g + j], out)
        return out

    v = [pick(sel, j) for j in range(EXPERTS_PER_GROUP)]
    s = [pick(scores, j) for j in range(EXPERTS_PER_GROUP)]
    i1 = jnp.zeros(g_idx.shape, jnp.int32)
    b1, g1 = v[0], s[0]
    for j in range(1, EXPERTS_PER_GROUP):
        better = v[j] > b1
        i1 = jnp.where(better, j, i1)
        b1 = jnp.where(better, v[j], b1)
        g1 = jnp.where(better, s[j], g1)
    i2 = jnp.full(g_idx.shape, -1, jnp.int32)
    b2 = jnp.zeros_like(b1)
    g2 = jnp.zeros_like(g1)
    for j in range(EXPERTS_PER_GROUP):
        better = (i1 != j) & ((i2 < 0) | (v[j] > b2))
        i2 = jnp.where(better, j, i2)
        b2 = jnp.where(better, v[j], b2)
        g2 = jnp.where(better, s[j], g2)
    tot = g1 + g2
    base = g_idx * EXPERTS_PER_GROUP
    return base + i1, base + i2, g1 / tot, g2 / tot


def _natural_order(ref, scr_ref, dil):
    if _dilation_layout(dil) != "copy":
        return ref[...].astype(F32)
    per = ref.shape[2]
    n_slab = scr_ref.shape[0]
    for r in range(dil):
        for j in range(n_slab):
            scr_ref[j, pl.ds(r, per, stride=dil), :] = ref[0, r, :, j * LANES:(j + 1) * LANES].astype(F32)
    if n_slab == 1:
        return scr_ref[0]
    return jnp.concatenate([scr_ref[j] for j in range(n_slab)], axis=1)


def _outproj_kernel(a_ref, o1_ref, o2_ref, o3_ref, st1_ref, st2_ref, st3_ref,
                    c_ref, x_ref, g1_ref, sh2_ref, sc2_ref, wout_ref, expand_ref, lng_ref, lnb_ref,
                    rwh_ref, rwl_ref, rb_ref, x1_ref, eidx_ref, gate_ref, tab_ref, xs_hbm,
                    oscr2_ref, oscr3_ref, sscr2_ref, sscr3_ref, hbuf, state_ref, slot_ref, fin_ref, fins_ref,
                    ssem, csem, *, alpha):
    o_refs = (o1_ref, o2_ref, o3_ref)
    st_refs = (st1_ref, st2_ref, st3_ref)
    o_scr = (None, oscr2_ref, oscr3_ref)
    st_scr = (None, sscr2_ref, sscr3_ref)
    ms = [_natural_order(st_refs[g], st_scr[g], DILATIONS[g]) for g in range(len(DILATIONS))]
    dens = [pltpu.roll(m, LANES - B_HEADS, 1) for m in ms]
    mx = jnp.maximum(jnp.maximum(ms[0], ms[1]), ms[2])
    ws = [dens[g] * jnp.exp2(ms[g] - mx) for g in range(len(DILATIONS))]
    lane = lax.broadcasted_iota(jnp.int32, mx.shape, 1)
    tot = jnp.where(lane < B_HEADS, ws[0] + ws[1] + ws[2], 1.0)
    expand = expand_ref[...]

    def widen(w):
        wn = jnp.where(lane < B_HEADS, w / tot, 0.0)
        hi = wn.astype(BF16)
        lo = (wn - hi.astype(F32)).astype(BF16)
        return (jnp.dot(hi, expand, preferred_element_type=F32)
                + jnp.dot(lo, expand, preferred_element_type=F32))

    b_out = widen(ws[0]) * _natural_order(o_refs[0], o_scr[0], DILATIONS[0])
    for g in range(1, len(DILATIONS)):
        b_out = b_out + widen(ws[g]) * _natural_order(o_refs[g], o_scr[g], DILATIONS[g])
    b_out = b_out.astype(BF16)

    wout = wout_ref
    mix = (jnp.dot(a_ref[...], wout[:A_WIDTH, :], preferred_element_type=F32)
           + jnp.dot(b_out, wout[A_WIDTH:A_WIDTH + B_WIDTH, :], preferred_element_type=F32)
           + jnp.dot(c_ref[...], wout[A_WIDTH + B_WIDTH:, :], preferred_element_type=F32))
    x1 = _ln(alpha * x_ref[...] + (1.0 + g1_ref[0]) * mix) * lng_ref[...] + lnb_ref[...]
    x1_ref[...] = x1
    h2 = _ln(x1) * (1.0 + sc2_ref[0]) + sh2_ref[0]

    hi = h2.astype(BF16)
    lo = (h2 - hi.astype(F32)).astype(BF16)
    nt = (((1,), (1,)), ((), ()))
    logits = (lax.dot_general(rwh_ref[...], hi, nt, preferred_element_type=F32)
              + lax.dot_general(rwh_ref[...], lo, nt, preferred_element_type=F32)
              + lax.dot_general(rwl_ref[...], hi, nt, preferred_element_type=F32))
    scores = jax.nn.sigmoid(logits)
    sel = scores + rb_ref[...]
    sel_rows = [sel[e:e + 1, :] for e in range(N_EXPERTS)]
    score_rows = [scores[e:e + 1, :] for e in range(N_EXPERTS)]
    e1, e2, gt1, gt2 = _route(sel_rows, score_rows)

    g = pl.program_id(0) * pl.num_programs(1) + pl.program_id(1)
    n_tiles = pl.num_programs(0) * pl.num_programs(1)
    tm = h2.shape[0]
    n_slab = h2.shape[1] // LANES
    blk = MOE_BLK
    par = g % 2

    @pl.when(g == 0)
    def _():
        state_ref[...] = jnp.zeros_like(state_ref)
        tab_ref[...] = jnp.full(tab_ref.shape, float(N_EXPERTS), F32)

    erow = lax.broadcasted_iota(jnp.int32, (N_EXPERTS, tm), 0)
    oh1 = erow == e1
    oh2 = erow == e2
    both = (oh1 | oh2).astype(BF16)
    earlier = (lax.broadcasted_iota(jnp.int32, (tm, tm), 0)
               < lax.broadcasted_iota(jnp.int32, (tm, tm), 1)).astype(BF16)
    base = state_ref[0, :, 0:1]
    cur = state_ref[1, :, 0:1]
    nfree = state_ref[2, :, 0:1]
    n_vec = jnp.sum(both.astype(F32), axis=1, keepdims=True)
    off = base - jnp.floor(base * (1.0 / blk)) * blk
    need_a = (off == 0.0) & (n_vec > 0.0)
    need_b = (off + n_vec) > float(blk)
    na = need_a.astype(F32)
    nb = need_b.astype(F32)
    alloc = na + nb
    lower = (lax.broadcasted_iota(jnp.int32, (N_EXPERTS, N_EXPERTS), 1)
             < lax.broadcasted_iota(jnp.int32, (N_EXPERTS, N_EXPERTS), 0)).astype(BF16)
    alloc_wide = jnp.broadcast_to(alloc, (N_EXPERTS, LANES))
    before = jnp.dot(lower, alloc_wide.astype(BF16), preferred_element_type=F32)[:, 0:1]
    a_id = jnp.where(need_a, nfree + before, cur)
    b_id = jnp.where(need_b, nfree + before + na, a_id)
    bound = base - off + float(blk)
    total_alloc = jnp.sum(alloc_wide, axis=0, keepdims=True)[:, 0:1]
    state_ref[0] = jnp.broadcast_to(base + n_vec, (N_EXPERTS, LANES))
    state_ref[1] = jnp.broadcast_to(jnp.where(need_b, b_id, a_id), (N_EXPERTS, LANES))
    state_ref[2] = jnp.broadcast_to(nfree + total_alloc, (N_EXPERTS, LANES))

    blane = lax.broadcasted_iota(jnp.int32, (N_EXPERTS, tab_ref.shape[1]), 1).astype(F32)
    hit = ((blane == a_id) & need_a) | ((blane == b_id) & need_b)
    e_col = lax.broadcasted_iota(jnp.int32, (N_EXPERTS, 1), 0).astype(F32)
    delta = jnp.sum(jnp.where(hit, e_col - float(N_EXPERTS), 0.0), axis=0, keepdims=True)
    tab_ref[...] = tab_ref[...] + jnp.broadcast_to(delta, tab_ref.shape)

    rank = jnp.dot(both, earlier, preferred_element_type=F32) + base
    slot = jnp.where(rank < bound, a_id * blk + (rank - bound + blk), b_id * blk + (rank - bound))
    s1 = jnp.sum(jnp.where(oh1, slot, 0.0), axis=0, keepdims=True)
    s2 = jnp.sum(jnp.where(oh2, slot, 0.0), axis=0, keepdims=True)

    eidx_ref[...] = jnp.zeros_like(eidx_ref)
    gate_ref[...] = jnp.zeros_like(gate_ref)
    eidx_ref[0:1, :] = e1
    eidx_ref[1:2, :] = e2
    eidx_ref[2:3, :] = s1.astype(jnp.int32)
    eidx_ref[3:4, :] = s2.astype(jnp.int32)
    gate_ref[0:1, :] = gt1
    gate_ref[1:2, :] = gt2

    slot_copy = pltpu.make_async_copy(eidx_ref, slot_ref, csem)
    slot_copy.start()

    def wait_rows(p):
        for _ in range(TOP_K):
            pltpu.make_async_copy(hbuf.at[p], xs_hbm.at[pl.ds(0, tm * n_slab), :], ssem.at[p]).wait()

    @pl.when(g >= 2)
    def _():
        wait_rows(par)

    _store_tile_rows(hbuf.at[par], h2)
    slot_copy.wait()

    def issue(s, c):
        src = hbuf.at[par, pl.ds(pl.multiple_of(s * n_slab, n_slab), n_slab), :]
        for kk in range(TOP_K):
            dst = pl.multiple_of(slot_ref[TOP_K + kk, s] * n_slab, n_slab)
            pltpu.make_async_copy(src, xs_hbm.at[pl.ds(dst, n_slab), :], ssem.at[par]).start()
        return c
    lax.fori_loop(0, tm, issue, 0, unroll=8)

    @pl.when(g == n_tiles - 1)
    def _():
        wait_rows(par)

        @pl.when(n_tiles >= 2)
        def _():
            wait_rows(1 - par)

        lane_e = lax.broadcasted_iota(jnp.int32, (N_EXPERTS, LANES), 1)
        sub_e = lax.broadcasted_iota(jnp.int32, (N_EXPERTS, LANES), 0)

        def by_lane(col):
            return jnp.sum(jnp.where(lane_e == sub_e, col, 0.0), axis=0, keepdims=True)

        end = base + n_vec
        end_off = end - jnp.floor(end * (1.0 / blk)) * blk
        fin_ref[...] = jnp.zeros_like(fin_ref)
        fin_ref[0:1, :] = by_lane(jnp.where(need_b, b_id, a_id)).astype(jnp.int32)
        fin_ref[1:2, :] = by_lane(end_off).astype(jnp.int32)
        fin_ref[2:3, :] = by_lane(nfree + total_alloc).astype(jnp.int32)
        fin_copy = pltpu.make_async_copy(fin_ref, fins_ref, csem)
        fin_copy.start()
        fin_copy.wait()

        hbuf[0] = jnp.zeros(hbuf.shape[1:], F32)

        def zero_rows(first_row, n_rows):
            cp = pltpu.make_async_copy(hbuf.at[0, pl.ds(0, n_rows * n_slab), :],
                                       xs_hbm.at[pl.ds(pl.multiple_of(first_row * n_slab, n_slab),
                                                       n_rows * n_slab), :], ssem.at[0])
            cp.start()
            cp.wait()

        for e in range(N_EXPERTS):
            pos = fins_ref[1, e]
            start = fins_ref[0, e] * blk
            for k in range(blk.bit_length() - 1):
                take = (fins_ref[1, e] > 0) & (((pos >> k) & 1) == 1)

                @pl.when(take)
                def _():
                    zero_rows(start + pos, 1 << k)
                pos = jnp.where(take, pos + (1 << k), pos)

        n_blk = xs_hbm.shape[0] // (blk * n_slab)
        for t in range(N_EXPERTS):
            spare = fins_ref[2, 0] + t

            @pl.when(spare < n_blk)
            def _():
                zero_rows(spare * blk, blk)


def _outproj(a_out, att, c_tm, x2d, mod3, w_out_bf, ln_g, ln_b, rw_hi, rw_lo, rbias, bsz, t, alpha):
    n, d = x2d.shape
    tm = min(ROW_TILE, t)
    assert tm == MOE_BLK and MOE_BLK & (MOE_BLK - 1) == 0
    nt = t // tm
    n_slab = d // LANES
    n_blk = n * TOP_K // MOE_BLK + N_EXPERTS
    tab_w = -(-n_blk // LANES) * LANES
    cw = c_tm.shape[1] // bsz
    tok = lambda b, i: (b * nt + i, 0)
    const2 = lambda b, i: (0, 0)
    head = jnp.arange(B_WIDTH) // B_HEAD_DIM
    expand = (jnp.arange(LANES)[:, None] == head[None, :]).astype(BF16)

    def branch_input(z, dil, w):
        if _dilation_layout(dil) == "copy":
            return z, pl.BlockSpec((1, dil, tm // dil, w), lambda b, i: (b, 0, i, 0))
        return z.reshape(n, w), pl.BlockSpec((tm, w), tok)

    o_in = [branch_input(o, dil, B_WIDTH) for (o, _), dil in zip(att, DILATIONS)]
    st_in = [branch_input(st, dil, LANES) for (_, st), dil in zip(att, DILATIONS)]
    (o1, st1), (o2, st2), (o3, st3) = [(o_in[g][0], st_in[g][0]) for g in range(len(DILATIONS))]
    modspec = lambda j: pl.BlockSpec((1, 1, d), lambda b, i: (b, 0, j))
    return pl.pallas_call(
        functools.partial(_outproj_kernel, alpha=alpha),
        grid=(bsz, nt),
        in_specs=[pl.BlockSpec((tm, A_WIDTH), tok)]
                 + [spec for _, spec in o_in]
                 + [spec for _, spec in st_in]
                 + [pl.BlockSpec((tm, cw), lambda b, i: (i, b)),
                  pl.BlockSpec((tm, d), tok),
                  modspec(2), modspec(3), modspec(4),
                  pl.BlockSpec((d, d), const2),
                  pl.BlockSpec((LANES, B_WIDTH), const2),
                  pl.BlockSpec((1, d), const2), pl.BlockSpec((1, d), const2),
                  pl.BlockSpec((N_EXPERTS, d), const2), pl.BlockSpec((N_EXPERTS, d), const2),
                  pl.BlockSpec((N_EXPERTS, 1), const2)],
        out_specs=[pl.BlockSpec((tm, d), tok),
                   pl.BlockSpec((SUBLANES, tm), lambda b, i: (0, b * nt + i)),
                   pl.BlockSpec((SUBLANES, tm), lambda b, i: (0, b * nt + i)),
                   pl.BlockSpec((SUBLANES, tab_w), const2),
                   pl.BlockSpec(memory_space=pl.ANY)],
        out_shape=[jax.ShapeDtypeStruct((n, d), F32),
                   jax.ShapeDtypeStruct((SUBLANES, n), jnp.int32),
                   jax.ShapeDtypeStruct((SUBLANES, n), F32),
                   jax.ShapeDtypeStruct((SUBLANES, tab_w), F32),
                   jax.ShapeDtypeStruct((n_blk * MOE_BLK * n_slab, LANES), F32)],
        scratch_shapes=[pltpu.VMEM((B_WIDTH // LANES, tm, LANES), F32),
                        pltpu.VMEM((B_WIDTH // LANES, tm, LANES), F32),
                        pltpu.VMEM((1, tm, LANES), F32), pltpu.VMEM((1, tm, LANES), F32),
                        pltpu.VMEM((2, tm * n_slab, LANES), F32),
                        pltpu.VMEM((3, N_EXPERTS, LANES), F32),
                        pltpu.SMEM((SUBLANES, tm), jnp.int32),
                        pltpu.VMEM((SUBLANES, LANES), jnp.int32),
                        pltpu.SMEM((SUBLANES, LANES), jnp.int32),
                        pltpu.SemaphoreType.DMA((2,)), pltpu.SemaphoreType.DMA],
        compiler_params=_cparams(("arbitrary", "arbitrary")),
        name="outproj_ln_router",
    )(a_out, o1, o2, o3, st1, st2, st3, c_tm, x2d, mod3, mod3, mod3, w_out_bf, expand,
      ln_g, ln_b, rw_hi, rw_lo, rbias)


def _block_order(tab_row, n_blk):
    tab = tab_row[:n_blk].astype(jnp.int32)
    idx = jnp.arange(n_blk, dtype=jnp.int32)
    before = (tab[None, :] < tab[:, None]) | ((tab[None, :] == tab[:, None]) & (idx[None, :] < idx[:, None]))
    pos = jnp.sum(before.astype(jnp.int32), axis=1)
    at_pos = pos[None, :] == idx[:, None]
    perm = jnp.sum(jnp.where(at_pos, idx[None, :], 0), axis=1).astype(jnp.int32)
    blk_e = jnp.minimum(jnp.sum(jnp.where(at_pos, tab[None, :], 0), axis=1), N_EXPERTS - 1).astype(jnp.int32)
    n_used = jnp.sum((tab < N_EXPERTS).astype(jnp.int32)).reshape(1)
    return perm, blk_e, n_used


def _expert_kernel(blk_e_ref, nused_ref, perm_ref, xs_ref, wg_ref, wu_ref, wd_ref, ys_ref, wg_bf, wu_bf, wd_bf,
                   *, blk, n_slab):
    j = pl.program_id(0)

    @pl.when((j == 0) | (blk_e_ref[j] != blk_e_ref[jnp.maximum(j - 1, 0)]))
    def _():
        wg_bf[...] = wg_ref[0, 0].astype(BF16)
        wu_bf[...] = wu_ref[0, 0].astype(BF16)
        wd_bf[...] = wd_ref[0, 0].astype(BF16)

    @pl.when(j < nused_ref[0])
    def _():
        rows = _load_tile_rows(xs_ref, 0, blk, n_slab).astype(BF16)
        gt = jnp.dot(rows, wg_bf[...], preferred_element_type=F32)
        up = jnp.dot(rows, wu_bf[...], preferred_element_type=F32)
        hid = (gt * jax.nn.sigmoid(gt) * up).astype(BF16)
        _store_tile_rows(ys_ref, jnp.dot(hid, wd_bf[...], preferred_element_type=F32))

    @pl.when(j >= nused_ref[0])
    def _():
        ys_ref[...] = jnp.zeros_like(ys_ref)


def _moe(xs, tab, w_gate, w_up, w_down, layer):
    d, de = w_gate.shape[-2:]
    n_slab = d // LANES
    blk = MOE_BLK
    n_blk = xs.shape[0] // (blk * n_slab)
    perm, blk_e, n_used = _block_order(tab[0], n_blk)

    def x_block(j, be, nu, pm):
        return (pm[jnp.minimum(j, nu[0] - 1)], 0)

    return pl.pallas_call(
        functools.partial(_expert_kernel, blk=blk, n_slab=n_slab),
        grid_spec=pltpu.PrefetchScalarGridSpec(
            num_scalar_prefetch=3,
            grid=(n_blk,),
            in_specs=[pl.BlockSpec((blk * n_slab, LANES), x_block),
                      pl.BlockSpec((1, 1, d, de), lambda j, be, nu, pm: (layer, be[j], 0, 0)),
                      pl.BlockSpec((1, 1, d, de), lambda j, be, nu, pm: (layer, be[j], 0, 0)),
                      pl.BlockSpec((1, 1, de, d), lambda j, be, nu, pm: (layer, be[j], 0, 0))],
            out_specs=pl.BlockSpec((blk * n_slab, LANES), lambda j, be, nu, pm: (pm[j], 0)),
            scratch_shapes=[pltpu.VMEM((d, de), BF16), pltpu.VMEM((d, de), BF16), pltpu.VMEM((de, d), BF16)],
        ),
        out_shape=jax.ShapeDtypeStruct(xs.shape, F32),
        compiler_params=_cparams(("arbitrary",)),
        name="moe_experts",
    )(blk_e, n_used, perm, xs, w_gate, w_up, w_down)


def _ffn_ln_kernel(dest_ref, next_ref, x_ref, gate_ref, g2_ref, lng_ref, lnb_ref, ys_hbm, o_ref, ybuf, sem,
                   *, alpha):
    tm, d = x_ref.shape
    n_slab = d // LANES
    g = pl.program_id(0) * pl.num_programs(1) + pl.program_id(1)
    n_tiles = pl.num_programs(0) * pl.num_programs(1)
    slot = g % 2

    def gather(idx_ref, buf_slot):
        def issue(s, c):
            src = pl.multiple_of(idx_ref[0, 0, s] * n_slab, n_slab)
            dst = pl.multiple_of(s * n_slab, n_slab)
            pltpu.make_async_copy(ys_hbm.at[pl.ds(src, n_slab), :],
                                  ybuf.at[buf_slot, pl.ds(dst, n_slab), :], sem.at[buf_slot]).start()
            return c
        lax.fori_loop(0, TOP_K * tm, issue, 0, unroll=8)

    @pl.when(g == 0)
    def _():
        gather(dest_ref, 0)

    @pl.when(g + 1 < n_tiles)
    def _():
        gather(next_ref, 1 - slot)

    pltpu.make_async_copy(ys_hbm.at[pl.ds(0, TOP_K * tm * n_slab), :], ybuf.at[slot], sem.at[slot]).wait()

    gate = gate_ref[...]
    buf = ybuf.at[slot]
    ffn = _load_tile_rows(buf, 0, tm, n_slab) * gate[:, 0:1]
    for kk in range(1, TOP_K):
        ffn = ffn + _load_tile_rows(buf, kk * tm, tm, n_slab) * gate[:, kk:kk + 1]
    o_ref[...] = _ln(alpha * x_ref[...] + (1.0 + g2_ref[0]) * ffn) * lng_ref[...] + lnb_ref[...]


def _ffn_ln(x1, ys, dest, gates, mod3, ln_g, ln_b, bsz, t, alpha):
    n, d = x1.shape
    n_slab = d // LANES
    tm = min(ROW_TILE, t)
    nt = t // tm
    n_tiles = bsz * nt
    dest_tiles = dest.reshape(TOP_K, n_tiles, tm).transpose(1, 0, 2).reshape(n_tiles, 1, TOP_K * tm)
    tok = lambda b, i, *_: (b * nt + i, 0)
    const2 = lambda b, i, *_: (0, 0)
    gate_cols = gates[:TOP_K].T
    idx_blk = (1, 1, TOP_K * tm)
    return pl.pallas_call(
        functools.partial(_ffn_ln_kernel, alpha=alpha),
        grid=(bsz, nt),
        in_specs=[pl.BlockSpec(idx_blk, lambda b, i: (b * nt + i, 0, 0), memory_space=pltpu.SMEM),
                  pl.BlockSpec(idx_blk, lambda b, i: (jnp.minimum(b * nt + i + 1, n_tiles - 1), 0, 0),
                               memory_space=pltpu.SMEM),
                  pl.BlockSpec((tm, d), tok), pl.BlockSpec((tm, TOP_K), tok),
                  pl.BlockSpec((1, 1, d), lambda b, i: (b, 0, 5)),
                  pl.BlockSpec((1, d), const2), pl.BlockSpec((1, d), const2),
                  pl.BlockSpec(memory_space=pl.ANY)],
        out_specs=pl.BlockSpec((tm, d), tok),
        out_shape=jax.ShapeDtypeStruct((n, d), F32),
        scratch_shapes=[pltpu.VMEM((2, TOP_K * tm * n_slab, LANES), F32), pltpu.SemaphoreType.DMA((2,))],
        compiler_params=_cparams(("arbitrary", "arbitrary")),
        name="ffn_combine_ln",
    )(dest_tiles, dest_tiles, x1, gate_cols, mod3, ln_g, ln_b, ys)


def kernel(x, c, positions, ada_w, ada_b, w_in, gm_ln_g, gm_ln_b, gm_ws, gm_bs, ssm_lam_re, ssm_lam_im, ssm_log_dt, ssm_b_re, ssm_b_im, ssm_c_re, ssm_c_im, ssm_d, glu_w, glu_b, w_out, ln1_g, ln1_b, router_w, router_bias, exp_w_gate, exp_w_up, exp_w_down, ln2_g, ln2_b):
    bsz, t, d = x.shape
    depth = ada_w.shape[0]
    alpha = (2.0 * depth) ** 0.25
    n = bsz * t

    mod = _adaln_mod(c.astype(F32), ada_w, ada_b)
    cos_t, s1_t, s2_t = _rope_tables(positions)
    rw_t = router_w.astype(F32).T
    rw_hi = rw_t.astype(BF16)
    rw_lo = (rw_t - rw_hi.astype(F32)).astype(BF16)
    rbias = router_bias.astype(F32).reshape(N_EXPERTS, 1)

    xf = x.astype(F32).reshape(n, d)
    for l in range(depth):
        mod3 = mod[l].reshape(bsz, 1, 6 * d)
        bs_full = jnp.repeat(gm_bs[l].T, A_HEAD_DIM, axis=1)
        a_out, s_in, qkv = _inproj(
            xf, mod3, w_in[l].astype(BF16), cos_t, s1_t, s2_t,
            gm_ln_g[l].reshape(1, A_WIDTH), gm_ln_b[l].reshape(1, A_WIDTH), gm_ws[l], bs_full, bsz, t)
        att = [_att_branch(*qkv[g], dil, bsz, t) for g, dil in enumerate(DILATIONS)]
        bmat, cmat, a_re, a_im = _ssm_weights(ssm_lam_re[l], ssm_lam_im[l], ssm_log_dt[l],
                                              ssm_b_re[l], ssm_b_im[l], ssm_c_re[l], ssm_c_im[l])
        cw = bmat.shape[0]
        c_out = _ssm(s_in, bmat, cmat, a_re, a_im, ssm_d[l].reshape(1, cw),
                     glu_w[l].astype(BF16), glu_b[l].reshape(1, cw), bsz, t)
        x1, eidx, gates, tab, xs = _outproj(a_out, att, c_out, xf, mod3, w_out[l].astype(BF16),
                                            ln1_g[l].reshape(1, d), ln1_b[l].reshape(1, d),
                                            rw_hi, rw_lo, rbias, bsz, t, alpha)
        ys = _moe(xs, tab, exp_w_gate.astype(F32), exp_w_up.astype(F32), exp_w_down.astype(F32), l)
        xf = _ffn_ln(x1, ys, eidx[TOP_K:2 * TOP_K], gates, mod3, ln2_g[l].reshape(1, d), ln2_b[l].reshape(1, d),
                     bsz, t, alpha)
    return xf.reshape(bsz, t, d)
```

```python
import functools
import math

import jax
import jax.numpy as jnp
from jax import lax
from jax.experimental import pallas as pl
from jax.experimental.pallas import tpu as pltpu

F32 = jnp.float32
BF16 = jnp.bfloat16

A_HEADS = 4
A_HEAD_DIM = 64
A_WIDTH = A_HEADS * A_HEAD_DIM
CHUNK = 128
B_HEADS = 8
B_HEAD_DIM = 64
B_WIDTH = B_HEADS * B_HEAD_DIM
DILATIONS = (1, 4, 16)
ATT_BLK = 128
ROT_DIM = B_HEAD_DIM // 4
ROPE_THETA = 500000.0
N_EXPERTS = 16
N_EXPERT_GROUPS = 4
EXPERTS_PER_GROUP = 4
TOP_K = 2
LN_EPS = 1e-5
NEG_INF = -1e30
Q_SCALE = B_HEAD_DIM ** -0.5 * math.log2(math.e)

LANES = 128
SUBLANES = 8
VMEM_LIMIT = 56 * 1024 * 1024
ROW_TILE = 512
ROPE_ROWS = 2048
SSM_STEPS = 128
MOE_BLK = 512
ATT_SUB_BLOCKS = 8


def _cparams(sem):
    return pltpu.CompilerParams(dimension_semantics=sem, vmem_limit_bytes=VMEM_LIMIT)


def _ln(x):
    mu = jnp.mean(x, axis=-1, keepdims=True)
    xc = x - mu
    var = jnp.mean(xc * xc, axis=-1, keepdims=True)
    return xc * lax.rsqrt(var + LN_EPS)


def _store_tile_rows(ref, val):
    n_slab = val.shape[1] // LANES
    for j in range(n_slab):
        ref[pl.ds(j, val.shape[0], stride=n_slab), :] = val[:, j * LANES:(j + 1) * LANES]


def _load_tile_rows(ref, start, rows, n_slab):
    return jnp.concatenate([ref[pl.ds(start * n_slab + j, rows, stride=n_slab), :] for j in range(n_slab)],
                           axis=1)


def _gelu_tanh(x):
    return 0.5 * x * (1.0 + jnp.tanh(math.sqrt(2.0 / math.pi) * (x + 0.044715 * (x * x * x))))


def _mod_kernel(c_ref, w_ref, b_ref, o_ref):
    c = c_ref[...]
    cond = c * jax.nn.sigmoid(c)
    o_ref[0] = jnp.dot(cond, w_ref[0], precision=lax.Precision.HIGHEST,
                       preferred_element_type=F32) + b_ref[0]


def _adaln_mod(c, ada_w, ada_b):
    depth, d, n = ada_w.shape
    bsz = c.shape[0]
    nb = d
    return pl.pallas_call(
        _mod_kernel,
        grid=(depth, n // nb),
        in_specs=[pl.BlockSpec((bsz, d), lambda l, j: (0, 0)),
                  pl.BlockSpec((1, d, nb), lambda l, j: (l, 0, j)),
                  pl.BlockSpec((1, 1, nb), lambda l, j: (l, 0, j))],
        out_specs=pl.BlockSpec((1, bsz, nb), lambda l, j: (l, 0, j)),
        out_shape=jax.ShapeDtypeStruct((depth, bsz, n), F32),
        compiler_params=_cparams(("arbitrary", "arbitrary")),
        name="adaln_mod",
    )(c, ada_w, ada_b.reshape(depth, 1, n))


def _rope_kernel(pos_ref, freq_ref, ecos_ref, es1_ref, es2_ref, one_ref, cos_ref, s1_ref, s2_ref):
    ang = pos_ref[0].astype(F32) * freq_ref[...]
    cs = jnp.cos(ang)
    sn = jnp.sin(ang)
    tn = (((0,), (0,)), ((), ()))

    def pieces(x):
        p1 = x.astype(BF16)
        r1 = x - p1.astype(F32)
        p2 = r1.astype(BF16)
        p3 = (r1 - p2.astype(F32)).astype(BF16)
        return p1, p2, p3

    def spread(parts, e_ref):
        e = e_ref[...]
        return sum(lax.dot_general(p, e, tn, preferred_element_type=F32) for p in parts)

    cs_parts = pieces(cs)
    sn_parts = pieces(sn)
    cos_ref[...] = spread(cs_parts, ecos_ref) + one_ref[...]
    s1_ref[...] = spread(sn_parts, es1_ref)
    s2_ref[...] = spread(sn_parts, es2_ref)


def _rope_tables(positions):
    n = positions.size
    half = ROT_DIM // 2
    freqs = (ROPE_THETA ** (-jnp.arange(half, dtype=F32) * 2.0 / ROT_DIM)).reshape(half, 1)
    lane = jnp.arange(LANES) % B_HEAD_DIM
    f_of_lane = lane % half
    pick = (jnp.arange(half)[:, None] == f_of_lane[None, :]).astype(F32)
    ecos = pick * (lane < ROT_DIM)[None, :]
    es1 = -pick * (lane < half)[None, :]
    es2 = pick * ((lane >= half) & (lane < ROT_DIM))[None, :]
    one_row = (lane >= ROT_DIM).astype(F32).reshape(1, LANES)
    tm = min(ROPE_ROWS, n)
    out = jax.ShapeDtypeStruct((n, LANES), F32)
    spec = pl.BlockSpec((tm, LANES), lambda i: (i, 0))
    sel = pl.BlockSpec((half, LANES), lambda i: (0, 0))
    return pl.pallas_call(
        _rope_kernel,
        grid=(n // tm,),
        in_specs=[pl.BlockSpec((1, 1, tm), lambda i: (i, 0, 0)),
                  pl.BlockSpec((half, 1), lambda i: (0, 0)),
                  sel, sel, sel, pl.BlockSpec((1, LANES), lambda i: (0, 0))],
        out_specs=[spec, spec, spec],
        out_shape=[out, out, out],
        compiler_params=_cparams(("arbitrary",)),
        name="rope_tables",
    )(positions.reshape(n // tm, 1, tm), freqs, ecos.astype(BF16), es1.astype(BF16), es2.astype(BF16), one_row)


def _inproj_kernel(x_ref, sh_ref, sc_ref, w_ref, cos_ref, s1_ref, s2_ref, lng_ref, lnb_ref,
                   ws_ref, bs_ref, a_ref, s_ref, qn_ref, kn_ref, vn_ref, *rest):
    copy_dils = [dil for dil in DILATIONS if _dilation_layout(dil) == "copy"]
    n_copy = len(copy_dils)
    q_outs, k_outs, v_outs = rest[:n_copy], rest[n_copy:2 * n_copy], rest[2 * n_copy:3 * n_copy]
    qs_ref, ks_ref, vs_ref, qt_ref, kt_ref, vt_ref = rest[3 * n_copy:]
    tm = x_ref.shape[0]
    h = _ln(x_ref[...]) * (1.0 + sc_ref[0]) + sh_ref[0]
    proj = jnp.dot(h.astype(BF16), w_ref[...], preferred_element_type=F32)

    uv = _gelu_tanh(proj[:, :2 * A_WIDTH])
    u = uv[:, :A_WIDTH]
    v = (_ln(uv[:, A_WIDTH:]) * lng_ref[...] + lnb_ref[...]).astype(BF16)
    row = lax.broadcasted_iota(jnp.int32, (CHUNK, CHUNK), 0)
    col = lax.broadcasted_iota(jnp.int32, (CHUNK, CHUNK), 1)
    head_of_lane = lax.broadcasted_iota(jnp.int32, (CHUNK, A_WIDTH), 1) // A_HEAD_DIM
    w_heads = [jnp.where(col <= row, ws_ref[hd], 0.0).astype(BF16) for hd in range(A_HEADS)]
    for cidx in range(tm // CHUNK):
        rows = slice(cidx * CHUNK, (cidx + 1) * CHUNK)
        vc = v[rows]
        sv = bs_ref[...]
        for hd in range(A_HEADS):
            full = jnp.dot(w_heads[hd], vc, preferred_element_type=F32)
            sv = sv + jnp.where(head_of_lane == hd, full, 0.0)
        a_ref[rows, :] = (u[rows] * sv).astype(a_ref.dtype)

    cos = cos_ref[...]
    s1 = s1_ref[...]
    s2 = s2_ref[...]
    q0 = 2 * A_WIDTH
    k0 = q0 + B_WIDTH
    v0 = k0 + B_WIDTH
    for j in range(B_WIDTH // LANES):
        lanes = slice(j * LANES, (j + 1) * LANES)
        for base, ref, nat, scale in ((q0, qs_ref, qn_ref, Q_SCALE), (k0, ks_ref, kn_ref, 1.0)):
            xs = proj[:, base + j * LANES: base + (j + 1) * LANES]
            rot = (xs * cos + pltpu.roll(xs, LANES - ROT_DIM // 2, 1) * s1
                   + pltpu.roll(xs, ROT_DIM // 2, 1) * s2) * scale
            nat[:, lanes] = rot.astype(nat.dtype)
            ref[j] = rot
        vs = proj[:, v0 + j * LANES:v0 + (j + 1) * LANES]
        vn_ref[:, lanes] = vs.astype(vn_ref.dtype)
        vs_ref[j] = vs

    for src, stage, outs in ((qs_ref, qt_ref, q_outs), (ks_ref, kt_ref, k_outs), (vs_ref, vt_ref, v_outs)):
        bufs = (src, stage)
        prev = 1
        for idx, (dil, out) in enumerate(zip(copy_dils, outs)):
            chained = prev > 1 and dil % prev == 0
            base_dil = prev if chained else 1
            source = bufs[idx % 2] if chained or idx == 0 else None
            assert source is not None, "copy dilations must form a divisibility chain"
            step = dil // base_dil
            per = tm // dil
            keep = idx + 1 < len(copy_dils)
            for r_prev in range(base_dil):
                for q in range(step):
                    r = r_prev + base_dil * q
                    start = r_prev * (tm // base_dil) + q
                    for j in range(B_WIDTH // LANES):
                        val = source[j, pl.ds(start, per, stride=step), :]
                        out[0, r, :, j * LANES:(j + 1) * LANES] = val.astype(out.dtype)
                        if keep:
                            bufs[(idx + 1) % 2][j, r * per:(r + 1) * per, :] = val
            prev = dil

    s_ref[...] = proj[:, v0 + B_WIDTH:].astype(s_ref.dtype)


def _inproj(x2d, mod3, w_in_bf, cos_t, s1_t, s2_t, ln_g, ln_b, ws, bs_full, bsz, t):
    n, d = x2d.shape
    tm = min(ROW_TILE, t)
    nt = t // tm
    pw = w_in_bf.shape[1]
    c_width = pw - 2 * A_WIDTH - 3 * B_WIDTH
    tok = lambda b, i: (b * nt + i, 0)
    const2 = lambda b, i: (0, 0)
    outs = [jax.ShapeDtypeStruct((n, A_WIDTH), BF16),
            jax.ShapeDtypeStruct((t, bsz * c_width), BF16)]
    out_specs = [pl.BlockSpec((tm, A_WIDTH), tok),
                 pl.BlockSpec((tm, c_width), lambda b, i: (i, b))]
    for _ in range(3):
        outs.append(jax.ShapeDtypeStruct((n, B_WIDTH), BF16))
        out_specs.append(pl.BlockSpec((tm, B_WIDTH), tok))
    copy_dils = [dil for dil in DILATIONS if _dilation_layout(dil) == "copy"]
    for _ in range(3):
        for dil in copy_dils:
            outs.append(jax.ShapeDtypeStruct((bsz, dil, t // dil, B_WIDTH), BF16))
            out_specs.append(pl.BlockSpec((1, dil, tm // dil, B_WIDTH), lambda b, i: (b, 0, i, 0)))
    res = pl.pallas_call(
        _inproj_kernel,
        grid=(bsz, nt),
        in_specs=[pl.BlockSpec((tm, d), tok),
                  pl.BlockSpec((1, 1, d), lambda b, i: (b, 0, 0)),
                  pl.BlockSpec((1, 1, d), lambda b, i: (b, 0, 1)),
                  pl.BlockSpec((d, pw), const2),
                  pl.BlockSpec((tm, LANES), tok),
                  pl.BlockSpec((tm, LANES), tok),
                  pl.BlockSpec((tm, LANES), tok),
                  pl.BlockSpec((1, A_WIDTH), const2),
                  pl.BlockSpec((1, A_WIDTH), const2),
                  pl.BlockSpec((A_HEADS, CHUNK, CHUNK), lambda b, i: (0, 0, 0)),
                  pl.BlockSpec((CHUNK, A_WIDTH), const2)],
        out_specs=out_specs,
        out_shape=outs,
        scratch_shapes=[pltpu.VMEM((B_WIDTH // LANES, tm, LANES), F32)] * 6,
        compiler_params=_cparams(("arbitrary", "arbitrary")),
        name="inproj_gmlp_rope",
    )(x2d, mod3, mod3, w_in_bf, cos_t, s1_t, s2_t, ln_g, ln_b, ws, bs_full)
    a_out, s_in = res[0], res[1]
    natural = res[2:5]
    n_copy = len(copy_dils)
    copies = {dil: tuple(res[5 + z * n_copy + g] for z in range(3)) for g, dil in enumerate(copy_dils)}
    qkv = [copies[dil] if dil in copies else natural for dil in DILATIONS]
    return a_out, s_in, qkv


def _att_kernel(q_ref, k_ref, v_ref, o_ref, st_ref, kcat_ref, vaug_ref):
    i = pl.program_id(2)
    blk = ATT_BLK
    n_res = q_ref.shape[1]
    n_sub = q_ref.shape[2] // blk
    n_pairs = B_WIDTH // LANES
    last = slice(n_sub * blk, (n_sub + 1) * blk)

    @pl.when(i == 0)
    def _():
        kcat_ref[:, 0:blk, :] = jnp.zeros((n_res, blk, B_WIDTH), BF16)
        vaug_ref[:, :, 0:blk, 0:LANES] = jnp.zeros((n_res, n_pairs, blk, LANES), BF16)
        vaug_ref[:, :, :, LANES:] = jnp.ones((n_res, n_pairs, (n_sub + 1) * blk, LANES), BF16)

    @pl.when(i > 0)
    def _():
        kcat_ref[:, 0:blk, :] = kcat_ref[:, last, :]
        vaug_ref[:, :, 0:blk, 0:LANES] = vaug_ref[:, :, last, 0:LANES]

    for rr in range(n_res):
        kcat_ref[rr, blk:, :] = k_ref[0, rr]
        for pair in range(n_pairs):
            vaug_ref[rr, pair, blk:, 0:LANES] = v_ref[0, rr, :, pair * LANES:(pair + 1) * LANES]

    qi = lax.broadcasted_iota(jnp.int32, (blk, 2 * blk), 0)
    kk = lax.broadcasted_iota(jnp.int32, (blk, 2 * blk), 1)
    band = (kk >= qi) & (kk <= qi + blk)
    first = band & ((i > 0) | (kk >= blk))
    lane = lax.broadcasted_iota(jnp.int32, (blk, LANES), 1)
    low_half = lane < B_HEAD_DIM
    nt = (((1,), (1,)), ((), ()))
    for rr, sub in [(rr, sub) for rr in range(n_res) for sub in range(n_sub)]:
        rows = slice(sub * blk, (sub + 1) * blk)
        keys = slice(sub * blk, (sub + 2) * blk)
        mask = first if sub == 0 else band
        stats = jnp.zeros((blk, LANES), F32)
        for pair in range(n_pairs):
            lanes = slice(pair * LANES, (pair + 1) * LANES)
            q2 = q_ref[0, rr, rows, lanes]
            kslab = kcat_ref[rr, keys, lanes]
            res = []
            for hh in range(2):
                hd = 2 * pair + hh
                qm = jnp.where(low_half if hh == 0 else jnp.logical_not(low_half), q2, jnp.zeros_like(q2))
                s = lax.dot_general(qm, kslab, nt, preferred_element_type=F32)
                s = jnp.where(mask, s, NEG_INF)
                m = jnp.max(jnp.maximum(s[:, :blk], s[:, blk:]), axis=1, keepdims=True)
                p = jnp.exp2(s - m).astype(BF16)
                ov = jnp.dot(p, vaug_ref[rr, pair, keys, :], preferred_element_type=F32)
                den = ov[:, LANES:]
                res.append(ov[:, :LANES] / den)
                stats = jnp.where(lane == hd, m, stats)
                stats = jnp.where(lane == B_HEADS + hd, den, stats)
            o_ref[0, rr, rows, lanes] = jnp.where(low_half, res[0], res[1]).astype(o_ref.dtype)
        st_ref[0, rr, rows, :] = stats


def _dilation_layout(dil):
    return "natural" if dil == 1 else "copy"


def _att_branch(q, k, v, dil, bsz, t):
    rows = t // dil
    n_sub = min(ATT_SUB_BLOCKS, rows // ATT_BLK)
    step_rows = n_sub * ATT_BLK
    n_steps = rows // step_rows
    n_res = math.gcd(dil, max(1, ATT_SUB_BLOCKS // n_sub)) if n_steps == 1 else 1
    shape = lambda w: (bsz, dil, rows, w)
    blk = lambda w: pl.BlockSpec((1, n_res, step_rows, w), lambda b, r, i: (b, r, i, 0))
    q, k, v = (z.reshape(shape(B_WIDTH)) for z in (q, k, v))
    return pl.pallas_call(
        _att_kernel,
        grid=(bsz, dil // n_res, n_steps),
        in_specs=[blk(B_WIDTH), blk(B_WIDTH), blk(B_WIDTH)],
        out_specs=[blk(B_WIDTH), blk(LANES)],
        out_shape=[jax.ShapeDtypeStruct(shape(B_WIDTH), BF16),
                   jax.ShapeDtypeStruct(shape(LANES), F32)],
        scratch_shapes=[pltpu.VMEM((n_res, step_rows + ATT_BLK, B_WIDTH), BF16),
                        pltpu.VMEM((n_res, B_WIDTH // LANES, step_rows + ATT_BLK, 2 * LANES), BF16)],
        compiler_params=_cparams(("arbitrary", "arbitrary", "arbitrary")),
        name=f"dilated_attn_d{dil}",
    )(q, k, v)


def _ssm_kernel(u_ref, bmat_ref, cmat_ref, are_ref, aim_ref, dskip_ref, gw_ref, gb_ref,
                o_ref, state_ref, xbuf_ref, rows_ref):
    bsz, two_n = state_ref.shape
    n_state = two_n // 2
    steps = u_ref.shape[0]
    cw = u_ref.shape[1] // bsz
    n_slab = cw // LANES

    @pl.when(pl.program_id(0) == 0)
    def _():
        state_ref[...] = jnp.zeros_like(state_ref)

    for b in range(bsz):
        for j in range(n_slab):
            rows_ref[j, pl.ds(b, steps, stride=bsz), :] = (
                u_ref[:, b * cw + j * LANES:b * cw + (j + 1) * LANES].astype(F32))
    u32 = jnp.concatenate([rows_ref[j] for j in range(n_slab)], axis=1)
    xbuf_ref[...] = jnp.dot(u32.astype(BF16), bmat_ref[...], preferred_element_type=F32)
    a_re = jnp.broadcast_to(are_ref[...], (bsz, n_state))
    a_im = jnp.broadcast_to(aim_ref[...], (bsz, n_state))

    def step(tt, carry):
        x_re, x_im = carry
        r0 = pl.multiple_of(tt * bsz, bsz)
        b_re = xbuf_ref[pl.ds(r0, bsz), :n_state]
        b_im = xbuf_ref[pl.ds(r0, bsz), n_state:]
        n_re = a_re * x_re - a_im * x_im + b_re
        n_im = a_re * x_im + a_im * x_re + b_im
        xbuf_ref[pl.ds(r0, bsz), :n_state] = n_re
        xbuf_ref[pl.ds(r0, bsz), n_state:] = n_im
        return n_re, n_im

    x_re, x_im = lax.fori_loop(0, steps, step, (state_ref[:, :n_state], state_ref[:, n_state:]))
    state_ref[:, :n_state] = x_re
    state_ref[:, n_state:] = x_im

    y = jnp.dot(xbuf_ref[...].astype(BF16), cmat_ref[...], preferred_element_type=F32)
    y = _gelu_tanh(y + dskip_ref[...] * u32)
    gate = jax.nn.sigmoid(jnp.dot(y.astype(BF16), gw_ref[...], preferred_element_type=F32) + gb_ref[...])
    out = y * gate
    for j in range(n_slab):
        rows_ref[j] = out[:, j * LANES:(j + 1) * LANES]
    for b in range(bsz):
        for j in range(n_slab):
            o_ref[:, b * cw + j * LANES:b * cw + (j + 1) * LANES] = (
                rows_ref[j, pl.ds(b, steps, stride=bsz), :].astype(o_ref.dtype))


def _ssm_weights(lam_re, lam_im, log_dt, b_re, b_im, c_re, c_im):
    g, n_st = lam_re.shape
    cg = b_re.shape[-1]
    dt = jnp.exp(log_dt)[:, None]
    mag = jnp.exp(lam_re * dt)
    ab_re = mag * jnp.cos(lam_im * dt)
    ab_im = mag * jnp.sin(lam_im * dt)
    nr = ab_re - 1.0
    ni = ab_im
    mod2 = lam_re * lam_re + lam_im * lam_im
    f_re = (nr * lam_re + ni * lam_im) / mod2
    f_im = (ni * lam_re - nr * lam_im) / mod2
    bb_re = f_re[..., None] * b_re - f_im[..., None] * b_im
    bb_im = f_re[..., None] * b_im + f_im[..., None] * b_re
    eye = jnp.eye(g, dtype=F32)
    bm_re = jnp.einsum('gnc,gh->gchn', bb_re, eye).reshape(g * cg, g * n_st)
    bm_im = jnp.einsum('gnc,gh->gchn', bb_im, eye).reshape(g * cg, g * n_st)
    bmat = jnp.concatenate([bm_re, bm_im], axis=1)
    cm_re = jnp.einsum('gcn,gh->gnhc', c_re, eye).reshape(g * n_st, g * cg)
    cm_im = jnp.einsum('gcn,gh->gnhc', c_im, eye).reshape(g * n_st, g * cg)
    cmat = jnp.concatenate([cm_re, -cm_im], axis=0)
    return (bmat.astype(BF16), cmat.astype(BF16),
            ab_re.reshape(1, g * n_st), ab_im.reshape(1, g * n_st))


def _ssm(u_tm, bmat, cmat, a_re, a_im, d_skip, glu_w_bf, glu_b, bsz, t):
    cw = bmat.shape[0]
    two_n = bmat.shape[1]
    steps = min(SSM_STEPS, t)
    rows = steps * bsz
    const = lambda i: (0, 0)
    return pl.pallas_call(
        _ssm_kernel,
        grid=(t // steps,),
        in_specs=[pl.BlockSpec((steps, bsz * cw), lambda i: (i, 0)),
                  pl.BlockSpec((cw, two_n), const),
                  pl.BlockSpec((two_n, cw), const),
                  pl.BlockSpec((1, two_n // 2), const),
                  pl.BlockSpec((1, two_n // 2), const),
                  pl.BlockSpec((1, cw), const),
                  pl.BlockSpec((cw, cw), const),
                  pl.BlockSpec((1, cw), const)],
        out_specs=pl.BlockSpec((steps, bsz * cw), lambda i: (i, 0)),
        out_shape=jax.ShapeDtypeStruct((t, bsz * cw), BF16),
        scratch_shapes=[pltpu.VMEM((bsz, two_n), F32), pltpu.VMEM((rows, two_n), F32),
                        pltpu.VMEM((cw // LANES, rows, LANES), F32)],
        compiler_params=_cparams(("arbitrary",)),
        name="s5_scan_glu",
    )(u_tm, bmat, cmat, a_re, a_im, d_skip, glu_w_bf, glu_b)


def _route(sel, scores):
    gs = []
    for g in range(N_EXPERT_GROUPS):
        a, b, c, d = sel[EXPERTS_PER_GROUP * g: EXPERTS_PER_GROUP * (g + 1)]
        hi1, lo1 = jnp.maximum(a, b), jnp.minimum(a, b)
        hi2, lo2 = jnp.maximum(c, d), jnp.minimum(c, d)
        gs.append(jnp.maximum(hi1, hi2) + jnp.maximum(jnp.minimum(hi1, hi2), jnp.maximum(lo1, lo2)))
    g_idx = jnp.zeros(gs[0].shape, jnp.int32)
    best = gs[0]
    for g in range(1, N_EXPERT_GROUPS):
        better = gs[g] > best
        g_idx = jnp.where(better, g, g_idx)
        best = jnp.where(better, gs[g], best)

    def pick(rows, j):
        out = rows[j]
        for g in range(1, N_EXPERT_GROUPS):
            out = jnp.where(g_idx == g, rows[EXPERTS_PER_GROUP * g + j], out)
        return out

    v = [pick(sel, j) for j in range(EXPERTS_PER_GROUP)]
    s = [pick(scores, j) for j in range(EXPERTS_PER_GROUP)]
    i1 = jnp.zeros(g_idx.shape, jnp.int32)
    b1, g1 = v[0], s[0]
    for j in range(1, EXPERTS_PER_GROUP):
        better = v[j] > b1
        i1 = jnp.where(better, j, i1)
        b1 = jnp.where(better, v[j], b1)
        g1 = jnp.where(better, s[j], g1)
    i2 = jnp.full(g_idx.shape, -1, jnp.int32)
    b2 = jnp.zeros_like(b1)
    g2 = jnp.zeros_like(g1)
    for j in range(EXPERTS_PER_GROUP):
        better = (i1 != j) & ((i2 < 0) | (v[j] > b2))
        i2 = jnp.where(better, j, i2)
        b2 = jnp.where(better, v[j], b2)
        g2 = jnp.where(better, s[j], g2)
    tot = g1 + g2
    base = g_idx * EXPERTS_PER_GROUP
    return base + i1, base + i2, g1 / tot, g2 / tot


def _natural_order(ref, scr_ref, dil):
    if _dilation_layout(dil) != "copy":
        return ref[...].astype(F32)
    per = ref.shape[2]
    n_slab = scr_ref.shape[0]
    for r in range(dil):
        for j in range(n_slab):
            scr_ref[j, pl.ds(r, per, stride=dil), :] = ref[0, r, :, j * LANES:(j + 1) * LANES].astype(F32)
    if n_slab == 1:
        return scr_ref[0]
    return jnp.concatenate([scr_ref[j] for j in range(n_slab)], axis=1)


def _outproj_kernel(a_ref, o1_ref, o2_ref, o3_ref, st1_ref, st2_ref, st3_ref,
                    c_ref, x_ref, g1_ref, sh2_ref, sc2_ref, wout_ref, expand_ref, lng_ref, lnb_ref,
                    rwh_ref, rwl_ref, rb_ref, x1_ref, eidx_ref, gate_ref, tab_ref, xs_hbm,
                    oscr2_ref, oscr3_ref, sscr2_ref, sscr3_ref, hbuf, state_ref, slot_ref, fin_ref, fins_ref,
                    ssem, csem, *, alpha):
    o_refs = (o1_ref, o2_ref, o3_ref)
    st_refs = (st1_ref, st2_ref, st3_ref)
    o_scr = (None, oscr2_ref, oscr3_ref)
    st_scr = (None, sscr2_ref, sscr3_ref)
    ms = [_natural_order(st_refs[g], st_scr[g], DILATIONS[g]) for g in range(len(DILATIONS))]
    dens = [pltpu.roll(m, LANES - B_HEADS, 1) for m in ms]
    mx = jnp.maximum(jnp.maximum(ms[0], ms[1]), ms[2])
    ws = [dens[g] * jnp.exp2(ms[g] - mx) for g in range(len(DILATIONS))]
    lane = lax.broadcasted_iota(jnp.int32, mx.shape, 1)
    tot = jnp.where(lane < B_HEADS, ws[0] + ws[1] + ws[2], 1.0)
    expand = expand_ref[...]

    def widen(w):
        wn = jnp.where(lane < B_HEADS, w / tot, 0.0)
        hi = wn.astype(BF16)
        lo = (wn - hi.astype(F32)).astype(BF16)
        return (jnp.dot(hi, expand, preferred_element_type=F32)
                + jnp.dot(lo, expand, preferred_element_type=F32))

    b_out = widen(ws[0]) * _natural_order(o_refs[0], o_scr[0], DILATIONS[0])
    for g in range(1, len(DILATIONS)):
        b_out = b_out + widen(ws[g]) * _natural_order(o_refs[g], o_scr[g], DILATIONS[g])
    b_out = b_out.astype(BF16)

    wout = wout_ref
    mix = (jnp.dot(a_ref[...], wout[:A_WIDTH, :], preferred_element_type=F32)
           + jnp.dot(b_out, wout[A_WIDTH:A_WIDTH + B_WIDTH, :], preferred_element_type=F32)
           + jnp.dot(c_ref[...], wout[A_WIDTH + B_WIDTH:, :], preferred_element_type=F32))
    x1 = _ln(alpha * x_ref[...] + (1.0 + g1_ref[0]) * mix) * lng_ref[...] + lnb_ref[...]
    x1_ref[...] = x1
    h2 = _ln(x1) * (1.0 + sc2_ref[0]) + sh2_ref[0]

    hi = h2.astype(BF16)
    lo = (h2 - hi.astype(F32)).astype(BF16)
    nt = (((1,), (1,)), ((), ()))
    logits = (lax.dot_general(rwh_ref[...], hi, nt, preferred_element_type=F32)
              + lax.dot_general(rwh_ref[...], lo, nt, preferred_element_type=F32)
              + lax.dot_general(rwl_ref[...], hi, nt, preferred_element_type=F32))
    scores = jax.nn.sigmoid(logits)
    sel = scores + rb_ref[...]
    sel_rows = [sel[e:e + 1, :] for e in range(N_EXPERTS)]
    score_rows = [scores[e:e + 1, :] for e in range(N_EXPERTS)]
    e1, e2, gt1, gt2 = _route(sel_rows, score_rows)

    g = pl.program_id(0) * pl.num_programs(1) + pl.program_id(1)
    n_tiles = pl.num_programs(0) * pl.num_programs(1)
    tm = h2.shape[0]
    n_slab = h2.shape[1] // LANES
    blk = MOE_BLK
    par = g % 2

    @pl.when(g == 0)
    def _():
        state_ref[...] = jnp.zeros_like(state_ref)
        tab_ref[...] = jnp.full(tab_ref.shape, float(N_EXPERTS), F32)

    erow = lax.broadcasted_iota(jnp.int32, (N_EXPERTS, tm), 0)
    oh1 = erow == e1
    oh2 = erow == e2
    both = (oh1 | oh2).astype(BF16)
    earlier = (lax.broadcasted_iota(jnp.int32, (tm, tm), 0)
               < lax.broadcasted_iota(jnp.int32, (tm, tm), 1)).astype(BF16)
    base = state_ref[0, :, 0:1]
    cur = state_ref[1, :, 0:1]
    nfree = state_ref[2, :, 0:1]
    n_vec = jnp.sum(both.astype(F32), axis=1, keepdims=True)
    off = base - jnp.floor(base * (1.0 / blk)) * blk
    need_a = (off == 0.0) & (n_vec > 0.0)
    need_b = (off + n_vec) > float(blk)
    na = need_a.astype(F32)
    nb = need_b.astype(F32)
    alloc = na + nb
    lower = (lax.broadcasted_iota(jnp.int32, (N_EXPERTS, N_EXPERTS), 1)
             < lax.broadcasted_iota(jnp.int32, (N_EXPERTS, N_EXPERTS), 0)).astype(BF16)
    alloc_wide = jnp.broadcast_to(alloc, (N_EXPERTS, LANES))
    before = jnp.dot(lower, alloc_wide.astype(BF16), preferred_element_type=F32)[:, 0:1]
    a_id = jnp.where(need_a, nfree + before, cur)
    b_id = jnp.where(need_b, nfree + before + na, a_id)
    bound = base - off + float(blk)
    total_alloc = jnp.sum(alloc_wide, axis=0, keepdims=True)[:, 0:1]
    state_ref[0] = jnp.broadcast_to(base + n_vec, (N_EXPERTS, LANES))
    state_ref[1] = jnp.broadcast_to(jnp.where(need_b, b_id, a_id), (N_EXPERTS, LANES))
    state_ref[2] = jnp.broadcast_to(nfree + total_alloc, (N_EXPERTS, LANES))

    blane = lax.broadcasted_iota(jnp.int32, (N_EXPERTS, tab_ref.shape[1]), 1).astype(F32)
    hit = ((blane == a_id) & need_a) | ((blane == b_id) & need_b)
    e_col = lax.broadcasted_iota(jnp.int32, (N_EXPERTS, 1), 0).astype(F32)
    delta = jnp.sum(jnp.where(hit, e_col - float(N_EXPERTS), 0.0), axis=0, keepdims=True)
    tab_ref[...] = tab_ref[...] + jnp.broadcast_to(delta, tab_ref.shape)

    rank = jnp.dot(both, earlier, preferred_element_type=F32) + base
    slot = jnp.where(rank < bound, a_id * blk + (rank - bound + blk), b_id * blk + (rank - bound))
    s1 = jnp.sum(jnp.where(oh1, slot, 0.0), axis=0, keepdims=True)
    s2 = jnp.sum(jnp.where(oh2, slot, 0.0), axis=0, keepdims=True)

    eidx_ref[...] = jnp.zeros_like(eidx_ref)
    gate_ref[...] = jnp.zeros_like(gate_ref)
    eidx_ref[0:1, :] = e1
    eidx_ref[1:2, :] = e2
    eidx_ref[2:3, :] = s1.astype(jnp.int32)
    eidx_ref[3:4, :] = s2.astype(jnp.int32)
    gate_ref[0:1, :] = gt1
    gate_ref[1:2, :] = gt2

    slot_copy = pltpu.make_async_copy(eidx_ref, slot_ref, csem)
    slot_copy.start()

    def wait_rows(p):
        for _ in range(TOP_K):
            pltpu.make_async_copy(hbuf.at[p], xs_hbm.at[pl.ds(0, tm * n_slab), :], ssem.at[p]).wait()

    @pl.when(g >= 2)
    def _():
        wait_rows(par)

    _store_tile_rows(hbuf.at[par], h2)
    slot_copy.wait()

    def issue(s, c):
        src = hbuf.at[par, pl.ds(pl.multiple_of(s * n_slab, n_slab), n_slab), :]
        for kk in range(TOP_K):
            dst = pl.multiple_of(slot_ref[TOP_K + kk, s] * n_slab, n_slab)
            pltpu.make_async_copy(src, xs_hbm.at[pl.ds(dst, n_slab), :], ssem.at[par]).start()
        return c
    lax.fori_loop(0, tm, issue, 0, unroll=8)

    @pl.when(g == n_tiles - 1)
    def _():
        wait_rows(par)

        @pl.when(n_tiles >= 2)
        def _():
            wait_rows(1 - par)

        lane_e = lax.broadcasted_iota(jnp.int32, (N_EXPERTS, LANES), 1)
        sub_e = lax.broadcasted_iota(jnp.int32, (N_EXPERTS, LANES), 0)

        def by_lane(col):
            return jnp.sum(jnp.where(lane_e == sub_e, col, 0.0), axis=0, keepdims=True)

        end = base + n_vec
        end_off = end - jnp.floor(end * (1.0 / blk)) * blk
        fin_ref[...] = jnp.zeros_like(fin_ref)
        fin_ref[0:1, :] = by_lane(jnp.where(need_b, b_id, a_id)).astype(jnp.int32)
        fin_ref[1:2, :] = by_lane(end_off).astype(jnp.int32)
        fin_ref[2:3, :] = by_lane(nfree + total_alloc).astype(jnp.int32)
        fin_copy = pltpu.make_async_copy(fin_ref, fins_ref, csem)
        fin_copy.start()
        fin_copy.wait()

        hbuf[0] = jnp.zeros(hbuf.shape[1:], F32)

        def zero_rows(first_row, n_rows):
            cp = pltpu.make_async_copy(hbuf.at[0, pl.ds(0, n_rows * n_slab), :],
                                       xs_hbm.at[pl.ds(pl.multiple_of(first_row * n_slab, n_slab),
                                                       n_rows * n_slab), :], ssem.at[0])
            cp.start()
            cp.wait()

        for e in range(N_EXPERTS):
            pos = fins_ref[1, e]
            start = fins_ref[0, e] * blk
            for k in range(blk.bit_length() - 1):
                take = (fins_ref[1, e] > 0) & (((pos >> k) & 1) == 1)

                @pl.when(take)
                def _():
                    zero_rows(start + pos, 1 << k)
                pos = jnp.where(take, pos + (1 << k), pos)

        n_blk = xs_hbm.shape[0] // (blk * n_slab)
        for t in range(N_EXPERTS):
            spare = fins_ref[2, 0] + t

            @pl.when(spare < n_blk)
            def _():
                zero_rows(spare * blk, blk)


def _outproj(a_out, att, c_tm, x2d, mod3, w_out_bf, ln_g, ln_b, rw_hi, rw_lo, rbias, bsz, t, alpha):
    n, d = x2d.shape
    tm = min(ROW_TILE, t)
    assert tm == MOE_BLK and MOE_BLK & (MOE_BLK - 1) == 0
    nt = t // tm
    n_slab = d // LANES
    n_blk = n * TOP_K // MOE_BLK + N_EXPERTS
    tab_w = -(-n_blk // LANES) * LANES
    cw = c_tm.shape[1] // bsz
    tok = lambda b, i: (b * nt + i, 0)
    const2 = lambda b, i: (0, 0)
    head = jnp.arange(B_WIDTH) // B_HEAD_DIM
    expand = (jnp.arange(LANES)[:, None] == head[None, :]).astype(BF16)

    def branch_input(z, dil, w):
        if _dilation_layout(dil) == "copy":
            return z, pl.BlockSpec((1, dil, tm // dil, w), lambda b, i: (b, 0, i, 0))
        return z.reshape(n, w), pl.BlockSpec((tm, w), tok)

    o_in = [branch_input(o, dil, B_WIDTH) for (o, _), dil in zip(att, DILATIONS)]
    st_in = [branch_input(st, dil, LANES) for (_, st), dil in zip(att, DILATIONS)]
    (o1, st1), (o2, st2), (o3, st3) = [(o_in[g][0], st_in[g][0]) for g in range(len(DILATIONS))]
    modspec = lambda j: pl.BlockSpec((1, 1, d), lambda b, i: (b, 0, j))
    return pl.pallas_call(
        functools.partial(_outproj_kernel, alpha=alpha),
        grid=(bsz, nt),
        in_specs=[pl.BlockSpec((tm, A_WIDTH), tok)]
                 + [spec for _, spec in o_in]
                 + [spec for _, spec in st_in]
                 + [pl.BlockSpec((tm, cw), lambda b, i: (i, b)),
                  pl.BlockSpec((tm, d), tok),
                  modspec(2), modspec(3), modspec(4),
                  pl.BlockSpec((d, d), const2),
                  pl.BlockSpec((LANES, B_WIDTH), const2),
                  pl.BlockSpec((1, d), const2), pl.BlockSpec((1, d), const2),
                  pl.BlockSpec((N_EXPERTS, d), const2), pl.BlockSpec((N_EXPERTS, d), const2),
                  pl.BlockSpec((N_EXPERTS, 1), const2)],
        out_specs=[pl.BlockSpec((tm, d), tok),
                   pl.BlockSpec((SUBLANES, tm), lambda b, i: (0, b * nt + i)),
                   pl.BlockSpec((SUBLANES, tm), lambda b, i: (0, b * nt + i)),
                   pl.BlockSpec((SUBLANES, tab_w), const2),
                   pl.BlockSpec(memory_space=pl.ANY)],
        out_shape=[jax.ShapeDtypeStruct((n, d), F32),
                   jax.ShapeDtypeStruct((SUBLANES, n), jnp.int32),
                   jax.ShapeDtypeStruct((SUBLANES, n), F32),
                   jax.ShapeDtypeStruct((SUBLANES, tab_w), F32),
                   jax.ShapeDtypeStruct((n_blk * MOE_BLK * n_slab, LANES), F32)],
        scratch_shapes=[pltpu.VMEM((B_WIDTH // LANES, tm, LANES), F32),
                        pltpu.VMEM((B_WIDTH // LANES, tm, LANES), F32),
                        pltpu.VMEM((1, tm, LANES), F32), pltpu.VMEM((1, tm, LANES), F32),
                        pltpu.VMEM((2, tm * n_slab, LANES), F32),
                        pltpu.VMEM((3, N_EXPERTS, LANES), F32),
                        pltpu.SMEM((SUBLANES, tm), jnp.int32),
                        pltpu.VMEM((SUBLANES, LANES), jnp.int32),
                        pltpu.SMEM((SUBLANES, LANES), jnp.int32),
                        pltpu.SemaphoreType.DMA((2,)), pltpu.SemaphoreType.DMA],
        compiler_params=_cparams(("arbitrary", "arbitrary")),
        name="outproj_ln_router",
    )(a_out, o1, o2, o3, st1, st2, st3, c_tm, x2d, mod3, mod3, mod3, w_out_bf, expand,
      ln_g, ln_b, rw_hi, rw_lo, rbias)


def _block_order(tab_row, n_blk):
    tab = tab_row[:n_blk].astype(jnp.int32)
    idx = jnp.arange(n_blk, dtype=jnp.int32)
    before = (tab[None, :] < tab[:, None]) | ((tab[None, :] == tab[:, None]) & (idx[None, :] < idx[:, None]))
    pos = jnp.sum(before.astype(jnp.int32), axis=1)
    at_pos = pos[None, :] == idx[:, None]
    perm = jnp.sum(jnp.where(at_pos, idx[None, :], 0), axis=1).astype(jnp.int32)
    blk_e = jnp.minimum(jnp.sum(jnp.where(at_pos, tab[None, :], 0), axis=1), N_EXPERTS - 1).astype(jnp.int32)
    n_used = jnp.sum((tab < N_EXPERTS).astype(jnp.int32)).reshape(1)
    return perm, blk_e, n_used


def _expert_kernel(blk_e_ref, nused_ref, perm_ref, xs_ref, wg_ref, wu_ref, wd_ref, ys_ref, wg_bf, wu_bf, wd_bf,
                   *, blk, n_slab):
    j = pl.program_id(0)

    @pl.when((j == 0) | (blk_e_ref[j] != blk_e_ref[jnp.maximum(j - 1, 0)]))
    def _():
        wg_bf[...] = wg_ref[0, 0].astype(BF16)
        wu_bf[...] = wu_ref[0, 0].astype(BF16)
        wd_bf[...] = wd_ref[0, 0].astype(BF16)

    @pl.when(j < nused_ref[0])
    def _():
        rows = _load_tile_rows(xs_ref, 0, blk, n_slab).astype(BF16)
        gt = jnp.dot(rows, wg_bf[...], preferred_element_type=F32)
        up = jnp.dot(rows, wu_bf[...], preferred_element_type=F32)
        hid = (gt * jax.nn.sigmoid(gt) * up).astype(BF16)
        _store_tile_rows(ys_ref, jnp.dot(hid, wd_bf[...], preferred_element_type=F32))

    @pl.when(j >= nused_ref[0])
    def _():
        ys_ref[...] = jnp.zeros_like(ys_ref)


def _moe(xs, tab, w_gate, w_up, w_down, layer):
    d, de = w_gate.shape[-2:]
    n_slab = d // LANES
    blk = MOE_BLK
    n_blk = xs.shape[0] // (blk * n_slab)
    perm, blk_e, n_used = _block_order(tab[0], n_blk)

    def x_block(j, be, nu, pm):
        return (pm[jnp.minimum(j, nu[0] - 1)], 0)

    return pl.pallas_call(
        functools.partial(_expert_kernel, blk=blk, n_slab=n_slab),
        grid_spec=pltpu.PrefetchScalarGridSpec(
            num_scalar_prefetch=3,
            grid=(n_blk,),
            in_specs=[pl.BlockSpec((blk * n_slab, LANES), x_block),
                      pl.BlockSpec((1, 1, d, de), lambda j, be, nu, pm: (layer, be[j], 0, 0)),
                      pl.BlockSpec((1, 1, d, de), lambda j, be, nu, pm: (layer, be[j], 0, 0)),
                      pl.BlockSpec((1, 1, de, d), lambda j, be, nu, pm: (layer, be[j], 0, 0))],
            out_specs=pl.BlockSpec((blk * n_slab, LANES), lambda j, be, nu, pm: (pm[j], 0)),
            scratch_shapes=[pltpu.VMEM((d, de), BF16), pltpu.VMEM((d, de), BF16), pltpu.VMEM((de, d), BF16)],
        ),
        out_shape=jax.ShapeDtypeStruct(xs.shape, F32),
        compiler_params=_cparams(("arbitrary",)),
        name="moe_experts",
    )(blk_e, n_used, perm, xs, w_gate, w_up, w_down)


def _ffn_ln_kernel(dest_ref, next_ref, x_ref, gate_ref, g2_ref, lng_ref, lnb_ref, ys_hbm, o_ref, ybuf, sem,
                   *, alpha):
    tm, d = x_ref.shape
    n_slab = d // LANES
    g = pl.program_id(0) * pl.num_programs(1) + pl.program_id(1)
    n_tiles = pl.num_programs(0) * pl.num_programs(1)
    slot = g % 2

    def gather(idx_ref, buf_slot):
        def issue(s, c):
            src = pl.multiple_of(idx_ref[0, 0, s] * n_slab, n_slab)
            dst = pl.multiple_of(s * n_slab, n_slab)
            pltpu.make_async_copy(ys_hbm.at[pl.ds(src, n_slab), :],
                                  ybuf.at[buf_slot, pl.ds(dst, n_slab), :], sem.at[buf_slot]).start()
            return c
        lax.fori_loop(0, TOP_K * tm, issue, 0, unroll=8)

    @pl.when(g == 0)
    def _():
        gather(dest_ref, 0)

    @pl.when(g + 1 < n_tiles)
    def _():
        gather(next_ref, 1 - slot)

    pltpu.make_async_copy(ys_hbm.at[pl.ds(0, TOP_K * tm * n_slab), :], ybuf.at[slot], sem.at[slot]).wait()

    gate = gate_ref[...]
    buf = ybuf.at[slot]
    ffn = _load_tile_rows(buf, 0, tm, n_slab) * gate[:, 0:1]
    for kk in range(1, TOP_K):
        ffn = ffn + _load_tile_rows(buf, kk * tm, tm, n_slab) * gate[:, kk:kk + 1]
    o_ref[...] = _ln(alpha * x_ref[...] + (1.0 + g2_ref[0]) * ffn) * lng_ref[...] + lnb_ref[...]


def _ffn_ln(x1, ys, dest, gates, mod3, ln_g, ln_b, bsz, t, alpha):
    n, d = x1.shape
    n_slab = d // LANES
    tm = min(ROW_TILE, t)
    nt = t // tm
    n_tiles = bsz * nt
    dest_tiles = dest.reshape(TOP_K, n_tiles, tm).transpose(1, 0, 2).reshape(n_tiles, 1, TOP_K * tm)
    tok = lambda b, i, *_: (b * nt + i, 0)
    const2 = lambda b, i, *_: (0, 0)
    gate_cols = gates[:TOP_K].T
    idx_blk = (1, 1, TOP_K * tm)
    return pl.pallas_call(
        functools.partial(_ffn_ln_kernel, alpha=alpha),
        grid=(bsz, nt),
        in_specs=[pl.BlockSpec(idx_blk, lambda b, i: (b * nt + i, 0, 0), memory_space=pltpu.SMEM),
                  pl.BlockSpec(idx_blk, lambda b, i: (jnp.minimum(b * nt + i + 1, n_tiles - 1), 0, 0),
                               memory_space=pltpu.SMEM),
                  pl.BlockSpec((tm, d), tok), pl.BlockSpec((tm, TOP_K), tok),
                  pl.BlockSpec((1, 1, d), lambda b, i: (b, 0, 5)),
                  pl.BlockSpec((1, d), const2), pl.BlockSpec((1, d), const2),
                  pl.BlockSpec(memory_space=pl.ANY)],
        out_specs=pl.BlockSpec((tm, d), tok),
        out_shape=jax.ShapeDtypeStruct((n, d), F32),
        scratch_shapes=[pltpu.VMEM((2, TOP_K * tm * n_slab, LANES), F32), pltpu.SemaphoreType.DMA((2,))],
        compiler_params=_cparams(("arbitrary", "arbitrary")),
        name="ffn_combine_ln",
    )(dest_tiles, dest_tiles, x1, gate_cols, mod3, ln_g, ln_b, ys)


def kernel(x, c, positions, ada_w, ada_b, w_in, gm_ln_g, gm_ln_b, gm_ws, gm_bs, ssm_lam_re, ssm_lam_im, ssm_log_dt, ssm_b_re, ssm_b_im, ssm_c_re, ssm_c_im, ssm_d, glu_w, glu_b, w_out, ln1_g, ln1_b, router_w, router_bias, exp_w_gate, exp_w_up, exp_w_down, ln2_g, ln2_b):
    bsz, t, d = x.shape
    depth = ada_w.shape[0]
    alpha = (2.0 * depth) ** 0.25
    n = bsz * t

    mod = _adaln_mod(c.astype(F32), ada_w, ada_b)
    cos_t, s1_t, s2_t = _rope_tables(positions)
    rw_t = router_w.astype(F32).T
    rw_hi = rw_t.astype(BF16)
    rw_lo = (rw_t - rw_hi.astype(F32)).astype(BF16)
    rbias = router_bias.astype(F32).reshape(N_EXPERTS, 1)

    xf = x.astype(F32).reshape(n, d)
    for l in range(depth):
        mod3 = mod[l].reshape(bsz, 1, 6 * d)
        bs_full = jnp.repeat(gm_bs[l].T, A_HEAD_DIM, axis=1)
        a_out, s_in, qkv = _inproj(
            xf, mod3, w_in[l].astype(BF16), cos_t, s1_t, s2_t,
            gm_ln_g[l].reshape(1, A_WIDTH), gm_ln_b[l].reshape(1, A_WIDTH), gm_ws[l], bs_full, bsz, t)
        att = [_att_branch(*qkv[g], dil, bsz, t) for g, dil in enumerate(DILATIONS)]
        bmat, cmat, a_re, a_im = _ssm_weights(ssm_lam_re[l], ssm_lam_im[l], ssm_log_dt[l],
                                              ssm_b_re[l], ssm_b_im[l], ssm_c_re[l], ssm_c_im[l])
        cw = bmat.shape[0]
        c_out = _ssm(s_in, bmat, cmat, a_re, a_im, ssm_d[l].reshape(1, cw),
                     glu_w[l].astype(BF16), glu_b[l].reshape(1, cw), bsz, t)
        x1, eidx, gates, tab, xs = _outproj(a_out, att, c_out, xf, mod3, w_out[l].astype(BF16),
                                            ln1_g[l].reshape(1, d), ln1_b[l].reshape(1, d),
                                            rw_hi, rw_lo, rbias, bsz, t, alpha)
        ys = _moe(xs, tab, exp_w_gate.astype(F32), exp_w_up.astype(F32), exp_w_down.astype(F32), l)
        xf = _ffn_ln(x1, ys, eidx[TOP_K:2 * TOP_K], gates, mod3, ln2_g[l].reshape(1, d), ln2_b[l].reshape(1, d),
                     bsz, t, alpha)
    return xf.reshape(bsz, t, d)
```

```python
import functools
import math

import jax
import jax.numpy as jnp
from jax import lax
from jax.experimental import pallas as pl
from jax.experimental.pallas import tpu as pltpu

F32 = jnp.float32
BF16 = jnp.bfloat16

A_HEADS = 4
A_HEAD_DIM = 64
A_WIDTH = A_HEADS * A_HEAD_DIM
CHUNK = 128
B_HEADS = 8
B_HEAD_DIM = 64
B_WIDTH = B_HEADS * B_HEAD_DIM
DILATIONS = (1, 4, 16)
ATT_BLK = 128
ROT_DIM = B_HEAD_DIM // 4
ROPE_THETA = 500000.0
N_EXPERTS = 16
N_EXPERT_GROUPS = 4
EXPERTS_PER_GROUP = 4
TOP_K = 2
LN_EPS = 1e-5
NEG_INF = -1e30
Q_SCALE = B_HEAD_DIM ** -0.5 * math.log2(math.e)

LANES = 128
SUBLANES = 8
VMEM_LIMIT = 56 * 1024 * 1024
ROW_TILE = 512
ROPE_ROWS = 2048
SSM_STEPS = 128
MOE_BLK = 512
ATT_SUB_BLOCKS = 8


def _cparams(sem):
    return pltpu.CompilerParams(dimension_semantics=sem, vmem_limit_bytes=VMEM_LIMIT)


def _ln(x):
    mu = jnp.mean(x, axis=-1, keepdims=True)
    xc = x - mu
    var = jnp.mean(xc * xc, axis=-1, keepdims=True)
    return xc * lax.rsqrt(var + LN_EPS)


def _store_tile_rows(ref, val):
    n_slab = val.shape[1] // LANES
    for j in range(n_slab):
        ref[pl.ds(j, val.shape[0], stride=n_slab), :] = val[:, j * LANES:(j + 1) * LANES]


def _load_tile_rows(ref, start, rows, n_slab):
    return jnp.concatenate([ref[pl.ds(start * n_slab + j, rows, stride=n_slab), :] for j in range(n_slab)],
                           axis=1)


def _gelu_tanh(x):
    return 0.5 * x * (1.0 + jnp.tanh(math.sqrt(2.0 / math.pi) * (x + 0.044715 * (x * x * x))))


def _mod_kernel(c_ref, w_ref, b_ref, o_ref):
    c = c_ref[...]
    cond = c * jax.nn.sigmoid(c)
    o_ref[0] = jnp.dot(cond, w_ref[0], precision=lax.Precision.HIGHEST,
                       preferred_element_type=F32) + b_ref[0]


def _adaln_mod(c, ada_w, ada_b):
    depth, d, n = ada_w.shape
    bsz = c.shape[0]
    nb = d
    return pl.pallas_call(
        _mod_kernel,
        grid=(depth, n // nb),
        in_specs=[pl.BlockSpec((bsz, d), lambda l, j: (0, 0)),
                  pl.BlockSpec((1, d, nb), lambda l, j: (l, 0, j)),
                  pl.BlockSpec((1, 1, nb), lambda l, j: (l, 0, j))],
        out_specs=pl.BlockSpec((1, bsz, nb), lambda l, j: (l, 0, j)),
        out_shape=jax.ShapeDtypeStruct((depth, bsz, n), F32),
        compiler_params=_cparams(("arbitrary", "arbitrary")),
        name="adaln_mod",
    )(c, ada_w, ada_b.reshape(depth, 1, n))


def _rope_kernel(pos_ref, freq_ref, ecos_ref, es1_ref, es2_ref, one_ref, cos_ref, s1_ref, s2_ref):
    ang = pos_ref[0].astype(F32) * freq_ref[...]
    cs = jnp.cos(ang)
    sn = jnp.sin(ang)
    tn = (((0,), (0,)), ((), ()))

    def pieces(x):
        p1 = x.astype(BF16)
        r1 = x - p1.astype(F32)
        p2 = r1.astype(BF16)
        p3 = (r1 - p2.astype(F32)).astype(BF16)
        return p1, p2, p3

    def spread(parts, e_ref):
        e = e_ref[...]
        return sum(lax.dot_general(p, e, tn, preferred_element_type=F32) for p in parts)

    cs_parts = pieces(cs)
    sn_parts = pieces(sn)
    cos_ref[...] = spread(cs_parts, ecos_ref) + one_ref[...]
    s1_ref[...] = spread(sn_parts, es1_ref)
    s2_ref[...] = spread(sn_parts, es2_ref)


def _rope_tables(positions):
    n = positions.size
    half = ROT_DIM // 2
    freqs = (ROPE_THETA ** (-jnp.arange(half, dtype=F32) * 2.0 / ROT_DIM)).reshape(half, 1)
    lane = jnp.arange(LANES) % B_HEAD_DIM
    f_of_lane = lane % half
    pick = (jnp.arange(half)[:, None] == f_of_lane[None, :]).astype(F32)
    ecos = pick * (lane < ROT_DIM)[None, :]
    es1 = -pick * (lane < half)[None, :]
    es2 = pick * ((lane >= half) & (lane < ROT_DIM))[None, :]
    one_row = (lane >= ROT_DIM).astype(F32).reshape(1, LANES)
    tm = min(ROPE_ROWS, n)
    out = jax.ShapeDtypeStruct((n, LANES), F32)
    spec = pl.BlockSpec((tm, LANES), lambda i: (i, 0))
    sel = pl.BlockSpec((half, LANES), lambda i: (0, 0))
    return pl.pallas_call(
        _rope_kernel,
        grid=(n // tm,),
        in_specs=[pl.BlockSpec((1, 1, tm), lambda i: (i, 0, 0)),
                  pl.BlockSpec((half, 1), lambda i: (0, 0)),
                  sel, sel, sel, pl.BlockSpec((1, LANES), lambda i: (0, 0))],
        out_specs=[spec, spec, spec],
        out_shape=[out, out, out],
        compiler_params=_cparams(("arbitrary",)),
        name="rope_tables",
    )(positions.reshape(n // tm, 1, tm), freqs, ecos.astype(BF16), es1.astype(BF16), es2.astype(BF16), one_row)


def _inproj_kernel(x_ref, sh_ref, sc_ref, w_ref, cos_ref, s1_ref, s2_ref, lng_ref, lnb_ref,
                   ws_ref, bs_ref, a_ref, s_ref, qn_ref, kn_ref, vn_ref, *rest):
    copy_dils = [dil for dil in DILATIONS if _dilation_layout(dil) == "copy"]
    n_copy = len(copy_dils)
    q_outs, k_outs, v_outs = rest[:n_copy], rest[n_copy:2 * n_copy], rest[2 * n_copy:3 * n_copy]
    qs_ref, ks_ref, vs_ref, qt_ref, kt_ref, vt_ref = rest[3 * n_copy:]
    tm = x_ref.shape[0]
    h = _ln(x_ref[...]) * (1.0 + sc_ref[0]) + sh_ref[0]
    proj = jnp.dot(h.astype(BF16), w_ref[...], preferred_element_type=F32)

    uv = _gelu_tanh(proj[:, :2 * A_WIDTH])
    u = uv[:, :A_WIDTH]
    v = (_ln(uv[:, A_WIDTH:]) * lng_ref[...] + lnb_ref[...]).astype(BF16)
    row = lax.broadcasted_iota(jnp.int32, (CHUNK, CHUNK), 0)
    col = lax.broadcasted_iota(jnp.int32, (CHUNK, CHUNK), 1)
    head_of_lane = lax.broadcasted_iota(jnp.int32, (CHUNK, A_WIDTH), 1) // A_HEAD_DIM
    w_heads = [jnp.where(col <= row, ws_ref[hd], 0.0).astype(BF16) for hd in range(A_HEADS)]
    for cidx in range(tm // CHUNK):
        rows = slice(cidx * CHUNK, (cidx + 1) * CHUNK)
        vc = v[rows]
        sv = bs_ref[...]
        for hd in range(A_HEADS):
            full = jnp.dot(w_heads[hd], vc, preferred_element_type=F32)
            sv = sv + jnp.where(head_of_lane == hd, full, 0.0)
        a_ref[rows, :] = (u[rows] * sv).astype(a_ref.dtype)

    cos = cos_ref[...]
    s1 = s1_ref[...]
    s2 = s2_ref[...]
    q0 = 2 * A_WIDTH
    k0 = q0 + B_WIDTH
    v0 = k0 + B_WIDTH
    for j in range(B_WIDTH // LANES):
        lanes = slice(j * LANES, (j + 1) * LANES)
        for base, ref, nat, scale in ((q0, qs_ref, qn_ref, Q_SCALE), (k0, ks_ref, kn_ref, 1.0)):
            xs = proj[:, base + j * LANES: base + (j + 1) * LANES]
            rot = (xs * cos + pltpu.roll(xs, LANES - ROT_DIM // 2, 1) * s1
                   + pltpu.roll(xs, ROT_DIM // 2, 1) * s2) * scale
            nat[:, lanes] = rot.astype(nat.dtype)
            ref[j] = rot
        vs = proj[:, v0 + j * LANES:v0 + (j + 1) * LANES]
        vn_ref[:, lanes] = vs.astype(vn_ref.dtype)
        vs_ref[j] = vs

    for src, stage, outs in ((qs_ref, qt_ref, q_outs), (ks_ref, kt_ref, k_outs), (vs_ref, vt_ref, v_outs)):
        bufs = (src, stage)
        prev = 1
        for idx, (dil, out) in enumerate(zip(copy_dils, outs)):
            chained = prev > 1 and dil % prev == 0
            base_dil = prev if chained else 1
            source = bufs[idx % 2] if chained or idx == 0 else None
            assert source is not None, "copy dilations must form a divisibility chain"
            step = dil // base_dil
            per = tm // dil
            keep = idx + 1 < len(copy_dils)
            for r_prev in range(base_dil):
                for q in range(step):
                    r = r_prev + base_dil * q
                    start = r_prev * (tm // base_dil) + q
                    for j in range(B_WIDTH // LANES):
                        val = source[j, pl.ds(start, per, stride=step), :]
                        out[0, r, :, j * LANES:(j + 1) * LANES] = val.astype(out.dtype)
                        if keep:
                            bufs[(idx + 1) % 2][j, r * per:(r + 1) * per, :] = val
            prev = dil

    s_ref[...] = proj[:, v0 + B_WIDTH:].astype(s_ref.dtype)


def _inproj(x2d, mod3, w_in_bf, cos_t, s1_t, s2_t, ln_g, ln_b, ws, bs_full, bsz, t):
    n, d = x2d.shape
    tm = min(ROW_TILE, t)
    nt = t // tm
    pw = w_in_bf.shape[1]
    c_width = pw - 2 * A_WIDTH - 3 * B_WIDTH
    tok = lambda b, i: (b * nt + i, 0)
    const2 = lambda b, i: (0, 0)
    outs = [jax.ShapeDtypeStruct((n, A_WIDTH), BF16),
            jax.ShapeDtypeStruct((t, bsz * c_width), BF16)]
    out_specs = [pl.BlockSpec((tm, A_WIDTH), tok),
                 pl.BlockSpec((tm, c_width), lambda b, i: (i, b))]
    for _ in range(3):
        outs.append(jax.ShapeDtypeStruct((n, B_WIDTH), BF16))
        out_specs.append(pl.BlockSpec((tm, B_WIDTH), tok))
    copy_dils = [dil for dil in DILATIONS if _dilation_layout(dil) == "copy"]
    for _ in range(3):
        for dil in copy_dils:
            outs.append(jax.ShapeDtypeStruct((bsz, dil, t // dil, B_WIDTH), BF16))
            out_specs.append(pl.BlockSpec((1, dil, tm // dil, B_WIDTH), lambda b, i: (b, 0, i, 0)))
    res = pl.pallas_call(
        _inproj_kernel,
        grid=(bsz, nt),
        in_specs=[pl.BlockSpec((tm, d), tok),
                  pl.BlockSpec((1, 1, d), lambda b, i: (b, 0, 0)),
                  pl.BlockSpec((1, 1, d), lambda b, i: (b, 0, 1)),
                  pl.BlockSpec((d, pw), const2),
                  pl.BlockSpec((tm, LANES), tok),
                  pl.BlockSpec((tm, LANES), tok),
                  pl.BlockSpec((tm, LANES), tok),
                  pl.BlockSpec((1, A_WIDTH), const2),
                  pl.BlockSpec((1, A_WIDTH), const2),
                  pl.BlockSpec((A_HEADS, CHUNK, CHUNK), lambda b, i: (0, 0, 0)),
                  pl.BlockSpec((CHUNK, A_WIDTH), const2)],
        out_specs=out_specs,
        out_shape=outs,
        scratch_shapes=[pltpu.VMEM((B_WIDTH // LANES, tm, LANES), F32)] * 6,
        compiler_params=_cparams(("arbitrary", "arbitrary")),
        name="inproj_gmlp_rope",
    )(x2d, mod3, mod3, w_in_bf, cos_t, s1_t, s2_t, ln_g, ln_b, ws, bs_full)
    a_out, s_in = res[0], res[1]
    natural = res[2:5]
    n_copy = len(copy_dils)
    copies = {dil: tuple(res[5 + z * n_copy + g] for z in range(3)) for g, dil in enumerate(copy_dils)}
    qkv = [copies[dil] if dil in copies else natural for dil in DILATIONS]
    return a_out, s_in, qkv


def _att_kernel(q_ref, k_ref, v_ref, o_ref, st_ref, kcat_ref, vaug_ref):
    i = pl.program_id(2)
    blk = ATT_BLK
    n_res = q_ref.shape[1]
    n_sub = q_ref.shape[2] // blk
    n_pairs = B_WIDTH // LANES
    last = slice(n_sub * blk, (n_sub + 1) * blk)

    @pl.when(i == 0)
    def _():
        kcat_ref[:, 0:blk, :] = jnp.zeros((n_res, blk, B_WIDTH), BF16)
        vaug_ref[:, :, 0:blk, 0:LANES] = jnp.zeros((n_res, n_pairs, blk, LANES), BF16)
        vaug_ref[:, :, :, LANES:] = jnp.ones((n_res, n_pairs, (n_sub + 1) * blk, LANES), BF16)

    @pl.when(i > 0)
    def _():
        kcat_ref[:, 0:blk, :] = kcat_ref[:, last, :]
        vaug_ref[:, :, 0:blk, 0:LANES] = vaug_ref[:, :, last, 0:LANES]

    for rr in range(n_res):
        kcat_ref[rr, blk:, :] = k_ref[0, rr]
        for pair in range(n_pairs):
            vaug_ref[rr, pair, blk:, 0:LANES] = v_ref[0, rr, :, pair * LANES:(pair + 1) * LANES]

    qi = lax.broadcasted_iota(jnp.int32, (blk, 2 * blk), 0)
    kk = lax.broadcasted_iota(jnp.int32, (blk, 2 * blk), 1)
    band = (kk >= qi) & (kk <= qi + blk)
    first = band & ((i > 0) | (kk >= blk))
    lane = lax.broadcasted_iota(jnp.int32, (blk, LANES), 1)
    low_half = lane < B_HEAD_DIM
    nt = (((1,), (1,)), ((), ()))
    for rr, sub in [(rr, sub) for rr in range(n_res) for sub in range(n_sub)]:
        rows = slice(sub * blk, (sub + 1) * blk)
        keys = slice(sub * blk, (sub + 2) * blk)
        mask = first if sub == 0 else band
        stats = jnp.zeros((blk, LANES), F32)
        for pair in range(n_pairs):
            lanes = slice(pair * LANES, (pair + 1) * LANES)
            q2 = q_ref[0, rr, rows, lanes]
            kslab = kcat_ref[rr, keys, lanes]
            res = []
            for hh in range(2):
                hd = 2 * pair + hh
                qm = jnp.where(low_half if hh == 0 else jnp.logical_not(low_half), q2, jnp.zeros_like(q2))
                s = lax.dot_general(qm, kslab, nt, preferred_element_type=F32)
                s = jnp.where(mask, s, NEG_INF)
                m = jnp.max(jnp.maximum(s[:, :blk], s[:, blk:]), axis=1, keepdims=True)
                p = jnp.exp2(s - m).astype(BF16)
                ov = jnp.dot(p, vaug_ref[rr, pair, keys, :], preferred_element_type=F32)
                den = ov[:, LANES:]
                res.append(ov[:, :LANES] / den)
                stats = jnp.where(lane == hd, m, stats)
                stats = jnp.where(lane == B_HEADS + hd, den, stats)
            o_ref[0, rr, rows, lanes] = jnp.where(low_half, res[0], res[1]).astype(o_ref.dtype)
        st_ref[0, rr, rows, :] = stats


def _dilation_layout(dil):
    return "natural" if dil == 1 else "copy"


def _att_branch(q, k, v, dil, bsz, t):
    rows = t // dil
    n_sub = min(ATT_SUB_BLOCKS, rows // ATT_BLK)
    step_rows = n_sub * ATT_BLK
    n_steps = rows // step_rows
    n_res = math.gcd(dil, max(1, ATT_SUB_BLOCKS // n_sub)) if n_steps == 1 else 1
    shape = lambda w: (bsz, dil, rows, w)
    blk = lambda w: pl.BlockSpec((1, n_res, step_rows, w), lambda b, r, i: (b, r, i, 0))
    q, k, v = (z.reshape(shape(B_WIDTH)) for z in (q, k, v))
    return pl.pallas_call(
        _att_kernel,
        grid=(bsz, dil // n_res, n_steps),
        in_specs=[blk(B_WIDTH), blk(B_WIDTH), blk(B_WIDTH)],
        out_specs=[blk(B_WIDTH), blk(LANES)],
        out_shape=[jax.ShapeDtypeStruct(shape(B_WIDTH), BF16),
                   jax.ShapeDtypeStruct(shape(LANES), F32)],
        scratch_shapes=[pltpu.VMEM((n_res, step_rows + ATT_BLK, B_WIDTH), BF16),
                        pltpu.VMEM((n_res, B_WIDTH // LANES, step_rows + ATT_BLK, 2 * LANES), BF16)],
        compiler_params=_cparams(("arbitrary", "arbitrary", "arbitrary")),
        name=f"dilated_attn_d{dil}",
    )(q, k, v)


def _ssm_kernel(u_ref, bmat_ref, cmat_ref, are_ref, aim_ref, dskip_ref, gw_ref, gb_ref,
                o_ref, state_ref, xbuf_ref, rows_ref):
    bsz, two_n = state_ref.shape
    n_state = two_n // 2
    steps = u_ref.shape[0]
    cw = u_ref.shape[1] // bsz
    n_slab = cw // LANES

    @pl.when(pl.program_id(0) == 0)
    def _():
        state_ref[...] = jnp.zeros_like(state_ref)

    for b in range(bsz):
        for j in range(n_slab):
            rows_ref[j, pl.ds(b, steps, stride=bsz), :] = (
                u_ref[:, b * cw + j * LANES:b * cw + (j + 1) * LANES].astype(F32))
    u32 = jnp.concatenate([rows_ref[j] for j in range(n_slab)], axis=1)
    xbuf_ref[...] = jnp.dot(u32.astype(BF16), bmat_ref[...], preferred_element_type=F32)
    a_re = jnp.broadcast_to(are_ref[...], (bsz, n_state))
    a_im = jnp.broadcast_to(aim_ref[...], (bsz, n_state))

    def step(tt, carry):
        x_re, x_im = carry
        r0 = pl.multiple_of(tt * bsz, bsz)
        b_re = xbuf_ref[pl.ds(r0, bsz), :n_state]
        b_im = xbuf_ref[pl.ds(r0, bsz), n_state:]
        n_re = a_re * x_re - a_im * x_im + b_re
        n_im = a_re * x_im + a_im * x_re + b_im
        xbuf_ref[pl.ds(r0, bsz), :n_state] = n_re
        xbuf_ref[pl.ds(r0, bsz), n_state:] = n_im
        return n_re, n_im

    x_re, x_im = lax.fori_loop(0, steps, step, (state_ref[:, :n_state], state_ref[:, n_state:]))
    state_ref[:, :n_state] = x_re
    state_ref[:, n_state:] = x_im

    y = jnp.dot(xbuf_ref[...].astype(BF16), cmat_ref[...], preferred_element_type=F32)
    y = _gelu_tanh(y + dskip_ref[...] * u32)
    gate = jax.nn.sigmoid(jnp.dot(y.astype(BF16), gw_ref[...], preferred_element_type=F32) + gb_ref[...])
    out = y * gate
    for j in range(n_slab):
        rows_ref[j] = out[:, j * LANES:(j + 1) * LANES]
    for b in range(bsz):
        for j in range(n_slab):
            o_ref[:, b * cw + j * LANES:b * cw + (j + 1) * LANES] = (
                rows_ref[j, pl.ds(b, steps, stride=bsz), :].astype(o_ref.dtype))


def _ssm_weights(lam_re, lam_im, log_dt, b_re, b_im, c_re, c_im):
    g, n_st = lam_re.shape
    cg = b_re.shape[-1]
    dt = jnp.exp(log_dt)[:, None]
    mag = jnp.exp(lam_re * dt)
    ab_re = mag * jnp.cos(lam_im * dt)
    ab_im = mag * jnp.sin(lam_im * dt)
    nr = ab_re - 1.0
    ni = ab_im
    mod2 = lam_re * lam_re + lam_im * lam_im
    f_re = (nr * lam_re + ni * lam_im) / mod2
    f_im = (ni * lam_re - nr * lam_im) / mod2
    bb_re = f_re[..., None] * b_re - f_im[..., None] * b_im
    bb_im = f_re[..., None] * b_im + f_im[..., None] * b_re
    eye = jnp.eye(g, dtype=F32)
    bm_re = jnp.einsum('gnc,gh->gchn', bb_re, eye).reshape(g * cg, g * n_st)
    bm_im = jnp.einsum('gnc,gh->gchn', bb_im, eye).reshape(g * cg, g * n_st)
    bmat = jnp.concatenate([bm_re, bm_im], axis=1)
    cm_re = jnp.einsum('gcn,gh->gnhc', c_re, eye).reshape(g * n_st, g * cg)
    cm_im = jnp.einsum('gcn,gh->gnhc', c_im, eye).reshape(g * n_st, g * cg)
    cmat = jnp.concatenate([cm_re, -cm_im], axis=0)
    return (bmat.astype(BF16), cmat.astype(BF16),
            ab_re.reshape(1, g * n_st), ab_im.reshape(1, g * n_st))


def _ssm(u_tm, bmat, cmat, a_re, a_im, d_skip, glu_w_bf, glu_b, bsz, t):
    cw = bmat.shape[0]
    two_n = bmat.shape[1]
    steps = min(SSM_STEPS, t)
    rows = steps * bsz
    const = lambda i: (0, 0)
    return pl.pallas_call(
        _ssm_kernel,
        grid=(t // steps,),
        in_specs=[pl.BlockSpec((steps, bsz * cw), lambda i: (i, 0)),
                  pl.BlockSpec((cw, two_n), const),
                  pl.BlockSpec((two_n, cw), const),
                  pl.BlockSpec((1, two_n // 2), const),
                  pl.BlockSpec((1, two_n // 2), const),
                  pl.BlockSpec((1, cw), const),
                  pl.BlockSpec((cw, cw), const),
                  pl.BlockSpec((1, cw), const)],
        out_specs=pl.BlockSpec((steps, bsz * cw), lambda i: (i, 0)),
        out_shape=jax.ShapeDtypeStruct((t, bsz * cw), BF16),
        scratch_shapes=[pltpu.VMEM((bsz, two_n), F32), pltpu.VMEM((rows, two_n), F32),
                        pltpu.VMEM((cw // LANES, rows, LANES), F32)],
        compiler_params=_cparams(("arbitrary",)),
        name="s5_scan_glu",
    )(u_tm, bmat, cmat, a_re, a_im, d_skip, glu_w_bf, glu_b)


def _route(sel, scores):
    gs = []
    for g in range(N_EXPERT_GROUPS):
        a, b, c, d = sel[EXPERTS_PER_GROUP * g: EXPERTS_PER_GROUP * (g + 1)]
        hi1, lo1 = jnp.maximum(a, b), jnp.minimum(a, b)
        hi2, lo2 = jnp.maximum(c, d), jnp.minimum(c, d)
        gs.append(jnp.maximum(hi1, hi2) + jnp.maximum(jnp.minimum(hi1, hi2), jnp.maximum(lo1, lo2)))
    g_idx = jnp.zeros(gs[0].shape, jnp.int32)
    best = gs[0]
    for g in range(1, N_EXPERT_GROUPS):
        better = gs[g] > best
        g_idx = jnp.where(better, g, g_idx)
        best = jnp.where(better, gs[g], best)

    def pick(rows, j):
        out = rows[j]
        for g in range(1, N_EXPERT_GROUPS):
            out = jnp.where(g_idx == g, rows[EXPERTS_PER_GROUP * g + j], out)
        return out

    v = [pick(sel, j) for j in range(EXPERTS_PER_GROUP)]
    s = [pick(scores, j) for j in range(EXPERTS_PER_GROUP)]
    i1 = jnp.zeros(g_idx.shape, jnp.int32)
    b1, g1 = v[0], s[0]
    for j in range(1, EXPERTS_PER_GROUP):
        better = v[j] > b1
        i1 = jnp.where(better, j, i1)
        b1 = jnp.where(better, v[j], b1)
        g1 = jnp.where(better, s[j], g1)
    i2 = jnp.full(g_idx.shape, -1, jnp.int32)
    b2 = jnp.zeros_like(b1)
    g2 = jnp.zeros_like(g1)
    for j in range(EXPERTS_PER_GROUP):
        better = (i1 != j) & ((i2 < 0) | (v[j] > b2))
        i2 = jnp.where(better, j, i2)
        b2 = jnp.where(better, v[j], b2)
        g2 = jnp.where(better, s[j], g2)
    tot = g1 + g2
    base = g_idx * EXPERTS_PER_GROUP
    return base + i1, base + i2, g1 / tot, g2 / tot


def _natural_order(ref, scr_ref, dil):
    if _dilation_layout(dil) != "copy":
        return ref[...].astype(F32)
    per = ref.shape[2]
    n_slab = scr_ref.shape[0]
    for r in range(dil):
        for j in range(n_slab):
            scr_ref[j, pl.ds(r, per, stride=dil), :] = ref[0, r, :, j * LANES:(j + 1) * LANES].astype(F32)
    if n_slab == 1:
        return scr_ref[0]
    return jnp.concatenate([scr_ref[j] for j in range(n_slab)], axis=1)


def _outproj_kernel(a_ref, o1_ref, o2_ref, o3_ref, st1_ref, st2_ref, st3_ref,
                    c_ref, x_ref, g1_ref, sh2_ref, sc2_ref, wout_ref, expand_ref, lng_ref, lnb_ref,
                    rwh_ref, rwl_ref, rb_ref, x1_ref, eidx_ref, gate_ref, tab_ref, xs_hbm,
                    oscr2_ref, oscr3_ref, sscr2_ref, sscr3_ref, hbuf, state_ref, slot_ref, fin_ref, fins_ref,
                    ssem, csem, *, alpha):
    o_refs = (o1_ref, o2_ref, o3_ref)
    st_refs = (st1_ref, st2_ref, st3_ref)
    o_scr = (None, oscr2_ref, oscr3_ref)
    st_scr = (None, sscr2_ref, sscr3_ref)
    ms = [_natural_order(st_refs[g], st_scr[g], DILATIONS[g]) for g in range(len(DILATIONS))]
    dens = [pltpu.roll(m, LANES - B_HEADS, 1) for m in ms]
    mx = jnp.maximum(jnp.maximum(ms[0], ms[1]), ms[2])
    ws = [dens[g] * jnp.exp2(ms[g] - mx) for g in range(len(DILATIONS))]
    lane = lax.broadcasted_iota(jnp.int32, mx.shape, 1)
    tot = jnp.where(lane < B_HEADS, ws[0] + ws[1] + ws[2], 1.0)
    expand = expand_ref[...]

    def widen(w):
        wn = jnp.where(lane < B_HEADS, w / tot, 0.0)
        hi = wn.astype(BF16)
        lo = (wn - hi.astype(F32)).astype(BF16)
        return (jnp.dot(hi, expand, preferred_element_type=F32)
                + jnp.dot(lo, expand, preferred_element_type=F32))

    b_out = widen(ws[0]) * _natural_order(o_refs[0], o_scr[0], DILATIONS[0])
    for g in range(1, len(DILATIONS)):
        b_out = b_out + widen(ws[g]) * _natural_order(o_refs[g], o_scr[g], DILATIONS[g])
    b_out = b_out.astype(BF16)

    wout = wout_ref
    mix = (jnp.dot(a_ref[...], wout[:A_WIDTH, :], preferred_element_type=F32)
           + jnp.dot(b_out, wout[A_WIDTH:A_WIDTH + B_WIDTH, :], preferred_element_type=F32)
           + jnp.dot(c_ref[...], wout[A_WIDTH + B_WIDTH:, :], preferred_element_type=F32))
    x1 = _ln(alpha * x_ref[...] + (1.0 + g1_ref[0]) * mix) * lng_ref[...] + lnb_ref[...]
    x1_ref[...] = x1
    h2 = _ln(x1) * (1.0 + sc2_ref[0]) + sh2_ref[0]

    hi = h2.astype(BF16)
    lo = (h2 - hi.astype(F32)).astype(BF16)
    nt = (((1,), (1,)), ((), ()))
    logits = (lax.dot_general(rwh_ref[...], hi, nt, preferred_element_type=F32)
              + lax.dot_general(rwh_ref[...], lo, nt, preferred_element_type=F32)
              + lax.dot_general(rwl_ref[...], hi, nt, preferred_element_type=F32))
    scores = jax.nn.sigmoid(logits)
    sel = scores + rb_ref[...]
    sel_rows = [sel[e:e + 1, :] for e in range(N_EXPERTS)]
    score_rows = [scores[e:e + 1, :] for e in range(N_EXPERTS)]
    e1, e2, gt1, gt2 = _route(sel_rows, score_rows)

    g = pl.program_id(0) * pl.num_programs(1) + pl.program_id(1)
    n_tiles = pl.num_programs(0) * pl.num_programs(1)
    tm = h2.shape[0]
    n_slab = h2.shape[1] // LANES
    blk = MOE_BLK
    par = g % 2

    @pl.when(g == 0)
    def _():
        state_ref[...] = jnp.zeros_like(state_ref)
        tab_ref[...] = jnp.full(tab_ref.shape, float(N_EXPERTS), F32)

    erow = lax.broadcasted_iota(jnp.int32, (N_EXPERTS, tm), 0)
    oh1 = erow == e1
    oh2 = erow == e2
    both = (oh1 | oh2).astype(BF16)
    earlier = (lax.broadcasted_iota(jnp.int32, (tm, tm), 0)
               < lax.broadcasted_iota(jnp.int32, (tm, tm), 1)).astype(BF16)
    base = state_ref[0, :, 0:1]
    cur = state_ref[1, :, 0:1]
    nfree = state_ref[2, :, 0:1]
    n_vec = jnp.sum(both.astype(F32), axis=1, keepdims=True)
    off = base - jnp.floor(base * (1.0 / blk)) * blk
    need_a = (off == 0.0) & (n_vec > 0.0)
    need_b = (off + n_vec) > float(blk)
    na = need_a.astype(F32)
    nb = need_b.astype(F32)
    alloc = na + nb
    lower = (lax.broadcasted_iota(jnp.int32, (N_EXPERTS, N_EXPERTS), 1)
             < lax.broadcasted_iota(jnp.int32, (N_EXPERTS, N_EXPERTS), 0)).astype(BF16)
    alloc_wide = jnp.broadcast_to(alloc, (N_EXPERTS, LANES))
    before = jnp.dot(lower, alloc_wide.astype(BF16), preferred_element_type=F32)[:, 0:1]
    a_id = jnp.where(need_a, nfree + before, cur)
    b_id = jnp.where(need_b, nfree + before + na, a_id)
    bound = base - off + float(blk)
    total_alloc = jnp.sum(alloc_wide, axis=0, keepdims=True)[:, 0:1]
    state_ref[0] = jnp.broadcast_to(base + n_vec, (N_EXPERTS, LANES))
    state_ref[1] = jnp.broadcast_to(jnp.where(need_b, b_id, a_id), (N_EXPERTS, LANES))
    state_ref[2] = jnp.broadcast_to(nfree + total_alloc, (N_EXPERTS, LANES))

    blane = lax.broadcasted_iota(jnp.int32, (N_EXPERTS, tab_ref.shape[1]), 1).astype(F32)
    hit = ((blane == a_id) & need_a) | ((blane == b_id) & need_b)
    e_col = lax.broadcasted_iota(jnp.int32, (N_EXPERTS, 1), 0).astype(F32)
    delta = jnp.sum(jnp.where(hit, e_col - float(N_EXPERTS), 0.0), axis=0, keepdims=True)
    tab_ref[...] = tab_ref[...] + jnp.broadcast_to(delta, tab_ref.shape)

    rank = jnp.dot(both, earlier, preferred_element_type=F32) + base
    slot = jnp.where(rank < bound, a_id * blk + (rank - bound + blk), b_id * blk + (rank - bound))
    s1 = jnp.sum(jnp.where(oh1, slot, 0.0), axis=0, keepdims=True)
    s2 = jnp.sum(jnp.where(oh2, slot, 0.0), axis=0, keepdims=True)

    eidx_ref[...] = jnp.zeros_like(eidx_ref)
    gate_ref[...] = jnp.zeros_like(gate_ref)
    eidx_ref[0:1, :] = e1
    eidx_ref[1:2, :] = e2
    eidx_ref[2:3, :] = s1.astype(jnp.int32)
    eidx_ref[3:4, :] = s2.astype(jnp.int32)
    gate_ref[0:1, :] = gt1
    gate_ref[1:2, :] = gt2

    slot_copy = pltpu.make_async_copy(eidx_ref, slot_ref, csem)
    slot_copy.start()

    def wait_rows(p):
        for _ in range(TOP_K):
            pltpu.make_async_copy(hbuf.at[p], xs_hbm.at[pl.ds(0, tm * n_slab), :], ssem.at[p]).wait()

    @pl.when(g >= 2)
    def _():
        wait_rows(par)

    _store_tile_rows(hbuf.at[par], h2)
    slot_copy.wait()

    def issue(s, c):
        src = hbuf.at[par, pl.ds(pl.multiple_of(s * n_slab, n_slab), n_slab), :]
        for kk in range(TOP_K):
            dst = pl.multiple_of(slot_ref[TOP_K + kk, s] * n_slab, n_slab)
            pltpu.async_copy(src, xs_hbm.at[pl.ds(dst, n_slab), :], ssem.at[par], priority=1)
        return c
    lax.fori_loop(0, tm, issue, 0, unroll=8)

    @pl.when(g == n_tiles - 1)
    def _():
        wait_rows(par)

        @pl.when(n_tiles >= 2)
        def _():
            wait_rows(1 - par)

        lane_e = lax.broadcasted_iota(jnp.int32, (N_EXPERTS, LANES), 1)
        sub_e = lax.broadcasted_iota(jnp.int32, (N_EXPERTS, LANES), 0)

        def by_lane(col):
            return jnp.sum(jnp.where(lane_e == sub_e, col, 0.0), axis=0, keepdims=True)

        end = base + n_vec
        end_off = end - jnp.floor(end * (1.0 / blk)) * blk
        fin_ref[...] = jnp.zeros_like(fin_ref)
        fin_ref[0:1, :] = by_lane(jnp.where(need_b, b_id, a_id)).astype(jnp.int32)
        fin_ref[1:2, :] = by_lane(end_off).astype(jnp.int32)
        fin_ref[2:3, :] = by_lane(nfree + total_alloc).astype(jnp.int32)
        fin_copy = pltpu.make_async_copy(fin_ref, fins_ref, csem)
        fin_copy.start()
        fin_copy.wait()

        hbuf[0] = jnp.zeros(hbuf.shape[1:], F32)

        def zero_rows(first_row, n_rows):
            cp = pltpu.make_async_copy(hbuf.at[0, pl.ds(0, n_rows * n_slab), :],
                                       xs_hbm.at[pl.ds(pl.multiple_of(first_row * n_slab, n_slab),
                                                       n_rows * n_slab), :], ssem.at[0])
            cp.start()
            cp.wait()

        for e in range(N_EXPERTS):
            pos = fins_ref[1, e]
            start = fins_ref[0, e] * blk
            for k in range(blk.bit_length() - 1):
                take = (fins_ref[1, e] > 0) & (((pos >> k) & 1) == 1)

                @pl.when(take)
                def _():
                    zero_rows(start + pos, 1 << k)
                pos = jnp.where(take, pos + (1 << k), pos)

        n_blk = xs_hbm.shape[0] // (blk * n_slab)
        for t in range(N_EXPERTS):
            spare = fins_ref[2, 0] + t

            @pl.when(spare < n_blk)
            def _():
                zero_rows(spare * blk, blk)


def _outproj(a_out, att, c_tm, x2d, mod3, w_out_bf, ln_g, ln_b, rw_hi, rw_lo, rbias, bsz, t, alpha):
    n, d = x2d.shape
    tm = min(ROW_TILE, t)
    assert tm == MOE_BLK and MOE_BLK & (MOE_BLK - 1) == 0
    nt = t // tm
    n_slab = d // LANES
    n_blk = n * TOP_K // MOE_BLK + N_EXPERTS
    tab_w = -(-n_blk // LANES) * LANES
    cw = c_tm.shape[1] // bsz
    tok = lambda b, i: (b * nt + i, 0)
    const2 = lambda b, i: (0, 0)
    head = jnp.arange(B_WIDTH) // B_HEAD_DIM
    expand = (jnp.arange(LANES)[:, None] == head[None, :]).astype(BF16)

    def branch_input(z, dil, w):
        if _dilation_layout(dil) == "copy":
            return z, pl.BlockSpec((1, dil, tm // dil, w), lambda b, i: (b, 0, i, 0))
        return z.reshape(n, w), pl.BlockSpec((tm, w), tok)

    o_in = [branch_input(o, dil, B_WIDTH) for (o, _), dil in zip(att, DILATIONS)]
    st_in = [branch_input(st, dil, LANES) for (_, st), dil in zip(att, DILATIONS)]
    (o1, st1), (o2, st2), (o3, st3) = [(o_in[g][0], st_in[g][0]) for g in range(len(DILATIONS))]
    modspec = lambda j: pl.BlockSpec((1, 1, d), lambda b, i: (b, 0, j))
    return pl.pallas_call(
        functools.partial(_outproj_kernel, alpha=alpha),
        grid=(bsz, nt),
        in_specs=[pl.BlockSpec((tm, A_WIDTH), tok)]
                 + [spec for _, spec in o_in]
                 + [spec for _, spec in st_in]
                 + [pl.BlockSpec((tm, cw), lambda b, i: (i, b)),
                  pl.BlockSpec((tm, d), tok),
                  modspec(2), modspec(3), modspec(4),
                  pl.BlockSpec((d, d), const2),
                  pl.BlockSpec((LANES, B_WIDTH), const2),
                  pl.BlockSpec((1, d), const2), pl.BlockSpec((1, d), const2),
                  pl.BlockSpec((N_EXPERTS, d), const2), pl.BlockSpec((N_EXPERTS, d), const2),
                  pl.BlockSpec((N_EXPERTS, 1), const2)],
        out_specs=[pl.BlockSpec((tm, d), tok),
                   pl.BlockSpec((SUBLANES, tm), lambda b, i: (0, b * nt + i)),
                   pl.BlockSpec((SUBLANES, tm), lambda b, i: (0, b * nt + i)),
                   pl.BlockSpec((SUBLANES, tab_w), const2),
                   pl.BlockSpec(memory_space=pl.ANY)],
        out_shape=[jax.ShapeDtypeStruct((n, d), F32),
                   jax.ShapeDtypeStruct((SUBLANES, n), jnp.int32),
                   jax.ShapeDtypeStruct((SUBLANES, n), F32),
                   jax.ShapeDtypeStruct((SUBLANES, tab_w), F32),
                   jax.ShapeDtypeStruct((n_blk * MOE_BLK * n_slab, LANES), F32)],
        scratch_shapes=[pltpu.VMEM((B_WIDTH // LANES, tm, LANES), F32),
                        pltpu.VMEM((B_WIDTH // LANES, tm, LANES), F32),
                        pltpu.VMEM((1, tm, LANES), F32), pltpu.VMEM((1, tm, LANES), F32),
                        pltpu.VMEM((2, tm * n_slab, LANES), F32),
                        pltpu.VMEM((3, N_EXPERTS, LANES), F32),
                        pltpu.SMEM((SUBLANES, tm), jnp.int32),
                        pltpu.VMEM((SUBLANES, LANES), jnp.int32),
                        pltpu.SMEM((SUBLANES, LANES), jnp.int32),
                        pltpu.SemaphoreType.DMA((2,)), pltpu.SemaphoreType.DMA],
        compiler_params=_cparams(("arbitrary", "arbitrary")),
        name="outproj_ln_router",
    )(a_out, o1, o2, o3, st1, st2, st3, c_tm, x2d, mod3, mod3, mod3, w_out_bf, expand,
      ln_g, ln_b, rw_hi, rw_lo, rbias)


def _block_order(tab_row, n_blk):
    tab = tab_row[:n_blk].astype(jnp.int32)
    idx = jnp.arange(n_blk, dtype=jnp.int32)
    before = (tab[None, :] < tab[:, None]) | ((tab[None, :] == tab[:, None]) & (idx[None, :] < idx[:, None]))
    pos = jnp.sum(before.astype(jnp.int32), axis=1)
    at_pos = pos[None, :] == idx[:, None]
    perm = jnp.sum(jnp.where(at_pos, idx[None, :], 0), axis=1).astype(jnp.int32)
    blk_e = jnp.minimum(jnp.sum(jnp.where(at_pos, tab[None, :], 0), axis=1), N_EXPERTS - 1).astype(jnp.int32)
    n_used = jnp.sum((tab < N_EXPERTS).astype(jnp.int32)).reshape(1)
    return perm, blk_e, n_used


def _expert_kernel(blk_e_ref, nused_ref, perm_ref, xs_ref, wg_ref, wu_ref, wd_ref, ys_ref, wg_bf, wu_bf, wd_bf,
                   *, blk, n_slab):
    j = pl.program_id(0)

    @pl.when((j == 0) | (blk_e_ref[j] != blk_e_ref[jnp.maximum(j - 1, 0)]))
    def _():
        wg_bf[...] = wg_ref[0, 0].astype(BF16)
        wu_bf[...] = wu_ref[0, 0].astype(BF16)
        wd_bf[...] = wd_ref[0, 0].astype(BF16)

    @pl.when(j < nused_ref[0])
    def _():
        rows = _load_tile_rows(xs_ref, 0, blk, n_slab).astype(BF16)
        gt = jnp.dot(rows, wg_bf[...], preferred_element_type=F32)
        up = jnp.dot(rows, wu_bf[...], preferred_element_type=F32)
        hid = (gt * jax.nn.sigmoid(gt) * up).astype(BF16)
        _store_tile_rows(ys_ref, jnp.dot(hid, wd_bf[...], preferred_element_type=F32))

    @pl.when(j >= nused_ref[0])
    def _():
        ys_ref[...] = jnp.zeros_like(ys_ref)


def _moe(xs, tab, w_gate, w_up, w_down, layer):
    d, de = w_gate.shape[-2:]
    n_slab = d // LANES
    blk = MOE_BLK
    n_blk = xs.shape[0] // (blk * n_slab)
    perm, blk_e, n_used = _block_order(tab[0], n_blk)

    def x_block(j, be, nu, pm):
        return (pm[jnp.minimum(j, nu[0] - 1)], 0)

    return pl.pallas_call(
        functools.partial(_expert_kernel, blk=blk, n_slab=n_slab),
        grid_spec=pltpu.PrefetchScalarGridSpec(
            num_scalar_prefetch=3,
            grid=(n_blk,),
            in_specs=[pl.BlockSpec((blk * n_slab, LANES), x_block),
                      pl.BlockSpec((1, 1, d, de), lambda j, be, nu, pm: (layer, be[j], 0, 0)),
                      pl.BlockSpec((1, 1, d, de), lambda j, be, nu, pm: (layer, be[j], 0, 0)),
                      pl.BlockSpec((1, 1, de, d), lambda j, be, nu, pm: (layer, be[j], 0, 0))],
            out_specs=pl.BlockSpec((blk * n_slab, LANES), lambda j, be, nu, pm: (pm[j], 0)),
            scratch_shapes=[pltpu.VMEM((d, de), BF16), pltpu.VMEM((d, de), BF16), pltpu.VMEM((de, d), BF16)],
        ),
        out_shape=jax.ShapeDtypeStruct(xs.shape, F32),
        compiler_params=_cparams(("arbitrary",)),
        name="moe_experts",
    )(blk_e, n_used, perm, xs, w_gate, w_up, w_down)


def _ffn_ln_kernel(dest_ref, next_ref, x_ref, gate_ref, g2_ref, lng_ref, lnb_ref, ys_hbm, o_ref, ybuf, sem,
                   *, alpha):
    tm, d = x_ref.shape
    n_slab = d // LANES
    g = pl.program_id(0) * pl.num_programs(1) + pl.program_id(1)
    n_tiles = pl.num_programs(0) * pl.num_programs(1)
    slot = g % 2

    def gather(idx_ref, buf_slot):
        def issue(s, c):
            src = pl.multiple_of(idx_ref[0, 0, s] * n_slab, n_slab)
            dst = pl.multiple_of(s * n_slab, n_slab)
            pltpu.make_async_copy(ys_hbm.at[pl.ds(src, n_slab), :],
                                  ybuf.at[buf_slot, pl.ds(dst, n_slab), :], sem.at[buf_slot]).start()
            return c
        lax.fori_loop(0, TOP_K * tm, issue, 0, unroll=8)

    @pl.when(g == 0)
    def _():
        gather(dest_ref, 0)

    @pl.when(g + 1 < n_tiles)
    def _():
        gather(next_ref, 1 - slot)

    pltpu.make_async_copy(ys_hbm.at[pl.ds(0, TOP_K * tm * n_slab), :], ybuf.at[slot], sem.at[slot]).wait()

    gate = gate_ref[...]
    buf = ybuf.at[slot]
    ffn = _load_tile_rows(buf, 0, tm, n_slab) * gate[:, 0:1]
    for kk in range(1, TOP_K):
        ffn = ffn + _load_tile_rows(buf, kk * tm, tm, n_slab) * gate[:, kk:kk + 1]
    o_ref[...] = _ln(alpha * x_ref[...] + (1.0 + g2_ref[0]) * ffn) * lng_ref[...] + lnb_ref[...]


def _ffn_ln(x1, ys, dest, gates, mod3, ln_g, ln_b, bsz, t, alpha):
    n, d = x1.shape
    n_slab = d // LANES
    tm = min(ROW_TILE, t)
    nt = t // tm
    n_tiles = bsz * nt
    dest_tiles = dest.reshape(TOP_K, n_tiles, tm).transpose(1, 0, 2).reshape(n_tiles, 1, TOP_K * tm)
    tok = lambda b, i, *_: (b * nt + i, 0)
    const2 = lambda b, i, *_: (0, 0)
    gate_cols = gates[:TOP_K].T
    idx_blk = (1, 1, TOP_K * tm)
    return pl.pallas_call(
        functools.partial(_ffn_ln_kernel, alpha=alpha),
        grid=(bsz, nt),
        in_specs=[pl.BlockSpec(idx_blk, lambda b, i: (b * nt + i, 0, 0), memory_space=pltpu.SMEM),
                  pl.BlockSpec(idx_blk, lambda b, i: (jnp.minimum(b * nt + i + 1, n_tiles - 1), 0, 0),
                               memory_space=pltpu.SMEM),
                  pl.BlockSpec((tm, d), tok), pl.BlockSpec((tm, TOP_K), tok),
                  pl.BlockSpec((1, 1, d), lambda b, i: (b, 0, 5)),
                  pl.BlockSpec((1, d), const2), pl.BlockSpec((1, d), const2),
                  pl.BlockSpec(memory_space=pl.ANY)],
        out_specs=pl.BlockSpec((tm, d), tok),
        out_shape=jax.ShapeDtypeStruct((n, d), F32),
        scratch_shapes=[pltpu.VMEM((2, TOP_K * tm * n_slab, LANES), F32), pltpu.SemaphoreType.DMA((2,))],
        compiler_params=_cparams(("arbitrary", "arbitrary")),
        name="ffn_combine_ln",
    )(dest_tiles, dest_tiles, x1, gate_cols, mod3, ln_g, ln_b, ys)


def kernel(x, c, positions, ada_w, ada_b, w_in, gm_ln_g, gm_ln_b, gm_ws, gm_bs, ssm_lam_re, ssm_lam_im, ssm_log_dt, ssm_b_re, ssm_b_im, ssm_c_re, ssm_c_im, ssm_d, glu_w, glu_b, w_out, ln1_g, ln1_b, router_w, router_bias, exp_w_gate, exp_w_up, exp_w_down, ln2_g, ln2_b):
    bsz, t, d = x.shape
    depth = ada_w.shape[0]
    alpha = (2.0 * depth) ** 0.25
    n = bsz * t

    mod = _adaln_mod(c.astype(F32), ada_w, ada_b)
    cos_t, s1_t, s2_t = _rope_tables(positions)
    rw_t = router_w.astype(F32).T
    rw_hi = rw_t.astype(BF16)
    rw_lo = (rw_t - rw_hi.astype(F32)).astype(BF16)
    rbias = router_bias.astype(F32).reshape(N_EXPERTS, 1)

    xf = x.astype(F32).reshape(n, d)
    for l in range(depth):
        mod3 = mod[l].reshape(bsz, 1, 6 * d)
        bs_full = jnp.repeat(gm_bs[l].T, A_HEAD_DIM, axis=1)
        a_out, s_in, qkv = _inproj(
            xf, mod3, w_in[l].astype(BF16), cos_t, s1_t, s2_t,
            gm_ln_g[l].reshape(1, A_WIDTH), gm_ln_b[l].reshape(1, A_WIDTH), gm_ws[l], bs_full, bsz, t)
        att = [_att_branch(*qkv[g], dil, bsz, t) for g, dil in enumerate(DILATIONS)]
        bmat, cmat, a_re, a_im = _ssm_weights(ssm_lam_re[l], ssm_lam_im[l], ssm_log_dt[l],
                                              ssm_b_re[l], ssm_b_im[l], ssm_c_re[l], ssm_c_im[l])
        cw = bmat.shape[0]
        c_out = _ssm(s_in, bmat, cmat, a_re, a_im, ssm_d[l].reshape(1, cw),
                     glu_w[l].astype(BF16), glu_b[l].reshape(1, cw), bsz, t)
        x1, eidx, gates, tab, xs = _outproj(a_out, att, c_out, xf, mod3, w_out[l].astype(BF16),
                                            ln1_g[l].reshape(1, d), ln1_b[l].reshape(1, d),
                                            rw_hi, rw_lo, rbias, bsz, t, alpha)
        ys = _moe(xs, tab, exp_w_gate.astype(F32), exp_w_up.astype(F32), exp_w_down.astype(F32), l)
        xf = _ffn_ln(x1, ys, eidx[TOP_K:2 * TOP_K], gates, mod3, ln2_g[l].reshape(1, d), ln2_b[l].reshape(1, d),
                     bsz, t, alpha)
    return xf.reshape(bsz, t, d)
```

```python
import functools
import math

import jax
import jax.numpy as jnp
from jax import lax
from jax.experimental import pallas as pl
from jax.experimental.pallas import tpu as pltpu

F32 = jnp.float32
BF16 = jnp.bfloat16

A_HEADS = 4
A_HEAD_DIM = 64
A_WIDTH = A_HEADS * A_HEAD_DIM
CHUNK = 128
B_HEADS = 8
B_HEAD_DIM = 64
B_WIDTH = B_HEADS * B_HEAD_DIM
DILATIONS = (1, 4, 16)
ATT_BLK = 128
ROT_DIM = B_HEAD_DIM // 4
ROPE_THETA = 500000.0
N_EXPERTS = 16
N_EXPERT_GROUPS = 4
EXPERTS_PER_GROUP = 4
TOP_K = 2
LN_EPS = 1e-5
NEG_INF = -1e30
Q_SCALE = B_HEAD_DIM ** -0.5 * math.log2(math.e)

LANES = 128
SUBLANES = 8
VMEM_LIMIT = 56 * 1024 * 1024
ROW_TILE = 512
ROPE_ROWS = 2048
SSM_STEPS = 128
MOE_BLK = 512
ATT_SUB_BLOCKS = 8


def _cparams(sem):
    return pltpu.CompilerParams(dimension_semantics=sem, vmem_limit_bytes=VMEM_LIMIT)


def _ln(x):
    mu = jnp.mean(x, axis=-1, keepdims=True)
    xc = x - mu
    var = jnp.mean(xc * xc, axis=-1, keepdims=True)
    return xc * lax.rsqrt(var + LN_EPS)


def _store_tile_rows(ref, val):
    n_slab = val.shape[1] // LANES
    for j in range(n_slab):
        ref[pl.ds(j, val.shape[0], stride=n_slab), :] = val[:, j * LANES:(j + 1) * LANES]


def _load_tile_rows(ref, start, rows, n_slab):
    return jnp.concatenate([ref[pl.ds(start * n_slab + j, rows, stride=n_slab), :] for j in range(n_slab)],
                           axis=1)


def _gelu_tanh(x):
    return 0.5 * x * (1.0 + jnp.tanh(math.sqrt(2.0 / math.pi) * (x + 0.044715 * (x * x * x))))


def _mod_kernel(c_ref, w_ref, b_ref, o_ref):
    c = c_ref[...]
    cond = c * jax.nn.sigmoid(c)
    o_ref[0] = jnp.dot(cond, w_ref[0], precision=lax.Precision.HIGHEST,
                       preferred_element_type=F32) + b_ref[0]


def _adaln_mod(c, ada_w, ada_b):
    depth, d, n = ada_w.shape
    bsz = c.shape[0]
    nb = d
    return pl.pallas_call(
        _mod_kernel,
        grid=(depth, n // nb),
        in_specs=[pl.BlockSpec((bsz, d), lambda l, j: (0, 0)),
                  pl.BlockSpec((1, d, nb), lambda l, j: (l, 0, j)),
                  pl.BlockSpec((1, 1, nb), lambda l, j: (l, 0, j))],
        out_specs=pl.BlockSpec((1, bsz, nb), lambda l, j: (l, 0, j)),
        out_shape=jax.ShapeDtypeStruct((depth, bsz, n), F32),
        compiler_params=_cparams(("arbitrary", "arbitrary")),
        name="adaln_mod",
    )(c, ada_w, ada_b.reshape(depth, 1, n))


def _rope_kernel(pos_ref, freq_ref, ecos_ref, es1_ref, es2_ref, one_ref, cos_ref, s1_ref, s2_ref):
    ang = pos_ref[0].astype(F32) * freq_ref[...]
    cs = jnp.cos(ang)
    sn = jnp.sin(ang)
    tn = (((0,), (0,)), ((), ()))

    def pieces(x):
        p1 = x.astype(BF16)
        r1 = x - p1.astype(F32)
        p2 = r1.astype(BF16)
        p3 = (r1 - p2.astype(F32)).astype(BF16)
        return p1, p2, p3

    def spread(parts, e_ref):
        e = e_ref[...]
        return sum(lax.dot_general(p, e, tn, preferred_element_type=F32) for p in parts)

    cs_parts = pieces(cs)
    sn_parts = pieces(sn)
    cos_ref[...] = spread(cs_parts, ecos_ref) + one_ref[...]
    s1_ref[...] = spread(sn_parts, es1_ref)
    s2_ref[...] = spread(sn_parts, es2_ref)


def _rope_tables(positions):
    n = positions.size
    half = ROT_DIM // 2
    freqs = (ROPE_THETA ** (-jnp.arange(half, dtype=F32) * 2.0 / ROT_DIM)).reshape(half, 1)
    lane = jnp.arange(LANES) % B_HEAD_DIM
    f_of_lane = lane % half
    pick = (jnp.arange(half)[:, None] == f_of_lane[None, :]).astype(F32)
    ecos = pick * (lane < ROT_DIM)[None, :]
    es1 = -pick * (lane < half)[None, :]
    es2 = pick * ((lane >= half) & (lane < ROT_DIM))[None, :]
    one_row = (lane >= ROT_DIM).astype(F32).reshape(1, LANES)
    tm = min(ROPE_ROWS, n)
    out = jax.ShapeDtypeStruct((n, LANES), F32)
    spec = pl.BlockSpec((tm, LANES), lambda i: (i, 0))
    sel = pl.BlockSpec((half, LANES), lambda i: (0, 0))
    return pl.pallas_call(
        _rope_kernel,
        grid=(n // tm,),
        in_specs=[pl.BlockSpec((1, 1, tm), lambda i: (i, 0, 0)),
                  pl.BlockSpec((half, 1), lambda i: (0, 0)),
                  sel, sel, sel, pl.BlockSpec((1, LANES), lambda i: (0, 0))],
        out_specs=[spec, spec, spec],
        out_shape=[out, out, out],
        compiler_params=_cparams(("arbitrary",)),
        name="rope_tables",
    )(positions.reshape(n // tm, 1, tm), freqs, ecos.astype(BF16), es1.astype(BF16), es2.astype(BF16), one_row)


def _inproj_kernel(x_ref, sh_ref, sc_ref, w_ref, cos_ref, s1_ref, s2_ref, lng_ref, lnb_ref,
                   ws_ref, bs_ref, a_ref, s_ref, qn_ref, kn_ref, vn_ref, *rest):
    copy_dils = [dil for dil in DILATIONS if _dilation_layout(dil) == "copy"]
    n_copy = len(copy_dils)
    q_outs, k_outs, v_outs = rest[:n_copy], rest[n_copy:2 * n_copy], rest[2 * n_copy:3 * n_copy]
    qs_ref, ks_ref, vs_ref, qt_ref, kt_ref, vt_ref = rest[3 * n_copy:]
    tm = x_ref.shape[0]
    h = _ln(x_ref[...]) * (1.0 + sc_ref[0]) + sh_ref[0]
    proj = jnp.dot(h.astype(BF16), w_ref[...], preferred_element_type=F32)

    uv = _gelu_tanh(proj[:, :2 * A_WIDTH])
    u = uv[:, :A_WIDTH]
    v = (_ln(uv[:, A_WIDTH:]) * lng_ref[...] + lnb_ref[...]).astype(BF16)
    row = lax.broadcasted_iota(jnp.int32, (CHUNK, CHUNK), 0)
    col = lax.broadcasted_iota(jnp.int32, (CHUNK, CHUNK), 1)
    head_of_lane = lax.broadcasted_iota(jnp.int32, (CHUNK, A_WIDTH), 1) // A_HEAD_DIM
    w_heads = [jnp.where(col <= row, ws_ref[hd], 0.0).astype(BF16) for hd in range(A_HEADS)]
    for cidx in range(tm // CHUNK):
        rows = slice(cidx * CHUNK, (cidx + 1) * CHUNK)
        vc = v[rows]
        sv = bs_ref[...]
        for hd in range(A_HEADS):
            full = jnp.dot(w_heads[hd], vc, preferred_element_type=F32)
            sv = sv + jnp.where(head_of_lane == hd, full, 0.0)
        a_ref[rows, :] = (u[rows] * sv).astype(a_ref.dtype)

    cos = cos_ref[...]
    s1 = s1_ref[...]
    s2 = s2_ref[...]
    q0 = 2 * A_WIDTH
    k0 = q0 + B_WIDTH
    v0 = k0 + B_WIDTH
    for j in range(B_WIDTH // LANES):
        lanes = slice(j * LANES, (j + 1) * LANES)
        for base, ref, nat, scale in ((q0, qs_ref, qn_ref, Q_SCALE), (k0, ks_ref, kn_ref, 1.0)):
            xs = proj[:, base + j * LANES: base + (j + 1) * LANES]
            rot = (xs * cos + pltpu.roll(xs, LANES - ROT_DIM // 2, 1) * s1
                   + pltpu.roll(xs, ROT_DIM // 2, 1) * s2) * scale
            nat[:, lanes] = rot.astype(nat.dtype)
            ref[j] = rot
        vs = proj[:, v0 + j * LANES:v0 + (j + 1) * LANES]
        vn_ref[:, lanes] = vs.astype(vn_ref.dtype)
        vs_ref[j] = vs

    for src, stage, outs in ((qs_ref, qt_ref, q_outs), (ks_ref, kt_ref, k_outs), (vs_ref, vt_ref, v_outs)):
        bufs = (src, stage)
        prev = 1
        for idx, (dil, out) in enumerate(zip(copy_dils, outs)):
            chained = prev > 1 and dil % prev == 0
            base_dil = prev if chained else 1
            source = bufs[idx % 2] if chained or idx == 0 else None
            assert source is not None, "copy dilations must form a divisibility chain"
            step = dil // base_dil
            per = tm // dil
            keep = idx + 1 < len(copy_dils)
            for r_prev in range(base_dil):
                for q in range(step):
                    r = r_prev + base_dil * q
                    start = r_prev * (tm // base_dil) + q
                    for j in range(B_WIDTH // LANES):
                        val = source[j, pl.ds(start, per, stride=step), :]
                        out[0, r, :, j * LANES:(j + 1) * LANES] = val.astype(out.dtype)
                        if keep:
                            bufs[(idx + 1) % 2][j, r * per:(r + 1) * per, :] = val
            prev = dil

    s_ref[...] = proj[:, v0 + B_WIDTH:].astype(s_ref.dtype)


def _inproj(x2d, mod3, w_in_bf, cos_t, s1_t, s2_t, ln_g, ln_b, ws, bs_full, bsz, t):
    n, d = x2d.shape
    tm = min(ROW_TILE, t)
    nt = t // tm
    pw = w_in_bf.shape[1]
    c_width = pw - 2 * A_WIDTH - 3 * B_WIDTH
    tok = lambda b, i: (b * nt + i, 0)
    const2 = lambda b, i: (0, 0)
    outs = [jax.ShapeDtypeStruct((n, A_WIDTH), BF16),
            jax.ShapeDtypeStruct((t, bsz * c_width), BF16)]
    out_specs = [pl.BlockSpec((tm, A_WIDTH), tok),
                 pl.BlockSpec((tm, c_width), lambda b, i: (i, b))]
    for _ in range(3):
        outs.append(jax.ShapeDtypeStruct((n, B_WIDTH), BF16))
        out_specs.append(pl.BlockSpec((tm, B_WIDTH), tok))
    copy_dils = [dil for dil in DILATIONS if _dilation_layout(dil) == "copy"]
    for _ in range(3):
        for dil in copy_dils:
            outs.append(jax.ShapeDtypeStruct((bsz, dil, t // dil, B_WIDTH), BF16))
            out_specs.append(pl.BlockSpec((1, dil, tm // dil, B_WIDTH), lambda b, i: (b, 0, i, 0)))
    res = pl.pallas_call(
        _inproj_kernel,
        grid=(bsz, nt),
        in_specs=[pl.BlockSpec((tm, d), tok),
                  pl.BlockSpec((1, 1, d), lambda b, i: (b, 0, 0)),
                  pl.BlockSpec((1, 1, d), lambda b, i: (b, 0, 1)),
                  pl.BlockSpec((d, pw), const2),
                  pl.BlockSpec((tm, LANES), tok),
                  pl.BlockSpec((tm, LANES), tok),
                  pl.BlockSpec((tm, LANES), tok),
                  pl.BlockSpec((1, A_WIDTH), const2),
                  pl.BlockSpec((1, A_WIDTH), const2),
                  pl.BlockSpec((A_HEADS, CHUNK, CHUNK), lambda b, i: (0, 0, 0)),
                  pl.BlockSpec((CHUNK, A_WIDTH), const2)],
        out_specs=out_specs,
        out_shape=outs,
        scratch_shapes=[pltpu.VMEM((B_WIDTH // LANES, tm, LANES), F32)] * 6,
        compiler_params=_cparams(("arbitrary", "arbitrary")),
        name="inproj_gmlp_rope",
    )(x2d, mod3, mod3, w_in_bf, cos_t, s1_t, s2_t, ln_g, ln_b, ws, bs_full)
    a_out, s_in = res[0], res[1]
    natural = res[2:5]
    n_copy = len(copy_dils)
    copies = {dil: tuple(res[5 + z * n_copy + g] for z in range(3)) for g, dil in enumerate(copy_dils)}
    qkv = [copies[dil] if dil in copies else natural for dil in DILATIONS]
    return a_out, s_in, qkv


def _att_kernel(q_ref, k_ref, v_ref, o_ref, st_ref, kcat_ref, vaug_ref):
    i = pl.program_id(2)
    blk = ATT_BLK
    n_res = q_ref.shape[1]
    n_sub = q_ref.shape[2] // blk
    n_pairs = B_WIDTH // LANES
    last = slice(n_sub * blk, (n_sub + 1) * blk)

    @pl.when(i == 0)
    def _():
        kcat_ref[:, 0:blk, :] = jnp.zeros((n_res, blk, B_WIDTH), BF16)
        vaug_ref[:, :, 0:blk, 0:LANES] = jnp.zeros((n_res, n_pairs, blk, LANES), BF16)
        vaug_ref[:, :, :, LANES:] = jnp.ones((n_res, n_pairs, (n_sub + 1) * blk, LANES), BF16)

    @pl.when(i > 0)
    def _():
        kcat_ref[:, 0:blk, :] = kcat_ref[:, last, :]
        vaug_ref[:, :, 0:blk, 0:LANES] = vaug_ref[:, :, last, 0:LANES]

    for rr in range(n_res):
        kcat_ref[rr, blk:, :] = k_ref[0, rr]
        for pair in range(n_pairs):
            vaug_ref[rr, pair, blk:, 0:LANES] = v_ref[0, rr, :, pair * LANES:(pair + 1) * LANES]

    qi = lax.broadcasted_iota(jnp.int32, (blk, 2 * blk), 0)
    kk = lax.broadcasted_iota(jnp.int32, (blk, 2 * blk), 1)
    band = (kk >= qi) & (kk <= qi + blk)
    first = band & ((i > 0) | (kk >= blk))
    lane = lax.broadcasted_iota(jnp.int32, (blk, LANES), 1)
    low_half = lane < B_HEAD_DIM
    nt = (((1,), (1,)), ((), ()))
    for rr, sub in [(rr, sub) for rr in range(n_res) for sub in range(n_sub)]:
        rows = slice(sub * blk, (sub + 1) * blk)
        keys = slice(sub * blk, (sub + 2) * blk)
        mask = first if sub == 0 else band
        stats = jnp.zeros((blk, LANES), F32)
        for pair in range(n_pairs):
            lanes = slice(pair * LANES, (pair + 1) * LANES)
            q2 = q_ref[0, rr, rows, lanes]
            kslab = kcat_ref[rr, keys, lanes]
            res = []
            for hh in range(2):
                hd = 2 * pair + hh
                qm = jnp.where(low_half if hh == 0 else jnp.logical_not(low_half), q2, jnp.zeros_like(q2))
                s = lax.dot_general(qm, kslab, nt, preferred_element_type=F32)
                s = jnp.where(mask, s, NEG_INF)
                m = jnp.max(jnp.maximum(s[:, :blk], s[:, blk:]), axis=1, keepdims=True)
                p = jnp.exp2(s - m).astype(BF16)
                ov = jnp.dot(p, vaug_ref[rr, pair, keys, :], preferred_element_type=F32)
                den = ov[:, LANES:]
                res.append(ov[:, :LANES] / den)
                stats = jnp.where(lane == hd, m, stats)
                stats = jnp.where(lane == B_HEADS + hd, den, stats)
            o_ref[0, rr, rows, lanes] = jnp.where(low_half, res[0], res[1]).astype(o_ref.dtype)
        st_ref[0, rr, rows, :] = stats


def _dilation_layout(dil):
    return "natural" if dil == 1 else "copy"


def _att_branch(q, k, v, dil, bsz, t):
    rows = t // dil
    n_sub = min(ATT_SUB_BLOCKS, rows // ATT_BLK)
    step_rows = n_sub * ATT_BLK
    n_steps = rows // step_rows
    n_res = math.gcd(dil, max(1, ATT_SUB_BLOCKS // n_sub)) if n_steps == 1 else 1
    shape = lambda w: (bsz, dil, rows, w)
    blk = lambda w: pl.BlockSpec((1, n_res, step_rows, w), lambda b, r, i: (b, r, i, 0))
    q, k, v = (z.reshape(shape(B_WIDTH)) for z in (q, k, v))
    return pl.pallas_call(
        _att_kernel,
        grid=(bsz, dil // n_res, n_steps),
        in_specs=[blk(B_WIDTH), blk(B_WIDTH), blk(B_WIDTH)],
        out_specs=[blk(B_WIDTH), blk(LANES)],
        out_shape=[jax.ShapeDtypeStruct(shape(B_WIDTH), BF16),
                   jax.ShapeDtypeStruct(shape(LANES), F32)],
        scratch_shapes=[pltpu.VMEM((n_res, step_rows + ATT_BLK, B_WIDTH), BF16),
                        pltpu.VMEM((n_res, B_WIDTH // LANES, step_rows + ATT_BLK, 2 * LANES), BF16)],
        compiler_params=_cparams(("arbitrary", "arbitrary", "arbitrary")),
        name=f"dilated_attn_d{dil}",
    )(q, k, v)


def _ssm_kernel(u_ref, bmat_ref, cmat_ref, are_ref, aim_ref, dskip_ref, gw_ref, gb_ref,
                o_ref, state_ref, xbuf_ref, rows_ref):
    bsz, two_n = state_ref.shape
    n_state = two_n // 2
    steps = u_ref.shape[0]
    cw = u_ref.shape[1] // bsz
    n_slab = cw // LANES

    @pl.when(pl.program_id(0) == 0)
    def _():
        state_ref[...] = jnp.zeros_like(state_ref)

    for b in range(bsz):
        for j in range(n_slab):
            rows_ref[j, pl.ds(b, steps, stride=bsz), :] = (
                u_ref[:, b * cw + j * LANES:b * cw + (j + 1) * LANES].astype(F32))
    u32 = jnp.concatenate([rows_ref[j] for j in range(n_slab)], axis=1)
    xbuf_ref[...] = jnp.dot(u32.astype(BF16), bmat_ref[...], preferred_element_type=F32)
    a_re = jnp.broadcast_to(are_ref[...], (bsz, n_state))
    a_im = jnp.broadcast_to(aim_ref[...], (bsz, n_state))

    def step(tt, carry):
        x_re, x_im = carry
        r0 = pl.multiple_of(tt * bsz, bsz)
        b_re = xbuf_ref[pl.ds(r0, bsz), :n_state]
        b_im = xbuf_ref[pl.ds(r0, bsz), n_state:]
        n_re = a_re * x_re - a_im * x_im + b_re
        n_im = a_re * x_im + a_im * x_re + b_im
        xbuf_ref[pl.ds(r0, bsz), :n_state] = n_re
        xbuf_ref[pl.ds(r0, bsz), n_state:] = n_im
        return n_re, n_im

    x_re, x_im = lax.fori_loop(0, steps, step, (state_ref[:, :n_state], state_ref[:, n_state:]))
    state_ref[:, :n_state] = x_re
    state_ref[:, n_state:] = x_im

    y = jnp.dot(xbuf_ref[...].astype(BF16), cmat_ref[...], preferred_element_type=F32)
    y = _gelu_tanh(y + dskip_ref[...] * u32)
    gate = jax.nn.sigmoid(jnp.dot(y.astype(BF16), gw_ref[...], preferred_element_type=F32) + gb_ref[...])
    out = y * gate
    for j in range(n_slab):
        rows_ref[j] = out[:, j * LANES:(j + 1) * LANES]
    for b in range(bsz):
        for j in range(n_slab):
            o_ref[:, b * cw + j * LANES:b * cw + (j + 1) * LANES] = (
                rows_ref[j, pl.ds(b, steps, stride=bsz), :].astype(o_ref.dtype))


def _ssm_weights(lam_re, lam_im, log_dt, b_re, b_im, c_re, c_im):
    g, n_st = lam_re.shape
    cg = b_re.shape[-1]
    dt = jnp.exp(log_dt)[:, None]
    mag = jnp.exp(lam_re * dt)
    ab_re = mag * jnp.cos(lam_im * dt)
    ab_im = mag * jnp.sin(lam_im * dt)
    nr = ab_re - 1.0
    ni = ab_im
    mod2 = lam_re * lam_re + lam_im * lam_im
    f_re = (nr * lam_re + ni * lam_im) / mod2
    f_im = (ni * lam_re - nr * lam_im) / mod2
    bb_re = f_re[..., None] * b_re - f_im[..., None] * b_im
    bb_im = f_re[..., None] * b_im + f_im[..., None] * b_re
    eye = jnp.eye(g, dtype=F32)
    bm_re = jnp.einsum('gnc,gh->gchn', bb_re, eye).reshape(g * cg, g * n_st)
    bm_im = jnp.einsum('gnc,gh->gchn', bb_im, eye).reshape(g * cg, g * n_st)
    bmat = jnp.concatenate([bm_re, bm_im], axis=1)
    cm_re = jnp.einsum('gcn,gh->gnhc', c_re, eye).reshape(g * n_st, g * cg)
    cm_im = jnp.einsum('gcn,gh->gnhc', c_im, eye).reshape(g * n_st, g * cg)
    cmat = jnp.concatenate([cm_re, -cm_im], axis=0)
    return (bmat.astype(BF16), cmat.astype(BF16),
            ab_re.reshape(1, g * n_st), ab_im.reshape(1, g * n_st))


def _ssm(u_tm, bmat, cmat, a_re, a_im, d_skip, glu_w_bf, glu_b, bsz, t):
    cw = bmat.shape[0]
    two_n = bmat.shape[1]
    steps = min(SSM_STEPS, t)
    rows = steps * bsz
    const = lambda i: (0, 0)
    return pl.pallas_call(
        _ssm_kernel,
        grid=(t // steps,),
        in_specs=[pl.BlockSpec((steps, bsz * cw), lambda i: (i, 0)),
                  pl.BlockSpec((cw, two_n), const),
                  pl.BlockSpec((two_n, cw), const),
                  pl.BlockSpec((1, two_n // 2), const),
                  pl.BlockSpec((1, two_n // 2), const),
                  pl.BlockSpec((1, cw), const),
                  pl.BlockSpec((cw, cw), const),
                  pl.BlockSpec((1, cw), const)],
        out_specs=pl.BlockSpec((steps, bsz * cw), lambda i: (i, 0)),
        out_shape=jax.ShapeDtypeStruct((t, bsz * cw), BF16),
        scratch_shapes=[pltpu.VMEM((bsz, two_n), F32), pltpu.VMEM((rows, two_n), F32),
                        pltpu.VMEM((cw // LANES, rows, LANES), F32)],
        compiler_params=_cparams(("arbitrary",)),
        name="s5_scan_glu",
    )(u_tm, bmat, cmat, a_re, a_im, d_skip, glu_w_bf, glu_b)


def _route(sel, scores):
    gs = []
    for g in range(N_EXPERT_GROUPS):
        a, b, c, d = sel[EXPERTS_PER_GROUP * g: EXPERTS_PER_GROUP * (g + 1)]
        hi1, lo1 = jnp.maximum(a, b), jnp.minimum(a, b)
        hi2, lo2 = jnp.maximum(c, d), jnp.minimum(c, d)
        gs.append(jnp.maximum(hi1, hi2) + jnp.maximum(jnp.minimum(hi1, hi2), jnp.maximum(lo1, lo2)))
    g_idx = jnp.zeros(gs[0].shape, jnp.int32)
    best = gs[0]
    for g in range(1, N_EXPERT_GROUPS):
        better = gs[g] > best
        g_idx = jnp.where(better, g, g_idx)
        best = jnp.where(better, gs[g], best)

    def pick(rows, j):
        out = rows[j]
        for g in range(1, N_EXPERT_GROUPS):
            out = jnp.where(g_idx == g, rows[EXPERTS_PER_GROUP * g + j], out)
        return out

    v = [pick(sel, j) for j in range(EXPERTS_PER_GROUP)]
    s = [pick(scores, j) for j in range(EXPERTS_PER_GROUP)]
    i1 = jnp.zeros(g_idx.shape, jnp.int32)
    b1, g1 = v[0], s[0]
    for j in range(1, EXPERTS_PER_GROUP):
        better = v[j] > b1
        i1 = jnp.where(better, j, i1)
        b1 = jnp.where(better, v[j], b1)
        g1 = jnp.where(better, s[j], g1)
    i2 = jnp.full(g_idx.shape, -1, jnp.int32)
    b2 = jnp.zeros_like(b1)
    g2 = jnp.zeros_like(g1)
    for j in range(EXPERTS_PER_GROUP):
        better = (i1 != j) & ((i2 < 0) | (v[j] > b2))
        i2 = jnp.where(better, j, i2)
        b2 = jnp.where(better, v[j], b2)
        g2 = jnp.where(better, s[j], g2)
    tot = g1 + g2
    base = g_idx * EXPERTS_PER_GROUP
    return base + i1, base + i2, g1 / tot, g2 / tot


def _natural_order(ref, scr_ref, dil):
    if _dilation_layout(dil) != "copy":
        return ref[...].astype(F32)
    per = ref.shape[2]
    n_slab = scr_ref.shape[0]
    for r in range(dil):
        for j in range(n_slab):
            scr_ref[j, pl.ds(r, per, stride=dil), :] = ref[0, r, :, j * LANES:(j + 1) * LANES].astype(F32)
    if n_slab == 1:
        return scr_ref[0]
    return jnp.concatenate([scr_ref[j] for j in range(n_slab)], axis=1)


def _outproj_kernel(a_ref, o1_ref, o2_ref, o3_ref, st1_ref, st2_ref, st3_ref,
                    c_ref, x_ref, g1_ref, sh2_ref, sc2_ref, wout_ref, expand_ref, lng_ref, lnb_ref,
                    rwh_ref, rwl_ref, rb_ref, x1_ref, eidx_ref, gate_ref, tab_ref, xs_hbm,
                    oscr2_ref, oscr3_ref, sscr2_ref, sscr3_ref, hbuf, state_ref, slot_ref, fin_ref, fins_ref,
                    ssem, csem, *, alpha):
    o_refs = (o1_ref, o2_ref, o3_ref)
    st_refs = (st1_ref, st2_ref, st3_ref)
    o_scr = (None, oscr2_ref, oscr3_ref)
    st_scr = (None, sscr2_ref, sscr3_ref)
    ms = [_natural_order(st_refs[g], st_scr[g], DILATIONS[g]) for g in range(len(DILATIONS))]
    dens = [pltpu.roll(m, LANES - B_HEADS, 1) for m in ms]
    mx = jnp.maximum(jnp.maximum(ms[0], ms[1]), ms[2])
    ws = [dens[g] * jnp.exp2(ms[g] - mx) for g in range(len(DILATIONS))]
    lane = lax.broadcasted_iota(jnp.int32, mx.shape, 1)
    tot = jnp.where(lane < B_HEADS, ws[0] + ws[1] + ws[2], 1.0)
    expand = expand_ref[...]

    def widen(w):
        wn = jnp.where(lane < B_HEADS, w / tot, 0.0)
        hi = wn.astype(BF16)
        lo = (wn - hi.astype(F32)).astype(BF16)
        return (jnp.dot(hi, expand, preferred_element_type=F32)
                + jnp.dot(lo, expand, preferred_element_type=F32))

    b_out = widen(ws[0]) * _natural_order(o_refs[0], o_scr[0], DILATIONS[0])
    for g in range(1, len(DILATIONS)):
        b_out = b_out + widen(ws[g]) * _natural_order(o_refs[g], o_scr[g], DILATIONS[g])
    b_out = b_out.astype(BF16)

    wout = wout_ref
    mix = (jnp.dot(a_ref[...], wout[:A_WIDTH, :], preferred_element_type=F32)
           + jnp.dot(b_out, wout[A_WIDTH:A_WIDTH + B_WIDTH, :], preferred_element_type=F32)
           + jnp.dot(c_ref[...], wout[A_WIDTH + B_WIDTH:, :], preferred_element_type=F32))
    x1 = _ln(alpha * x_ref[...] + (1.0 + g1_ref[0]) * mix) * lng_ref[...] + lnb_ref[...]
    x1_ref[...] = x1
    h2 = _ln(x1) * (1.0 + sc2_ref[0]) + sh2_ref[0]

    hi = h2.astype(BF16)
    lo = (h2 - hi.astype(F32)).astype(BF16)
    nt = (((1,), (1,)), ((), ()))
    logits = (lax.dot_general(rwh_ref[...], hi, nt, preferred_element_type=F32)
              + lax.dot_general(rwh_ref[...], lo, nt, preferred_element_type=F32)
              + lax.dot_general(rwl_ref[...], hi, nt, preferred_element_type=F32))
    scores = jax.nn.sigmoid(logits)
    sel = scores + rb_ref[...]
    sel_rows = [sel[e:e + 1, :] for e in range(N_EXPERTS)]
    score_rows = [scores[e:e + 1, :] for e in range(N_EXPERTS)]
    e1, e2, gt1, gt2 = _route(sel_rows, score_rows)

    g = pl.program_id(0) * pl.num_programs(1) + pl.program_id(1)
    n_tiles = pl.num_programs(0) * pl.num_programs(1)
    tm = h2.shape[0]
    n_slab = h2.shape[1] // LANES
    blk = MOE_BLK
    par = g % 2

    @pl.when(g == 0)
    def _():
        state_ref[...] = jnp.zeros_like(state_ref)
        tab_ref[...] = jnp.full(tab_ref.shape, float(N_EXPERTS), F32)

    erow = lax.broadcasted_iota(jnp.int32, (N_EXPERTS, tm), 0)
    oh1 = erow == e1
    oh2 = erow == e2
    both = (oh1 | oh2).astype(BF16)
    earlier = (lax.broadcasted_iota(jnp.int32, (tm, tm), 0)
               < lax.broadcasted_iota(jnp.int32, (tm, tm), 1)).astype(BF16)
    base = state_ref[0, :, 0:1]
    cur = state_ref[1, :, 0:1]
    nfree = state_ref[2, :, 0:1]
    n_vec = jnp.sum(both.astype(F32), axis=1, keepdims=True)
    off = base - jnp.floor(base * (1.0 / blk)) * blk
    need_a = (off == 0.0) & (n_vec > 0.0)
    need_b = (off + n_vec) > float(blk)
    na = need_a.astype(F32)
    nb = need_b.astype(F32)
    alloc = na + nb
    lower = (lax.broadcasted_iota(jnp.int32, (N_EXPERTS, N_EXPERTS), 1)
             < lax.broadcasted_iota(jnp.int32, (N_EXPERTS, N_EXPERTS), 0)).astype(BF16)
    alloc_wide = jnp.broadcast_to(alloc, (N_EXPERTS, LANES))
    before = jnp.dot(lower, alloc_wide.astype(BF16), preferred_element_type=F32)[:, 0:1]
    a_id = jnp.where(need_a, nfree + before, cur)
    b_id = jnp.where(need_b, nfree + before + na, a_id)
    bound = base - off + float(blk)
    total_alloc = jnp.sum(alloc_wide, axis=0, keepdims=True)[:, 0:1]
    state_ref[0] = jnp.broadcast_to(base + n_vec, (N_EXPERTS, LANES))
    state_ref[1] = jnp.broadcast_to(jnp.where(need_b, b_id, a_id), (N_EXPERTS, LANES))
    state_ref[2] = jnp.broadcast_to(nfree + total_alloc, (N_EXPERTS, LANES))

    blane = lax.broadcasted_iota(jnp.int32, (N_EXPERTS, tab_ref.shape[1]), 1).astype(F32)
    hit = ((blane == a_id) & need_a) | ((blane == b_id) & need_b)
    e_col = lax.broadcasted_iota(jnp.int32, (N_EXPERTS, 1), 0).astype(F32)
    delta = jnp.sum(jnp.where(hit, e_col - float(N_EXPERTS), 0.0), axis=0, keepdims=True)
    tab_ref[...] = tab_ref[...] + jnp.broadcast_to(delta, tab_ref.shape)

    rank = jnp.dot(both, earlier, preferred_element_type=F32) + base
    slot = jnp.where(rank < bound, a_id * blk + (rank - bound + blk), b_id * blk + (rank - bound))
    s1 = jnp.sum(jnp.where(oh1, slot, 0.0), axis=0, keepdims=True)
    s2 = jnp.sum(jnp.where(oh2, slot, 0.0), axis=0, keepdims=True)

    eidx_ref[...] = jnp.zeros_like(eidx_ref)
    gate_ref[...] = jnp.zeros_like(gate_ref)
    eidx_ref[0:1, :] = e1
    eidx_ref[1:2, :] = e2
    eidx_ref[2:3, :] = s1.astype(jnp.int32)
    eidx_ref[3:4, :] = s2.astype(jnp.int32)
    gate_ref[0:1, :] = gt1
    gate_ref[1:2, :] = gt2

    slot_copy = pltpu.make_async_copy(eidx_ref, slot_ref, csem)
    slot_copy.start()

    def wait_rows(p):
        for _ in range(TOP_K):
            pltpu.make_async_copy(hbuf.at[p], xs_hbm.at[pl.ds(0, tm * n_slab), :], ssem.at[p]).wait()

    @pl.when(g >= 2)
    def _():
        wait_rows(par)

    _store_tile_rows(hbuf.at[par], h2)
    slot_copy.wait()

    def issue(s, c):
        src = hbuf.at[par, pl.ds(pl.multiple_of(s * n_slab, n_slab), n_slab), :]
        for kk in range(TOP_K):
            dst = pl.multiple_of(slot_ref[TOP_K + kk, s] * n_slab, n_slab)
            pltpu.async_copy(src, xs_hbm.at[pl.ds(dst, n_slab), :], ssem.at[par], priority=kk % 2)
        return c
    lax.fori_loop(0, tm, issue, 0, unroll=8)

    @pl.when(g == n_tiles - 1)
    def _():
        wait_rows(par)

        @pl.when(n_tiles >= 2)
        def _():
            wait_rows(1 - par)

        lane_e = lax.broadcasted_iota(jnp.int32, (N_EXPERTS, LANES), 1)
        sub_e = lax.broadcasted_iota(jnp.int32, (N_EXPERTS, LANES), 0)

        def by_lane(col):
            return jnp.sum(jnp.where(lane_e == sub_e, col, 0.0), axis=0, keepdims=True)

        end = base + n_vec
        end_off = end - jnp.floor(end * (1.0 / blk)) * blk
        fin_ref[...] = jnp.zeros_like(fin_ref)
        fin_ref[0:1, :] = by_lane(jnp.where(need_b, b_id, a_id)).astype(jnp.int32)
        fin_ref[1:2, :] = by_lane(end_off).astype(jnp.int32)
        fin_ref[2:3, :] = by_lane(nfree + total_alloc).astype(jnp.int32)
        fin_copy = pltpu.make_async_copy(fin_ref, fins_ref, csem)
        fin_copy.start()
        fin_copy.wait()

        hbuf[0] = jnp.zeros(hbuf.shape[1:], F32)

        def zero_rows(first_row, n_rows):
            cp = pltpu.make_async_copy(hbuf.at[0, pl.ds(0, n_rows * n_slab), :],
                                       xs_hbm.at[pl.ds(pl.multiple_of(first_row * n_slab, n_slab),
                                                       n_rows * n_slab), :], ssem.at[0])
            cp.start()
            cp.wait()

        for e in range(N_EXPERTS):
            pos = fins_ref[1, e]
            start = fins_ref[0, e] * blk
            for k in range(blk.bit_length() - 1):
                take = (fins_ref[1, e] > 0) & (((pos >> k) & 1) == 1)

                @pl.when(take)
                def _():
                    zero_rows(start + pos, 1 << k)
                pos = jnp.where(take, pos + (1 << k), pos)

        n_blk = xs_hbm.shape[0] // (blk * n_slab)
        for t in range(N_EXPERTS):
            spare = fins_ref[2, 0] + t

            @pl.when(spare < n_blk)
            def _():
                zero_rows(spare * blk, blk)


def _outproj(a_out, att, c_tm, x2d, mod3, w_out_bf, ln_g, ln_b, rw_hi, rw_lo, rbias, bsz, t, alpha):
    n, d = x2d.shape
    tm = min(ROW_TILE, t)
    assert tm == MOE_BLK and MOE_BLK & (MOE_BLK - 1) == 0
    nt = t // tm
    n_slab = d // LANES
    n_blk = n * TOP_K // MOE_BLK + N_EXPERTS
    tab_w = -(-n_blk // LANES) * LANES
    cw = c_tm.shape[1] // bsz
    tok = lambda b, i: (b * nt + i, 0)
    const2 = lambda b, i: (0, 0)
    head = jnp.arange(B_WIDTH) // B_HEAD_DIM
    expand = (jnp.arange(LANES)[:, None] == head[None, :]).astype(BF16)

    def branch_input(z, dil, w):
        if _dilation_layout(dil) == "copy":
            return z, pl.BlockSpec((1, dil, tm // dil, w), lambda b, i: (b, 0, i, 0))
        return z.reshape(n, w), pl.BlockSpec((tm, w), tok)

    o_in = [branch_input(o, dil, B_WIDTH) for (o, _), dil in zip(att, DILATIONS)]
    st_in = [branch_input(st, dil, LANES) for (_, st), dil in zip(att, DILATIONS)]
    (o1, st1), (o2, st2), (o3, st3) = [(o_in[g][0], st_in[g][0]) for g in range(len(DILATIONS))]
    modspec = lambda j: pl.BlockSpec((1, 1, d), lambda b, i: (b, 0, j))
    return pl.pallas_call(
        functools.partial(_outproj_kernel, alpha=alpha),
        grid=(bsz, nt),
        in_specs=[pl.BlockSpec((tm, A_WIDTH), tok)]
                 + [spec for _, spec in o_in]
                 + [spec for _, spec in st_in]
                 + [pl.BlockSpec((tm, cw), lambda b, i: (i, b)),
                  pl.BlockSpec((tm, d), tok),
                  modspec(2), modspec(3), modspec(4),
                  pl.BlockSpec((d, d), const2),
                  pl.BlockSpec((LANES, B_WIDTH), const2),
                  pl.BlockSpec((1, d), const2), pl.BlockSpec((1, d), const2),
                  pl.BlockSpec((N_EXPERTS, d), const2), pl.BlockSpec((N_EXPERTS, d), const2),
                  pl.BlockSpec((N_EXPERTS, 1), const2)],
        out_specs=[pl.BlockSpec((tm, d), tok),
                   pl.BlockSpec((SUBLANES, tm), lambda b, i: (0, b * nt + i)),
                   pl.BlockSpec((SUBLANES, tm), lambda b, i: (0, b * nt + i)),
                   pl.BlockSpec((SUBLANES, tab_w), const2),
                   pl.BlockSpec(memory_space=pl.ANY)],
        out_shape=[jax.ShapeDtypeStruct((n, d), F32),
                   jax.ShapeDtypeStruct((SUBLANES, n), jnp.int32),
                   jax.ShapeDtypeStruct((SUBLANES, n), F32),
                   jax.ShapeDtypeStruct((SUBLANES, tab_w), F32),
                   jax.ShapeDtypeStruct((n_blk * MOE_BLK * n_slab, LANES), F32)],
        scratch_shapes=[pltpu.VMEM((B_WIDTH // LANES, tm, LANES), F32),
                        pltpu.VMEM((B_WIDTH // LANES, tm, LANES), F32),
                        pltpu.VMEM((1, tm, LANES), F32), pltpu.VMEM((1, tm, LANES), F32),
                        pltpu.VMEM((2, tm * n_slab, LANES), F32),
                        pltpu.VMEM((3, N_EXPERTS, LANES), F32),
                        pltpu.SMEM((SUBLANES, tm), jnp.int32),
                        pltpu.VMEM((SUBLANES, LANES), jnp.int32),
                        pltpu.SMEM((SUBLANES, LANES), jnp.int32),
                        pltpu.SemaphoreType.DMA((2,)), pltpu.SemaphoreType.DMA],
        compiler_params=_cparams(("arbitrary", "arbitrary")),
        name="outproj_ln_router",
    )(a_out, o1, o2, o3, st1, st2, st3, c_tm, x2d, mod3, mod3, mod3, w_out_bf, expand,
      ln_g, ln_b, rw_hi, rw_lo, rbias)


def _block_order(tab_row, n_blk):
    tab = tab_row[:n_blk].astype(jnp.int32)
    idx = jnp.arange(n_blk, dtype=jnp.int32)
    before = (tab[None, :] < tab[:, None]) | ((tab[None, :] == tab[:, None]) & (idx[None, :] < idx[:, None]))
    pos = jnp.sum(before.astype(jnp.int32), axis=1)
    at_pos = pos[None, :] == idx[:, None]
    perm = jnp.sum(jnp.where(at_pos, idx[None, :], 0), axis=1).astype(jnp.int32)
    blk_e = jnp.minimum(jnp.sum(jnp.where(at_pos, tab[None, :], 0), axis=1), N_EXPERTS - 1).astype(jnp.int32)
    n_used = jnp.sum((tab < N_EXPERTS).astype(jnp.int32)).reshape(1)
    return perm, blk_e, n_used


def _expert_kernel(blk_e_ref, nused_ref, perm_ref, xs_ref, wg_ref, wu_ref, wd_ref, ys_ref, wg_bf, wu_bf, wd_bf,
                   *, blk, n_slab):
    j = pl.program_id(0)

    @pl.when((j == 0) | (blk_e_ref[j] != blk_e_ref[jnp.maximum(j - 1, 0)]))
    def _():
        wg_bf[...] = wg_ref[0, 0].astype(BF16)
        wu_bf[...] = wu_ref[0, 0].astype(BF16)
        wd_bf[...] = wd_ref[0, 0].astype(BF16)

    @pl.when(j < nused_ref[0])
    def _():
        rows = _load_tile_rows(xs_ref, 0, blk, n_slab).astype(BF16)
        gt = jnp.dot(rows, wg_bf[...], preferred_element_type=F32)
        up = jnp.dot(rows, wu_bf[...], preferred_element_type=F32)
        hid = (gt * jax.nn.sigmoid(gt) * up).astype(BF16)
        _store_tile_rows(ys_ref, jnp.dot(hid, wd_bf[...], preferred_element_type=F32))

    @pl.when(j >= nused_ref[0])
    def _():
        ys_ref[...] = jnp.zeros_like(ys_ref)


def _moe(xs, tab, w_gate, w_up, w_down, layer):
    d, de = w_gate.shape[-2:]
    n_slab = d // LANES
    blk = MOE_BLK
    n_blk = xs.shape[0] // (blk * n_slab)
    perm, blk_e, n_used = _block_order(tab[0], n_blk)

    def x_block(j, be, nu, pm):
        return (pm[jnp.minimum(j, nu[0] - 1)], 0)

    return pl.pallas_call(
        functools.partial(_expert_kernel, blk=blk, n_slab=n_slab),
        grid_spec=pltpu.PrefetchScalarGridSpec(
            num_scalar_prefetch=3,
            grid=(n_blk,),
            in_specs=[pl.BlockSpec((blk * n_slab, LANES), x_block),
                      pl.BlockSpec((1, 1, d, de), lambda j, be, nu, pm: (layer, be[j], 0, 0)),
                      pl.BlockSpec((1, 1, d, de), lambda j, be, nu, pm: (layer, be[j], 0, 0)),
                      pl.BlockSpec((1, 1, de, d), lambda j, be, nu, pm: (layer, be[j], 0, 0))],
            out_specs=pl.BlockSpec((blk * n_slab, LANES), lambda j, be, nu, pm: (pm[j], 0)),
            scratch_shapes=[pltpu.VMEM((d, de), BF16), pltpu.VMEM((d, de), BF16), pltpu.VMEM((de, d), BF16)],
        ),
        out_shape=jax.ShapeDtypeStruct(xs.shape, F32),
        compiler_params=_cparams(("arbitrary",)),
        name="moe_experts",
    )(blk_e, n_used, perm, xs, w_gate, w_up, w_down)


def _ffn_ln_kernel(dest_ref, next_ref, x_ref, gate_ref, g2_ref, lng_ref, lnb_ref, ys_hbm, o_ref, ybuf, sem,
                   *, alpha):
    tm, d = x_ref.shape
    n_slab = d // LANES
    g = pl.program_id(0) * pl.num_programs(1) + pl.program_id(1)
    n_tiles = pl.num_programs(0) * pl.num_programs(1)
    slot = g % 2

    def gather(idx_ref, buf_slot):
        def issue(pair, c):
            for q in range(2):
                s = 2 * pair + q
                src = pl.multiple_of(idx_ref[0, 0, s] * n_slab, n_slab)
                dst = pl.multiple_of(s * n_slab, n_slab)
                pltpu.async_copy(ys_hbm.at[pl.ds(src, n_slab), :],
                                 ybuf.at[buf_slot, pl.ds(dst, n_slab), :], sem.at[buf_slot], priority=q)
            return c
        lax.fori_loop(0, TOP_K * tm // 2, issue, 0, unroll=4)

    @pl.when(g == 0)
    def _():
        gather(dest_ref, 0)

    @pl.when(g + 1 < n_tiles)
    def _():
        gather(next_ref, 1 - slot)

    pltpu.make_async_copy(ys_hbm.at[pl.ds(0, TOP_K * tm * n_slab), :], ybuf.at[slot], sem.at[slot]).wait()

    gate = gate_ref[...]
    buf = ybuf.at[slot]
    ffn = _load_tile_rows(buf, 0, tm, n_slab) * gate[:, 0:1]
    for kk in range(1, TOP_K):
        ffn = ffn + _load_tile_rows(buf, kk * tm, tm, n_slab) * gate[:, kk:kk + 1]
    o_ref[...] = _ln(alpha * x_ref[...] + (1.0 + g2_ref[0]) * ffn) * lng_ref[...] + lnb_ref[...]


def _ffn_ln(x1, ys, dest, gates, mod3, ln_g, ln_b, bsz, t, alpha):
    n, d = x1.shape
    n_slab = d // LANES
    tm = min(ROW_TILE, t)
    nt = t // tm
    n_tiles = bsz * nt
    dest_tiles = dest.reshape(TOP_K, n_tiles, tm).transpose(1, 0, 2).reshape(n_tiles, 1, TOP_K * tm)
    tok = lambda b, i, *_: (b * nt + i, 0)
    const2 = lambda b, i, *_: (0, 0)
    gate_cols = gates[:TOP_K].T
    idx_blk = (1, 1, TOP_K * tm)
    return pl.pallas_call(
        functools.partial(_ffn_ln_kernel, alpha=alpha),
        grid=(bsz, nt),
        in_specs=[pl.BlockSpec(idx_blk, lambda b, i: (b * nt + i, 0, 0), memory_space=pltpu.SMEM),
                  pl.BlockSpec(idx_blk, lambda b, i: (jnp.minimum(b * nt + i + 1, n_tiles - 1), 0, 0),
                               memory_space=pltpu.SMEM),
                  pl.BlockSpec((tm, d), tok), pl.BlockSpec((tm, TOP_K), tok),
                  pl.BlockSpec((1, 1, d), lambda b, i: (b, 0, 5)),
                  pl.BlockSpec((1, d), const2), pl.BlockSpec((1, d), const2),
                  pl.BlockSpec(memory_space=pl.ANY)],
        out_specs=pl.BlockSpec((tm, d), tok),
        out_shape=jax.ShapeDtypeStruct((n, d), F32),
        scratch_shapes=[pltpu.VMEM((2, TOP_K * tm * n_slab, LANES), F32), pltpu.SemaphoreType.DMA((2,))],
        compiler_params=_cparams(("arbitrary", "arbitrary")),
        name="ffn_combine_ln",
    )(dest_tiles, dest_tiles, x1, gate_cols, mod3, ln_g, ln_b, ys)


def kernel(x, c, positions, ada_w, ada_b, w_in, gm_ln_g, gm_ln_b, gm_ws, gm_bs, ssm_lam_re, ssm_lam_im, ssm_log_dt, ssm_b_re, ssm_b_im, ssm_c_re, ssm_c_im, ssm_d, glu_w, glu_b, w_out, ln1_g, ln1_b, router_w, router_bias, exp_w_gate, exp_w_up, exp_w_down, ln2_g, ln2_b):
    bsz, t, d = x.shape
    depth = ada_w.shape[0]
    alpha = (2.0 * depth) ** 0.25
    n = bsz * t

    mod = _adaln_mod(c.astype(F32), ada_w, ada_b)
    cos_t, s1_t, s2_t = _rope_tables(positions)
    rw_t = router_w.astype(F32).T
    rw_hi = rw_t.astype(BF16)
    rw_lo = (rw_t - rw_hi.astype(F32)).astype(BF16)
    rbias = router_bias.astype(F32).reshape(N_EXPERTS, 1)

    xf = x.astype(F32).reshape(n, d)
    for l in range(depth):
        mod3 = mod[l].reshape(bsz, 1, 6 * d)
        bs_full = jnp.repeat(gm_bs[l].T, A_HEAD_DIM, axis=1)
        a_out, s_in, qkv = _inproj(
            xf, mod3, w_in[l].astype(BF16), cos_t, s1_t, s2_t,
            gm_ln_g[l].reshape(1, A_WIDTH), gm_ln_b[l].reshape(1, A_WIDTH), gm_ws[l], bs_full, bsz, t)
        att = [_att_branch(*qkv[g], dil, bsz, t) for g, dil in enumerate(DILATIONS)]
        bmat, cmat, a_re, a_im = _ssm_weights(ssm_lam_re[l], ssm_lam_im[l], ssm_log_dt[l],
                                              ssm_b_re[l], ssm_b_im[l], ssm_c_re[l], ssm_c_im[l])
        cw = bmat.shape[0]
        c_out = _ssm(s_in, bmat, cmat, a_re, a_im, ssm_d[l].reshape(1, cw),
                     glu_w[l].astype(BF16), glu_b[l].reshape(1, cw), bsz, t)
        x1, eidx, gates, tab, xs = _outproj(a_out, att, c_out, xf, mod3, w_out[l].astype(BF16),
                                            ln1_g[l].reshape(1, d), ln1_b[l].reshape(1, d),
                                            rw_hi, rw_lo, rbias, bsz, t, alpha)
        ys = _moe(xs, tab, exp_w_gate.astype(F32), exp_w_up.astype(F32), exp_w_down.astype(F32), l)
        xf = _ffn_ln(x1, ys, eidx[TOP_K:2 * TOP_K], gates, mod3, ln2_g[l].reshape(1, d), ln2_b[l].reshape(1, d),
                     bsz, t, alpha)
    return xf.reshape(bsz, t, d)
```

```python
import functools
import math

import jax
import jax.numpy as jnp
from jax import lax
from jax.experimental import pallas as pl
from jax.experimental.pallas import tpu as pltpu

F32 = jnp.float32
BF16 = jnp.bfloat16

A_HEADS = 4
A_HEAD_DIM = 64
A_WIDTH = A_HEADS * A_HEAD_DIM
CHUNK = 128
B_HEADS = 8
B_HEAD_DIM = 64
B_WIDTH = B_HEADS * B_HEAD_DIM
DILATIONS = (1, 4, 16)
ATT_BLK = 128
ROT_DIM = B_HEAD_DIM // 4
ROPE_THETA = 500000.0
N_EXPERTS = 16
N_EXPERT_GROUPS = 4
EXPERTS_PER_GROUP = 4
TOP_K = 2
LN_EPS = 1e-5
NEG_INF = -1e30
Q_SCALE = B_HEAD_DIM ** -0.5 * math.log2(math.e)

LANES = 128
SUBLANES = 8
VMEM_LIMIT = 56 * 1024 * 1024
ROW_TILE = 512
ROPE_ROWS = 2048
SSM_STEPS = 128
MOE_BLK = 512
ATT_SUB_BLOCKS = 8


def _cparams(sem):
    return pltpu.CompilerParams(dimension_semantics=sem, vmem_limit_bytes=VMEM_LIMIT)


def _ln(x):
    mu = jnp.mean(x, axis=-1, keepdims=True)
    xc = x - mu
    var = jnp.mean(xc * xc, axis=-1, keepdims=True)
    return xc * lax.rsqrt(var + LN_EPS)


def _store_tile_rows(ref, val):
    n_slab = val.shape[1] // LANES
    for j in range(n_slab):
        ref[pl.ds(j, val.shape[0], stride=n_slab), :] = val[:, j * LANES:(j + 1) * LANES]


def _load_tile_rows(ref, start, rows, n_slab):
    return jnp.concatenate([ref[pl.ds(start * n_slab + j, rows, stride=n_slab), :] for j in range(n_slab)],
                           axis=1)


def _gelu_tanh(x):
    return 0.5 * x * (1.0 + jnp.tanh(math.sqrt(2.0 / math.pi) * (x + 0.044715 * (x * x * x))))


def _mod_kernel(c_ref, w_ref, b_ref, o_ref):
    c = c_ref[...]
    cond = c * jax.nn.sigmoid(c)
    o_ref[0] = jnp.dot(cond, w_ref[0], precision=lax.Precision.HIGHEST,
                       preferred_element_type=F32) + b_ref[0]


def _adaln_mod(c, ada_w, ada_b):
    depth, d, n = ada_w.shape
    bsz = c.shape[0]
    nb = d
    return pl.pallas_call(
        _mod_kernel,
        grid=(depth, n // nb),
        in_specs=[pl.BlockSpec((bsz, d), lambda l, j: (0, 0)),
                  pl.BlockSpec((1, d, nb), lambda l, j: (l, 0, j)),
                  pl.BlockSpec((1, 1, nb), lambda l, j: (l, 0, j))],
        out_specs=pl.BlockSpec((1, bsz, nb), lambda l, j: (l, 0, j)),
        out_shape=jax.ShapeDtypeStruct((depth, bsz, n), F32),
        compiler_params=_cparams(("arbitrary", "arbitrary")),
        name="adaln_mod",
    )(c, ada_w, ada_b.reshape(depth, 1, n))


def _rope_kernel(pos_ref, freq_ref, ecos_ref, es1_ref, es2_ref, one_ref, cos_ref, s1_ref, s2_ref):
    ang = pos_ref[0].astype(F32) * freq_ref[...]
    cs = jnp.cos(ang)
    sn = jnp.sin(ang)
    tn = (((0,), (0,)), ((), ()))

    def pieces(x):
        p1 = x.astype(BF16)
        r1 = x - p1.astype(F32)
        p2 = r1.astype(BF16)
        p3 = (r1 - p2.astype(F32)).astype(BF16)
        return p1, p2, p3

    def spread(parts, e_ref):
        e = e_ref[...]
        return sum(lax.dot_general(p, e, tn, preferred_element_type=F32) for p in parts)

    cs_parts = pieces(cs)
    sn_parts = pieces(sn)
    cos_ref[...] = spread(cs_parts, ecos_ref) + one_ref[...]
    s1_ref[...] = spread(sn_parts, es1_ref)
    s2_ref[...] = spread(sn_parts, es2_ref)


def _rope_tables(positions):
    n = positions.size
    half = ROT_DIM // 2
    freqs = (ROPE_THETA ** (-jnp.arange(half, dtype=F32) * 2.0 / ROT_DIM)).reshape(half, 1)
    lane = jnp.arange(LANES) % B_HEAD_DIM
    f_of_lane = lane % half
    pick = (jnp.arange(half)[:, None] == f_of_lane[None, :]).astype(F32)
    ecos = pick * (lane < ROT_DIM)[None, :]
    es1 = -pick * (lane < half)[None, :]
    es2 = pick * ((lane >= half) & (lane < ROT_DIM))[None, :]
    one_row = (lane >= ROT_DIM).astype(F32).reshape(1, LANES)
    tm = min(ROPE_ROWS, n)
    out = jax.ShapeDtypeStruct((n, LANES), F32)
    spec = pl.BlockSpec((tm, LANES), lambda i: (i, 0))
    sel = pl.BlockSpec((half, LANES), lambda i: (0, 0))
    return pl.pallas_call(
        _rope_kernel,
        grid=(n // tm,),
        in_specs=[pl.BlockSpec((1, 1, tm), lambda i: (i, 0, 0)),
                  pl.BlockSpec((half, 1), lambda i: (0, 0)),
                  sel, sel, sel, pl.BlockSpec((1, LANES), lambda i: (0, 0))],
        out_specs=[spec, spec, spec],
        out_shape=[out, out, out],
        compiler_params=_cparams(("arbitrary",)),
        name="rope_tables",
    )(positions.reshape(n // tm, 1, tm), freqs, ecos.astype(BF16), es1.astype(BF16), es2.astype(BF16), one_row)


def _inproj_kernel(x_ref, sh_ref, sc_ref, w_ref, cos_ref, s1_ref, s2_ref, lng_ref, lnb_ref,
                   ws_ref, bs_ref, a_ref, s_ref, qn_ref, kn_ref, vn_ref, *rest):
    copy_dils = [dil for dil in DILATIONS if _dilation_layout(dil) == "copy"]
    n_copy = len(copy_dils)
    q_outs, k_outs, v_outs = rest[:n_copy], rest[n_copy:2 * n_copy], rest[2 * n_copy:3 * n_copy]
    qs_ref, ks_ref, vs_ref, qt_ref, kt_ref, vt_ref = rest[3 * n_copy:]
    tm = x_ref.shape[0]
    h = _ln(x_ref[...]) * (1.0 + sc_ref[0]) + sh_ref[0]
    proj = jnp.dot(h.astype(BF16), w_ref[...], preferred_element_type=F32)

    uv = _gelu_tanh(proj[:, :2 * A_WIDTH])
    u = uv[:, :A_WIDTH]
    v = (_ln(uv[:, A_WIDTH:]) * lng_ref[...] + lnb_ref[...]).astype(BF16)
    row = lax.broadcasted_iota(jnp.int32, (CHUNK, CHUNK), 0)
    col = lax.broadcasted_iota(jnp.int32, (CHUNK, CHUNK), 1)
    head_of_lane = lax.broadcasted_iota(jnp.int32, (CHUNK, A_WIDTH), 1) // A_HEAD_DIM
    w_heads = [jnp.where(col <= row, ws_ref[hd], 0.0).astype(BF16) for hd in range(A_HEADS)]
    for cidx in range(tm // CHUNK):
        rows = slice(cidx * CHUNK, (cidx + 1) * CHUNK)
        vc = v[rows]
        sv = bs_ref[...]
        for hd in range(A_HEADS):
            full = jnp.dot(w_heads[hd], vc, preferred_element_type=F32)
            sv = sv + jnp.where(head_of_lane == hd, full, 0.0)
        a_ref[rows, :] = (u[rows] * sv).astype(a_ref.dtype)

    cos = cos_ref[...]
    s1 = s1_ref[...]
    s2 = s2_ref[...]
    q0 = 2 * A_WIDTH
    k0 = q0 + B_WIDTH
    v0 = k0 + B_WIDTH
    for j in range(B_WIDTH // LANES):
        lanes = slice(j * LANES, (j + 1) * LANES)
        for base, ref, nat, scale in ((q0, qs_ref, qn_ref, Q_SCALE), (k0, ks_ref, kn_ref, 1.0)):
            xs = proj[:, base + j * LANES: base + (j + 1) * LANES]
            rot = (xs * cos + pltpu.roll(xs, LANES - ROT_DIM // 2, 1) * s1
                   + pltpu.roll(xs, ROT_DIM // 2, 1) * s2) * scale
            nat[:, lanes] = rot.astype(nat.dtype)
            ref[j] = rot
        vs = proj[:, v0 + j * LANES:v0 + (j + 1) * LANES]
        vn_ref[:, lanes] = vs.astype(vn_ref.dtype)
        vs_ref[j] = vs

    for src, stage, outs in ((qs_ref, qt_ref, q_outs), (ks_ref, kt_ref, k_outs), (vs_ref, vt_ref, v_outs)):
        bufs = (src, stage)
        prev = 1
        for idx, (dil, out) in enumerate(zip(copy_dils, outs)):
            chained = prev > 1 and dil % prev == 0
            base_dil = prev if chained else 1
            source = bufs[idx % 2] if chained or idx == 0 else None
            assert source is not None, "copy dilations must form a divisibility chain"
            step = dil // base_dil
            per = tm // dil
            keep = idx + 1 < len(copy_dils)
            for r_prev in range(base_dil):
                for q in range(step):
                    r = r_prev + base_dil * q
                    start = r_prev * (tm // base_dil) + q
                    for j in range(B_WIDTH // LANES):
                        val = source[j, pl.ds(start, per, stride=step), :]
                        out[0, r, :, j * LANES:(j + 1) * LANES] = val.astype(out.dtype)
                        if keep:
                            bufs[(idx + 1) % 2][j, r * per:(r + 1) * per, :] = val
            prev = dil

    s_ref[...] = proj[:, v0 + B_WIDTH:].astype(s_ref.dtype)


def _inproj(x2d, mod3, w_in_bf, cos_t, s1_t, s2_t, ln_g, ln_b, ws, bs_full, bsz, t):
    n, d = x2d.shape
    tm = min(ROW_TILE, t)
    nt = t // tm
    pw = w_in_bf.shape[1]
    c_width = pw - 2 * A_WIDTH - 3 * B_WIDTH
    tok = lambda b, i: (b * nt + i, 0)
    const2 = lambda b, i: (0, 0)
    outs = [jax.ShapeDtypeStruct((n, A_WIDTH), BF16),
            jax.ShapeDtypeStruct((t, bsz * c_width), BF16)]
    out_specs = [pl.BlockSpec((tm, A_WIDTH), tok),
                 pl.BlockSpec((tm, c_width), lambda b, i: (i, b))]
    for _ in range(3):
        outs.append(jax.ShapeDtypeStruct((n, B_WIDTH), BF16))
        out_specs.append(pl.BlockSpec((tm, B_WIDTH), tok))
    copy_dils = [dil for dil in DILATIONS if _dilation_layout(dil) == "copy"]
    for _ in range(3):
        for dil in copy_dils:
            outs.append(jax.ShapeDtypeStruct((bsz, dil, t // dil, B_WIDTH), BF16))
            out_specs.append(pl.BlockSpec((1, dil, tm // dil, B_WIDTH), lambda b, i: (b, 0, i, 0)))
    res = pl.pallas_call(
        _inproj_kernel,
        grid=(bsz, nt),
        in_specs=[pl.BlockSpec((tm, d), tok),
                  pl.BlockSpec((1, 1, d), lambda b, i: (b, 0, 0)),
                  pl.BlockSpec((1, 1, d), lambda b, i: (b, 0, 1)),
                  pl.BlockSpec((d, pw), const2),
                  pl.BlockSpec((tm, LANES), tok),
                  pl.BlockSpec((tm, LANES), tok),
                  pl.BlockSpec((tm, LANES), tok),
                  pl.BlockSpec((1, A_WIDTH), const2),
                  pl.BlockSpec((1, A_WIDTH), const2),
                  pl.BlockSpec((A_HEADS, CHUNK, CHUNK), lambda b, i: (0, 0, 0)),
                  pl.BlockSpec((CHUNK, A_WIDTH), const2)],
        out_specs=out_specs,
        out_shape=outs,
        scratch_shapes=[pltpu.VMEM((B_WIDTH // LANES, tm, LANES), F32)] * 6,
        compiler_params=_cparams(("arbitrary", "arbitrary")),
        name="inproj_gmlp_rope",
    )(x2d, mod3, mod3, w_in_bf, cos_t, s1_t, s2_t, ln_g, ln_b, ws, bs_full)
    a_out, s_in = res[0], res[1]
    natural = res[2:5]
    n_copy = len(copy_dils)
    copies = {dil: tuple(res[5 + z * n_copy + g] for z in range(3)) for g, dil in enumerate(copy_dils)}
    qkv = [copies[dil] if dil in copies else natural for dil in DILATIONS]
    return a_out, s_in, qkv


def _att_kernel(q_ref, k_ref, v_ref, o_ref, st_ref, kcat_ref, vaug_ref):
    i = pl.program_id(2)
    blk = ATT_BLK
    n_res = q_ref.shape[1]
    n_sub = q_ref.shape[2] // blk
    n_pairs = B_WIDTH // LANES
    last = slice(n_sub * blk, (n_sub + 1) * blk)

    @pl.when(i == 0)
    def _():
        kcat_ref[:, 0:blk, :] = jnp.zeros((n_res, blk, B_WIDTH), BF16)
        vaug_ref[:, :, 0:blk, 0:LANES] = jnp.zeros((n_res, n_pairs, blk, LANES), BF16)
        vaug_ref[:, :, :, LANES:] = jnp.ones((n_res, n_pairs, (n_sub + 1) * blk, LANES), BF16)

    @pl.when(i > 0)
    def _():
        kcat_ref[:, 0:blk, :] = kcat_ref[:, last, :]
        vaug_ref[:, :, 0:blk, 0:LANES] = vaug_ref[:, :, last, 0:LANES]

    for rr in range(n_res):
        kcat_ref[rr, blk:, :] = k_ref[0, rr]
        for pair in range(n_pairs):
            vaug_ref[rr, pair, blk:, 0:LANES] = v_ref[0, rr, :, pair * LANES:(pair + 1) * LANES]

    qi = lax.broadcasted_iota(jnp.int32, (blk, 2 * blk), 0)
    kk = lax.broadcasted_iota(jnp.int32, (blk, 2 * blk), 1)
    band = (kk >= qi) & (kk <= qi + blk)
    first = band & ((i > 0) | (kk >= blk))
    lane = lax.broadcasted_iota(jnp.int32, (blk, LANES), 1)
    low_half = lane < B_HEAD_DIM
    nt = (((1,), (1,)), ((), ()))
    for rr, sub in [(rr, sub) for rr in range(n_res) for sub in range(n_sub)]:
        rows = slice(sub * blk, (sub + 1) * blk)
        keys = slice(sub * blk, (sub + 2) * blk)
        mask = first if sub == 0 else band
        stats = jnp.zeros((blk, LANES), F32)
        for pair in range(n_pairs):
            lanes = slice(pair * LANES, (pair + 1) * LANES)
            q2 = q_ref[0, rr, rows, lanes]
            kslab = kcat_ref[rr, keys, lanes]
            res = []
            for hh in range(2):
                hd = 2 * pair + hh
                qm = jnp.where(low_half if hh == 0 else jnp.logical_not(low_half), q2, jnp.zeros_like(q2))
                s = lax.dot_general(qm, kslab, nt, preferred_element_type=F32)
                s = jnp.where(mask, s, NEG_INF)
                m = jnp.max(jnp.maximum(s[:, :blk], s[:, blk:]), axis=1, keepdims=True)
                p = jnp.exp2(s - m).astype(BF16)
                ov = jnp.dot(p, vaug_ref[rr, pair, keys, :], preferred_element_type=F32)
                den = ov[:, LANES:]
                res.append(ov[:, :LANES] / den)
                stats = jnp.where(lane == hd, m, stats)
                stats = jnp.where(lane == B_HEADS + hd, den, stats)
            o_ref[0, rr, rows, lanes] = jnp.where(low_half, res[0], res[1]).astype(o_ref.dtype)
        st_ref[0, rr, rows, :] = stats


def _dilation_layout(dil):
    return "natural" if dil == 1 else "copy"


def _att_branch(q, k, v, dil, bsz, t):
    rows = t // dil
    n_sub = min(ATT_SUB_BLOCKS, rows // ATT_BLK)
    step_rows = n_sub * ATT_BLK
    n_steps = rows // step_rows
    n_res = math.gcd(dil, max(1, ATT_SUB_BLOCKS // n_sub)) if n_steps == 1 else 1
    shape = lambda w: (bsz, dil, rows, w)
    blk = lambda w: pl.BlockSpec((1, n_res, step_rows, w), lambda b, r, i: (b, r, i, 0))
    q, k, v = (z.reshape(shape(B_WIDTH)) for z in (q, k, v))
    return pl.pallas_call(
        _att_kernel,
        grid=(bsz, dil // n_res, n_steps),
        in_specs=[blk(B_WIDTH), blk(B_WIDTH), blk(B_WIDTH)],
        out_specs=[blk(B_WIDTH), blk(LANES)],
        out_shape=[jax.ShapeDtypeStruct(shape(B_WIDTH), BF16),
                   jax.ShapeDtypeStruct(shape(LANES), F32)],
        scratch_shapes=[pltpu.VMEM((n_res, step_rows + ATT_BLK, B_WIDTH), BF16),
                        pltpu.VMEM((n_res, B_WIDTH // LANES, step_rows + ATT_BLK, 2 * LANES), BF16)],
        compiler_params=_cparams(("arbitrary", "arbitrary", "arbitrary")),
        name=f"dilated_attn_d{dil}",
    )(q, k, v)


def _ssm_kernel(u_ref, bmat_ref, cmat_ref, are_ref, aim_ref, dskip_ref, gw_ref, gb_ref,
                o_ref, state_ref, xbuf_ref, rows_ref):
    bsz, two_n = state_ref.shape
    n_state = two_n // 2
    steps = u_ref.shape[0]
    cw = u_ref.shape[1] // bsz
    n_slab = cw // LANES

    @pl.when(pl.program_id(0) == 0)
    def _():
        state_ref[...] = jnp.zeros_like(state_ref)

    for b in range(bsz):
        for j in range(n_slab):
            rows_ref[j, pl.ds(b, steps, stride=bsz), :] = (
                u_ref[:, b * cw + j * LANES:b * cw + (j + 1) * LANES].astype(F32))
    u32 = jnp.concatenate([rows_ref[j] for j in range(n_slab)], axis=1)
    xbuf_ref[...] = jnp.dot(u32.astype(BF16), bmat_ref[...], preferred_element_type=F32)
    a_re = jnp.broadcast_to(are_ref[...], (bsz, n_state))
    a_im = jnp.broadcast_to(aim_ref[...], (bsz, n_state))

    def step(tt, carry):
        x_re, x_im = carry
        r0 = pl.multiple_of(tt * bsz, bsz)
        b_re = xbuf_ref[pl.ds(r0, bsz), :n_state]
        b_im = xbuf_ref[pl.ds(r0, bsz), n_state:]
        n_re = a_re * x_re - a_im * x_im + b_re
        n_im = a_re * x_im + a_im * x_re + b_im
        xbuf_ref[pl.ds(r0, bsz), :n_state] = n_re
        xbuf_ref[pl.ds(r0, bsz), n_state:] = n_im
        return n_re, n_im

    x_re, x_im = lax.fori_loop(0, steps, step, (state_ref[:, :n_state], state_ref[:, n_state:]))
    state_ref[:, :n_state] = x_re
    state_ref[:, n_state:] = x_im

    y = jnp.dot(xbuf_ref[...].astype(BF16), cmat_ref[...], preferred_element_type=F32)
    y = _gelu_tanh(y + dskip_ref[...] * u32)
    gate = jax.nn.sigmoid(jnp.dot(y.astype(BF16), gw_ref[...], preferred_element_type=F32) + gb_ref[...])
    out = y * gate
    for j in range(n_slab):
        rows_ref[j] = out[:, j * LANES:(j + 1) * LANES]
    for b in range(bsz):
        for j in range(n_slab):
            o_ref[:, b * cw + j * LANES:b * cw + (j + 1) * LANES] = (
                rows_ref[j, pl.ds(b, steps, stride=bsz), :].astype(o_ref.dtype))


def _ssm_weights(lam_re, lam_im, log_dt, b_re, b_im, c_re, c_im):
    g, n_st = lam_re.shape
    cg = b_re.shape[-1]
    dt = jnp.exp(log_dt)[:, None]
    mag = jnp.exp(lam_re * dt)
    ab_re = mag * jnp.cos(lam_im * dt)
    ab_im = mag * jnp.sin(lam_im * dt)
    nr = ab_re - 1.0
    ni = ab_im
    mod2 = lam_re * lam_re + lam_im * lam_im
    f_re = (nr * lam_re + ni * lam_im) / mod2
    f_im = (ni * lam_re - nr * lam_im) / mod2
    bb_re = f_re[..., None] * b_re - f_im[..., None] * b_im
    bb_im = f_re[..., None] * b_im + f_im[..., None] * b_re
    eye = jnp.eye(g, dtype=F32)
    bm_re = jnp.einsum('gnc,gh->gchn', bb_re, eye).reshape(g * cg, g * n_st)
    bm_im = jnp.einsum('gnc,gh->gchn', bb_im, eye).reshape(g * cg, g * n_st)
    bmat = jnp.concatenate([bm_re, bm_im], axis=1)
    cm_re = jnp.einsum('gcn,gh->gnhc', c_re, eye).reshape(g * n_st, g * cg)
    cm_im = jnp.einsum('gcn,gh->gnhc', c_im, eye).reshape(g * n_st, g * cg)
    cmat = jnp.concatenate([cm_re, -cm_im], axis=0)
    return (bmat.astype(BF16), cmat.astype(BF16),
            ab_re.reshape(1, g * n_st), ab_im.reshape(1, g * n_st))


def _ssm(u_tm, bmat, cmat, a_re, a_im, d_skip, glu_w_bf, glu_b, bsz, t):
    cw = bmat.shape[0]
    two_n = bmat.shape[1]
    steps = min(SSM_STEPS, t)
    rows = steps * bsz
    const = lambda i: (0, 0)
    return pl.pallas_call(
        _ssm_kernel,
        grid=(t // steps,),
        in_specs=[pl.BlockSpec((steps, bsz * cw), lambda i: (i, 0)),
                  pl.BlockSpec((cw, two_n), const),
                  pl.BlockSpec((two_n, cw), const),
                  pl.BlockSpec((1, two_n // 2), const),
                  pl.BlockSpec((1, two_n // 2), const),
                  pl.BlockSpec((1, cw), const),
                  pl.BlockSpec((cw, cw), const),
                  pl.BlockSpec((1, cw), const)],
        out_specs=pl.BlockSpec((steps, bsz * cw), lambda i: (i, 0)),
        out_shape=jax.ShapeDtypeStruct((t, bsz * cw), BF16),
        scratch_shapes=[pltpu.VMEM((bsz, two_n), F32), pltpu.VMEM((rows, two_n), F32),
                        pltpu.VMEM((cw // LANES, rows, LANES), F32)],
        compiler_params=_cparams(("arbitrary",)),
        name="s5_scan_glu",
    )(u_tm, bmat, cmat, a_re, a_im, d_skip, glu_w_bf, glu_b)


def _route(sel, scores):
    gs = []
    for g in range(N_EXPERT_GROUPS):
        a, b, c, d = sel[EXPERTS_PER_GROUP * g: EXPERTS_PER_GROUP * (g + 1)]
        hi1, lo1 = jnp.maximum(a, b), jnp.minimum(a, b)
        hi2, lo2 = jnp.maximum(c, d), jnp.minimum(c, d)
        gs.append(jnp.maximum(hi1, hi2) + jnp.maximum(jnp.minimum(hi1, hi2), jnp.maximum(lo1, lo2)))
    g_idx = jnp.zeros(gs[0].shape, jnp.int32)
    best = gs[0]
    for g in range(1, N_EXPERT_GROUPS):
        better = gs[g] > best
        g_idx = jnp.where(better, g, g_idx)
        best = jnp.where(better, gs[g], best)

    def pick(rows, j):
        out = rows[j]
        for g in range(1, N_EXPERT_GROUPS):
            out = jnp.where(g_idx == g, rows[EXPERTS_PER_GROUP * g + j], out)
        return out

    v = [pick(sel, j) for j in range(EXPERTS_PER_GROUP)]
    s = [pick(scores, j) for j in range(EXPERTS_PER_GROUP)]
    i1 = jnp.zeros(g_idx.shape, jnp.int32)
    b1, g1 = v[0], s[0]
    for j in range(1, EXPERTS_PER_GROUP):
        better = v[j] > b1
        i1 = jnp.where(better, j, i1)
        b1 = jnp.where(better, v[j], b1)
        g1 = jnp.where(better, s[j], g1)
    i2 = jnp.full(g_idx.shape, -1, jnp.int32)
    b2 = jnp.zeros_like(b1)
    g2 = jnp.zeros_like(g1)
    for j in range(EXPERTS_PER_GROUP):
        better = (i1 != j) & ((i2 < 0) | (v[j] > b2))
        i2 = jnp.where(better, j, i2)
        b2 = jnp.where(better, v[j], b2)
        g2 = jnp.where(better, s[j], g2)
    tot = g1 + g2
    base = g_idx * EXPERTS_PER_GROUP
    return base + i1, base + i2, g1 / tot, g2 / tot


def _natural_order(ref, scr_ref, dil):
    if _dilation_layout(dil) != "copy":
        return ref[...].astype(F32)
    per = ref.shape[2]
    n_slab = scr_ref.shape[0]
    for r in range(dil):
        for j in range(n_slab):
            scr_ref[j, pl.ds(r, per, stride=dil), :] = ref[0, r, :, j * LANES:(j + 1) * LANES].astype(F32)
    if n_slab == 1:
        return scr_ref[0]
    return jnp.concatenate([scr_ref[j] for j in range(n_slab)], axis=1)


def _outproj_kernel(a_ref, o1_ref, o2_ref, o3_ref, st1_ref, st2_ref, st3_ref,
                    c_ref, x_ref, g1_ref, sh2_ref, sc2_ref, wout_ref, expand_ref, lng_ref, lnb_ref,
                    rwh_ref, rwl_ref, rb_ref, x1_ref, eidx_ref, gate_ref, tab_ref, xs_hbm,
                    oscr2_ref, oscr3_ref, sscr2_ref, sscr3_ref, hbuf, state_ref, slot_ref, fin_ref, fins_ref,
                    ssem, csem, *, alpha):
    o_refs = (o1_ref, o2_ref, o3_ref)
    st_refs = (st1_ref, st2_ref, st3_ref)
    o_scr = (None, oscr2_ref, oscr3_ref)
    st_scr = (None, sscr2_ref, sscr3_ref)
    ms = [_natural_order(st_refs[g], st_scr[g], DILATIONS[g]) for g in range(len(DILATIONS))]
    dens = [pltpu.roll(m, LANES - B_HEADS, 1) for m in ms]
    mx = jnp.maximum(jnp.maximum(ms[0], ms[1]), ms[2])
    ws = [dens[g] * jnp.exp2(ms[g] - mx) for g in range(len(DILATIONS))]
    lane = lax.broadcasted_iota(jnp.int32, mx.shape, 1)
    tot = jnp.where(lane < B_HEADS, ws[0] + ws[1] + ws[2], 1.0)
    expand = expand_ref[...]

    def widen(w):
        wn = jnp.where(lane < B_HEADS, w / tot, 0.0)
        hi = wn.astype(BF16)
        lo = (wn - hi.astype(F32)).astype(BF16)
        return (jnp.dot(hi, expand, preferred_element_type=F32)
                + jnp.dot(lo, expand, preferred_element_type=F32))

    b_out = widen(ws[0]) * _natural_order(o_refs[0], o_scr[0], DILATIONS[0])
    for g in range(1, len(DILATIONS)):
        b_out = b_out + widen(ws[g]) * _natural_order(o_refs[g], o_scr[g], DILATIONS[g])
    b_out = b_out.astype(BF16)

    wout = wout_ref
    mix = (jnp.dot(a_ref[...], wout[:A_WIDTH, :], preferred_element_type=F32)
           + jnp.dot(b_out, wout[A_WIDTH:A_WIDTH + B_WIDTH, :], preferred_element_type=F32)
           + jnp.dot(c_ref[...], wout[A_WIDTH + B_WIDTH:, :], preferred_element_type=F32))
    x1 = _ln(alpha * x_ref[...] + (1.0 + g1_ref[0]) * mix) * lng_ref[...] + lnb_ref[...]
    x1_ref[...] = x1
    h2 = _ln(x1) * (1.0 + sc2_ref[0]) + sh2_ref[0]

    hi = h2.astype(BF16)
    lo = (h2 - hi.astype(F32)).astype(BF16)
    nt = (((1,), (1,)), ((), ()))
    logits = (lax.dot_general(rwh_ref[...], hi, nt, preferred_element_type=F32)
              + lax.dot_general(rwh_ref[...], lo, nt, preferred_element_type=F32)
              + lax.dot_general(rwl_ref[...], hi, nt, preferred_element_type=F32))
    scores = jax.nn.sigmoid(logits)
    sel = scores + rb_ref[...]
    sel_rows = [sel[e:e + 1, :] for e in range(N_EXPERTS)]
    score_rows = [scores[e:e + 1, :] for e in range(N_EXPERTS)]
    e1, e2, gt1, gt2 = _route(sel_rows, score_rows)

    g = pl.program_id(0) * pl.num_programs(1) + pl.program_id(1)
    n_tiles = pl.num_programs(0) * pl.num_programs(1)
    tm = h2.shape[0]
    n_slab = h2.shape[1] // LANES
    blk = MOE_BLK
    par = g % 2

    @pl.when(g == 0)
    def _():
        state_ref[...] = jnp.zeros_like(state_ref)
        tab_ref[...] = jnp.full(tab_ref.shape, float(N_EXPERTS), F32)

    erow = lax.broadcasted_iota(jnp.int32, (N_EXPERTS, tm), 0)
    oh1 = erow == e1
    oh2 = erow == e2
    both = (oh1 | oh2).astype(BF16)
    earlier = (lax.broadcasted_iota(jnp.int32, (tm, tm), 0)
               < lax.broadcasted_iota(jnp.int32, (tm, tm), 1)).astype(BF16)
    base = state_ref[0, :, 0:1]
    cur = state_ref[1, :, 0:1]
    nfree = state_ref[2, :, 0:1]
    n_vec = jnp.sum(both.astype(F32), axis=1, keepdims=True)
    off = base - jnp.floor(base * (1.0 / blk)) * blk
    need_a = (off == 0.0) & (n_vec > 0.0)
    need_b = (off + n_vec) > float(blk)
    na = need_a.astype(F32)
    nb = need_b.astype(F32)
    alloc = na + nb
    lower = (lax.broadcasted_iota(jnp.int32, (N_EXPERTS, N_EXPERTS), 1)
             < lax.broadcasted_iota(jnp.int32, (N_EXPERTS, N_EXPERTS), 0)).astype(BF16)
    alloc_wide = jnp.broadcast_to(alloc, (N_EXPERTS, LANES))
    before = jnp.dot(lower, alloc_wide.astype(BF16), preferred_element_type=F32)[:, 0:1]
    a_id = jnp.where(need_a, nfree + before, cur)
    b_id = jnp.where(need_b, nfree + before + na, a_id)
    bound = base - off + float(blk)
    total_alloc = jnp.sum(alloc_wide, axis=0, keepdims=True)[:, 0:1]
    state_ref[0] = jnp.broadcast_to(base + n_vec, (N_EXPERTS, LANES))
    state_ref[1] = jnp.broadcast_to(jnp.where(need_b, b_id, a_id), (N_EXPERTS, LANES))
    state_ref[2] = jnp.broadcast_to(nfree + total_alloc, (N_EXPERTS, LANES))

    blane = lax.broadcasted_iota(jnp.int32, (N_EXPERTS, tab_ref.shape[1]), 1).astype(F32)
    hit = ((blane == a_id) & need_a) | ((blane == b_id) & need_b)
    e_col = lax.broadcasted_iota(jnp.int32, (N_EXPERTS, 1), 0).astype(F32)
    delta = jnp.sum(jnp.where(hit, e_col - float(N_EXPERTS), 0.0), axis=0, keepdims=True)
    tab_ref[...] = tab_ref[...] + jnp.broadcast_to(delta, tab_ref.shape)

    rank = jnp.dot(both, earlier, preferred_element_type=F32) + base
    slot = jnp.where(rank < bound, a_id * blk + (rank - bound + blk), b_id * blk + (rank - bound))
    s1 = jnp.sum(jnp.where(oh1, slot, 0.0), axis=0, keepdims=True)
    s2 = jnp.sum(jnp.where(oh2, slot, 0.0), axis=0, keepdims=True)

    eidx_ref[...] = jnp.zeros_like(eidx_ref)
    gate_ref[...] = jnp.zeros_like(gate_ref)
    eidx_ref[0:1, :] = e1
    eidx_ref[1:2, :] = e2
    eidx_ref[2:3, :] = s1.astype(jnp.int32)
    eidx_ref[3:4, :] = s2.astype(jnp.int32)
    gate_ref[0:1, :] = gt1
    gate_ref[1:2, :] = gt2

    slot_copy = pltpu.make_async_copy(eidx_ref, slot_ref, csem)
    slot_copy.start()

    def wait_rows(p):
        for _ in range(TOP_K):
            pltpu.make_async_copy(hbuf.at[p], xs_hbm.at[pl.ds(0, tm * n_slab), :], ssem.at[p]).wait()

    @pl.when(g >= 2)
    def _():
        wait_rows(par)

    _store_tile_rows(hbuf.at[par], h2)
    slot_copy.wait()

    rows_src = hbuf.at[par]
    for s in range(tm):
        src = rows_src.at[pl.ds(s * n_slab, n_slab), :]
        for kk in range(TOP_K):
            dst = pl.multiple_of(slot_ref[TOP_K + kk, s] * n_slab, n_slab)
            pltpu.async_copy(src, xs_hbm.at[pl.ds(dst, n_slab), :], ssem.at[par], priority=kk % 2)

    @pl.when(g == n_tiles - 1)
    def _():
        wait_rows(par)

        @pl.when(n_tiles >= 2)
        def _():
            wait_rows(1 - par)

        lane_e = lax.broadcasted_iota(jnp.int32, (N_EXPERTS, LANES), 1)
        sub_e = lax.broadcasted_iota(jnp.int32, (N_EXPERTS, LANES), 0)

        def by_lane(col):
            return jnp.sum(jnp.where(lane_e == sub_e, col, 0.0), axis=0, keepdims=True)

        end = base + n_vec
        end_off = end - jnp.floor(end * (1.0 / blk)) * blk
        fin_ref[...] = jnp.zeros_like(fin_ref)
        fin_ref[0:1, :] = by_lane(jnp.where(need_b, b_id, a_id)).astype(jnp.int32)
        fin_ref[1:2, :] = by_lane(end_off).astype(jnp.int32)
        fin_ref[2:3, :] = by_lane(nfree + total_alloc).astype(jnp.int32)
        fin_copy = pltpu.make_async_copy(fin_ref, fins_ref, csem)
        fin_copy.start()
        fin_copy.wait()

        hbuf[0] = jnp.zeros(hbuf.shape[1:], F32)

        def zero_rows(first_row, n_rows):
            cp = pltpu.make_async_copy(hbuf.at[0, pl.ds(0, n_rows * n_slab), :],
                                       xs_hbm.at[pl.ds(pl.multiple_of(first_row * n_slab, n_slab),
                                                       n_rows * n_slab), :], ssem.at[0])
            cp.start()
            cp.wait()

        for e in range(N_EXPERTS):
            pos = fins_ref[1, e]
            start = fins_ref[0, e] * blk
            for k in range(blk.bit_length() - 1):
                take = (fins_ref[1, e] > 0) & (((pos >> k) & 1) == 1)

                @pl.when(take)
                def _():
                    zero_rows(start + pos, 1 << k)
                pos = jnp.where(take, pos + (1 << k), pos)

        n_blk = xs_hbm.shape[0] // (blk * n_slab)
        for t in range(N_EXPERTS):
            spare = fins_ref[2, 0] + t

            @pl.when(spare < n_blk)
            def _():
                zero_rows(spare * blk, blk)


def _outproj(a_out, att, c_tm, x2d, mod3, w_out_bf, ln_g, ln_b, rw_hi, rw_lo, rbias, bsz, t, alpha):
    n, d = x2d.shape
    tm = min(ROW_TILE, t)
    assert tm == MOE_BLK and MOE_BLK & (MOE_BLK - 1) == 0
    nt = t // tm
    n_slab = d // LANES
    n_blk = n * TOP_K // MOE_BLK + N_EXPERTS
    tab_w = -(-n_blk // LANES) * LANES
    cw = c_tm.shape[1] // bsz
    tok = lambda b, i: (b * nt + i, 0)
    const2 = lambda b, i: (0, 0)
    head = jnp.arange(B_WIDTH) // B_HEAD_DIM
    expand = (jnp.arange(LANES)[:, None] == head[None, :]).astype(BF16)

    def branch_input(z, dil, w):
        if _dilation_layout(dil) == "copy":
            return z, pl.BlockSpec((1, dil, tm // dil, w), lambda b, i: (b, 0, i, 0))
        return z.reshape(n, w), pl.BlockSpec((tm, w), tok)

    o_in = [branch_input(o, dil, B_WIDTH) for (o, _), dil in zip(att, DILATIONS)]
    st_in = [branch_input(st, dil, LANES) for (_, st), dil in zip(att, DILATIONS)]
    (o1, st1), (o2, st2), (o3, st3) = [(o_in[g][0], st_in[g][0]) for g in range(len(DILATIONS))]
    modspec = lambda j: pl.BlockSpec((1, 1, d), lambda b, i: (b, 0, j))
    return pl.pallas_call(
        functools.partial(_outproj_kernel, alpha=alpha),
        grid=(bsz, nt),
        in_specs=[pl.BlockSpec((tm, A_WIDTH), tok)]
                 + [spec for _, spec in o_in]
                 + [spec for _, spec in st_in]
                 + [pl.BlockSpec((tm, cw), lambda b, i: (i, b)),
                  pl.BlockSpec((tm, d), tok),
                  modspec(2), modspec(3), modspec(4),
                  pl.BlockSpec((d, d), const2),
                  pl.BlockSpec((LANES, B_WIDTH), const2),
                  pl.BlockSpec((1, d), const2), pl.BlockSpec((1, d), const2),
                  pl.BlockSpec((N_EXPERTS, d), const2), pl.BlockSpec((N_EXPERTS, d), const2),
                  pl.BlockSpec((N_EXPERTS, 1), const2)],
        out_specs=[pl.BlockSpec((tm, d), tok),
                   pl.BlockSpec((SUBLANES, tm), lambda b, i: (0, b * nt + i)),
                   pl.BlockSpec((SUBLANES, tm), lambda b, i: (0, b * nt + i)),
                   pl.BlockSpec((SUBLANES, tab_w), const2),
                   pl.BlockSpec(memory_space=pl.ANY)],
        out_shape=[jax.ShapeDtypeStruct((n, d), F32),
                   jax.ShapeDtypeStruct((SUBLANES, n), jnp.int32),
                   jax.ShapeDtypeStruct((SUBLANES, n), F32),
                   jax.ShapeDtypeStruct((SUBLANES, tab_w), F32),
                   jax.ShapeDtypeStruct((n_blk * MOE_BLK * n_slab, LANES), F32)],
        scratch_shapes=[pltpu.VMEM((B_WIDTH // LANES, tm, LANES), F32),
                        pltpu.VMEM((B_WIDTH // LANES, tm, LANES), F32),
                        pltpu.VMEM((1, tm, LANES), F32), pltpu.VMEM((1, tm, LANES), F32),
                        pltpu.VMEM((2, tm * n_slab, LANES), F32),
                        pltpu.VMEM((3, N_EXPERTS, LANES), F32),
                        pltpu.SMEM((SUBLANES, tm), jnp.int32),
                        pltpu.VMEM((SUBLANES, LANES), jnp.int32),
                        pltpu.SMEM((SUBLANES, LANES), jnp.int32),
                        pltpu.SemaphoreType.DMA((2,)), pltpu.SemaphoreType.DMA],
        compiler_params=_cparams(("arbitrary", "arbitrary")),
        name="outproj_ln_router",
    )(a_out, o1, o2, o3, st1, st2, st3, c_tm, x2d, mod3, mod3, mod3, w_out_bf, expand,
      ln_g, ln_b, rw_hi, rw_lo, rbias)


def _block_order(tab_row, n_blk):
    tab = tab_row[:n_blk].astype(jnp.int32)
    idx = jnp.arange(n_blk, dtype=jnp.int32)
    before = (tab[None, :] < tab[:, None]) | ((tab[None, :] == tab[:, None]) & (idx[None, :] < idx[:, None]))
    pos = jnp.sum(before.astype(jnp.int32), axis=1)
    at_pos = pos[None, :] == idx[:, None]
    perm = jnp.sum(jnp.where(at_pos, idx[None, :], 0), axis=1).astype(jnp.int32)
    blk_e = jnp.minimum(jnp.sum(jnp.where(at_pos, tab[None, :], 0), axis=1), N_EXPERTS - 1).astype(jnp.int32)
    n_used = jnp.sum((tab < N_EXPERTS).astype(jnp.int32)).reshape(1)
    return perm, blk_e, n_used


def _expert_kernel(blk_e_ref, nused_ref, perm_ref, xs_ref, wg_ref, wu_ref, wd_ref, ys_ref, wg_bf, wu_bf, wd_bf,
                   *, blk, n_slab):
    j = pl.program_id(0)

    @pl.when((j == 0) | (blk_e_ref[j] != blk_e_ref[jnp.maximum(j - 1, 0)]))
    def _():
        wg_bf[...] = wg_ref[0, 0].astype(BF16)
        wu_bf[...] = wu_ref[0, 0].astype(BF16)
        wd_bf[...] = wd_ref[0, 0].astype(BF16)

    @pl.when(j < nused_ref[0])
    def _():
        rows = _load_tile_rows(xs_ref, 0, blk, n_slab).astype(BF16)
        gt = jnp.dot(rows, wg_bf[...], preferred_element_type=F32)
        up = jnp.dot(rows, wu_bf[...], preferred_element_type=F32)
        hid = (gt * jax.nn.sigmoid(gt) * up).astype(BF16)
        _store_tile_rows(ys_ref, jnp.dot(hid, wd_bf[...], preferred_element_type=F32))

    @pl.when(j >= nused_ref[0])
    def _():
        ys_ref[...] = jnp.zeros_like(ys_ref)


def _moe(xs, tab, w_gate, w_up, w_down, layer):
    d, de = w_gate.shape[-2:]
    n_slab = d // LANES
    blk = MOE_BLK
    n_blk = xs.shape[0] // (blk * n_slab)
    perm, blk_e, n_used = _block_order(tab[0], n_blk)

    def x_block(j, be, nu, pm):
        return (pm[jnp.minimum(j, nu[0] - 1)], 0)

    return pl.pallas_call(
        functools.partial(_expert_kernel, blk=blk, n_slab=n_slab),
        grid_spec=pltpu.PrefetchScalarGridSpec(
            num_scalar_prefetch=3,
            grid=(n_blk,),
            in_specs=[pl.BlockSpec((blk * n_slab, LANES), x_block),
                      pl.BlockSpec((1, 1, d, de), lambda j, be, nu, pm: (layer, be[j], 0, 0)),
                      pl.BlockSpec((1, 1, d, de), lambda j, be, nu, pm: (layer, be[j], 0, 0)),
                      pl.BlockSpec((1, 1, de, d), lambda j, be, nu, pm: (layer, be[j], 0, 0))],
            out_specs=pl.BlockSpec((blk * n_slab, LANES), lambda j, be, nu, pm: (pm[j], 0)),
            scratch_shapes=[pltpu.VMEM((d, de), BF16), pltpu.VMEM((d, de), BF16), pltpu.VMEM((de, d), BF16)],
        ),
        out_shape=jax.ShapeDtypeStruct(xs.shape, F32),
        compiler_params=_cparams(("arbitrary",)),
        name="moe_experts",
    )(blk_e, n_used, perm, xs, w_gate, w_up, w_down)


def _ffn_ln_kernel(dest_ref, next_ref, x_ref, gate_ref, g2_ref, lng_ref, lnb_ref, ys_hbm, o_ref, ybuf, sem,
                   *, alpha):
    tm, d = x_ref.shape
    n_slab = d // LANES
    g = pl.program_id(0) * pl.num_programs(1) + pl.program_id(1)
    n_tiles = pl.num_programs(0) * pl.num_programs(1)
    slot = g % 2

    def gather(idx_ref, buf_slot):
        def issue(pair, c):
            for q in range(2):
                s = 2 * pair + q
                src = pl.multiple_of(idx_ref[0, 0, s] * n_slab, n_slab)
                dst = pl.multiple_of(s * n_slab, n_slab)
                pltpu.async_copy(ys_hbm.at[pl.ds(src, n_slab), :],
                                 ybuf.at[buf_slot, pl.ds(dst, n_slab), :], sem.at[buf_slot], priority=q)
            return c
        lax.fori_loop(0, TOP_K * tm // 2, issue, 0, unroll=4)

    @pl.when(g == 0)
    def _():
        gather(dest_ref, 0)

    @pl.when(g + 1 < n_tiles)
    def _():
        gather(next_ref, 1 - slot)

    pltpu.make_async_copy(ys_hbm.at[pl.ds(0, TOP_K * tm * n_slab), :], ybuf.at[slot], sem.at[slot]).wait()

    gate = gate_ref[...]
    buf = ybuf.at[slot]
    ffn = _load_tile_rows(buf, 0, tm, n_slab) * gate[:, 0:1]
    for kk in range(1, TOP_K):
        ffn = ffn + _load_tile_rows(buf, kk * tm, tm, n_slab) * gate[:, kk:kk + 1]
    o_ref[...] = _ln(alpha * x_ref[...] + (1.0 + g2_ref[0]) * ffn) * lng_ref[...] + lnb_ref[...]


def _ffn_ln(x1, ys, dest, gates, mod3, ln_g, ln_b, bsz, t, alpha):
    n, d = x1.shape
    n_slab = d // LANES
    tm = min(ROW_TILE, t)
    nt = t // tm
    n_tiles = bsz * nt
    dest_tiles = dest.reshape(TOP_K, n_tiles, tm).transpose(1, 0, 2).reshape(n_tiles, 1, TOP_K * tm)
    tok = lambda b, i, *_: (b * nt + i, 0)
    const2 = lambda b, i, *_: (0, 0)
    gate_cols = gates[:TOP_K].T
    idx_blk = (1, 1, TOP_K * tm)
    return pl.pallas_call(
        functools.partial(_ffn_ln_kernel, alpha=alpha),
        grid=(bsz, nt),
        in_specs=[pl.BlockSpec(idx_blk, lambda b, i: (b * nt + i, 0, 0), memory_space=pltpu.SMEM),
                  pl.BlockSpec(idx_blk, lambda b, i: (jnp.minimum(b * nt + i + 1, n_tiles - 1), 0, 0),
                               memory_space=pltpu.SMEM),
                  pl.BlockSpec((tm, d), tok), pl.BlockSpec((tm, TOP_K), tok),
                  pl.BlockSpec((1, 1, d), lambda b, i: (b, 0, 5)),
                  pl.BlockSpec((1, d), const2), pl.BlockSpec((1, d), const2),
                  pl.BlockSpec(memory_space=pl.ANY)],
        out_specs=pl.BlockSpec((tm, d), tok),
        out_shape=jax.ShapeDtypeStruct((n, d), F32),
        scratch_shapes=[pltpu.VMEM((2, TOP_K * tm * n_slab, LANES), F32), pltpu.SemaphoreType.DMA((2,))],
        compiler_params=_cparams(("arbitrary", "arbitrary")),
        name="ffn_combine_ln",
    )(dest_tiles, dest_tiles, x1, gate_cols, mod3, ln_g, ln_b, ys)


def kernel(x, c, positions, ada_w, ada_b, w_in, gm_ln_g, gm_ln_b, gm_ws, gm_bs, ssm_lam_re, ssm_lam_im, ssm_log_dt, ssm_b_re, ssm_b_im, ssm_c_re, ssm_c_im, ssm_d, glu_w, glu_b, w_out, ln1_g, ln1_b, router_w, router_bias, exp_w_gate, exp_w_up, exp_w_down, ln2_g, ln2_b):
    bsz, t, d = x.shape
    depth = ada_w.shape[0]
    alpha = (2.0 * depth) ** 0.25
    n = bsz * t

    mod = _adaln_mod(c.astype(F32), ada_w, ada_b)
    cos_t, s1_t, s2_t = _rope_tables(positions)
    rw_t = router_w.astype(F32).T
    rw_hi = rw_t.astype(BF16)
    rw_lo = (rw_t - rw_hi.astype(F32)).astype(BF16)
    rbias = router_bias.astype(F32).reshape(N_EXPERTS, 1)

    xf = x.astype(F32).reshape(n, d)
    for l in range(depth):
        mod3 = mod[l].reshape(bsz, 1, 6 * d)
        bs_full = jnp.repeat(gm_bs[l].T, A_HEAD_DIM, axis=1)
        a_out, s_in, qkv = _inproj(
            xf, mod3, w_in[l].astype(BF16), cos_t, s1_t, s2_t,
            gm_ln_g[l].reshape(1, A_WIDTH), gm_ln_b[l].reshape(1, A_WIDTH), gm_ws[l], bs_full, bsz, t)
        att = [_att_branch(*qkv[g], dil, bsz, t) for g, dil in enumerate(DILATIONS)]
        bmat, cmat, a_re, a_im = _ssm_weights(ssm_lam_re[l], ssm_lam_im[l], ssm_log_dt[l],
                                              ssm_b_re[l], ssm_b_im[l], ssm_c_re[l], ssm_c_im[l])
        cw = bmat.shape[0]
        c_out = _ssm(s_in, bmat, cmat, a_re, a_im, ssm_d[l].reshape(1, cw),
                     glu_w[l].astype(BF16), glu_b[l].reshape(1, cw), bsz, t)
        x1, eidx, gates, tab, xs = _outproj(a_out, att, c_out, xf, mod3, w_out[l].astype(BF16),
                                            ln1_g[l].reshape(1, d), ln1_b[l].reshape(1, d),
                                            rw_hi, rw_lo, rbias, bsz, t, alpha)
        ys = _moe(xs, tab, exp_w_gate.astype(F32), exp_w_up.astype(F32), exp_w_down.astype(F32), l)
        xf = _ffn_ln(x1, ys, eidx[TOP_K:2 * TOP_K], gates, mod3, ln2_g[l].reshape(1, d), ln2_b[l].reshape(1, d),
                     bsz, t, alpha)
    return xf.reshape(bsz, t, d)
```

```python
import functools
import math

import jax
import jax.numpy as jnp
from jax import lax
from jax.experimental import pallas as pl
from jax.experimental.pallas import tpu as pltpu

F32 = jnp.float32
BF16 = jnp.bfloat16

A_HEADS = 4
A_HEAD_DIM = 64
A_WIDTH = A_HEADS * A_HEAD_DIM
CHUNK = 128
B_HEADS = 8
B_HEAD_DIM = 64
B_WIDTH = B_HEADS * B_HEAD_DIM
DILATIONS = (1, 4, 16)
ATT_BLK = 128
ROT_DIM = B_HEAD_DIM // 4
ROPE_THETA = 500000.0
N_EXPERTS = 16
N_EXPERT_GROUPS = 4
EXPERTS_PER_GROUP = 4
TOP_K = 2
LN_EPS = 1e-5
NEG_INF = -1e30
Q_SCALE = B_HEAD_DIM ** -0.5 * math.log2(math.e)

LANES = 128
SUBLANES = 8
VMEM_LIMIT = 56 * 1024 * 1024
ROW_TILE = 512
ROPE_ROWS = 2048
SSM_STEPS = 128
MOE_BLK = 512
ATT_SUB_BLOCKS = 8


def _cparams(sem):
    return pltpu.CompilerParams(dimension_semantics=sem, vmem_limit_bytes=VMEM_LIMIT)


def _ln(x):
    mu = jnp.mean(x, axis=-1, keepdims=True)
    xc = x - mu
    var = jnp.mean(xc * xc, axis=-1, keepdims=True)
    return xc * lax.rsqrt(var + LN_EPS)


def _store_tile_rows(ref, val):
    n_slab = val.shape[1] // LANES
    for j in range(n_slab):
        ref[pl.ds(j, val.shape[0], stride=n_slab), :] = val[:, j * LANES:(j + 1) * LANES]


def _load_tile_rows(ref, start, rows, n_slab):
    return jnp.concatenate([ref[pl.ds(start * n_slab + j, rows, stride=n_slab), :] for j in range(n_slab)],
                           axis=1)


def _gelu_tanh(x):
    return 0.5 * x * (1.0 + jnp.tanh(math.sqrt(2.0 / math.pi) * (x + 0.044715 * (x * x * x))))


def _mod_kernel(c_ref, w_ref, b_ref, o_ref):
    c = c_ref[...]
    cond = c * jax.nn.sigmoid(c)
    o_ref[0] = jnp.dot(cond, w_ref[0], precision=lax.Precision.HIGHEST,
                       preferred_element_type=F32) + b_ref[0]


def _adaln_mod(c, ada_w, ada_b):
    depth, d, n = ada_w.shape
    bsz = c.shape[0]
    nb = d
    return pl.pallas_call(
        _mod_kernel,
        grid=(depth, n // nb),
        in_specs=[pl.BlockSpec((bsz, d), lambda l, j: (0, 0)),
                  pl.BlockSpec((1, d, nb), lambda l, j: (l, 0, j)),
                  pl.BlockSpec((1, 1, nb), lambda l, j: (l, 0, j))],
        out_specs=pl.BlockSpec((1, bsz, nb), lambda l, j: (l, 0, j)),
        out_shape=jax.ShapeDtypeStruct((depth, bsz, n), F32),
        compiler_params=_cparams(("arbitrary", "arbitrary")),
        name="adaln_mod",
    )(c, ada_w, ada_b.reshape(depth, 1, n))


def _rope_kernel(pos_ref, freq_ref, ecos_ref, es1_ref, es2_ref, one_ref, cos_ref, s1_ref, s2_ref):
    ang = pos_ref[0].astype(F32) * freq_ref[...]
    cs = jnp.cos(ang)
    sn = jnp.sin(ang)
    tn = (((0,), (0,)), ((), ()))

    def pieces(x):
        p1 = x.astype(BF16)
        r1 = x - p1.astype(F32)
        p2 = r1.astype(BF16)
        p3 = (r1 - p2.astype(F32)).astype(BF16)
        return p1, p2, p3

    def spread(parts, e_ref):
        e = e_ref[...]
        return sum(lax.dot_general(p, e, tn, preferred_element_type=F32) for p in parts)

    cs_parts = pieces(cs)
    sn_parts = pieces(sn)
    cos_ref[...] = spread(cs_parts, ecos_ref) + one_ref[...]
    s1_ref[...] = spread(sn_parts, es1_ref)
    s2_ref[...] = spread(sn_parts, es2_ref)


def _rope_tables(positions):
    n = positions.size
    half = ROT_DIM // 2
    freqs = (ROPE_THETA ** (-jnp.arange(half, dtype=F32) * 2.0 / ROT_DIM)).reshape(half, 1)
    lane = jnp.arange(LANES) % B_HEAD_DIM
    f_of_lane = lane % half
    pick = (jnp.arange(half)[:, None] == f_of_lane[None, :]).astype(F32)
    ecos = pick * (lane < ROT_DIM)[None, :]
    es1 = -pick * (lane < half)[None, :]
    es2 = pick * ((lane >= half) & (lane < ROT_DIM))[None, :]
    one_row = (lane >= ROT_DIM).astype(F32).reshape(1, LANES)
    tm = min(ROPE_ROWS, n)
    out = jax.ShapeDtypeStruct((n, LANES), F32)
    spec = pl.BlockSpec((tm, LANES), lambda i: (i, 0))
    sel = pl.BlockSpec((half, LANES), lambda i: (0, 0))
    return pl.pallas_call(
        _rope_kernel,
        grid=(n // tm,),
        in_specs=[pl.BlockSpec((1, 1, tm), lambda i: (i, 0, 0)),
                  pl.BlockSpec((half, 1), lambda i: (0, 0)),
                  sel, sel, sel, pl.BlockSpec((1, LANES), lambda i: (0, 0))],
        out_specs=[spec, spec, spec],
        out_shape=[out, out, out],
        compiler_params=_cparams(("arbitrary",)),
        name="rope_tables",
    )(positions.reshape(n // tm, 1, tm), freqs, ecos.astype(BF16), es1.astype(BF16), es2.astype(BF16), one_row)


def _inproj_kernel(x_ref, sh_ref, sc_ref, w_ref, cos_ref, s1_ref, s2_ref, lng_ref, lnb_ref,
                   ws_ref, bs_ref, a_ref, s_ref, qn_ref, kn_ref, vn_ref, *rest):
    copy_dils = [dil for dil in DILATIONS if _dilation_layout(dil) == "copy"]
    n_copy = len(copy_dils)
    q_outs, k_outs, v_outs = rest[:n_copy], rest[n_copy:2 * n_copy], rest[2 * n_copy:3 * n_copy]
    qs_ref, ks_ref, vs_ref, qt_ref, kt_ref, vt_ref = rest[3 * n_copy:]
    tm = x_ref.shape[0]
    h = _ln(x_ref[...]) * (1.0 + sc_ref[0]) + sh_ref[0]
    proj = jnp.dot(h.astype(BF16), w_ref[...], preferred_element_type=F32)

    uv = _gelu_tanh(proj[:, :2 * A_WIDTH])
    u = uv[:, :A_WIDTH]
    v = (_ln(uv[:, A_WIDTH:]) * lng_ref[...] + lnb_ref[...]).astype(BF16)
    row = lax.broadcasted_iota(jnp.int32, (CHUNK, CHUNK), 0)
    col = lax.broadcasted_iota(jnp.int32, (CHUNK, CHUNK), 1)
    head_of_lane = lax.broadcasted_iota(jnp.int32, (CHUNK, A_WIDTH), 1) // A_HEAD_DIM
    w_heads = [jnp.where(col <= row, ws_ref[hd], 0.0).astype(BF16) for hd in range(A_HEADS)]
    for cidx in range(tm // CHUNK):
        rows = slice(cidx * CHUNK, (cidx + 1) * CHUNK)
        vc = v[rows]
        sv = bs_ref[...]
        for hd in range(A_HEADS):
            full = jnp.dot(w_heads[hd], vc, preferred_element_type=F32)
            sv = sv + jnp.where(head_of_lane == hd, full, 0.0)
        a_ref[rows, :] = (u[rows] * sv).astype(a_ref.dtype)

    cos = cos_ref[...]
    s1 = s1_ref[...]
    s2 = s2_ref[...]
    q0 = 2 * A_WIDTH
    k0 = q0 + B_WIDTH
    v0 = k0 + B_WIDTH
    for j in range(B_WIDTH // LANES):
        lanes = slice(j * LANES, (j + 1) * LANES)
        for base, ref, nat, scale in ((q0, qs_ref, qn_ref, Q_SCALE), (k0, ks_ref, kn_ref, 1.0)):
            xs = proj[:, base + j * LANES: base + (j + 1) * LANES]
            rot = (xs * cos + pltpu.roll(xs, LANES - ROT_DIM // 2, 1) * s1
                   + pltpu.roll(xs, ROT_DIM // 2, 1) * s2) * scale
            nat[:, lanes] = rot.astype(nat.dtype)
            ref[j] = rot
        vs = proj[:, v0 + j * LANES:v0 + (j + 1) * LANES]
        vn_ref[:, lanes] = vs.astype(vn_ref.dtype)
        vs_ref[j] = vs

    for src, stage, outs in ((qs_ref, qt_ref, q_outs), (ks_ref, kt_ref, k_outs), (vs_ref, vt_ref, v_outs)):
        bufs = (src, stage)
        prev = 1
        for idx, (dil, out) in enumerate(zip(copy_dils, outs)):
            chained = prev > 1 and dil % prev == 0
            base_dil = prev if chained else 1
            source = bufs[idx % 2] if chained or idx == 0 else None
            assert source is not None, "copy dilations must form a divisibility chain"
            step = dil // base_dil
            per = tm // dil
            keep = idx + 1 < len(copy_dils)
            for r_prev in range(base_dil):
                for q in range(step):
                    r = r_prev + base_dil * q
                    start = r_prev * (tm // base_dil) + q
                    for j in range(B_WIDTH // LANES):
                        val = source[j, pl.ds(start, per, stride=step), :]
                        out[0, r, :, j * LANES:(j + 1) * LANES] = val.astype(out.dtype)
                        if keep:
                            bufs[(idx + 1) % 2][j, r * per:(r + 1) * per, :] = val
            prev = dil

    s_ref[...] = proj[:, v0 + B_WIDTH:].astype(s_ref.dtype)


def _inproj(x2d, mod3, w_in_bf, cos_t, s1_t, s2_t, ln_g, ln_b, ws, bs_full, bsz, t):
    n, d = x2d.shape
    tm = min(ROW_TILE, t)
    nt = t // tm
    pw = w_in_bf.shape[1]
    c_width = pw - 2 * A_WIDTH - 3 * B_WIDTH
    tok = lambda b, i: (b * nt + i, 0)
    const2 = lambda b, i: (0, 0)
    outs = [jax.ShapeDtypeStruct((n, A_WIDTH), BF16),
            jax.ShapeDtypeStruct((t, bsz * c_width), BF16)]
    out_specs = [pl.BlockSpec((tm, A_WIDTH), tok),
                 pl.BlockSpec((tm, c_width), lambda b, i: (i, b))]
    for _ in range(3):
        outs.append(jax.ShapeDtypeStruct((n, B_WIDTH), BF16))
        out_specs.append(pl.BlockSpec((tm, B_WIDTH), tok))
    copy_dils = [dil for dil in DILATIONS if _dilation_layout(dil) == "copy"]
    for _ in range(3):
        for dil in copy_dils:
            outs.append(jax.ShapeDtypeStruct((bsz, dil, t // dil, B_WIDTH), BF16))
            out_specs.append(pl.BlockSpec((1, dil, tm // dil, B_WIDTH), lambda b, i: (b, 0, i, 0)))
    res = pl.pallas_call(
        _inproj_kernel,
        grid=(bsz, nt),
        in_specs=[pl.BlockSpec((tm, d), tok),
                  pl.BlockSpec((1, 1, d), lambda b, i: (b, 0, 0)),
                  pl.BlockSpec((1, 1, d), lambda b, i: (b, 0, 1)),
                  pl.BlockSpec((d, pw), const2),
                  pl.BlockSpec((tm, LANES), tok),
                  pl.BlockSpec((tm, LANES), tok),
                  pl.BlockSpec((tm, LANES), tok),
                  pl.BlockSpec((1, A_WIDTH), const2),
                  pl.BlockSpec((1, A_WIDTH), const2),
                  pl.BlockSpec((A_HEADS, CHUNK, CHUNK), lambda b, i: (0, 0, 0)),
                  pl.BlockSpec((CHUNK, A_WIDTH), const2)],
        out_specs=out_specs,
        out_shape=outs,
        scratch_shapes=[pltpu.VMEM((B_WIDTH // LANES, tm, LANES), F32)] * 6,
        compiler_params=_cparams(("arbitrary", "arbitrary")),
        name="inproj_gmlp_rope",
    )(x2d, mod3, mod3, w_in_bf, cos_t, s1_t, s2_t, ln_g, ln_b, ws, bs_full)
    a_out, s_in = res[0], res[1]
    natural = res[2:5]
    n_copy = len(copy_dils)
    copies = {dil: tuple(res[5 + z * n_copy + g] for z in range(3)) for g, dil in enumerate(copy_dils)}
    qkv = [copies[dil] if dil in copies else natural for dil in DILATIONS]
    return a_out, s_in, qkv


def _att_kernel(q_ref, k_ref, v_ref, o_ref, st_ref, kcat_ref, vaug_ref):
    i = pl.program_id(2)
    blk = ATT_BLK
    n_res = q_ref.shape[1]
    n_sub = q_ref.shape[2] // blk
    n_pairs = B_WIDTH // LANES
    last = slice(n_sub * blk, (n_sub + 1) * blk)

    @pl.when(i == 0)
    def _():
        kcat_ref[:, 0:blk, :] = jnp.zeros((n_res, blk, B_WIDTH), BF16)
        vaug_ref[:, :, 0:blk, 0:LANES] = jnp.zeros((n_res, n_pairs, blk, LANES), BF16)
        vaug_ref[:, :, :, LANES:] = jnp.ones((n_res, n_pairs, (n_sub + 1) * blk, LANES), BF16)

    @pl.when(i > 0)
    def _():
        kcat_ref[:, 0:blk, :] = kcat_ref[:, last, :]
        vaug_ref[:, :, 0:blk, 0:LANES] = vaug_ref[:, :, last, 0:LANES]

    for rr in range(n_res):
        kcat_ref[rr, blk:, :] = k_ref[0, rr]
        for pair in range(n_pairs):
            vaug_ref[rr, pair, blk:, 0:LANES] = v_ref[0, rr, :, pair * LANES:(pair + 1) * LANES]

    qi = lax.broadcasted_iota(jnp.int32, (blk, 2 * blk), 0)
    kk = lax.broadcasted_iota(jnp.int32, (blk, 2 * blk), 1)
    band = (kk >= qi) & (kk <= qi + blk)
    first = band & ((i > 0) | (kk >= blk))
    lane = lax.broadcasted_iota(jnp.int32, (blk, LANES), 1)
    low_half = lane < B_HEAD_DIM
    nt = (((1,), (1,)), ((), ()))
    for rr, sub in [(rr, sub) for rr in range(n_res) for sub in range(n_sub)]:
        rows = slice(sub * blk, (sub + 1) * blk)
        keys = slice(sub * blk, (sub + 2) * blk)
        mask = first if sub == 0 else band
        stats = jnp.zeros((blk, LANES), F32)
        for pair in range(n_pairs):
            lanes = slice(pair * LANES, (pair + 1) * LANES)
            q2 = q_ref[0, rr, rows, lanes]
            kslab = kcat_ref[rr, keys, lanes]
            res = []
            for hh in range(2):
                hd = 2 * pair + hh
                qm = jnp.where(low_half if hh == 0 else jnp.logical_not(low_half), q2, jnp.zeros_like(q2))
                s = lax.dot_general(qm, kslab, nt, preferred_element_type=F32)
                s = jnp.where(mask, s, NEG_INF)
                m = jnp.max(jnp.maximum(s[:, :blk], s[:, blk:]), axis=1, keepdims=True)
                p = jnp.exp2(s - m).astype(BF16)
                ov = jnp.dot(p, vaug_ref[rr, pair, keys, :], preferred_element_type=F32)
                den = ov[:, LANES:]
                res.append(ov[:, :LANES] / den)
                stats = jnp.where(lane == hd, m, stats)
                stats = jnp.where(lane == B_HEADS + hd, den, stats)
            o_ref[0, rr, rows, lanes] = jnp.where(low_half, res[0], res[1]).astype(o_ref.dtype)
        st_ref[0, rr, rows, :] = stats


def _dilation_layout(dil):
    return "natural" if dil == 1 else "copy"


def _att_branch(q, k, v, dil, bsz, t):
    rows = t // dil
    n_sub = min(ATT_SUB_BLOCKS, rows // ATT_BLK)
    step_rows = n_sub * ATT_BLK
    n_steps = rows // step_rows
    n_res = math.gcd(dil, max(1, ATT_SUB_BLOCKS // n_sub)) if n_steps == 1 else 1
    shape = lambda w: (bsz, dil, rows, w)
    blk = lambda w: pl.BlockSpec((1, n_res, step_rows, w), lambda b, r, i: (b, r, i, 0))
    q, k, v = (z.reshape(shape(B_WIDTH)) for z in (q, k, v))
    return pl.pallas_call(
        _att_kernel,
        grid=(bsz, dil // n_res, n_steps),
        in_specs=[blk(B_WIDTH), blk(B_WIDTH), blk(B_WIDTH)],
        out_specs=[blk(B_WIDTH), blk(LANES)],
        out_shape=[jax.ShapeDtypeStruct(shape(B_WIDTH), BF16),
                   jax.ShapeDtypeStruct(shape(LANES), F32)],
        scratch_shapes=[pltpu.VMEM((n_res, step_rows + ATT_BLK, B_WIDTH), BF16),
                        pltpu.VMEM((n_res, B_WIDTH // LANES, step_rows + ATT_BLK, 2 * LANES), BF16)],
        compiler_params=_cparams(("arbitrary", "arbitrary", "arbitrary")),
        name=f"dilated_attn_d{dil}",
    )(q, k, v)


def _ssm_kernel(u_ref, bmat_ref, cmat_ref, are_ref, aim_ref, dskip_ref, gw_ref, gb_ref,
                o_ref, state_ref, xbuf_ref, rows_ref):
    bsz, two_n = state_ref.shape
    n_state = two_n // 2
    steps = u_ref.shape[0]
    cw = u_ref.shape[1] // bsz
    n_slab = cw // LANES

    @pl.when(pl.program_id(0) == 0)
    def _():
        state_ref[...] = jnp.zeros_like(state_ref)

    for b in range(bsz):
        for j in range(n_slab):
            rows_ref[j, pl.ds(b, steps, stride=bsz), :] = (
                u_ref[:, b * cw + j * LANES:b * cw + (j + 1) * LANES].astype(F32))
    u32 = jnp.concatenate([rows_ref[j] for j in range(n_slab)], axis=1)
    xbuf_ref[...] = jnp.dot(u32.astype(BF16), bmat_ref[...], preferred_element_type=F32)
    a_re = jnp.broadcast_to(are_ref[...], (bsz, n_state))
    a_im = jnp.broadcast_to(aim_ref[...], (bsz, n_state))

    def step(tt, carry):
        x_re, x_im = carry
        r0 = pl.multiple_of(tt * bsz, bsz)
        b_re = xbuf_ref[pl.ds(r0, bsz), :n_state]
        b_im = xbuf_ref[pl.ds(r0, bsz), n_state:]
        n_re = a_re * x_re - a_im * x_im + b_re
        n_im = a_re * x_im + a_im * x_re + b_im
        xbuf_ref[pl.ds(r0, bsz), :n_state] = n_re
        xbuf_ref[pl.ds(r0, bsz), n_state:] = n_im
        return n_re, n_im

    x_re, x_im = lax.fori_loop(0, steps, step, (state_ref[:, :n_state], state_ref[:, n_state:]))
    state_ref[:, :n_state] = x_re
    state_ref[:, n_state:] = x_im

    y = jnp.dot(xbuf_ref[...].astype(BF16), cmat_ref[...], preferred_element_type=F32)
    y = _gelu_tanh(y + dskip_ref[...] * u32)
    gate = jax.nn.sigmoid(jnp.dot(y.astype(BF16), gw_ref[...], preferred_element_type=F32) + gb_ref[...])
    out = y * gate
    for j in range(n_slab):
        rows_ref[j] = out[:, j * LANES:(j + 1) * LANES]
    for b in range(bsz):
        for j in range(n_slab):
            o_ref[:, b * cw + j * LANES:b * cw + (j + 1) * LANES] = (
                rows_ref[j, pl.ds(b, steps, stride=bsz), :].astype(o_ref.dtype))


def _ssm_weights(lam_re, lam_im, log_dt, b_re, b_im, c_re, c_im):
    g, n_st = lam_re.shape
    cg = b_re.shape[-1]
    dt = jnp.exp(log_dt)[:, None]
    mag = jnp.exp(lam_re * dt)
    ab_re = mag * jnp.cos(lam_im * dt)
    ab_im = mag * jnp.sin(lam_im * dt)
    nr = ab_re - 1.0
    ni = ab_im
    mod2 = lam_re * lam_re + lam_im * lam_im
    f_re = (nr * lam_re + ni * lam_im) / mod2
    f_im = (ni * lam_re - nr * lam_im) / mod2
    bb_re = f_re[..., None] * b_re - f_im[..., None] * b_im
    bb_im = f_re[..., None] * b_im + f_im[..., None] * b_re
    eye = jnp.eye(g, dtype=F32)
    bm_re = jnp.einsum('gnc,gh->gchn', bb_re, eye).reshape(g * cg, g * n_st)
    bm_im = jnp.einsum('gnc,gh->gchn', bb_im, eye).reshape(g * cg, g * n_st)
    bmat = jnp.concatenate([bm_re, bm_im], axis=1)
    cm_re = jnp.einsum('gcn,gh->gnhc', c_re, eye).reshape(g * n_st, g * cg)
    cm_im = jnp.einsum('gcn,gh->gnhc', c_im, eye).reshape(g * n_st, g * cg)
    cmat = jnp.concatenate([cm_re, -cm_im], axis=0)
    return (bmat.astype(BF16), cmat.astype(BF16),
            ab_re.reshape(1, g * n_st), ab_im.reshape(1, g * n_st))


def _ssm(u_tm, bmat, cmat, a_re, a_im, d_skip, glu_w_bf, glu_b, bsz, t):
    cw = bmat.shape[0]
    two_n = bmat.shape[1]
    steps = min(SSM_STEPS, t)
    rows = steps * bsz
    const = lambda i: (0, 0)
    return pl.pallas_call(
        _ssm_kernel,
        grid=(t // steps,),
        in_specs=[pl.BlockSpec((steps, bsz * cw), lambda i: (i, 0)),
                  pl.BlockSpec((cw, two_n), const),
                  pl.BlockSpec((two_n, cw), const),
                  pl.BlockSpec((1, two_n // 2), const),
                  pl.BlockSpec((1, two_n // 2), const),
                  pl.BlockSpec((1, cw), const),
                  pl.BlockSpec((cw, cw), const),
                  pl.BlockSpec((1, cw), const)],
        out_specs=pl.BlockSpec((steps, bsz * cw), lambda i: (i, 0)),
        out_shape=jax.ShapeDtypeStruct((t, bsz * cw), BF16),
        scratch_shapes=[pltpu.VMEM((bsz, two_n), F32), pltpu.VMEM((rows, two_n), F32),
                        pltpu.VMEM((cw // LANES, rows, LANES), F32)],
        compiler_params=_cparams(("arbitrary",)),
        name="s5_scan_glu",
    )(u_tm, bmat, cmat, a_re, a_im, d_skip, glu_w_bf, glu_b)


def _route(sel, scores):
    gs = []
    for g in range(N_EXPERT_GROUPS):
        a, b, c, d = sel[EXPERTS_PER_GROUP * g: EXPERTS_PER_GROUP * (g + 1)]
        hi1, lo1 = jnp.maximum(a, b), jnp.minimum(a, b)
        hi2, lo2 = jnp.maximum(c, d), jnp.minimum(c, d)
        gs.append(jnp.maximum(hi1, hi2) + jnp.maximum(jnp.minimum(hi1, hi2), jnp.maximum(lo1, lo2)))
    g_idx = jnp.zeros(gs[0].shape, jnp.int32)
    best = gs[0]
    for g in range(1, N_EXPERT_GROUPS):
        better = gs[g] > best
        g_idx = jnp.where(better, g, g_idx)
        best = jnp.where(better, gs[g], best)

    def pick(rows, j):
        out = rows[j]
        for g in range(1, N_EXPERT_GROUPS):
            out = jnp.where(g_idx == g, rows[EXPERTS_PER_GROUP * g + j], out)
        return out

    v = [pick(sel, j) for j in range(EXPERTS_PER_GROUP)]
    s = [pick(scores, j) for j in range(EXPERTS_PER_GROUP)]
    i1 = jnp.zeros(g_idx.shape, jnp.int32)
    b1, g1 = v[0], s[0]
    for j in range(1, EXPERTS_PER_GROUP):
        better = v[j] > b1
        i1 = jnp.where(better, j, i1)
        b1 = jnp.where(better, v[j], b1)
        g1 = jnp.where(better, s[j], g1)
    i2 = jnp.full(g_idx.shape, -1, jnp.int32)
    b2 = jnp.zeros_like(b1)
    g2 = jnp.zeros_like(g1)
    for j in range(EXPERTS_PER_GROUP):
        better = (i1 != j) & ((i2 < 0) | (v[j] > b2))
        i2 = jnp.where(better, j, i2)
        b2 = jnp.where(better, v[j], b2)
        g2 = jnp.where(better, s[j], g2)
    tot = g1 + g2
    base = g_idx * EXPERTS_PER_GROUP
    return base + i1, base + i2, g1 / tot, g2 / tot


def _natural_order(ref, scr_ref, dil):
    if _dilation_layout(dil) != "copy":
        return ref[...].astype(F32)
    per = ref.shape[2]
    n_slab = scr_ref.shape[0]
    for r in range(dil):
        for j in range(n_slab):
            scr_ref[j, pl.ds(r, per, stride=dil), :] = ref[0, r, :, j * LANES:(j + 1) * LANES].astype(F32)
    if n_slab == 1:
        return scr_ref[0]
    return jnp.concatenate([scr_ref[j] for j in range(n_slab)], axis=1)


def _outproj_kernel(a_ref, o1_ref, o2_ref, o3_ref, st1_ref, st2_ref, st3_ref,
                    c_ref, x_ref, g1_ref, sh2_ref, sc2_ref, wout_ref, expand_ref, lng_ref, lnb_ref,
                    rwh_ref, rwl_ref, rb_ref, x1_ref, eidx_ref, gate_ref, tab_ref, xs_hbm,
                    oscr2_ref, oscr3_ref, sscr2_ref, sscr3_ref, hbuf, state_ref, slot_ref, fin_ref, fins_ref,
                    ssem, csem, *, alpha):
    o_refs = (o1_ref, o2_ref, o3_ref)
    st_refs = (st1_ref, st2_ref, st3_ref)
    o_scr = (None, oscr2_ref, oscr3_ref)
    st_scr = (None, sscr2_ref, sscr3_ref)
    ms = [_natural_order(st_refs[g], st_scr[g], DILATIONS[g]) for g in range(len(DILATIONS))]
    dens = [pltpu.roll(m, LANES - B_HEADS, 1) for m in ms]
    mx = jnp.maximum(jnp.maximum(ms[0], ms[1]), ms[2])
    ws = [dens[g] * jnp.exp2(ms[g] - mx) for g in range(len(DILATIONS))]
    lane = lax.broadcasted_iota(jnp.int32, mx.shape, 1)
    tot = jnp.where(lane < B_HEADS, ws[0] + ws[1] + ws[2], 1.0)
    expand = expand_ref[...]

    def widen(w):
        wn = jnp.where(lane < B_HEADS, w / tot, 0.0)
        hi = wn.astype(BF16)
        lo = (wn - hi.astype(F32)).astype(BF16)
        return (jnp.dot(hi, expand, preferred_element_type=F32)
                + jnp.dot(lo, expand, preferred_element_type=F32))

    b_out = widen(ws[0]) * _natural_order(o_refs[0], o_scr[0], DILATIONS[0])
    for g in range(1, len(DILATIONS)):
        b_out = b_out + widen(ws[g]) * _natural_order(o_refs[g], o_scr[g], DILATIONS[g])
    b_out = b_out.astype(BF16)

    wout = wout_ref
    mix = (jnp.dot(a_ref[...], wout[:A_WIDTH, :], preferred_element_type=F32)
           + jnp.dot(b_out, wout[A_WIDTH:A_WIDTH + B_WIDTH, :], preferred_element_type=F32)
           + jnp.dot(c_ref[...], wout[A_WIDTH + B_WIDTH:, :], preferred_element_type=F32))
    x1 = _ln(alpha * x_ref[...] + (1.0 + g1_ref[0]) * mix) * lng_ref[...] + lnb_ref[...]
    x1_ref[...] = x1
    h2 = _ln(x1) * (1.0 + sc2_ref[0]) + sh2_ref[0]

    hi = h2.astype(BF16)
    lo = (h2 - hi.astype(F32)).astype(BF16)
    nt = (((1,), (1,)), ((), ()))
    logits = (lax.dot_general(rwh_ref[...], hi, nt, preferred_element_type=F32)
              + lax.dot_general(rwh_ref[...], lo, nt, preferred_element_type=F32)
              + lax.dot_general(rwl_ref[...], hi, nt, preferred_element_type=F32))
    scores = jax.nn.sigmoid(logits)
    sel = scores + rb_ref[...]
    sel_rows = [sel[e:e + 1, :] for e in range(N_EXPERTS)]
    score_rows = [scores[e:e + 1, :] for e in range(N_EXPERTS)]
    e1, e2, gt1, gt2 = _route(sel_rows, score_rows)

    g = pl.program_id(0) * pl.num_programs(1) + pl.program_id(1)
    n_tiles = pl.num_programs(0) * pl.num_programs(1)
    tm = h2.shape[0]
    n_slab = h2.shape[1] // LANES
    blk = MOE_BLK
    par = g % 2

    @pl.when(g == 0)
    def _():
        state_ref[...] = jnp.zeros_like(state_ref)
        tab_ref[...] = jnp.full(tab_ref.shape, float(N_EXPERTS), F32)

    erow = lax.broadcasted_iota(jnp.int32, (N_EXPERTS, tm), 0)
    oh1 = erow == e1
    oh2 = erow == e2
    both = (oh1 | oh2).astype(BF16)
    earlier = (lax.broadcasted_iota(jnp.int32, (tm, tm), 0)
               < lax.broadcasted_iota(jnp.int32, (tm, tm), 1)).astype(BF16)
    base = state_ref[0, :, 0:1]
    cur = state_ref[1, :, 0:1]
    nfree = state_ref[2, :, 0:1]
    n_vec = jnp.sum(both.astype(F32), axis=1, keepdims=True)
    off = base - jnp.floor(base * (1.0 / blk)) * blk
    need_a = (off == 0.0) & (n_vec > 0.0)
    need_b = (off + n_vec) > float(blk)
    na = need_a.astype(F32)
    nb = need_b.astype(F32)
    alloc = na + nb
    lower = (lax.broadcasted_iota(jnp.int32, (N_EXPERTS, N_EXPERTS), 1)
             < lax.broadcasted_iota(jnp.int32, (N_EXPERTS, N_EXPERTS), 0)).astype(BF16)
    alloc_wide = jnp.broadcast_to(alloc, (N_EXPERTS, LANES))
    before = jnp.dot(lower, alloc_wide.astype(BF16), preferred_element_type=F32)[:, 0:1]
    a_id = jnp.where(need_a, nfree + before, cur)
    b_id = jnp.where(need_b, nfree + before + na, a_id)
    bound = base - off + float(blk)
    total_alloc = jnp.sum(alloc_wide, axis=0, keepdims=True)[:, 0:1]
    state_ref[0] = jnp.broadcast_to(base + n_vec, (N_EXPERTS, LANES))
    state_ref[1] = jnp.broadcast_to(jnp.where(need_b, b_id, a_id), (N_EXPERTS, LANES))
    state_ref[2] = jnp.broadcast_to(nfree + total_alloc, (N_EXPERTS, LANES))

    blane = lax.broadcasted_iota(jnp.int32, (N_EXPERTS, tab_ref.shape[1]), 1).astype(F32)
    hit = ((blane == a_id) & need_a) | ((blane == b_id) & need_b)
    e_col = lax.broadcasted_iota(jnp.int32, (N_EXPERTS, 1), 0).astype(F32)
    delta = jnp.sum(jnp.where(hit, e_col - float(N_EXPERTS), 0.0), axis=0, keepdims=True)
    tab_ref[...] = tab_ref[...] + jnp.broadcast_to(delta, tab_ref.shape)

    rank = jnp.dot(both, earlier, preferred_element_type=F32) + base
    slot = jnp.where(rank < bound, a_id * blk + (rank - bound + blk), b_id * blk + (rank - bound))
    s1 = jnp.sum(jnp.where(oh1, slot, 0.0), axis=0, keepdims=True)
    s2 = jnp.sum(jnp.where(oh2, slot, 0.0), axis=0, keepdims=True)

    eidx_ref[...] = jnp.zeros_like(eidx_ref)
    gate_ref[...] = jnp.zeros_like(gate_ref)
    eidx_ref[0:1, :] = e1
    eidx_ref[1:2, :] = e2
    eidx_ref[2:3, :] = s1.astype(jnp.int32)
    eidx_ref[3:4, :] = s2.astype(jnp.int32)
    gate_ref[0:1, :] = gt1
    gate_ref[1:2, :] = gt2

    slot_copy = pltpu.make_async_copy(eidx_ref, slot_ref, csem)
    slot_copy.start()

    def wait_rows(p):
        for _ in range(TOP_K):
            pltpu.make_async_copy(hbuf.at[p], xs_hbm.at[pl.ds(0, tm * n_slab), :], ssem.at[p]).wait()

    @pl.when(g >= 2)
    def _():
        wait_rows(par)

    _store_tile_rows(hbuf.at[par], h2)
    slot_copy.wait()

    rows_src = hbuf.at[par]
    for s in range(tm):
        src = rows_src.at[pl.ds(s * n_slab, n_slab), :]
        for kk in range(TOP_K):
            dst = pl.multiple_of(slot_ref[TOP_K + kk, s] * n_slab, n_slab)
            pltpu.async_copy(src, xs_hbm.at[pl.ds(dst, n_slab), :], ssem.at[par], priority=kk % 2)

    @pl.when(g == n_tiles - 1)
    def _():
        wait_rows(par)

        @pl.when(n_tiles >= 2)
        def _():
            wait_rows(1 - par)

        lane_e = lax.broadcasted_iota(jnp.int32, (N_EXPERTS, LANES), 1)
        sub_e = lax.broadcasted_iota(jnp.int32, (N_EXPERTS, LANES), 0)

        def by_lane(col):
            return jnp.sum(jnp.where(lane_e == sub_e, col, 0.0), axis=0, keepdims=True)

        end = base + n_vec
        end_off = end - jnp.floor(end * (1.0 / blk)) * blk
        fin_ref[...] = jnp.zeros_like(fin_ref)
        fin_ref[0:1, :] = by_lane(jnp.where(need_b, b_id, a_id)).astype(jnp.int32)
        fin_ref[1:2, :] = by_lane(end_off).astype(jnp.int32)
        fin_ref[2:3, :] = by_lane(nfree + total_alloc).astype(jnp.int32)
        fin_copy = pltpu.make_async_copy(fin_ref, fins_ref, csem)
        fin_copy.start()
        fin_copy.wait()

        hbuf[0] = jnp.zeros(hbuf.shape[1:], F32)

        def zero_rows(first_row, n_rows):
            cp = pltpu.make_async_copy(hbuf.at[0, pl.ds(0, n_rows * n_slab), :],
                                       xs_hbm.at[pl.ds(pl.multiple_of(first_row * n_slab, n_slab),
                                                       n_rows * n_slab), :], ssem.at[0])
            cp.start()
            cp.wait()

        for e in range(N_EXPERTS):
            pos = fins_ref[1, e]
            start = fins_ref[0, e] * blk
            for k in range(blk.bit_length() - 1):
                take = (fins_ref[1, e] > 0) & (((pos >> k) & 1) == 1)

                @pl.when(take)
                def _():
                    zero_rows(start + pos, 1 << k)
                pos = jnp.where(take, pos + (1 << k), pos)

        n_blk = xs_hbm.shape[0] // (blk * n_slab)
        for t in range(N_EXPERTS):
            spare = fins_ref[2, 0] + t

            @pl.when(spare < n_blk)
            def _():
                zero_rows(spare * blk, blk)


def _outproj(a_out, att, c_tm, x2d, mod3, w_out_bf, ln_g, ln_b, rw_hi, rw_lo, rbias, bsz, t, alpha):
    n, d = x2d.shape
    tm = min(ROW_TILE, t)
    assert tm == MOE_BLK and MOE_BLK & (MOE_BLK - 1) == 0
    nt = t // tm
    n_slab = d // LANES
    n_blk = n * TOP_K // MOE_BLK + N_EXPERTS
    tab_w = -(-n_blk // LANES) * LANES
    cw = c_tm.shape[1] // bsz
    tok = lambda b, i: (b * nt + i, 0)
    const2 = lambda b, i: (0, 0)
    head = jnp.arange(B_WIDTH) // B_HEAD_DIM
    expand = (jnp.arange(LANES)[:, None] == head[None, :]).astype(BF16)

    def branch_input(z, dil, w):
        if _dilation_layout(dil) == "copy":
            return z, pl.BlockSpec((1, dil, tm // dil, w), lambda b, i: (b, 0, i, 0))
        return z.reshape(n, w), pl.BlockSpec((tm, w), tok)

    o_in = [branch_input(o, dil, B_WIDTH) for (o, _), dil in zip(att, DILATIONS)]
    st_in = [branch_input(st, dil, LANES) for (_, st), dil in zip(att, DILATIONS)]
    (o1, st1), (o2, st2), (o3, st3) = [(o_in[g][0], st_in[g][0]) for g in range(len(DILATIONS))]
    modspec = lambda j: pl.BlockSpec((1, 1, d), lambda b, i: (b, 0, j))
    return pl.pallas_call(
        functools.partial(_outproj_kernel, alpha=alpha),
        grid=(bsz, nt),
        in_specs=[pl.BlockSpec((tm, A_WIDTH), tok)]
                 + [spec for _, spec in o_in]
                 + [spec for _, spec in st_in]
                 + [pl.BlockSpec((tm, cw), lambda b, i: (i, b)),
                  pl.BlockSpec((tm, d), tok),
                  modspec(2), modspec(3), modspec(4),
                  pl.BlockSpec((d, d), const2),
                  pl.BlockSpec((LANES, B_WIDTH), const2),
                  pl.BlockSpec((1, d), const2), pl.BlockSpec((1, d), const2),
                  pl.BlockSpec((N_EXPERTS, d), const2), pl.BlockSpec((N_EXPERTS, d), const2),
                  pl.BlockSpec((N_EXPERTS, 1), const2)],
        out_specs=[pl.BlockSpec((tm, d), tok),
                   pl.BlockSpec((SUBLANES, tm), lambda b, i: (0, b * nt + i)),
                   pl.BlockSpec((SUBLANES, tm), lambda b, i: (0, b * nt + i)),
                   pl.BlockSpec((SUBLANES, tab_w), const2),
                   pl.BlockSpec(memory_space=pl.ANY)],
        out_shape=[jax.ShapeDtypeStruct((n, d), F32),
                   jax.ShapeDtypeStruct((SUBLANES, n), jnp.int32),
                   jax.ShapeDtypeStruct((SUBLANES, n), F32),
                   jax.ShapeDtypeStruct((SUBLANES, tab_w), F32),
                   jax.ShapeDtypeStruct((n_blk * MOE_BLK * n_slab, LANES), F32)],
        scratch_shapes=[pltpu.VMEM((B_WIDTH // LANES, tm, LANES), F32),
                        pltpu.VMEM((B_WIDTH // LANES, tm, LANES), F32),
                        pltpu.VMEM((1, tm, LANES), F32), pltpu.VMEM((1, tm, LANES), F32),
                        pltpu.VMEM((2, tm * n_slab, LANES), F32),
                        pltpu.VMEM((3, N_EXPERTS, LANES), F32),
                        pltpu.SMEM((SUBLANES, tm), jnp.int32),
                        pltpu.VMEM((SUBLANES, LANES), jnp.int32),
                        pltpu.SMEM((SUBLANES, LANES), jnp.int32),
                        pltpu.SemaphoreType.DMA((2,)), pltpu.SemaphoreType.DMA],
        compiler_params=_cparams(("arbitrary", "arbitrary")),
        name="outproj_ln_router",
    )(a_out, o1, o2, o3, st1, st2, st3, c_tm, x2d, mod3, mod3, mod3, w_out_bf, expand,
      ln_g, ln_b, rw_hi, rw_lo, rbias)


def _block_order(tab_row, n_blk):
    tab = tab_row[:n_blk].astype(jnp.int32)
    idx = jnp.arange(n_blk, dtype=jnp.int32)
    before = (tab[None, :] < tab[:, None]) | ((tab[None, :] == tab[:, None]) & (idx[None, :] < idx[:, None]))
    pos = jnp.sum(before.astype(jnp.int32), axis=1)
    at_pos = pos[None, :] == idx[:, None]
    perm = jnp.sum(jnp.where(at_pos, idx[None, :], 0), axis=1).astype(jnp.int32)
    blk_e = jnp.minimum(jnp.sum(jnp.where(at_pos, tab[None, :], 0), axis=1), N_EXPERTS - 1).astype(jnp.int32)
    n_used = jnp.sum((tab < N_EXPERTS).astype(jnp.int32)).reshape(1)
    return perm, blk_e, n_used


def _expert_kernel(blk_e_ref, nused_ref, perm_ref, xs_ref, wg_ref, wu_ref, wd_ref, ys_ref, wg_bf, wu_bf, wd_bf,
                   *, blk, n_slab):
    j = pl.program_id(0)

    @pl.when((j == 0) | (blk_e_ref[j] != blk_e_ref[jnp.maximum(j - 1, 0)]))
    def _():
        wg_bf[...] = wg_ref[0, 0].astype(BF16)
        wu_bf[...] = wu_ref[0, 0].astype(BF16)
        wd_bf[...] = wd_ref[0, 0].astype(BF16)

    @pl.when(j < nused_ref[0])
    def _():
        rows = _load_tile_rows(xs_ref, 0, blk, n_slab).astype(BF16)
        gt = jnp.dot(rows, wg_bf[...], preferred_element_type=F32)
        up = jnp.dot(rows, wu_bf[...], preferred_element_type=F32)
        hid = (gt * jax.nn.sigmoid(gt) * up).astype(BF16)
        _store_tile_rows(ys_ref, jnp.dot(hid, wd_bf[...], preferred_element_type=F32))

    @pl.when(j >= nused_ref[0])
    def _():
        ys_ref[...] = jnp.zeros_like(ys_ref)


def _moe(xs, tab, w_gate, w_up, w_down, layer):
    d, de = w_gate.shape[-2:]
    n_slab = d // LANES
    blk = MOE_BLK
    n_blk = xs.shape[0] // (blk * n_slab)
    perm, blk_e, n_used = _block_order(tab[0], n_blk)

    def x_block(j, be, nu, pm):
        return (pm[jnp.minimum(j, nu[0] - 1)], 0)

    return pl.pallas_call(
        functools.partial(_expert_kernel, blk=blk, n_slab=n_slab),
        grid_spec=pltpu.PrefetchScalarGridSpec(
            num_scalar_prefetch=3,
            grid=(n_blk,),
            in_specs=[pl.BlockSpec((blk * n_slab, LANES), x_block),
                      pl.BlockSpec((1, 1, d, de), lambda j, be, nu, pm: (layer, be[j], 0, 0)),
                      pl.BlockSpec((1, 1, d, de), lambda j, be, nu, pm: (layer, be[j], 0, 0)),
                      pl.BlockSpec((1, 1, de, d), lambda j, be, nu, pm: (layer, be[j], 0, 0))],
            out_specs=pl.BlockSpec((blk * n_slab, LANES), lambda j, be, nu, pm: (pm[j], 0)),
            scratch_shapes=[pltpu.VMEM((d, de), BF16), pltpu.VMEM((d, de), BF16), pltpu.VMEM((de, d), BF16)],
        ),
        out_shape=jax.ShapeDtypeStruct(xs.shape, F32),
        compiler_params=_cparams(("arbitrary",)),
        name="moe_experts",
    )(blk_e, n_used, perm, xs, w_gate, w_up, w_down)


def _ffn_ln_kernel(dest_ref, next_ref, x_ref, gate_ref, g2_ref, lng_ref, lnb_ref, ys_hbm, o_ref, ybuf, sem,
                   *, alpha):
    tm, d = x_ref.shape
    n_slab = d // LANES
    g = pl.program_id(0) * pl.num_programs(1) + pl.program_id(1)
    n_tiles = pl.num_programs(0) * pl.num_programs(1)
    slot = g % 2

    def gather(idx_ref, buf_slot):
        def issue(pair, c):
            for q in range(2):
                s = 2 * pair + q
                src = pl.multiple_of(idx_ref[0, 0, s] * n_slab, n_slab)
                dst = pl.multiple_of(s * n_slab, n_slab)
                pltpu.async_copy(ys_hbm.at[pl.ds(src, n_slab), :],
                                 ybuf.at[buf_slot, pl.ds(dst, n_slab), :], sem.at[buf_slot], priority=q)
            return c
        lax.fori_loop(0, TOP_K * tm // 2, issue, 0, unroll=4)

    def wait_rows(b):
        pltpu.make_async_copy(ys_hbm.at[pl.ds(0, TOP_K * tm * n_slab), :], ybuf.at[b], sem.at[b]).wait()

    @pl.when(g == 0)
    def _():
        gather(dest_ref, 0)

    wait_rows(slot)

    nxt = ybuf.at[1 - slot]
    for s in range(TOP_K * tm):
        src = pl.multiple_of(next_ref[0, 0, s] * n_slab, n_slab)
        pltpu.async_copy(ys_hbm.at[pl.ds(src, n_slab), :], nxt.at[pl.ds(s * n_slab, n_slab), :],
                         sem.at[1 - slot], priority=s % 2)

    gate = gate_ref[...]
    buf = ybuf.at[slot]
    ffn = _load_tile_rows(buf, 0, tm, n_slab) * gate[:, 0:1]
    for kk in range(1, TOP_K):
        ffn = ffn + _load_tile_rows(buf, kk * tm, tm, n_slab) * gate[:, kk:kk + 1]
    o_ref[...] = _ln(alpha * x_ref[...] + (1.0 + g2_ref[0]) * ffn) * lng_ref[...] + lnb_ref[...]

    @pl.when(g == n_tiles - 1)
    def _():
        wait_rows(1 - slot)


def _ffn_ln(x1, ys, dest, gates, mod3, ln_g, ln_b, bsz, t, alpha):
    n, d = x1.shape
    n_slab = d // LANES
    tm = min(ROW_TILE, t)
    nt = t // tm
    n_tiles = bsz * nt
    dest_tiles = dest.reshape(TOP_K, n_tiles, tm).transpose(1, 0, 2).reshape(n_tiles, 1, TOP_K * tm)
    tok = lambda b, i, *_: (b * nt + i, 0)
    const2 = lambda b, i, *_: (0, 0)
    gate_cols = gates[:TOP_K].T
    idx_blk = (1, 1, TOP_K * tm)
    return pl.pallas_call(
        functools.partial(_ffn_ln_kernel, alpha=alpha),
        grid=(bsz, nt),
        in_specs=[pl.BlockSpec(idx_blk, lambda b, i: (b * nt + i, 0, 0), memory_space=pltpu.SMEM),
                  pl.BlockSpec(idx_blk, lambda b, i: (jnp.minimum(b * nt + i + 1, n_tiles - 1), 0, 0),
                               memory_space=pltpu.SMEM),
                  pl.BlockSpec((tm, d), tok), pl.BlockSpec((tm, TOP_K), tok),
                  pl.BlockSpec((1, 1, d), lambda b, i: (b, 0, 5)),
                  pl.BlockSpec((1, d), const2), pl.BlockSpec((1, d), const2),
                  pl.BlockSpec(memory_space=pl.ANY)],
        out_specs=pl.BlockSpec((tm, d), tok),
        out_shape=jax.ShapeDtypeStruct((n, d), F32),
        scratch_shapes=[pltpu.VMEM((2, TOP_K * tm * n_slab, LANES), F32), pltpu.SemaphoreType.DMA((2,))],
        compiler_params=_cparams(("arbitrary", "arbitrary")),
        name="ffn_combine_ln",
    )(dest_tiles, dest_tiles, x1, gate_cols, mod3, ln_g, ln_b, ys)


def kernel(x, c, positions, ada_w, ada_b, w_in, gm_ln_g, gm_ln_b, gm_ws, gm_bs, ssm_lam_re, ssm_lam_im, ssm_log_dt, ssm_b_re, ssm_b_im, ssm_c_re, ssm_c_im, ssm_d, glu_w, glu_b, w_out, ln1_g, ln1_b, router_w, router_bias, exp_w_gate, exp_w_up, exp_w_down, ln2_g, ln2_b):
    bsz, t, d = x.shape
    depth = ada_w.shape[0]
    alpha = (2.0 * depth) ** 0.25
    n = bsz * t

    mod = _adaln_mod(c.astype(F32), ada_w, ada_b)
    cos_t, s1_t, s2_t = _rope_tables(positions)
    rw_t = router_w.astype(F32).T
    rw_hi = rw_t.astype(BF16)
    rw_lo = (rw_t - rw_hi.astype(F32)).astype(BF16)
    rbias = router_bias.astype(F32).reshape(N_EXPERTS, 1)

    xf = x.astype(F32).reshape(n, d)
    for l in range(depth):
        mod3 = mod[l].reshape(bsz, 1, 6 * d)
        bs_full = jnp.repeat(gm_bs[l].T, A_HEAD_DIM, axis=1)
        a_out, s_in, qkv = _inproj(
            xf, mod3, w_in[l].astype(BF16), cos_t, s1_t, s2_t,
            gm_ln_g[l].reshape(1, A_WIDTH), gm_ln_b[l].reshape(1, A_WIDTH), gm_ws[l], bs_full, bsz, t)
        att = [_att_branch(*qkv[g], dil, bsz, t) for g, dil in enumerate(DILATIONS)]
        bmat, cmat, a_re, a_im = _ssm_weights(ssm_lam_re[l], ssm_lam_im[l], ssm_log_dt[l],
                                              ssm_b_re[l], ssm_b_im[l], ssm_c_re[l], ssm_c_im[l])
        cw = bmat.shape[0]
        c_out = _ssm(s_in, bmat, cmat, a_re, a_im, ssm_d[l].reshape(1, cw),
                     glu_w[l].astype(BF16), glu_b[l].reshape(1, cw), bsz, t)
        x1, eidx, gates, tab, xs = _outproj(a_out, att, c_out, xf, mod3, w_out[l].astype(BF16),
                                            ln1_g[l].reshape(1, d), ln1_b[l].reshape(1, d),
                                            rw_hi, rw_lo, rbias, bsz, t, alpha)
        ys = _moe(xs, tab, exp_w_gate.astype(F32), exp_w_up.astype(F32), exp_w_down.astype(F32), l)
        xf = _ffn_ln(x1, ys, eidx[TOP_K:2 * TOP_K], gates, mod3, ln2_g[l].reshape(1, d), ln2_b[l].reshape(1, d),
                     bsz, t, alpha)
    return xf.reshape(bsz, t, d)
```

```python
import functools
import math

import jax
import jax.numpy as jnp
from jax import lax
from jax.experimental import pallas as pl
from jax.experimental.pallas import tpu as pltpu

F32 = jnp.float32
BF16 = jnp.bfloat16

A_HEADS = 4
A_HEAD_DIM = 64
A_WIDTH = A_HEADS * A_HEAD_DIM
CHUNK = 128
B_HEADS = 8
B_HEAD_DIM = 64
B_WIDTH = B_HEADS * B_HEAD_DIM
DILATIONS = (1, 4, 16)
ATT_BLK = 128
ROT_DIM = B_HEAD_DIM // 4
ROPE_THETA = 500000.0
N_EXPERTS = 16
N_EXPERT_GROUPS = 4
EXPERTS_PER_GROUP = 4
TOP_K = 2
LN_EPS = 1e-5
NEG_INF = -1e30
Q_SCALE = B_HEAD_DIM ** -0.5 * math.log2(math.e)

LANES = 128
SUBLANES = 8
VMEM_LIMIT = 56 * 1024 * 1024
ROW_TILE = 512
ROPE_ROWS = 2048
SSM_STEPS = 128
MOE_BLK = 512
ATT_SUB_BLOCKS = 8


def _cparams(sem):
    return pltpu.CompilerParams(dimension_semantics=sem, vmem_limit_bytes=VMEM_LIMIT)


def _ln(x):
    mu = jnp.mean(x, axis=-1, keepdims=True)
    xc = x - mu
    var = jnp.mean(xc * xc, axis=-1, keepdims=True)
    return xc * lax.rsqrt(var + LN_EPS)


def _store_tile_rows(ref, val):
    n_slab = val.shape[1] // LANES
    for j in range(n_slab):
        ref[pl.ds(j, val.shape[0], stride=n_slab), :] = val[:, j * LANES:(j + 1) * LANES]


def _load_tile_rows(ref, start, rows, n_slab):
    return jnp.concatenate([ref[pl.ds(start * n_slab + j, rows, stride=n_slab), :] for j in range(n_slab)],
                           axis=1)


def _gelu_tanh(x):
    return 0.5 * x * (1.0 + jnp.tanh(math.sqrt(2.0 / math.pi) * (x + 0.044715 * (x * x * x))))


def _mod_kernel(c_ref, w_ref, b_ref, o_ref):
    c = c_ref[...]
    cond = c * jax.nn.sigmoid(c)
    o_ref[0] = jnp.dot(cond, w_ref[0], precision=lax.Precision.HIGHEST,
                       preferred_element_type=F32) + b_ref[0]


def _adaln_mod(c, ada_w, ada_b):
    depth, d, n = ada_w.shape
    bsz = c.shape[0]
    nb = d
    return pl.pallas_call(
        _mod_kernel,
        grid=(depth, n // nb),
        in_specs=[pl.BlockSpec((bsz, d), lambda l, j: (0, 0)),
                  pl.BlockSpec((1, d, nb), lambda l, j: (l, 0, j)),
                  pl.BlockSpec((1, 1, nb), lambda l, j: (l, 0, j))],
        out_specs=pl.BlockSpec((1, bsz, nb), lambda l, j: (l, 0, j)),
        out_shape=jax.ShapeDtypeStruct((depth, bsz, n), F32),
        compiler_params=_cparams(("arbitrary", "arbitrary")),
        name="adaln_mod",
    )(c, ada_w, ada_b.reshape(depth, 1, n))


def _rope_kernel(pos_ref, freq_ref, ecos_ref, es1_ref, es2_ref, one_ref, cos_ref, s1_ref, s2_ref):
    ang = pos_ref[0].astype(F32) * freq_ref[...]
    cs = jnp.cos(ang)
    sn = jnp.sin(ang)
    tn = (((0,), (0,)), ((), ()))

    def pieces(x):
        p1 = x.astype(BF16)
        r1 = x - p1.astype(F32)
        p2 = r1.astype(BF16)
        p3 = (r1 - p2.astype(F32)).astype(BF16)
        return p1, p2, p3

    def spread(parts, e_ref):
        e = e_ref[...]
        return sum(lax.dot_general(p, e, tn, preferred_element_type=F32) for p in parts)

    cs_parts = pieces(cs)
    sn_parts = pieces(sn)
    cos_ref[...] = spread(cs_parts, ecos_ref) + one_ref[...]
    s1_ref[...] = spread(sn_parts, es1_ref)
    s2_ref[...] = spread(sn_parts, es2_ref)


def _rope_tables(positions):
    n = positions.size
    half = ROT_DIM // 2
    freqs = (ROPE_THETA ** (-jnp.arange(half, dtype=F32) * 2.0 / ROT_DIM)).reshape(half, 1)
    lane = jnp.arange(LANES) % B_HEAD_DIM
    f_of_lane = lane % half
    pick = (jnp.arange(half)[:, None] == f_of_lane[None, :]).astype(F32)
    ecos = pick * (lane < ROT_DIM)[None, :]
    es1 = -pick * (lane < half)[None, :]
    es2 = pick * ((lane >= half) & (lane < ROT_DIM))[None, :]
    one_row = (lane >= ROT_DIM).astype(F32).reshape(1, LANES)
    tm = min(ROPE_ROWS, n)
    out = jax.ShapeDtypeStruct((n, LANES), F32)
    spec = pl.BlockSpec((tm, LANES), lambda i: (i, 0))
    sel = pl.BlockSpec((half, LANES), lambda i: (0, 0))
    return pl.pallas_call(
        _rope_kernel,
        grid=(n // tm,),
        in_specs=[pl.BlockSpec((1, 1, tm), lambda i: (i, 0, 0)),
                  pl.BlockSpec((half, 1), lambda i: (0, 0)),
                  sel, sel, sel, pl.BlockSpec((1, LANES), lambda i: (0, 0))],
        out_specs=[spec, spec, spec],
        out_shape=[out, out, out],
        compiler_params=_cparams(("arbitrary",)),
        name="rope_tables",
    )(positions.reshape(n // tm, 1, tm), freqs, ecos.astype(BF16), es1.astype(BF16), es2.astype(BF16), one_row)


def _inproj_kernel(x_ref, sh_ref, sc_ref, w_ref, cos_ref, s1_ref, s2_ref, lng_ref, lnb_ref,
                   ws_ref, bs_ref, a_ref, s_ref, qn_ref, kn_ref, vn_ref, *rest):
    copy_dils = [dil for dil in DILATIONS if _dilation_layout(dil) == "copy"]
    n_copy = len(copy_dils)
    q_outs, k_outs, v_outs = rest[:n_copy], rest[n_copy:2 * n_copy], rest[2 * n_copy:3 * n_copy]
    qs_ref, ks_ref, vs_ref, qt_ref, kt_ref, vt_ref = rest[3 * n_copy:]
    tm = x_ref.shape[0]
    h = _ln(x_ref[...]) * (1.0 + sc_ref[0]) + sh_ref[0]
    proj = jnp.dot(h.astype(BF16), w_ref[...], preferred_element_type=F32)

    uv = _gelu_tanh(proj[:, :2 * A_WIDTH])
    u = uv[:, :A_WIDTH]
    v = (_ln(uv[:, A_WIDTH:]) * lng_ref[...] + lnb_ref[...]).astype(BF16)
    row = lax.broadcasted_iota(jnp.int32, (CHUNK, CHUNK), 0)
    col = lax.broadcasted_iota(jnp.int32, (CHUNK, CHUNK), 1)
    head_of_lane = lax.broadcasted_iota(jnp.int32, (CHUNK, A_WIDTH), 1) // A_HEAD_DIM
    w_heads = [jnp.where(col <= row, ws_ref[hd], 0.0).astype(BF16) for hd in range(A_HEADS)]
    for cidx in range(tm // CHUNK):
        rows = slice(cidx * CHUNK, (cidx + 1) * CHUNK)
        vc = v[rows]
        sv = bs_ref[...]
        for hd in range(A_HEADS):
            full = jnp.dot(w_heads[hd], vc, preferred_element_type=F32)
            sv = sv + jnp.where(head_of_lane == hd, full, 0.0)
        a_ref[rows, :] = (u[rows] * sv).astype(a_ref.dtype)

    cos = cos_ref[...]
    s1 = s1_ref[...]
    s2 = s2_ref[...]
    q0 = 2 * A_WIDTH
    k0 = q0 + B_WIDTH
    v0 = k0 + B_WIDTH
    for j in range(B_WIDTH // LANES):
        lanes = slice(j * LANES, (j + 1) * LANES)
        for base, ref, nat, scale in ((q0, qs_ref, qn_ref, Q_SCALE), (k0, ks_ref, kn_ref, 1.0)):
            xs = proj[:, base + j * LANES: base + (j + 1) * LANES]
            rot = (xs * cos + pltpu.roll(xs, LANES - ROT_DIM // 2, 1) * s1
                   + pltpu.roll(xs, ROT_DIM // 2, 1) * s2) * scale
            nat[:, lanes] = rot.astype(nat.dtype)
            ref[j] = rot
        vs = proj[:, v0 + j * LANES:v0 + (j + 1) * LANES]
        vn_ref[:, lanes] = vs.astype(vn_ref.dtype)
        vs_ref[j] = vs

    for src, stage, outs in ((qs_ref, qt_ref, q_outs), (ks_ref, kt_ref, k_outs), (vs_ref, vt_ref, v_outs)):
        bufs = (src, stage)
        prev = 1
        for idx, (dil, out) in enumerate(zip(copy_dils, outs)):
            chained = prev > 1 and dil % prev == 0
            base_dil = prev if chained else 1
            source = bufs[idx % 2] if chained or idx == 0 else None
            assert source is not None, "copy dilations must form a divisibility chain"
            step = dil // base_dil
            per = tm // dil
            keep = idx + 1 < len(copy_dils)
            for r_prev in range(base_dil):
                for q in range(step):
                    r = r_prev + base_dil * q
                    start = r_prev * (tm // base_dil) + q
                    for j in range(B_WIDTH // LANES):
                        val = source[j, pl.ds(start, per, stride=step), :]
                        out[0, r, :, j * LANES:(j + 1) * LANES] = val.astype(out.dtype)
                        if keep:
                            bufs[(idx + 1) % 2][j, r * per:(r + 1) * per, :] = val
            prev = dil

    s_ref[...] = proj[:, v0 + B_WIDTH:].astype(s_ref.dtype)


def _inproj(x2d, mod3, w_in_bf, cos_t, s1_t, s2_t, ln_g, ln_b, ws, bs_full, bsz, t):
    n, d = x2d.shape
    tm = min(ROW_TILE, t)
    nt = t // tm
    pw = w_in_bf.shape[1]
    c_width = pw - 2 * A_WIDTH - 3 * B_WIDTH
    tok = lambda b, i: (b * nt + i, 0)
    const2 = lambda b, i: (0, 0)
    outs = [jax.ShapeDtypeStruct((n, A_WIDTH), BF16),
            jax.ShapeDtypeStruct((t, bsz * c_width), BF16)]
    out_specs = [pl.BlockSpec((tm, A_WIDTH), tok),
                 pl.BlockSpec((tm, c_width), lambda b, i: (i, b))]
    for _ in range(3):
        outs.append(jax.ShapeDtypeStruct((n, B_WIDTH), BF16))
        out_specs.append(pl.BlockSpec((tm, B_WIDTH), tok))
    copy_dils = [dil for dil in DILATIONS if _dilation_layout(dil) == "copy"]
    for _ in range(3):
        for dil in copy_dils:
            outs.append(jax.ShapeDtypeStruct((bsz, dil, t // dil, B_WIDTH), BF16))
            out_specs.append(pl.BlockSpec((1, dil, tm // dil, B_WIDTH), lambda b, i: (b, 0, i, 0)))
    res = pl.pallas_call(
        _inproj_kernel,
        grid=(bsz, nt),
        in_specs=[pl.BlockSpec((tm, d), tok),
                  pl.BlockSpec((1, 1, d), lambda b, i: (b, 0, 0)),
                  pl.BlockSpec((1, 1, d), lambda b, i: (b, 0, 1)),
                  pl.BlockSpec((d, pw), const2),
                  pl.BlockSpec((tm, LANES), tok),
                  pl.BlockSpec((tm, LANES), tok),
                  pl.BlockSpec((tm, LANES), tok),
                  pl.BlockSpec((1, A_WIDTH), const2),
                  pl.BlockSpec((1, A_WIDTH), const2),
                  pl.BlockSpec((A_HEADS, CHUNK, CHUNK), lambda b, i: (0, 0, 0)),
                  pl.BlockSpec((CHUNK, A_WIDTH), const2)],
        out_specs=out_specs,
        out_shape=outs,
        scratch_shapes=[pltpu.VMEM((B_WIDTH // LANES, tm, LANES), F32)] * 6,
        compiler_params=_cparams(("arbitrary", "arbitrary")),
        name="inproj_gmlp_rope",
    )(x2d, mod3, mod3, w_in_bf, cos_t, s1_t, s2_t, ln_g, ln_b, ws, bs_full)
    a_out, s_in = res[0], res[1]
    natural = res[2:5]
    n_copy = len(copy_dils)
    copies = {dil: tuple(res[5 + z * n_copy + g] for z in range(3)) for g, dil in enumerate(copy_dils)}
    qkv = [copies[dil] if dil in copies else natural for dil in DILATIONS]
    return a_out, s_in, qkv


def _att_kernel(q_ref, k_ref, v_ref, o_ref, st_ref, kcat_ref, vaug_ref):
    i = pl.program_id(2)
    blk = ATT_BLK
    n_res = q_ref.shape[1]
    n_sub = q_ref.shape[2] // blk
    n_pairs = B_WIDTH // LANES
    last = slice(n_sub * blk, (n_sub + 1) * blk)

    @pl.when(i == 0)
    def _():
        kcat_ref[:, 0:blk, :] = jnp.zeros((n_res, blk, B_WIDTH), BF16)
        vaug_ref[:, :, 0:blk, 0:LANES] = jnp.zeros((n_res, n_pairs, blk, LANES), BF16)
        vaug_ref[:, :, :, LANES:] = jnp.ones((n_res, n_pairs, (n_sub + 1) * blk, LANES), BF16)

    @pl.when(i > 0)
    def _():
        kcat_ref[:, 0:blk, :] = kcat_ref[:, last, :]
        vaug_ref[:, :, 0:blk, 0:LANES] = vaug_ref[:, :, last, 0:LANES]

    for rr in range(n_res):
        kcat_ref[rr, blk:, :] = k_ref[0, rr]
        for pair in range(n_pairs):
            vaug_ref[rr, pair, blk:, 0:LANES] = v_ref[0, rr, :, pair * LANES:(pair + 1) * LANES]

    qi = lax.broadcasted_iota(jnp.int32, (blk, 2 * blk), 0)
    kk = lax.broadcasted_iota(jnp.int32, (blk, 2 * blk), 1)
    band = (kk >= qi) & (kk <= qi + blk)
    first = band & ((i > 0) | (kk >= blk))
    lane = lax.broadcasted_iota(jnp.int32, (blk, LANES), 1)
    low_half = lane < B_HEAD_DIM
    nt = (((1,), (1,)), ((), ()))
    for rr, sub in [(rr, sub) for rr in range(n_res) for sub in range(n_sub)]:
        rows = slice(sub * blk, (sub + 1) * blk)
        keys = slice(sub * blk, (sub + 2) * blk)
        mask = first if sub == 0 else band
        stats = jnp.zeros((blk, LANES), F32)
        for pair in range(n_pairs):
            lanes = slice(pair * LANES, (pair + 1) * LANES)
            q2 = q_ref[0, rr, rows, lanes]
            kslab = kcat_ref[rr, keys, lanes]
            res = []
            for hh in range(2):
                hd = 2 * pair + hh
                qm = jnp.where(low_half if hh == 0 else jnp.logical_not(low_half), q2, jnp.zeros_like(q2))
                s = lax.dot_general(qm, kslab, nt, preferred_element_type=F32)
                s = jnp.where(mask, s, NEG_INF)
                m = jnp.max(jnp.maximum(s[:, :blk], s[:, blk:]), axis=1, keepdims=True)
                p = jnp.exp2(s - m).astype(BF16)
                ov = jnp.dot(p, vaug_ref[rr, pair, keys, :], preferred_element_type=F32)
                den = ov[:, LANES:]
                res.append(ov[:, :LANES] / den)
                stats = jnp.where(lane == hd, m, stats)
                stats = jnp.where(lane == B_HEADS + hd, den, stats)
            o_ref[0, rr, rows, lanes] = jnp.where(low_half, res[0], res[1]).astype(o_ref.dtype)
        st_ref[0, rr, rows, :] = stats


def _dilation_layout(dil):
    return "natural" if dil == 1 else "copy"


def _att_branch(q, k, v, dil, bsz, t):
    rows = t // dil
    n_sub = min(ATT_SUB_BLOCKS, rows // ATT_BLK)
    step_rows = n_sub * ATT_BLK
    n_steps = rows // step_rows
    n_res = math.gcd(dil, max(1, ATT_SUB_BLOCKS // n_sub)) if n_steps == 1 else 1
    shape = lambda w: (bsz, dil, rows, w)
    blk = lambda w: pl.BlockSpec((1, n_res, step_rows, w), lambda b, r, i: (b, r, i, 0))
    q, k, v = (z.reshape(shape(B_WIDTH)) for z in (q, k, v))
    return pl.pallas_call(
        _att_kernel,
        grid=(bsz, dil // n_res, n_steps),
        in_specs=[blk(B_WIDTH), blk(B_WIDTH), blk(B_WIDTH)],
        out_specs=[blk(B_WIDTH), blk(LANES)],
        out_shape=[jax.ShapeDtypeStruct(shape(B_WIDTH), BF16),
                   jax.ShapeDtypeStruct(shape(LANES), F32)],
        scratch_shapes=[pltpu.VMEM((n_res, step_rows + ATT_BLK, B_WIDTH), BF16),
                        pltpu.VMEM((n_res, B_WIDTH // LANES, step_rows + ATT_BLK, 2 * LANES), BF16)],
        compiler_params=_cparams(("arbitrary", "arbitrary", "arbitrary")),
        name=f"dilated_attn_d{dil}",
    )(q, k, v)


def _ssm_kernel(u_ref, bmat_ref, cmat_ref, are_ref, aim_ref, dskip_ref, gw_ref, gb_ref,
                o_ref, state_ref, xbuf_ref, rows_ref):
    bsz, two_n = state_ref.shape
    n_state = two_n // 2
    steps = u_ref.shape[0]
    cw = u_ref.shape[1] // bsz
    n_slab = cw // LANES

    @pl.when(pl.program_id(0) == 0)
    def _():
        state_ref[...] = jnp.zeros_like(state_ref)

    for b in range(bsz):
        for j in range(n_slab):
            rows_ref[j, pl.ds(b, steps, stride=bsz), :] = (
                u_ref[:, b * cw + j * LANES:b * cw + (j + 1) * LANES].astype(F32))
    u32 = jnp.concatenate([rows_ref[j] for j in range(n_slab)], axis=1)
    xbuf_ref[...] = jnp.dot(u32.astype(BF16), bmat_ref[...], preferred_element_type=F32)
    a_re = jnp.broadcast_to(are_ref[...], (bsz, n_state))
    a_im = jnp.broadcast_to(aim_ref[...], (bsz, n_state))

    def step(tt, carry):
        x_re, x_im = carry
        r0 = pl.multiple_of(tt * bsz, bsz)
        b_re = xbuf_ref[pl.ds(r0, bsz), :n_state]
        b_im = xbuf_ref[pl.ds(r0, bsz), n_state:]
        n_re = a_re * x_re - a_im * x_im + b_re
        n_im = a_re * x_im + a_im * x_re + b_im
        xbuf_ref[pl.ds(r0, bsz), :n_state] = n_re
        xbuf_ref[pl.ds(r0, bsz), n_state:] = n_im
        return n_re, n_im

    x_re, x_im = lax.fori_loop(0, steps, step, (state_ref[:, :n_state], state_ref[:, n_state:]), unroll=4)
    state_ref[:, :n_state] = x_re
    state_ref[:, n_state:] = x_im

    y = jnp.dot(xbuf_ref[...].astype(BF16), cmat_ref[...], preferred_element_type=F32)
    y = _gelu_tanh(y + dskip_ref[...] * u32)
    gate = jax.nn.sigmoid(jnp.dot(y.astype(BF16), gw_ref[...], preferred_element_type=F32) + gb_ref[...])
    out = y * gate
    for j in range(n_slab):
        rows_ref[j] = out[:, j * LANES:(j + 1) * LANES]
    for b in range(bsz):
        for j in range(n_slab):
            o_ref[:, b * cw + j * LANES:b * cw + (j + 1) * LANES] = (
                rows_ref[j, pl.ds(b, steps, stride=bsz), :].astype(o_ref.dtype))


def _ssm_weights(lam_re, lam_im, log_dt, b_re, b_im, c_re, c_im):
    g, n_st = lam_re.shape
    cg = b_re.shape[-1]
    dt = jnp.exp(log_dt)[:, None]
    mag = jnp.exp(lam_re * dt)
    ab_re = mag * jnp.cos(lam_im * dt)
    ab_im = mag * jnp.sin(lam_im * dt)
    nr = ab_re - 1.0
    ni = ab_im
    mod2 = lam_re * lam_re + lam_im * lam_im
    f_re = (nr * lam_re + ni * lam_im) / mod2
    f_im = (ni * lam_re - nr * lam_im) / mod2
    bb_re = f_re[..., None] * b_re - f_im[..., None] * b_im
    bb_im = f_re[..., None] * b_im + f_im[..., None] * b_re
    eye = jnp.eye(g, dtype=F32)
    bm_re = jnp.einsum('gnc,gh->gchn', bb_re, eye).reshape(g * cg, g * n_st)
    bm_im = jnp.einsum('gnc,gh->gchn', bb_im, eye).reshape(g * cg, g * n_st)
    bmat = jnp.concatenate([bm_re, bm_im], axis=1)
    cm_re = jnp.einsum('gcn,gh->gnhc', c_re, eye).reshape(g * n_st, g * cg)
    cm_im = jnp.einsum('gcn,gh->gnhc', c_im, eye).reshape(g * n_st, g * cg)
    cmat = jnp.concatenate([cm_re, -cm_im], axis=0)
    return (bmat.astype(BF16), cmat.astype(BF16),
            ab_re.reshape(1, g * n_st), ab_im.reshape(1, g * n_st))


def _ssm(u_tm, bmat, cmat, a_re, a_im, d_skip, glu_w_bf, glu_b, bsz, t):
    cw = bmat.shape[0]
    two_n = bmat.shape[1]
    steps = min(SSM_STEPS, t)
    rows = steps * bsz
    const = lambda i: (0, 0)
    return pl.pallas_call(
        _ssm_kernel,
        grid=(t // steps,),
        in_specs=[pl.BlockSpec((steps, bsz * cw), lambda i: (i, 0)),
                  pl.BlockSpec((cw, two_n), const),
                  pl.BlockSpec((two_n, cw), const),
                  pl.BlockSpec((1, two_n // 2), const),
                  pl.BlockSpec((1, two_n // 2), const),
                  pl.BlockSpec((1, cw), const),
                  pl.BlockSpec((cw, cw), const),
                  pl.BlockSpec((1, cw), const)],
        out_specs=pl.BlockSpec((steps, bsz * cw), lambda i: (i, 0)),
        out_shape=jax.ShapeDtypeStruct((t, bsz * cw), BF16),
        scratch_shapes=[pltpu.VMEM((bsz, two_n), F32), pltpu.VMEM((rows, two_n), F32),
                        pltpu.VMEM((cw // LANES, rows, LANES), F32)],
        compiler_params=_cparams(("arbitrary",)),
        name="s5_scan_glu",
    )(u_tm, bmat, cmat, a_re, a_im, d_skip, glu_w_bf, glu_b)


def _route(sel, scores):
    gs = []
    for g in range(N_EXPERT_GROUPS):
        a, b, c, d = sel[EXPERTS_PER_GROUP * g: EXPERTS_PER_GROUP * (g + 1)]
        hi1, lo1 = jnp.maximum(a, b), jnp.minimum(a, b)
        hi2, lo2 = jnp.maximum(c, d), jnp.minimum(c, d)
        gs.append(jnp.maximum(hi1, hi2) + jnp.maximum(jnp.minimum(hi1, hi2), jnp.maximum(lo1, lo2)))
    g_idx = jnp.zeros(gs[0].shape, jnp.int32)
    best = gs[0]
    for g in range(1, N_EXPERT_GROUPS):
        better = gs[g] > best
        g_idx = jnp.where(better, g, g_idx)
        best = jnp.where(better, gs[g], best)

    def pick(rows, j):
        out = rows[j]
        for g in range(1, N_EXPERT_GROUPS):
            out = jnp.where(g_idx == g, rows[EXPERTS_PER_GROUP * g + j], out)
        return out

    v = [pick(sel, j) for j in range(EXPERTS_PER_GROUP)]
    s = [pick(scores, j) for j in range(EXPERTS_PER_GROUP)]
    i1 = jnp.zeros(g_idx.shape, jnp.int32)
    b1, g1 = v[0], s[0]
    for j in range(1, EXPERTS_PER_GROUP):
        better = v[j] > b1
        i1 = jnp.where(better, j, i1)
        b1 = jnp.where(better, v[j], b1)
        g1 = jnp.where(better, s[j], g1)
    i2 = jnp.full(g_idx.shape, -1, jnp.int32)
    b2 = jnp.zeros_like(b1)
    g2 = jnp.zeros_like(g1)
    for j in range(EXPERTS_PER_GROUP):
        better = (i1 != j) & ((i2 < 0) | (v[j] > b2))
        i2 = jnp.where(better, j, i2)
        b2 = jnp.where(better, v[j], b2)
        g2 = jnp.where(better, s[j], g2)
    tot = g1 + g2
    base = g_idx * EXPERTS_PER_GROUP
    return base + i1, base + i2, g1 / tot, g2 / tot


def _natural_order(ref, scr_ref, dil):
    if _dilation_layout(dil) != "copy":
        return ref[...].astype(F32)
    per = ref.shape[2]
    n_slab = scr_ref.shape[0]
    for r in range(dil):
        for j in range(n_slab):
            scr_ref[j, pl.ds(r, per, stride=dil), :] = ref[0, r, :, j * LANES:(j + 1) * LANES].astype(F32)
    if n_slab == 1:
        return scr_ref[0]
    return jnp.concatenate([scr_ref[j] for j in range(n_slab)], axis=1)


def _outproj_kernel(a_ref, o1_ref, o2_ref, o3_ref, st1_ref, st2_ref, st3_ref,
                    c_ref, x_ref, g1_ref, sh2_ref, sc2_ref, wout_ref, expand_ref, lng_ref, lnb_ref,
                    rwh_ref, rwl_ref, rb_ref, x1_ref, eidx_ref, gate_ref, tab_ref, xs_hbm,
                    oscr2_ref, oscr3_ref, sscr2_ref, sscr3_ref, hbuf, state_ref, slot_ref, fin_ref, fins_ref,
                    ssem, csem, *, alpha):
    o_refs = (o1_ref, o2_ref, o3_ref)
    st_refs = (st1_ref, st2_ref, st3_ref)
    o_scr = (None, oscr2_ref, oscr3_ref)
    st_scr = (None, sscr2_ref, sscr3_ref)
    ms = [_natural_order(st_refs[g], st_scr[g], DILATIONS[g]) for g in range(len(DILATIONS))]
    dens = [pltpu.roll(m, LANES - B_HEADS, 1) for m in ms]
    mx = jnp.maximum(jnp.maximum(ms[0], ms[1]), ms[2])
    ws = [dens[g] * jnp.exp2(ms[g] - mx) for g in range(len(DILATIONS))]
    lane = lax.broadcasted_iota(jnp.int32, mx.shape, 1)
    tot = jnp.where(lane < B_HEADS, ws[0] + ws[1] + ws[2], 1.0)
    expand = expand_ref[...]

    def widen(w):
        wn = jnp.where(lane < B_HEADS, w / tot, 0.0)
        hi = wn.astype(BF16)
        lo = (wn - hi.astype(F32)).astype(BF16)
        return (jnp.dot(hi, expand, preferred_element_type=F32)
                + jnp.dot(lo, expand, preferred_element_type=F32))

    b_out = widen(ws[0]) * _natural_order(o_refs[0], o_scr[0], DILATIONS[0])
    for g in range(1, len(DILATIONS)):
        b_out = b_out + widen(ws[g]) * _natural_order(o_refs[g], o_scr[g], DILATIONS[g])
    b_out = b_out.astype(BF16)

    wout = wout_ref
    mix = (jnp.dot(a_ref[...], wout[:A_WIDTH, :], preferred_element_type=F32)
           + jnp.dot(b_out, wout[A_WIDTH:A_WIDTH + B_WIDTH, :], preferred_element_type=F32)
           + jnp.dot(c_ref[...], wout[A_WIDTH + B_WIDTH:, :], preferred_element_type=F32))
    x1 = _ln(alpha * x_ref[...] + (1.0 + g1_ref[0]) * mix) * lng_ref[...] + lnb_ref[...]
    x1_ref[...] = x1
    h2 = _ln(x1) * (1.0 + sc2_ref[0]) + sh2_ref[0]

    hi = h2.astype(BF16)
    lo = (h2 - hi.astype(F32)).astype(BF16)
    nt = (((1,), (1,)), ((), ()))
    logits = (lax.dot_general(rwh_ref[...], hi, nt, preferred_element_type=F32)
              + lax.dot_general(rwh_ref[...], lo, nt, preferred_element_type=F32)
              + lax.dot_general(rwl_ref[...], hi, nt, preferred_element_type=F32))
    scores = jax.nn.sigmoid(logits)
    sel = scores + rb_ref[...]
    sel_rows = [sel[e:e + 1, :] for e in range(N_EXPERTS)]
    score_rows = [scores[e:e + 1, :] for e in range(N_EXPERTS)]
    e1, e2, gt1, gt2 = _route(sel_rows, score_rows)

    g = pl.program_id(0) * pl.num_programs(1) + pl.program_id(1)
    n_tiles = pl.num_programs(0) * pl.num_programs(1)
    tm = h2.shape[0]
    n_slab = h2.shape[1] // LANES
    blk = MOE_BLK
    par = g % 2

    @pl.when(g == 0)
    def _():
        state_ref[...] = jnp.zeros_like(state_ref)
        tab_ref[...] = jnp.full(tab_ref.shape, float(N_EXPERTS), F32)

    erow = lax.broadcasted_iota(jnp.int32, (N_EXPERTS, tm), 0)
    oh1 = erow == e1
    oh2 = erow == e2
    both = (oh1 | oh2).astype(BF16)
    earlier = (lax.broadcasted_iota(jnp.int32, (tm, tm), 0)
               < lax.broadcasted_iota(jnp.int32, (tm, tm), 1)).astype(BF16)
    base = state_ref[0, :, 0:1]
    cur = state_ref[1, :, 0:1]
    nfree = state_ref[2, :, 0:1]
    n_vec = jnp.sum(both.astype(F32), axis=1, keepdims=True)
    off = base - jnp.floor(base * (1.0 / blk)) * blk
    need_a = (off == 0.0) & (n_vec > 0.0)
    need_b = (off + n_vec) > float(blk)
    na = need_a.astype(F32)
    nb = need_b.astype(F32)
    alloc = na + nb
    lower = (lax.broadcasted_iota(jnp.int32, (N_EXPERTS, N_EXPERTS), 1)
             < lax.broadcasted_iota(jnp.int32, (N_EXPERTS, N_EXPERTS), 0)).astype(BF16)
    alloc_wide = jnp.broadcast_to(alloc, (N_EXPERTS, LANES))
    before = jnp.dot(lower, alloc_wide.astype(BF16), preferred_element_type=F32)[:, 0:1]
    a_id = jnp.where(need_a, nfree + before, cur)
    b_id = jnp.where(need_b, nfree + before + na, a_id)
    bound = base - off + float(blk)
    total_alloc = jnp.sum(alloc_wide, axis=0, keepdims=True)[:, 0:1]
    state_ref[0] = jnp.broadcast_to(base + n_vec, (N_EXPERTS, LANES))
    state_ref[1] = jnp.broadcast_to(jnp.where(need_b, b_id, a_id), (N_EXPERTS, LANES))
    state_ref[2] = jnp.broadcast_to(nfree + total_alloc, (N_EXPERTS, LANES))

    blane = lax.broadcasted_iota(jnp.int32, (N_EXPERTS, tab_ref.shape[1]), 1).astype(F32)
    hit = ((blane == a_id) & need_a) | ((blane == b_id) & need_b)
    e_col = lax.broadcasted_iota(jnp.int32, (N_EXPERTS, 1), 0).astype(F32)
    delta = jnp.sum(jnp.where(hit, e_col - float(N_EXPERTS), 0.0), axis=0, keepdims=True)
    tab_ref[...] = tab_ref[...] + jnp.broadcast_to(delta, tab_ref.shape)

    rank = jnp.dot(both, earlier, preferred_element_type=F32) + base
    slot = jnp.where(rank < bound, a_id * blk + (rank - bound + blk), b_id * blk + (rank - bound))
    s1 = jnp.sum(jnp.where(oh1, slot, 0.0), axis=0, keepdims=True)
    s2 = jnp.sum(jnp.where(oh2, slot, 0.0), axis=0, keepdims=True)

    eidx_ref[...] = jnp.zeros_like(eidx_ref)
    gate_ref[...] = jnp.zeros_like(gate_ref)
    eidx_ref[0:1, :] = e1
    eidx_ref[1:2, :] = e2
    eidx_ref[2:3, :] = s1.astype(jnp.int32)
    eidx_ref[3:4, :] = s2.astype(jnp.int32)
    gate_ref[0:1, :] = gt1
    gate_ref[1:2, :] = gt2

    slot_copy = pltpu.make_async_copy(eidx_ref, slot_ref, csem)
    slot_copy.start()

    def wait_rows(p):
        for _ in range(TOP_K):
            pltpu.make_async_copy(hbuf.at[p], xs_hbm.at[pl.ds(0, tm * n_slab), :], ssem.at[p]).wait()

    @pl.when(g >= 2)
    def _():
        wait_rows(par)

    _store_tile_rows(hbuf.at[par], h2)
    slot_copy.wait()

    rows_src = hbuf.at[par]
    for s in range(tm):
        src = rows_src.at[pl.ds(s * n_slab, n_slab), :]
        for kk in range(TOP_K):
            dst = pl.multiple_of(slot_ref[TOP_K + kk, s] * n_slab, n_slab)
            pltpu.async_copy(src, xs_hbm.at[pl.ds(dst, n_slab), :], ssem.at[par], priority=kk % 2)

    @pl.when(g == n_tiles - 1)
    def _():
        wait_rows(par)

        @pl.when(n_tiles >= 2)
        def _():
            wait_rows(1 - par)

        lane_e = lax.broadcasted_iota(jnp.int32, (N_EXPERTS, LANES), 1)
        sub_e = lax.broadcasted_iota(jnp.int32, (N_EXPERTS, LANES), 0)

        def by_lane(col):
            return jnp.sum(jnp.where(lane_e == sub_e, col, 0.0), axis=0, keepdims=True)

        end = base + n_vec
        end_off = end - jnp.floor(end * (1.0 / blk)) * blk
        fin_ref[...] = jnp.zeros_like(fin_ref)
        fin_ref[0:1, :] = by_lane(jnp.where(need_b, b_id, a_id)).astype(jnp.int32)
        fin_ref[1:2, :] = by_lane(end_off).astype(jnp.int32)
        fin_ref[2:3, :] = by_lane(nfree + total_alloc).astype(jnp.int32)
        fin_copy = pltpu.make_async_copy(fin_ref, fins_ref, csem)
        fin_copy.start()
        fin_copy.wait()

        hbuf[0] = jnp.zeros(hbuf.shape[1:], F32)

        def zero_rows(first_row, n_rows):
            cp = pltpu.make_async_copy(hbuf.at[0, pl.ds(0, n_rows * n_slab), :],
                                       xs_hbm.at[pl.ds(pl.multiple_of(first_row * n_slab, n_slab),
                                                       n_rows * n_slab), :], ssem.at[0])
            cp.start()
            cp.wait()

        for e in range(N_EXPERTS):
            pos = fins_ref[1, e]
            start = fins_ref[0, e] * blk
            for k in range(blk.bit_length() - 1):
                take = (fins_ref[1, e] > 0) & (((pos >> k) & 1) == 1)

                @pl.when(take)
                def _():
                    zero_rows(start + pos, 1 << k)
                pos = jnp.where(take, pos + (1 << k), pos)

        n_blk = xs_hbm.shape[0] // (blk * n_slab)
        for t in range(N_EXPERTS):
            spare = fins_ref[2, 0] + t

            @pl.when(spare < n_blk)
            def _():
                zero_rows(spare * blk, blk)


def _outproj(a_out, att, c_tm, x2d, mod3, w_out_bf, ln_g, ln_b, rw_hi, rw_lo, rbias, bsz, t, alpha):
    n, d = x2d.shape
    tm = min(ROW_TILE, t)
    assert tm == MOE_BLK and MOE_BLK & (MOE_BLK - 1) == 0
    nt = t // tm
    n_slab = d // LANES
    n_blk = n * TOP_K // MOE_BLK + N_EXPERTS
    tab_w = -(-n_blk // LANES) * LANES
    cw = c_tm.shape[1] // bsz
    tok = lambda b, i: (b * nt + i, 0)
    const2 = lambda b, i: (0, 0)
    head = jnp.arange(B_WIDTH) // B_HEAD_DIM
    expand = (jnp.arange(LANES)[:, None] == head[None, :]).astype(BF16)

    def branch_input(z, dil, w):
        if _dilation_layout(dil) == "copy":
            return z, pl.BlockSpec((1, dil, tm // dil, w), lambda b, i: (b, 0, i, 0))
        return z.reshape(n, w), pl.BlockSpec((tm, w), tok)

    o_in = [branch_input(o, dil, B_WIDTH) for (o, _), dil in zip(att, DILATIONS)]
    st_in = [branch_input(st, dil, LANES) for (_, st), dil in zip(att, DILATIONS)]
    (o1, st1), (o2, st2), (o3, st3) = [(o_in[g][0], st_in[g][0]) for g in range(len(DILATIONS))]
    modspec = lambda j: pl.BlockSpec((1, 1, d), lambda b, i: (b, 0, j))
    return pl.pallas_call(
        functools.partial(_outproj_kernel, alpha=alpha),
        grid=(bsz, nt),
        in_specs=[pl.BlockSpec((tm, A_WIDTH), tok)]
                 + [spec for _, spec in o_in]
                 + [spec for _, spec in st_in]
                 + [pl.BlockSpec((tm, cw), lambda b, i: (i, b)),
                  pl.BlockSpec((tm, d), tok),
                  modspec(2), modspec(3), modspec(4),
                  pl.BlockSpec((d, d), const2),
                  pl.BlockSpec((LANES, B_WIDTH), const2),
                  pl.BlockSpec((1, d), const2), pl.BlockSpec((1, d), const2),
                  pl.BlockSpec((N_EXPERTS, d), const2), pl.BlockSpec((N_EXPERTS, d), const2),
                  pl.BlockSpec((N_EXPERTS, 1), const2)],
        out_specs=[pl.BlockSpec((tm, d), tok),
                   pl.BlockSpec((SUBLANES, tm), lambda b, i: (0, b * nt + i)),
                   pl.BlockSpec((SUBLANES, tm), lambda b, i: (0, b * nt + i)),
                   pl.BlockSpec((SUBLANES, tab_w), const2),
                   pl.BlockSpec(memory_space=pl.ANY)],
        out_shape=[jax.ShapeDtypeStruct((n, d), F32),
                   jax.ShapeDtypeStruct((SUBLANES, n), jnp.int32),
                   jax.ShapeDtypeStruct((SUBLANES, n), F32),
                   jax.ShapeDtypeStruct((SUBLANES, tab_w), F32),
                   jax.ShapeDtypeStruct((n_blk * MOE_BLK * n_slab, LANES), F32)],
        scratch_shapes=[pltpu.VMEM((B_WIDTH // LANES, tm, LANES), F32),
                        pltpu.VMEM((B_WIDTH // LANES, tm, LANES), F32),
                        pltpu.VMEM((1, tm, LANES), F32), pltpu.VMEM((1, tm, LANES), F32),
                        pltpu.VMEM((2, tm * n_slab, LANES), F32),
                        pltpu.VMEM((3, N_EXPERTS, LANES), F32),
                        pltpu.SMEM((SUBLANES, tm), jnp.int32),
                        pltpu.VMEM((SUBLANES, LANES), jnp.int32),
                        pltpu.SMEM((SUBLANES, LANES), jnp.int32),
                        pltpu.SemaphoreType.DMA((2,)), pltpu.SemaphoreType.DMA],
        compiler_params=_cparams(("arbitrary", "arbitrary")),
        name="outproj_ln_router",
    )(a_out, o1, o2, o3, st1, st2, st3, c_tm, x2d, mod3, mod3, mod3, w_out_bf, expand,
      ln_g, ln_b, rw_hi, rw_lo, rbias)


def _block_order(tab_row, n_blk):
    tab = tab_row[:n_blk].astype(jnp.int32)
    idx = jnp.arange(n_blk, dtype=jnp.int32)
    before = (tab[None, :] < tab[:, None]) | ((tab[None, :] == tab[:, None]) & (idx[None, :] < idx[:, None]))
    pos = jnp.sum(before.astype(jnp.int32), axis=1)
    at_pos = pos[None, :] == idx[:, None]
    perm = jnp.sum(jnp.where(at_pos, idx[None, :], 0), axis=1).astype(jnp.int32)
    blk_e = jnp.minimum(jnp.sum(jnp.where(at_pos, tab[None, :], 0), axis=1), N_EXPERTS - 1).astype(jnp.int32)
    n_used = jnp.sum((tab < N_EXPERTS).astype(jnp.int32)).reshape(1)
    return perm, blk_e, n_used


def _expert_kernel(blk_e_ref, nused_ref, perm_ref, xs_ref, wg_ref, wu_ref, wd_ref, ys_ref, wg_bf, wu_bf, wd_bf,
                   *, blk, n_slab):
    j = pl.program_id(0)

    @pl.when((j == 0) | (blk_e_ref[j] != blk_e_ref[jnp.maximum(j - 1, 0)]))
    def _():
        wg_bf[...] = wg_ref[0, 0].astype(BF16)
        wu_bf[...] = wu_ref[0, 0].astype(BF16)
        wd_bf[...] = wd_ref[0, 0].astype(BF16)

    @pl.when(j < nused_ref[0])
    def _():
        rows = _load_tile_rows(xs_ref, 0, blk, n_slab).astype(BF16)
        gt = jnp.dot(rows, wg_bf[...], preferred_element_type=F32)
        up = jnp.dot(rows, wu_bf[...], preferred_element_type=F32)
        hid = (gt * jax.nn.sigmoid(gt) * up).astype(BF16)
        _store_tile_rows(ys_ref, jnp.dot(hid, wd_bf[...], preferred_element_type=F32))

    @pl.when(j >= nused_ref[0])
    def _():
        ys_ref[...] = jnp.zeros_like(ys_ref)


def _moe(xs, tab, w_gate, w_up, w_down, layer):
    d, de = w_gate.shape[-2:]
    n_slab = d // LANES
    blk = MOE_BLK
    n_blk = xs.shape[0] // (blk * n_slab)
    perm, blk_e, n_used = _block_order(tab[0], n_blk)

    def x_block(j, be, nu, pm):
        return (pm[jnp.minimum(j, nu[0] - 1)], 0)

    return pl.pallas_call(
        functools.partial(_expert_kernel, blk=blk, n_slab=n_slab),
        grid_spec=pltpu.PrefetchScalarGridSpec(
            num_scalar_prefetch=3,
            grid=(n_blk,),
            in_specs=[pl.BlockSpec((blk * n_slab, LANES), x_block),
                      pl.BlockSpec((1, 1, d, de), lambda j, be, nu, pm: (layer, be[j], 0, 0)),
                      pl.BlockSpec((1, 1, d, de), lambda j, be, nu, pm: (layer, be[j], 0, 0)),
                      pl.BlockSpec((1, 1, de, d), lambda j, be, nu, pm: (layer, be[j], 0, 0))],
            out_specs=pl.BlockSpec((blk * n_slab, LANES), lambda j, be, nu, pm: (pm[j], 0)),
            scratch_shapes=[pltpu.VMEM((d, de), BF16), pltpu.VMEM((d, de), BF16), pltpu.VMEM((de, d), BF16)],
        ),
        out_shape=jax.ShapeDtypeStruct(xs.shape, F32),
        compiler_params=_cparams(("arbitrary",)),
        name="moe_experts",
    )(blk_e, n_used, perm, xs, w_gate, w_up, w_down)


def _ffn_ln_kernel(dest_ref, next_ref, x_ref, gate_ref, g2_ref, lng_ref, lnb_ref, ys_hbm, o_ref, ybuf, sem,
                   *, alpha):
    tm, d = x_ref.shape
    n_slab = d // LANES
    g = pl.program_id(0) * pl.num_programs(1) + pl.program_id(1)
    n_tiles = pl.num_programs(0) * pl.num_programs(1)
    slot = g % 2

    def gather(idx_ref, buf_slot):
        def issue(pair, c):
            for q in range(2):
                s = 2 * pair + q
                src = pl.multiple_of(idx_ref[0, 0, s] * n_slab, n_slab)
                dst = pl.multiple_of(s * n_slab, n_slab)
                pltpu.async_copy(ys_hbm.at[pl.ds(src, n_slab), :],
                                 ybuf.at[buf_slot, pl.ds(dst, n_slab), :], sem.at[buf_slot], priority=q)
            return c
        lax.fori_loop(0, TOP_K * tm // 2, issue, 0, unroll=4)

    def wait_rows(b):
        pltpu.make_async_copy(ys_hbm.at[pl.ds(0, TOP_K * tm * n_slab), :], ybuf.at[b], sem.at[b]).wait()

    @pl.when(g == 0)
    def _():
        gather(dest_ref, 0)

    wait_rows(slot)

    nxt = ybuf.at[1 - slot]
    for s in range(TOP_K * tm):
        src = pl.multiple_of(next_ref[0, 0, s] * n_slab, n_slab)
        pltpu.async_copy(ys_hbm.at[pl.ds(src, n_slab), :], nxt.at[pl.ds(s * n_slab, n_slab), :],
                         sem.at[1 - slot], priority=s % 2)

    gate = gate_ref[...]
    buf = ybuf.at[slot]
    ffn = _load_tile_rows(buf, 0, tm, n_slab) * gate[:, 0:1]
    for kk in range(1, TOP_K):
        ffn = ffn + _load_tile_rows(buf, kk * tm, tm, n_slab) * gate[:, kk:kk + 1]
    o_ref[...] = _ln(alpha * x_ref[...] + (1.0 + g2_ref[0]) * ffn) * lng_ref[...] + lnb_ref[...]

    @pl.when(g == n_tiles - 1)
    def _():
        wait_rows(1 - slot)


def _ffn_ln(x1, ys, dest, gates, mod3, ln_g, ln_b, bsz, t, alpha):
    n, d = x1.shape
    n_slab = d // LANES
    tm = min(ROW_TILE, t)
    nt = t // tm
    n_tiles = bsz * nt
    dest_tiles = dest.reshape(TOP_K, n_tiles, tm).transpose(1, 0, 2).reshape(n_tiles, 1, TOP_K * tm)
    tok = lambda b, i, *_: (b * nt + i, 0)
    const2 = lambda b, i, *_: (0, 0)
    gate_cols = gates[:TOP_K].T
    idx_blk = (1, 1, TOP_K * tm)
    return pl.pallas_call(
        functools.partial(_ffn_ln_kernel, alpha=alpha),
        grid=(bsz, nt),
        in_specs=[pl.BlockSpec(idx_blk, lambda b, i: (b * nt + i, 0, 0), memory_space=pltpu.SMEM),
                  pl.BlockSpec(idx_blk, lambda b, i: (jnp.minimum(b * nt + i + 1, n_tiles - 1), 0, 0),
                               memory_space=pltpu.SMEM),
                  pl.BlockSpec((tm, d), tok), pl.BlockSpec((tm, TOP_K), tok),
                  pl.BlockSpec((1, 1, d), lambda b, i: (b, 0, 5)),
                  pl.BlockSpec((1, d), const2), pl.BlockSpec((1, d), const2),
                  pl.BlockSpec(memory_space=pl.ANY)],
        out_specs=pl.BlockSpec((tm, d), tok),
        out_shape=jax.ShapeDtypeStruct((n, d), F32),
        scratch_shapes=[pltpu.VMEM((2, TOP_K * tm * n_slab, LANES), F32), pltpu.SemaphoreType.DMA((2,))],
        compiler_params=_cparams(("arbitrary", "arbitrary")),
        name="ffn_combine_ln",
    )(dest_tiles, dest_tiles, x1, gate_cols, mod3, ln_g, ln_b, ys)


def kernel(x, c, positions, ada_w, ada_b, w_in, gm_ln_g, gm_ln_b, gm_ws, gm_bs, ssm_lam_re, ssm_lam_im, ssm_log_dt, ssm_b_re, ssm_b_im, ssm_c_re, ssm_c_im, ssm_d, glu_w, glu_b, w_out, ln1_g, ln1_b, router_w, router_bias, exp_w_gate, exp_w_up, exp_w_down, ln2_g, ln2_b):
    bsz, t, d = x.shape
    depth = ada_w.shape[0]
    alpha = (2.0 * depth) ** 0.25
    n = bsz * t

    mod = _adaln_mod(c.astype(F32), ada_w, ada_b)
    cos_t, s1_t, s2_t = _rope_tables(positions)
    rw_t = router_w.astype(F32).T
    rw_hi = rw_t.astype(BF16)
    rw_lo = (rw_t - rw_hi.astype(F32)).astype(BF16)
    rbias = router_bias.astype(F32).reshape(N_EXPERTS, 1)

    xf = x.astype(F32).reshape(n, d)
    for l in range(depth):
        mod3 = mod[l].reshape(bsz, 1, 6 * d)
        bs_full = jnp.repeat(gm_bs[l].T, A_HEAD_DIM, axis=1)
        a_out, s_in, qkv = _inproj(
            xf, mod3, w_in[l].astype(BF16), cos_t, s1_t, s2_t,
            gm_ln_g[l].reshape(1, A_WIDTH), gm_ln_b[l].reshape(1, A_WIDTH), gm_ws[l], bs_full, bsz, t)
        att = [_att_branch(*qkv[g], dil, bsz, t) for g, dil in enumerate(DILATIONS)]
        bmat, cmat, a_re, a_im = _ssm_weights(ssm_lam_re[l], ssm_lam_im[l], ssm_log_dt[l],
                                              ssm_b_re[l], ssm_b_im[l], ssm_c_re[l], ssm_c_im[l])
        cw = bmat.shape[0]
        c_out = _ssm(s_in, bmat, cmat, a_re, a_im, ssm_d[l].reshape(1, cw),
                     glu_w[l].astype(BF16), glu_b[l].reshape(1, cw), bsz, t)
        x1, eidx, gates, tab, xs = _outproj(a_out, att, c_out, xf, mod3, w_out[l].astype(BF16),
                                            ln1_g[l].reshape(1, d), ln1_b[l].reshape(1, d),
                                            rw_hi, rw_lo, rbias, bsz, t, alpha)
        ys = _moe(xs, tab, exp_w_gate.astype(F32), exp_w_up.astype(F32), exp_w_down.astype(F32), l)
        xf = _ffn_ln(x1, ys, eidx[TOP_K:2 * TOP_K], gates, mod3, ln2_g[l].reshape(1, d), ln2_b[l].reshape(1, d),
                     bsz, t, alpha)
    return xf.reshape(bsz, t, d)
```
